```python
import math
import jax, jax.numpy as jnp
from jax import lax
import numpy as np

D_MODEL = 1024
BATCH = 16
SEQ = 2048
DEPTH = 2
DEC_BATCH = 128
DEC_SEQ = 4
PAST_LEN = 16384
PAGE_SIZE = 128

GLA_HEADS = 4
GLA_DK = 64
GLA_DV = 128
GLA_RANK = 16
GLA_TAU = 16.0
GLA_CHUNK = 64
GMLP_GROUPS = 4
GMLP_WIDTH = 512
GMLP_CHUNK = 128
CONV_WIDTH = 512
CONV_K = 31
SWA_HEADS = 8
SWA_KV_HEADS = 2
SWA_GROUP = SWA_HEADS // SWA_KV_HEADS
SWA_HEAD_DIM = 64
WINDOW = 128
ROPE_DIM = SWA_HEAD_DIM // 4
ROPE_THETA = 500000.0
N_BRANCH = 4
BRANCH_WIDTH = 512
FFN_HIDDEN = ((8 * D_MODEL // 3 + 255) // 256) * 256
IN_SPLITS = (GLA_HEADS * GLA_DK, GLA_HEADS * GLA_DK, GLA_HEADS * GLA_DV, GLA_HEADS * GLA_DV, GLA_RANK,
             GMLP_WIDTH, GMLP_WIDTH, CONV_WIDTH, CONV_WIDTH,
             SWA_HEADS * SWA_HEAD_DIM, SWA_KV_HEADS * SWA_HEAD_DIM, SWA_KV_HEADS * SWA_HEAD_DIM)
IN_COLS = sum(IN_SPLITS)
NORM_EPS = 1e-6
LN_EPS = 1e-5

kernel_name = "hybrid_gla_gmlp_conv_swa_step"


def rmsnorm(x, g):
    xf = x.astype(jnp.float32)
    y = xf * lax.rsqrt(jnp.mean(xf * xf, axis=-1, keepdims=True) + NORM_EPS)
    return (y * g.astype(jnp.float32)).astype(x.dtype)


def layernorm(x, g, b):
    xf = x.astype(jnp.float32)
    mu = jnp.mean(xf, axis=-1, keepdims=True)
    xc = xf - mu
    var = jnp.mean(xc * xc, axis=-1, keepdims=True)
    return (xc * lax.rsqrt(var + LN_EPS) * g.astype(jnp.float32) + b.astype(jnp.float32)).astype(x.dtype)


def split_cols(z):
    idx = [int(i) for i in np.cumsum(IN_SPLITS)[:-1]]
    return jnp.split(z, idx, axis=-1)


def gla_recurrence(q, k, v, log_a, s0):
    B, L, H, _ = q.shape
    c = math.gcd(L, GLA_CHUNK)
    n = L // c

    def blocks(t):
        return t.astype(jnp.float32).reshape(B, n, c, H, t.shape[-1]).transpose(1, 0, 3, 2, 4)

    causal = jnp.tril(jnp.ones((c, c), dtype=bool))

    def step(s, inp):
        qc, kc, vc, ac = inp
        bcum = lax.cumsum(ac, axis=2)
        q_in = qc * jnp.exp(bcum)
        att = jnp.einsum('bhid,bhjd->bhij', q_in, kc * jnp.exp(-bcum))
        att = jnp.where(causal, att, 0.0)
        o = jnp.einsum('bhij,bhjv->bhiv', att, vc) + jnp.einsum('bhid,bhdv->bhiv', q_in, s)
        b_last = bcum[:, :, -1:, :]
        s_new = jnp.exp(b_last[:, :, 0, :])[..., None] * s + jnp.einsum(
            'bhjd,bhjv->bhdv', kc * jnp.exp(b_last - bcum), vc)
        return s_new, o

    s_fin, ob = lax.scan(step, s0.astype(jnp.float32), (blocks(q), blocks(k), blocks(v), blocks(log_a)))
    o = ob.transpose(1, 0, 3, 2, 4).reshape(B, L, H, v.shape[-1])
    return o, s_fin


def gmlp_spatial(v, w_s, b_s):
    B, L, _ = v.shape
    n = -(-L // GMLP_CHUNK)
    pad = n * GMLP_CHUNK - L
    vp = jnp.pad(v, ((0, 0), (0, pad), (0, 0))).reshape(
        B, n, GMLP_CHUNK, GMLP_GROUPS, GMLP_WIDTH // GMLP_GROUPS)
    w = jnp.where(jnp.tril(jnp.ones((GMLP_CHUNK, GMLP_CHUNK), dtype=bool)), w_s, 0.0)
    s = jnp.einsum('gts,bnsgc->bntgc', w, vp) + b_s.T[None, None, :, :, None]
    return s.reshape(B, n * GMLP_CHUNK, GMLP_WIDTH)[:, :L]


def causal_depthwise_conv(h_ext, w, b):
    C = h_ext.shape[-1]
    y = lax.conv_general_dilated(h_ext, w[:, None, :].astype(h_ext.dtype), window_strides=(1,),
                                 padding='VALID', dimension_numbers=('NWC', 'WIO', 'NWC'),
                                 feature_group_count=C)
    return y + b


def rope_partial(x, pos):
    half = ROPE_DIM // 2
    inv = jnp.exp(-math.log(ROPE_THETA) * jnp.arange(half, dtype=jnp.float32) * (2.0 / ROPE_DIM))
    ang = pos.astype(jnp.float32)[:, None] * inv[None, :]
    cos = jnp.cos(ang)[:, None, :]
    sin = jnp.sin(ang)[:, None, :]
    xf = x.astype(jnp.float32)
    x1 = xf[..., :half]
    x2 = xf[..., half:ROPE_DIM]
    out = jnp.concatenate([x1 * cos - x2 * sin, x2 * cos + x1 * sin, xf[..., ROPE_DIM:]], axis=-1)
    return out.astype(x.dtype)


def sink_attention(q, k, v, mask, sinks):
    s = jnp.einsum('...qhgd,...shd->...hgqs', q, k, preferred_element_type=jnp.float32) * (SWA_HEAD_DIM ** -0.5)
    s = jnp.where(mask[..., None, None, :, :], s, -jnp.inf)
    sink = sinks.astype(jnp.float32)[:, :, None, None]
    m = jnp.maximum(jnp.max(s, axis=-1, keepdims=True), sink)
    p = jnp.exp(s - m)
    denom = jnp.sum(p, axis=-1, keepdims=True) + jnp.exp(sink - m)
    return jnp.einsum('...hgqs,...shd->...qhgd', (p / denom).astype(v.dtype), v)


def swa_prompt(qg, k, v, sinks):
    B, L = qg.shape[:2]
    n = L // WINDOW
    qb = qg.reshape(B, n, WINDOW, SWA_KV_HEADS, SWA_GROUP, SWA_HEAD_DIM)
    kb = k.reshape(B, n, WINDOW, SWA_KV_HEADS, SWA_HEAD_DIM)
    vb = v.reshape(B, n, WINDOW, SWA_KV_HEADS, SWA_HEAD_DIM)
    shift = ((0, 0), (1, 0), (0, 0), (0, 0), (0, 0))
    kk = jnp.concatenate([jnp.pad(kb, shift)[:, :-1], kb], axis=2)
    vv = jnp.concatenate([jnp.pad(vb, shift)[:, :-1], vb], axis=2)
    blk = jnp.arange(n)[:, None]
    qpos = blk * WINDOW + jnp.arange(WINDOW)[None, :]
    kpos = (blk - 1) * WINDOW + jnp.arange(2 * WINDOW)[None, :]
    d = qpos[:, :, None] - kpos[:, None, :]
    mask = (d >= 0) & (d < WINDOW) & (kpos[:, None, :] >= 0)
    o = sink_attention(qb, kk, vv, mask, sinks)
    return o.reshape(B, L, SWA_HEADS * SWA_HEAD_DIM), k[:, -WINDOW:], v[:, -WINDOW:]


def swa_sample(qg, k, v, k_hist, v_hist, sinks):
    B, Lq = qg.shape[:2]
    kk = jnp.concatenate([k_hist.astype(k.dtype), k], axis=1)
    vv = jnp.concatenate([v_hist.astype(v.dtype), v], axis=1)
    qpos = PAST_LEN + jnp.arange(Lq)
    kpos = PAST_LEN - WINDOW + jnp.arange(WINDOW + Lq)
    d = qpos[:, None] - kpos[None, :]
    mask = (d >= 0) & (d < WINDOW)
    o = sink_attention(qg, kk, vv, mask, sinks)
    return o.reshape(B, Lq, SWA_HEADS * SWA_HEAD_DIM), kk[:, -WINDOW:], vv[:, -WINDOW:]


def trunk_layer(x, pos0, gla_s0, conv_hist, k_hist, v_hist, prompt, p):
    (norm1_g, w_in, b_in, w_alpha2, b_alpha, gla_norm_g, gmlp_ln_g, gmlp_ln_b, w_spatial, b_spatial,
     conv_w, conv_b, conv_ln_g, conv_ln_b, q_norm_g, k_norm_g, sinks, w_gate, b_gate, w_branch,
     w_out, norm2_g, w_gate_up, w_down) = p
    B, L, _ = x.shape
    xn = rmsnorm(x, norm1_g)
    z = xn @ w_in + b_in
    a_q, a_k, a_v, a_r, a_lr, b_u, b_v, c_a, c_g, d_q, d_k, d_v = split_cols(z)

    q = a_q.reshape(B, L, GLA_HEADS, GLA_DK) * (GLA_DK ** -0.5)
    k = a_k.reshape(B, L, GLA_HEADS, GLA_DK)
    v = a_v.reshape(B, L, GLA_HEADS, GLA_DV)
    log_a = jax.nn.log_sigmoid((a_lr @ w_alpha2 + b_alpha).astype(jnp.float32)).reshape(
        B, L, GLA_HEADS, GLA_DK) / GLA_TAU
    o_gla, gla_new = gla_recurrence(q, k, v, log_a, gla_s0)
    o_a = rmsnorm(o_gla.astype(x.dtype), gla_norm_g).reshape(B, L, GLA_HEADS * GLA_DV) * jax.nn.silu(a_r)

    u = jax.nn.gelu(b_u, approximate=False)
    gv = layernorm(jax.nn.gelu(b_v, approximate=False), gmlp_ln_g, gmlp_ln_b)
    o_b = u * gmlp_spatial(gv, w_spatial, b_spatial)

    h = c_a * jax.nn.sigmoid(c_g)
    h_ext = jnp.concatenate([conv_hist.astype(h.dtype), h], axis=1)
    o_c = jax.nn.silu(layernorm(causal_depthwise_conv(h_ext, conv_w, conv_b), conv_ln_g, conv_ln_b))
    conv_new = h_ext[:, -(CONV_K - 1):]

    pos = pos0 + jnp.arange(L)
    qd = rope_partial(rmsnorm(d_q.reshape(B, L, SWA_HEADS, SWA_HEAD_DIM), q_norm_g), pos)
    kd = rope_partial(rmsnorm(d_k.reshape(B, L, SWA_KV_HEADS, SWA_HEAD_DIM), k_norm_g), pos)
    vd = d_v.reshape(B, L, SWA_KV_HEADS, SWA_HEAD_DIM)
    qg = qd.reshape(B, L, SWA_KV_HEADS, SWA_GROUP, SWA_HEAD_DIM)
    sk = sinks.reshape(SWA_KV_HEADS, SWA_GROUP)
    if prompt:
        o_d, k_new, v_new = swa_prompt(qg, kd, vd, sk)
    else:
        o_d, k_new, v_new = swa_sample(qg, kd, vd, k_hist, v_hist, sk)

    gates = jax.nn.sigmoid(xn @ w_gate + b_gate).reshape(B, L, N_BRANCH, D_MODEL)
    merged = 0.0
    for i, o_br in enumerate((o_a, o_b, o_c, o_d)):
        merged = merged + gates[:, :, i] * (o_br @ w_branch[i])
    x = x + merged @ w_out

    g, up = jnp.split(rmsnorm(x, norm2_g) @ w_gate_up, 2, axis=-1)
    x = x + (jax.nn.silu(g) * up) @ w_down
    return x, gla_new, conv_new, k_new, v_new, gv


def setup_inputs(seed: int = 0) -> dict:
    key = jax.random.key(seed)
    counter = [0]

    def nrm(shape, scale):
        counter[0] += 1
        return jax.random.normal(jax.random.fold_in(key, counter[0]), shape, jnp.float32) * scale

    def gain(shape):
        return 1.0 + nrm(shape, 0.05)

    Dp = DEPTH
    return {
        "x_prompt": nrm((BATCH, SEQ, D_MODEL), 1.0),
        "x_sample": nrm((DEC_BATCH, DEC_SEQ, D_MODEL), 1.0),
        "state_gla": nrm((Dp, DEC_BATCH, GLA_HEADS, GLA_DK, GLA_DV), 1.0),
        "state_conv": nrm((Dp, DEC_BATCH, CONV_K - 1, CONV_WIDTH), 0.5),
        "cache_swa_k": nrm((Dp, DEC_BATCH, WINDOW, SWA_KV_HEADS, SWA_HEAD_DIM), 1.0),
        "cache_swa_v": nrm((Dp, DEC_BATCH, WINDOW, SWA_KV_HEADS, SWA_HEAD_DIM), 1.0),
        "norm1_g": gain((Dp, D_MODEL)),
        "w_in": nrm((Dp, D_MODEL, IN_COLS), D_MODEL ** -0.5),
        "b_in": nrm((Dp, IN_COLS), 0.02),
        "w_alpha2": nrm((Dp, GLA_RANK, GLA_HEADS * GLA_DK), GLA_RANK ** -0.5),
        "b_alpha": nrm((Dp, GLA_HEADS * GLA_DK), 0.1),
        "gla_norm_g": gain((Dp, GLA_DV)),
        "gmlp_ln_g": gain((Dp, GMLP_WIDTH)),
        "gmlp_ln_b": nrm((Dp, GMLP_WIDTH), 0.02),
        "w_spatial": nrm((Dp, GMLP_GROUPS, GMLP_CHUNK, GMLP_CHUNK), 0.5 * GMLP_CHUNK ** -0.5),
        "b_spatial": gain((Dp, GMLP_GROUPS, GMLP_CHUNK)),
        "conv_w": nrm((Dp, CONV_K, CONV_WIDTH), CONV_K ** -0.5),
        "conv_b": nrm((Dp, CONV_WIDTH), 0.02),
        "conv_ln_g": gain((Dp, CONV_WIDTH)),
        "conv_ln_b": nrm((Dp, CONV_WIDTH), 0.02),
        "q_norm_g": gain((Dp, SWA_HEAD_DIM)),
        "k_norm_g": gain((Dp, SWA_HEAD_DIM)),
        "sinks": nrm((Dp, SWA_HEADS), 0.5),
        "w_gate": nrm((Dp, D_MODEL, N_BRANCH * D_MODEL), D_MODEL ** -0.5),
        "b_gate": nrm((Dp, N_BRANCH * D_MODEL), 0.02),
        "w_branch": nrm((Dp, N_BRANCH, BRANCH_WIDTH, D_MODEL), BRANCH_WIDTH ** -0.5),
        "w_out": nrm((Dp, D_MODEL, D_MODEL), 0.5 * D_MODEL ** -0.5),
        "norm2_g": gain((Dp, D_MODEL)),
        "w_gate_up": nrm((Dp, D_MODEL, 2 * FFN_HIDDEN), D_MODEL ** -0.5),
        "w_down": nrm((Dp, FFN_HIDDEN, D_MODEL), 0.5 * FFN_HIDDEN ** -0.5),
    }


def reference(x_prompt, x_sample, state_gla, state_conv, cache_swa_k, cache_swa_v,
              norm1_g, w_in, b_in, w_alpha2, b_alpha, gla_norm_g, gmlp_ln_g, gmlp_ln_b,
              w_spatial, b_spatial, conv_w, conv_b, conv_ln_g, conv_ln_b, q_norm_g, k_norm_g,
              sinks, w_gate, b_gate, w_branch, w_out, norm2_g, w_gate_up, w_down):
    yp, ys = x_prompt, x_sample
    bp = x_prompt.shape[0]
    gla_p, gla_s, conv_p, conv_s, kp, vp, ks, vs, gm_s = [], [], [], [], [], [], [], [], []
    for l in range(DEPTH):
        prm = (norm1_g[l], w_in[l], b_in[l], w_alpha2[l], b_alpha[l], gla_norm_g[l], gmlp_ln_g[l],
               gmlp_ln_b[l], w_spatial[l], b_spatial[l], conv_w[l], conv_b[l], conv_ln_g[l],
               conv_ln_b[l], q_norm_g[l], k_norm_g[l], sinks[l], w_gate[l], b_gate[l], w_branch[l],
               w_out[l], norm2_g[l], w_gate_up[l], w_down[l])
        gla0 = jnp.zeros((bp, GLA_HEADS, GLA_DK, GLA_DV), jnp.float32)
        conv0 = jnp.zeros((bp, CONV_K - 1, CONV_WIDTH), yp.dtype)
        yp, g1, c1, k1, v1, _ = trunk_layer(yp, 0, gla0, conv0, None, None, True, prm)
        ys, g2, c2, k2, v2, gv2 = trunk_layer(ys, PAST_LEN, state_gla[l], state_conv[l],
                                              cache_swa_k[l], cache_swa_v[l], False, prm)
        gla_p.append(g1); conv_p.append(c1); kp.append(k1); vp.append(v1)
        gla_s.append(g2); conv_s.append(c2); ks.append(k2); vs.append(v2); gm_s.append(gv2)
    new_gla_prompt = jnp.stack(gla_p, 0)
    new_gla_sample = jnp.stack(gla_s, 0)
    new_conv_prompt = jnp.stack(conv_p, 0)
    new_conv_sample = jnp.stack(conv_s, 0)
    new_k_prompt = jnp.stack(kp, 0)
    new_v_prompt = jnp.stack(vp, 0)
    new_k_sample = jnp.stack(ks, 0)
    new_v_sample = jnp.stack(vs, 0)
    new_gmlp_v_sample = jnp.stack(gm_s, 0)
    return (yp, ys, new_gla_prompt, new_gla_sample, new_conv_prompt, new_conv_sample,
            new_k_prompt, new_v_prompt, new_k_sample, new_v_sample, new_gmlp_v_sample)
```

```python
import functools
import math

import jax
import jax.numpy as jnp
import numpy as np
from jax import lax
from jax.experimental import pallas as pl
from jax.experimental.pallas import tpu as pltpu

F32 = jnp.float32
BF16 = jnp.bfloat16

D_MODEL = 1024
DEPTH = 2
PAST_LEN = 16384
GLA_HEADS = 4
GLA_DK = 64
GLA_DV = 128
GLA_RANK = 16
GLA_TAU = 16.0
GLA_CHUNK = 64
GMLP_GROUPS = 4
GMLP_WIDTH = 512
GMLP_CHUNK = 128
CONV_WIDTH = 512
CONV_K = 31
SWA_HEADS = 8
SWA_KV_HEADS = 2
SWA_GROUP = SWA_HEADS // SWA_KV_HEADS
SWA_HEAD_DIM = 64
WINDOW = 128
ROPE_DIM = SWA_HEAD_DIM // 4
ROPE_THETA = 500000.0
N_BRANCH = 4
BRANCH_WIDTH = 512
FFN_HIDDEN = 2816
NORM_EPS = 1e-6
LN_EPS = 1e-5

LANES = 128
SUBLANES = 8
VMEM_LIMIT_BYTES = 56 * 1024 * 1024

QK = GLA_HEADS * GLA_DK
QV = GLA_HEADS * GLA_DV
SWA_Q = SWA_HEADS * SWA_HEAD_DIM
SWA_KV = SWA_KV_HEADS * SWA_HEAD_DIM

C_AQ = 0
C_AK = C_AQ + QK
C_AV = C_AK + QK
C_AR = C_AV + QV
C_BU = C_AR + QV
C_BV = C_BU + GMLP_WIDTH
C_CA = C_BV + GMLP_WIDTH
C_CG = C_CA + CONV_WIDTH
C_DQ = C_CG + CONV_WIDTH
C_DK = C_DQ + SWA_Q
C_DV = C_DK + SWA_KV
C_LR = C_DV + SWA_KV
IN_COLS_PACKED = C_LR + LANES

PROMPT_TILE = 256
FFN_TILE = 512
FFN_CHUNK = 256
CONV_PAD = 32
SAMPLE_ROWS = 8


def _mm(a, b):
    return jnp.dot(a, b, preferred_element_type=F32)


def _mm_nt(a, b):
    return lax.dot_general(a, b, (((1,), (1,)), ((), ())), preferred_element_type=F32)


def _mm_tn(a, b):
    return lax.dot_general(a, b, (((0,), (0,)), ((), ())), preferred_element_type=F32)


def _hi_lo(a):
    hi = a.astype(BF16)
    lo = (a - hi.astype(F32)).astype(BF16)
    return hi, lo


def _mm_exact_lhs(m_bf16, a_f32):
    hi, lo = _hi_lo(a_f32)
    return _mm(m_bf16, hi) + _mm(m_bf16, lo)


def _rms_rows(x, g):
    return x * lax.rsqrt(jnp.mean(x * x, axis=-1, keepdims=True) + NORM_EPS) * g


def _ln_rows(x, g, b):
    mu = jnp.mean(x, axis=-1, keepdims=True)
    xc = x - mu
    var = jnp.mean(xc * xc, axis=-1, keepdims=True)
    return xc * lax.rsqrt(var + LN_EPS) * g + b


def _gelu(x):
    return 0.5 * x * (1.0 + lax.erf(x * (1.0 / math.sqrt(2.0))))


def _silu(x):
    return x * jax.nn.sigmoid(x)


def _log_sigmoid(x):
    return jnp.minimum(x, 0.0) - jnp.log1p(jnp.exp(-jnp.abs(x)))


def _iota(shape, dim):
    return lax.broadcasted_iota(jnp.int32, shape, dim)


def _group_ones(n, group, dtype=BF16):
    return (_iota((n, n), 0) // group == _iota((n, n), 1) // group).astype(dtype)


def _group_mean_sq(x, group):
    n = x.shape[-1]
    return _mm_exact_lhs_rhs(x * x, _group_ones(n, group)) * (1.0 / group)


def _mm_exact_lhs_rhs(a_f32, m_bf16):
    hi, lo = _hi_lo(a_f32)
    return _mm(hi, m_bf16) + _mm(lo, m_bf16)


def _rope_slab(x, c, a, b):
    return x * c + pltpu.roll(x, LANES - ROPE_DIM // 2, 1) * a + pltpu.roll(x, ROPE_DIM // 2, 1) * b


def _rope(x, c, a, b):
    n = x.shape[-1] // LANES
    if n == 1:
        return _rope_slab(x, c, a, b)
    return jnp.concatenate(
        [_rope_slab(x[:, i * LANES:(i + 1) * LANES], c, a, b) for i in range(n)], axis=1)


def _head_rms_gla(o, g):
    outs = []
    for h in range(GLA_HEADS):
        oh = o[:, h * GLA_DV:(h + 1) * GLA_DV]
        outs.append(_rms_rows(oh, g))
    return jnp.concatenate(outs, axis=1)


def _swa_q_for_group(q, g, i):
    hq = g * SWA_GROUP + i
    slab = q[:, (hq // 2) * LANES:(hq // 2 + 1) * LANES]
    if hq % 2 != g:
        slab = pltpu.roll(slab, SWA_HEAD_DIM, 1)
    lane_head = _iota(slab.shape, 1) // SWA_HEAD_DIM
    return jnp.where(lane_head == g, slab, 0.0)


def _swa_assemble(outs, rows):
    lane_half = _iota((rows, LANES), 1) // SWA_HEAD_DIM
    slabs = []
    for s in range(SWA_HEADS // 2):
        pair = []
        for hq in (2 * s, 2 * s + 1):
            g, i = hq // SWA_GROUP, hq % SWA_GROUP
            o = outs[g][i]
            if hq % 2 != g:
                o = pltpu.roll(o, SWA_HEAD_DIM, 1)
            pair.append(o)
        slabs.append(jnp.where(lane_half == 0, pair[0], pair[1]))
    return jnp.concatenate(slabs, axis=1)


def _prompt_mixer_kernel(
        sinks_ref,
        x_ref, rc_ref, ra_ref, rb_ref,
        n1g_ref, win_ref, bin_ref, wal_ref, bal_ref, glag_ref,
        lng_ref, lnb_ref, wsp_ref, bspt_ref,
        cw_ref, cb_ref, clg_ref, clb_ref,
        qg_ref, kg_ref,
        wg_ref, bg_ref, wbr_ref, wout_ref,
        y_ref, gla_ref, conv_ref, ko_ref, vo_ref,
        sbd_ref, hbuf_ref, kfull_ref, vfull_ref, obr_ref):
    T = PROMPT_TILE
    t = pl.program_id(1)
    last = pl.num_programs(1) - 1

    @pl.when(t == 0)
    def _():
        sbd_ref[...] = jnp.zeros_like(sbd_ref)
        hbuf_ref[0:CONV_PAD, :] = jnp.zeros((CONV_PAD, CONV_WIDTH), F32)
        kfull_ref[0:WINDOW, :] = jnp.zeros((WINDOW, SWA_KV), F32)
        vfull_ref[0:WINDOW, :] = jnp.zeros((WINDOW, SWA_KV), F32)

    x = x_ref[0]
    xb = _rms_rows(x, n1g_ref[...]).astype(BF16)

    def proj(c0, width):
        return _mm(xb, win_ref[:, c0:c0 + width]) + bin_ref[:, c0:c0 + width]

    q = proj(C_AQ, QK) * (GLA_DK ** -0.5)
    k = proj(C_AK, QK)
    v = proj(C_AV, QV)
    lr = proj(C_LR, LANES)
    la = _log_sigmoid(_mm(lr.astype(BF16), wal_ref[...]) + bal_ref[...]) * (1.0 / GLA_TAU)

    ri = _iota((T, T), 0)
    ci = _iota((T, T), 1)
    same_chunk = (ri // GLA_CHUNK) == (ci // GLA_CHUNK)
    m_blk = same_chunk.astype(BF16)
    m_tril = jnp.logical_and(same_chunk, ci <= ri).astype(BF16)
    la_hi, la_lo = _hi_lo(la)
    bcum = _mm(m_tril, la_hi) + _mm(m_tril, la_lo)
    btot = _mm(m_blk, la_hi) + _mm(m_blk, la_lo)
    q_in = (q * jnp.exp(bcum)).astype(BF16)
    k_out = (k * jnp.exp(-bcum)).astype(BF16)
    k_dec = (k * jnp.exp(btot - bcum)).astype(BF16)
    v_b = v.astype(BF16)

    lane_head = _iota((GLA_CHUNK, QK), 1) // GLA_DK
    causal = (_iota((QK, GLA_CHUNK), 0) % GLA_CHUNK) >= _iota((QK, GLA_CHUNK), 1)
    bd_mask = (_iota((QK, QV), 0) // GLA_DK) == (_iota((QK, QV), 1) // GLA_DV)
    ones_tn = jnp.ones((GLA_CHUNK, GLA_DV), BF16)
    o_chunks = []
    for c in range(T // GLA_CHUNK):
        rs = slice(c * GLA_CHUNK, (c + 1) * GLA_CHUNK)
        qc, kc, kd, vc = q_in[rs], k_out[rs], k_dec[rs], v_b[rs]
        q_stack = jnp.concatenate(
            [jnp.where(lane_head == h, qc, jnp.zeros_like(qc)) for h in range(GLA_HEADS)], axis=0)
        att = jnp.where(causal, _mm_nt(q_stack, kc), 0.0)
        pv = _mm(att.astype(BF16), vc)
        o_intra = jnp.concatenate(
            [pv[h * GLA_CHUNK:(h + 1) * GLA_CHUNK, h * GLA_DV:(h + 1) * GLA_DV]
             for h in range(GLA_HEADS)], axis=1)
        s_bd = sbd_ref[...]
        o_chunks.append(o_intra + _mm(qc, s_bd.astype(BF16)))
        dsum = _mm_tn(la_hi[rs], ones_tn) + _mm_tn(la_lo[rs], ones_tn)
        dcol = jnp.exp(dsum)
        dcol4 = jnp.concatenate([dcol] * GLA_HEADS, axis=1)
        upd = _mm_tn(kd, vc)
        sbd_ref[...] = dcol4 * s_bd + jnp.where(bd_mask, upd, 0.0)
    o_gla = jnp.concatenate(o_chunks, axis=0)
    r = proj(C_AR, QV)
    obr_ref[0] = (_head_rms_gla(o_gla, glag_ref[...]) * _silu(r)).astype(BF16)

    u = _gelu(proj(C_BU, GMLP_WIDTH))
    gv = _ln_rows(_gelu(proj(C_BV, GMLP_WIDTH)), lng_ref[...], lnb_ref[...])
    gv_b = gv.astype(BF16)
    gw = GMLP_WIDTH // GMLP_GROUPS
    tril = _iota((GMLP_CHUNK, GMLP_CHUNK), 1) <= _iota((GMLP_CHUNK, GMLP_CHUNK), 0)
    bspt = bspt_ref[...]
    s_rows = []
    for n in range(T // GMLP_CHUNK):
        rs = slice(n * GMLP_CHUNK, (n + 1) * GMLP_CHUNK)
        cols = []
        for g in range(GMLP_GROUPS):
            w = jnp.where(tril, wsp_ref[g], 0.0).astype(BF16)
            sg = _mm(w, gv_b[rs, g * gw:(g + 1) * gw])
            cols.append(sg + jnp.broadcast_to(bspt[:, g:g + 1], (GMLP_CHUNK, gw)))
        s_rows.append(jnp.concatenate(cols, axis=1))
    obr_ref[1] = (u * jnp.concatenate(s_rows, axis=0)).astype(BF16)

    h = proj(C_CA, CONV_WIDTH) * jax.nn.sigmoid(proj(C_CG, CONV_WIDTH))
    hbuf_ref[CONV_PAD:CONV_PAD + T, :] = h
    acc = jnp.broadcast_to(cb_ref[...], (T, CONV_WIDTH))
    for kk in range(CONV_K):
        off = CONV_PAD - (CONV_K - 1) + kk
        acc = acc + cw_ref[kk:kk + 1, :] * hbuf_ref[off:off + T, :]
    obr_ref[2] = _silu(_ln_rows(acc, clg_ref[...], clb_ref[...])).astype(BF16)

    @pl.when(t == last)
    def _():
        conv_ref[0] = hbuf_ref[T + CONV_PAD - (CONV_K - 1):T + CONV_PAD, :]

    hbuf_ref[0:CONV_PAD, :] = hbuf_ref[T:T + CONV_PAD, :]

    zq = proj(C_DQ, SWA_Q)
    zk = proj(C_DK, SWA_KV)
    zv = proj(C_DV, SWA_KV)
    rc, ra, rb = rc_ref[...], ra_ref[...], rb_ref[...]
    qd = _rope(zq * lax.rsqrt(_group_mean_sq(zq, SWA_HEAD_DIM) + NORM_EPS) * qg_ref[...], rc, ra, rb)
    kd_new = _rope(zk * lax.rsqrt(_group_mean_sq(zk, SWA_HEAD_DIM) + NORM_EPS) * kg_ref[...], rc, ra, rb)
    kfull_ref[WINDOW:WINDOW + T, :] = kd_new
    vfull_ref[WINDOW:WINDOW + T, :] = zv

    rows = SWA_GROUP * WINDOW
    qi = _iota((rows, 2 * WINDOW), 0) % WINDOW
    kj = _iota((rows, 2 * WINDOW), 1)
    own_ok = jnp.logical_and(kj >= WINDOW, kj - WINDOW <= qi)
    prev_ok = jnp.logical_and(kj < WINDOW, kj > qi)
    row_head = _iota((rows, 1), 0) // WINDOW
    scale = SWA_HEAD_DIM ** -0.5
    o_blocks = []
    for qb in range(T // WINDOW):
        rs = slice(qb * WINDOW, (qb + 1) * WINDOW)
        kblk = kfull_ref[qb * WINDOW:(qb + 2) * WINDOW, :].astype(BF16)
        vblk = vfull_ref[qb * WINDOW:(qb + 2) * WINDOW, :].astype(BF16)
        if qb == 0:
            valid = jnp.logical_or(own_ok, jnp.logical_and(prev_ok, t > 0))
        else:
            valid = jnp.logical_or(own_ok, prev_ok)
        outs = []
        for g in range(SWA_KV_HEADS):
            q_stack = jnp.concatenate(
                [_swa_q_for_group(qd[rs], g, i) for i in range(SWA_GROUP)], axis=0).astype(BF16)
            s = jnp.where(valid, _mm_nt(q_stack, kblk) * scale, -jnp.inf)
            sink = jnp.zeros((rows, 1), F32)
            for i in range(SWA_GROUP):
                sink = jnp.where(row_head == i, sinks_ref[g * SWA_GROUP + i], sink)
            m = jnp.maximum(jnp.max(s, axis=-1, keepdims=True), sink)
            p = jnp.exp(s - m)
            den = jnp.sum(p, axis=-1, keepdims=True) + jnp.exp(sink - m)
            o = _mm(p.astype(BF16), vblk) / den
            outs.append([o[i * WINDOW:(i + 1) * WINDOW] for i in range(SWA_GROUP)])
        o_blocks.append(_swa_assemble(outs, WINDOW))
    obr_ref[3] = jnp.concatenate(o_blocks, axis=0).astype(BF16)

    @pl.when(t == last)
    def _():
        ko_ref[0] = kfull_ref[T:T + WINDOW, :]
        vo_ref[0] = vfull_ref[T:T + WINDOW, :]
        sb = sbd_ref[...]
        gla_ref[0] = jnp.concatenate(
            [sb[hh * GLA_DK:(hh + 1) * GLA_DK, hh * GLA_DV:(hh + 1) * GLA_DV]
             for hh in range(GLA_HEADS)], axis=0)

    kfull_ref[0:WINDOW, :] = kfull_ref[T:T + WINDOW, :]
    vfull_ref[0:WINDOW, :] = vfull_ref[T:T + WINDOW, :]

    merged = jnp.zeros((T, D_MODEL), F32)
    for i in range(N_BRANCH):
        gate = jax.nn.sigmoid(
            _mm(xb, wg_ref[:, i * D_MODEL:(i + 1) * D_MODEL]) + bg_ref[:, i * D_MODEL:(i + 1) * D_MODEL])
        merged = merged + gate * _mm(obr_ref[i], wbr_ref[i])
    y_ref[0] = x + _mm(merged.astype(BF16), wout_ref[...])


def _const_spec(shape):
    nd = len(shape)
    return pl.BlockSpec(shape, lambda *_: (0,) * nd, pipeline_mode=pl.Buffered(1))


def _full_spec(shape):
    nd = len(shape)
    return pl.BlockSpec(shape, lambda *_: (0,) * nd)


def _prompt_mixer(x, p, tables):
    B, L, _ = x.shape
    T = PROMPT_TILE
    weights = (p["n1g"], p["win"], p["bin"], p["wal"], p["bal"], p["glag"],
               p["lng"], p["lnb"], p["wsp"], p["bspt"],
               p["cw"], p["cb"], p["clg"], p["clb"],
               p["qg"], p["kg"],
               p["wg"], p["bg"], p["wbr"], p["wout"])
    tab_spec = pl.BlockSpec((T, LANES), lambda b, t, *_: (t, 0))
    grid_spec = pltpu.PrefetchScalarGridSpec(
        num_scalar_prefetch=1,
        grid=(B, L // T),
        in_specs=[pl.BlockSpec((1, T, D_MODEL), lambda b, t, *_: (b, t, 0)),
                  tab_spec, tab_spec, tab_spec]
                 + [_const_spec(w.shape) for w in weights],
        out_specs=[pl.BlockSpec((1, T, D_MODEL), lambda b, t, *_: (b, t, 0)),
                   pl.BlockSpec((1, QK, GLA_DV), lambda b, t, *_: (b, 0, 0)),
                   pl.BlockSpec((1, CONV_K - 1, CONV_WIDTH), lambda b, t, *_: (b, 0, 0)),
                   pl.BlockSpec((1, WINDOW, SWA_KV), lambda b, t, *_: (b, 0, 0)),
                   pl.BlockSpec((1, WINDOW, SWA_KV), lambda b, t, *_: (b, 0, 0))],
        scratch_shapes=[pltpu.VMEM((QK, QV), F32),
                        pltpu.VMEM((T + CONV_PAD, CONV_WIDTH), F32),
                        pltpu.VMEM((T + WINDOW, SWA_KV), F32),
                        pltpu.VMEM((T + WINDOW, SWA_KV), F32),
                        pltpu.VMEM((N_BRANCH, T, BRANCH_WIDTH), BF16)])
    return pl.pallas_call(
        _prompt_mixer_kernel,
        grid_spec=grid_spec,
        out_shape=[jax.ShapeDtypeStruct((B, L, D_MODEL), F32),
                   jax.ShapeDtypeStruct((B, QK, GLA_DV), F32),
                   jax.ShapeDtypeStruct((B, CONV_K - 1, CONV_WIDTH), F32),
                   jax.ShapeDtypeStruct((B, WINDOW, SWA_KV), F32),
                   jax.ShapeDtypeStruct((B, WINDOW, SWA_KV), F32)],
        compiler_params=pltpu.CompilerParams(
            dimension_semantics=("arbitrary", "arbitrary"),
            vmem_limit_bytes=VMEM_LIMIT_BYTES),
        name="prompt_mixer",
    )(p["sinks"], x, *tables, *weights)


def _ffn_kernel(x_ref, g_ref, wgu_ref, wd_ref, y_ref, h_ref):
    x = x_ref[...]
    xb = _rms_rows(x, g_ref[...]).astype(BF16)
    for c in range(FFN_HIDDEN // FFN_CHUNK):
        c0 = c * FFN_CHUNK
        gate = _mm(xb, wgu_ref[:, c0:c0 + FFN_CHUNK])
        up = _mm(xb, wgu_ref[:, FFN_HIDDEN + c0:FFN_HIDDEN + c0 + FFN_CHUNK])
        h_ref[:, c0:c0 + FFN_CHUNK] = (_silu(gate) * up).astype(BF16)
    y_ref[...] = x + _mm(h_ref[...], wd_ref[...])


def _ffn(x2d, p):
    n = x2d.shape[0]
    tile = min(FFN_TILE, n)
    return pl.pallas_call(
        _ffn_kernel,
        grid=(n // tile,),
        in_specs=[pl.BlockSpec((tile, D_MODEL), lambda i: (i, 0)),
                  _const_spec(p["n2g"].shape), _const_spec(p["wgu"].shape), _const_spec(p["wd"].shape)],
        out_specs=pl.BlockSpec((tile, D_MODEL), lambda i: (i, 0)),
        out_shape=jax.ShapeDtypeStruct((n, D_MODEL), F32),
        scratch_shapes=[pltpu.VMEM((tile, FFN_HIDDEN), BF16)],
        compiler_params=pltpu.CompilerParams(
            dimension_semantics=("arbitrary",), vmem_limit_bytes=VMEM_LIMIT_BYTES),
        name="swiglu",
    )(x2d, p["n2g"], p["wgu"], p["wd"])


def _sample_proj_kernel(x_ref, n1g_ref, win_ref, bin_ref, wal_ref, bal_ref, z_ref, ya_ref):
    xb = _rms_rows(x_ref[...], n1g_ref[...]).astype(BF16)
    z = _mm(xb, win_ref[...]) + bin_ref[...]
    z_ref[...] = z
    ya_ref[...] = _mm(z[:, C_LR:C_LR + LANES].astype(BF16), wal_ref[...]) + bal_ref[...]


def _sample_proj(x2d, p):
    n = x2d.shape[0]
    args = (x2d, p["n1g"], p["win"], p["bin"], p["wal"], p["bal"])
    return pl.pallas_call(
        _sample_proj_kernel,
        grid=(1,),
        in_specs=[_const_spec(a.shape) for a in args],
        out_specs=[_full_spec((n, IN_COLS_PACKED)), _full_spec((n, QK))],
        out_shape=[jax.ShapeDtypeStruct((n, IN_COLS_PACKED), F32),
                   jax.ShapeDtypeStruct((n, QK), F32)],
        compiler_params=pltpu.CompilerParams(
            dimension_semantics=("arbitrary",), vmem_limit_bytes=VMEM_LIMIT_BYTES),
        name="sample_proj",
    )(*args)


def _sample_mixer_kernel(
        sinks_ref,
        z_ref, ya_ref, s0_ref, hist_ref, kc_ref, vc_ref,
        rc_ref, ra_ref, rb_ref,
        glag_ref, lng_ref, lnb_ref, wsc_ref, bsc_ref,
        wsh_ref, cb_ref, clg_ref, clb_ref, qg_ref, kg_ref,
        obr_ref, s1_ref, convo_ref, ko_ref, vo_ref, gvo_ref,
        hext_ref):
    R = SAMPLE_ROWS
    nt = 4
    z = z_ref[0]
    row = _iota((R, 1), 0)

    q = z[:, C_AQ:C_AQ + QK] * (GLA_DK ** -0.5)
    k = z[:, C_AK:C_AK + QK]
    v = z[:, C_AV:C_AV + QV]
    r = z[:, C_AR:C_AR + QV]
    la = jnp.where(row < nt, _log_sigmoid(ya_ref[0]) * (1.0 / GLA_TAU), 0.0)
    bcum = jnp.zeros_like(la)
    for s in range(nt):
        bcum = bcum + jnp.where(row >= s, jnp.broadcast_to(la[s:s + 1, :], la.shape), 0.0)
    btot = jnp.broadcast_to(bcum[nt - 1:nt, :], la.shape)
    q_in = q * jnp.exp(bcum)
    k_out = k * jnp.exp(-bcum)
    k_dec = k * jnp.exp(btot - bcum)
    s0 = s0_ref[0]
    s0_b = s0.astype(BF16)

    head_sel = (_iota((QK, LANES), 0) // GLA_DK == _iota((QK, LANES), 1)).astype(BF16)
    head_exp = (_iota((LANES, QV), 0) == _iota((LANES, QV), 1) // GLA_DV).astype(BF16)
    o_intra = jnp.zeros((R, QV), F32)
    for s in range(nt):
        prod = q_in * jnp.broadcast_to(k_out[s:s + 1, :], q_in.shape)
        att_s = _mm_exact_lhs_rhs(prod, head_sel)
        att_e = _mm(att_s.astype(BF16), head_exp)
        o_intra = o_intra + jnp.where(row >= s, att_e, 0.0) * jnp.broadcast_to(v[s:s + 1, :], att_e.shape)
    lane_head = _iota((R, QK), 1) // GLA_DK
    o_state = jnp.concatenate(
        [_mm(jnp.where(lane_head == h, q_in, 0.0).astype(BF16), s0_b) for h in range(GLA_HEADS)], axis=1)
    o_gla = o_intra + o_state
    o_a = _head_rms_gla(o_gla, glag_ref[...]) * _silu(r)

    la_hi = la.astype(BF16).astype(F32)
    la_lo = la - la_hi
    ones_tn = jnp.ones((R, GLA_DV), F32)
    dcol = jnp.exp(_mm_tn(la_hi, ones_tn) + _mm_tn(la_lo, ones_tn))
    upd = _mm_tn(k_dec.astype(BF16).astype(F32), v.astype(BF16).astype(F32))
    s1_ref[0] = dcol * s0 + jnp.concatenate(
        [upd[h * GLA_DK:(h + 1) * GLA_DK, h * GLA_DV:(h + 1) * GLA_DV] for h in range(GLA_HEADS)], axis=0)

    u = _gelu(z[:, C_BU:C_BU + GMLP_WIDTH])
    gv = _ln_rows(_gelu(z[:, C_BV:C_BV + GMLP_WIDTH]), lng_ref[...], lnb_ref[...])
    gvo_ref[0] = gv
    sp = bsc_ref[...]
    for s in range(nt):
        sp = sp + jnp.where(row >= s, wsc_ref[s], 0.0) * jnp.broadcast_to(gv[s:s + 1, :], gv.shape)
    o_b = u * sp

    h = z[:, C_CA:C_CA + CONV_WIDTH] * jax.nn.sigmoid(z[:, C_CG:C_CG + CONV_WIDTH])
    hext_ref[0:CONV_PAD, :] = hist_ref[0]
    hext_ref[CONV_PAD:CONV_PAD + R, :] = h
    hext = hext_ref[...]
    conv = jnp.zeros((R, CONV_WIDTH), F32)
    for tt in range(nt):
        c_t = jnp.sum(hext * wsh_ref[tt], axis=0, keepdims=True)
        conv = jnp.where(row == tt, jnp.broadcast_to(c_t, conv.shape), conv)
    o_c = _silu(_ln_rows(conv + cb_ref[...], clg_ref[...], clb_ref[...]))
    convo_ref[0] = hext_ref[CONV_PAD + nt - (CONV_K - 1):CONV_PAD + nt, :]

    zq = z[:, C_DQ:C_DQ + SWA_Q]
    zk = z[:, C_DK:C_DK + SWA_KV]
    zv = z[:, C_DV:C_DV + SWA_KV]
    rc, ra, rb = rc_ref[...], ra_ref[...], rb_ref[...]
    qd = _rope(zq * lax.rsqrt(_group_mean_sq(zq, SWA_HEAD_DIM) + NORM_EPS) * qg_ref[...], rc, ra, rb)
    k_new = _rope(zk * lax.rsqrt(_group_mean_sq(zk, SWA_HEAD_DIM) + NORM_EPS) * kg_ref[...], rc, ra, rb)
    k_hist = kc_ref[0]
    v_hist = vc_ref[0]
    k_hist_b = k_hist.astype(BF16)
    v_hist_b = v_hist.astype(BF16)

    rows = SWA_GROUP * R
    qt = _iota((rows, WINDOW), 0) % R
    hist_ok = _iota((rows, WINDOW), 1) > qt
    qt1 = _iota((rows, 1), 0) % R
    row_head = _iota((rows, 1), 0) // R
    scale = SWA_HEAD_DIM ** -0.5
    outs = []
    for g in range(SWA_KV_HEADS):
        q_stack = jnp.concatenate([_swa_q_for_group(qd, g, i) for i in range(SWA_GROUP)], axis=0)
        s_hist = jnp.where(hist_ok, _mm_nt(q_stack.astype(BF16), k_hist_b) * scale, -jnp.inf)
        s_new = []
        for s in range(nt):
            sn = jnp.sum(q_stack * jnp.broadcast_to(k_new[s:s + 1, :], q_stack.shape),
                         axis=-1, keepdims=True) * scale
            s_new.append(jnp.where(qt1 >= s, sn, -jnp.inf))
        sink = jnp.zeros((rows, 1), F32)
        for i in range(SWA_GROUP):
            sink = jnp.where(row_head == i, sinks_ref[g * SWA_GROUP + i], sink)
        m = jnp.maximum(jnp.max(s_hist, axis=-1, keepdims=True), sink)
        for s in range(nt):
            m = jnp.maximum(m, s_new[s])
        p_hist = jnp.exp(s_hist - m)
        den = jnp.sum(p_hist, axis=-1, keepdims=True) + jnp.exp(sink - m)
        o = _mm(p_hist.astype(BF16), v_hist_b)
        for s in range(nt):
            p_s = jnp.exp(s_new[s] - m)
            den = den + p_s
            o = o + p_s * jnp.broadcast_to(zv[s:s + 1, :], o.shape)
        o = o / den
        outs.append([o[i * R:(i + 1) * R] for i in range(SWA_GROUP)])
    o_d = _swa_assemble(outs, R)

    ko_ref[0, 0:WINDOW - nt, :] = kc_ref[0, nt:WINDOW, :]
    ko_ref[0, WINDOW - nt:WINDOW, :] = k_new[0:nt, :]
    vo_ref[0, 0:WINDOW - nt, :] = vc_ref[0, nt:WINDOW, :]
    vo_ref[0, WINDOW - nt:WINDOW, :] = zv[0:nt, :]

    obr_ref[0] = jnp.concatenate([o_a, o_b, o_c, o_d], axis=1)


def _sample_mixer(z3, ya3, s0, hist, kc, vc, p, tables):
    nb = z3.shape[0]
    R = SAMPLE_ROWS
    consts = (*tables, p["glag"], p["lng"], p["lnb"], p["wsc"], p["bsc"],
              p["wsh"], p["cb"], p["clg"], p["clb"], p["qg"], p["kg"])

    def per_seq(shape):
        nd = len(shape)
        return pl.BlockSpec((1,) + tuple(shape[1:]), lambda b, *_: (b,) + (0,) * (nd - 1))

    def const(shape):
        nd = len(shape)
        return pl.BlockSpec(shape, lambda b, *_: (0,) * nd)

    out_shapes = [jax.ShapeDtypeStruct((nb, R, N_BRANCH * BRANCH_WIDTH), F32),
                  jax.ShapeDtypeStruct((nb, QK, GLA_DV), F32),
                  jax.ShapeDtypeStruct((nb, CONV_K - 1, CONV_WIDTH), F32),
                  jax.ShapeDtypeStruct((nb, WINDOW, SWA_KV), F32),
                  jax.ShapeDtypeStruct((nb, WINDOW, SWA_KV), F32),
                  jax.ShapeDtypeStruct((nb, R, GMLP_WIDTH), F32)]
    grid_spec = pltpu.PrefetchScalarGridSpec(
        num_scalar_prefetch=1,
        grid=(nb,),
        in_specs=[per_seq(a.shape) for a in (z3, ya3, s0, hist, kc, vc)]
                 + [const(a.shape) for a in consts],
        out_specs=[per_seq(s.shape) for s in out_shapes],
        scratch_shapes=[pltpu.VMEM((CONV_PAD + R, CONV_WIDTH), F32)])
    return pl.pallas_call(
        _sample_mixer_kernel,
        grid_spec=grid_spec,
        out_shape=out_shapes,
        compiler_params=pltpu.CompilerParams(
            dimension_semantics=("arbitrary",), vmem_limit_bytes=VMEM_LIMIT_BYTES),
        name="sample_mixer",
    )(p["sinks"], z3, ya3, s0, hist, kc, vc, *consts)


def _sample_merge_kernel(x_ref, obr_ref, n1g_ref, wg_ref, bg_ref, wbr_ref, wout_ref, y_ref):
    x = x_ref[...]
    xb = _rms_rows(x, n1g_ref[...]).astype(BF16)
    merged = jnp.zeros(x.shape, F32)
    for i in range(N_BRANCH):
        gate = jax.nn.sigmoid(
            _mm(xb, wg_ref[:, i * D_MODEL:(i + 1) * D_MODEL]) + bg_ref[:, i * D_MODEL:(i + 1) * D_MODEL])
        o_i = obr_ref[:, i * BRANCH_WIDTH:(i + 1) * BRANCH_WIDTH].astype(BF16)
        merged = merged + gate * _mm(o_i, wbr_ref[i])
    y_ref[...] = x + _mm(merged.astype(BF16), wout_ref[...])


def _sample_merge(x2d, obr, p):
    args = (x2d, obr, p["n1g"], p["wg"], p["bg"], p["wbr"], p["wout"])
    return pl.pallas_call(
        _sample_merge_kernel,
        grid=(1,),
        in_specs=[_const_spec(a.shape) for a in args],
        out_specs=_full_spec(x2d.shape),
        out_shape=jax.ShapeDtypeStruct(x2d.shape, F32),
        compiler_params=pltpu.CompilerParams(
            dimension_semantics=("arbitrary",), vmem_limit_bytes=VMEM_LIMIT_BYTES),
        name="sample_merge",
    )(*args)


def _rope_tables(pos):
    half = ROPE_DIM // 2
    inv = jnp.exp(-math.log(ROPE_THETA) * jnp.arange(half, dtype=F32) * (2.0 / ROPE_DIM))
    ang = pos.astype(F32)[:, None] * inv[None, :]
    cos, sin = jnp.cos(ang), jnp.sin(ang)
    n = pos.shape[0]
    pad = jnp.zeros((n, SWA_HEAD_DIM - ROPE_DIM), F32)
    zero = jnp.zeros((n, half), F32)
    c_head = jnp.concatenate([cos, cos, pad + 1.0], axis=1)
    a_head = jnp.concatenate([-sin, zero, pad], axis=1)
    b_head = jnp.concatenate([zero, sin, pad], axis=1)
    rep = LANES // SWA_HEAD_DIM
    return tuple(jnp.tile(tb, (1, rep)) for tb in (c_head, a_head, b_head))


def _row(v):
    return v.reshape(1, -1).astype(F32)


def _layer_params(l, norm1_g, w_in, b_in, w_alpha2, b_alpha, gla_norm_g, gmlp_ln_g, gmlp_ln_b,
                  w_spatial, b_spatial, conv_w, conv_b, conv_ln_g, conv_ln_b, q_norm_g, k_norm_g,
                  sinks, w_gate, b_gate, w_branch, w_out, norm2_g, w_gate_up, w_down):
    n_aq = QK
    o_lr = 2 * QK + 2 * QV
    o_rest = o_lr + GLA_RANK

    def repack(m):
        pad = jnp.zeros(m.shape[:-1] + (LANES - GLA_RANK,), m.dtype)
        return jnp.concatenate([m[..., :o_lr], m[..., o_rest:], m[..., o_lr:o_rest], pad], axis=-1)

    wi = w_in[l]
    nt = 4
    ws = w_spatial[l]
    gw = GMLP_WIDTH // GMLP_GROUPS
    wsc = jnp.stack([
        jnp.pad(jnp.repeat(ws[:, :nt, s].T, gw, axis=1), ((0, SAMPLE_ROWS - nt), (0, 0)))
        for s in range(nt)], axis=0)
    bsc = jnp.pad(jnp.repeat(b_spatial[l][:, :nt].T, gw, axis=1), ((0, SAMPLE_ROWS - nt), (0, 0)))
    cwl = conv_w[l]
    lead = CONV_PAD - (CONV_K - 1)
    wsh = jnp.stack([
        jnp.pad(cwl, ((lead + t, CONV_PAD + SAMPLE_ROWS - lead - t - CONV_K), (0, 0)))
        for t in range(nt)], axis=0)
    return {
        "n1g": _row(norm1_g[l]),
        "win": repack(wi).astype(BF16),
        "bin": repack(b_in[l]).reshape(1, -1).astype(F32),
        "wal": jnp.pad(w_alpha2[l], ((0, LANES - GLA_RANK), (0, 0))).astype(BF16),
        "bal": _row(b_alpha[l]),
        "glag": _row(gla_norm_g[l]),
        "lng": _row(gmlp_ln_g[l]),
        "lnb": _row(gmlp_ln_b[l]),
        "wsp": ws.astype(F32),
        "bspt": b_spatial[l].T.astype(F32),
        "wsc": wsc.astype(F32),
        "bsc": bsc.astype(F32),
        "cw": jnp.pad(cwl, ((0, CONV_PAD - CONV_K), (0, 0))).astype(F32),
        "wsh": wsh.astype(F32),
        "cb": _row(conv_b[l]),
        "clg": _row(conv_ln_g[l]),
        "clb": _row(conv_ln_b[l]),
        "qg": _row(jnp.tile(q_norm_g[l], SWA_HEADS)),
        "kg": _row(jnp.tile(k_norm_g[l], SWA_KV_HEADS)),
        "sinks": sinks[l].astype(F32),
        "wg": w_gate[l].astype(BF16),
        "bg": _row(b_gate[l]),
        "wbr": w_branch[l].astype(BF16),
        "wout": w_out[l].astype(BF16),
        "n2g": _row(norm2_g[l]),
        "wgu": w_gate_up[l].astype(BF16),
        "wd": w_down[l].astype(BF16),
    }


def kernel(x_prompt, x_sample, state_gla, state_conv, cache_swa_k, cache_swa_v, norm1_g, w_in, b_in, w_alpha2, b_alpha, gla_norm_g, gmlp_ln_g, gmlp_ln_b, w_spatial, b_spatial, conv_w, conv_b, conv_ln_g, conv_ln_b, q_norm_g, k_norm_g, sinks, w_gate, b_gate, w_branch, w_out, norm2_g, w_gate_up, w_down):
    bp, lp, _ = x_prompt.shape
    bs, ls, _ = x_sample.shape
    R = SAMPLE_ROWS
    prompt_tables = _rope_tables(jnp.arange(lp))
    sample_tables = tuple(
        jnp.pad(tb, ((0, R - ls), (0, 0))) for tb in _rope_tables(PAST_LEN + jnp.arange(ls)))

    yp = x_prompt
    ys = x_sample.reshape(bs * ls, D_MODEL)
    outs = {k: [] for k in ("gla_p", "gla_s", "conv_p", "conv_s", "kp", "vp", "ks", "vs", "gm")}
    for l in range(DEPTH):
        p = _layer_params(l, norm1_g, w_in, b_in, w_alpha2, b_alpha, gla_norm_g, gmlp_ln_g, gmlp_ln_b,
                          w_spatial, b_spatial, conv_w, conv_b, conv_ln_g, conv_ln_b, q_norm_g,
                          k_norm_g, sinks, w_gate, b_gate, w_branch, w_out, norm2_g, w_gate_up, w_down)
        yp, g1, c1, k1, v1 = _prompt_mixer(yp, p, prompt_tables)
        yp = _ffn(yp.reshape(bp * lp, D_MODEL), p).reshape(bp, lp, D_MODEL)
        outs["gla_p"].append(g1.reshape(bp, GLA_HEADS, GLA_DK, GLA_DV))
        outs["conv_p"].append(c1)
        outs["kp"].append(k1.reshape(bp, WINDOW, SWA_KV_HEADS, SWA_HEAD_DIM))
        outs["vp"].append(v1.reshape(bp, WINDOW, SWA_KV_HEADS, SWA_HEAD_DIM))
        z, ya = _sample_proj(ys, p)
        pad_rows = ((0, 0), (0, R - ls), (0, 0))
        z3 = jnp.pad(z.reshape(bs, ls, IN_COLS_PACKED), pad_rows)
        ya3 = jnp.pad(ya.reshape(bs, ls, QK), pad_rows)
        hist = jnp.pad(state_conv[l], ((0, 0), (CONV_PAD - (CONV_K - 1), 0), (0, 0)))
        obr, g2, c2, k2, v2, gv2 = _sample_mixer(
            z3, ya3, state_gla[l].reshape(bs, QK, GLA_DV), hist,
            cache_swa_k[l].reshape(bs, WINDOW, SWA_KV), cache_swa_v[l].reshape(bs, WINDOW, SWA_KV),
            p, sample_tables)
        ys = _sample_merge(ys, obr[:, :ls].reshape(bs * ls, N_BRANCH * BRANCH_WIDTH), p)
        ys = _ffn(ys, p)
        outs["gla_s"].append(g2.reshape(bs, GLA_HEADS, GLA_DK, GLA_DV))
        outs["conv_s"].append(c2)
        outs["ks"].append(k2.reshape(bs, WINDOW, SWA_KV_HEADS, SWA_HEAD_DIM))
        outs["vs"].append(v2.reshape(bs, WINDOW, SWA_KV_HEADS, SWA_HEAD_DIM))
        outs["gm"].append(gv2[:, :ls])
    st = lambda name: jnp.stack(outs[name], 0)
    return (yp, ys.reshape(bs, ls, D_MODEL), st("gla_p"), st("gla_s"), st("conv_p"), st("conv_s"),
            st("kp"), st("vp"), st("ks"), st("vs"), st("gm"))
```

```python
import functools
import math

import jax
import jax.numpy as jnp
import numpy as np
from jax import lax
from jax.experimental import pallas as pl
from jax.experimental.pallas import tpu as pltpu

F32 = jnp.float32
BF16 = jnp.bfloat16

D_MODEL = 1024
DEPTH = 2
PAST_LEN = 16384
GLA_HEADS = 4
GLA_DK = 64
GLA_DV = 128
GLA_RANK = 16
GLA_TAU = 16.0
GLA_CHUNK = 64
GMLP_GROUPS = 4
GMLP_WIDTH = 512
GMLP_CHUNK = 128
CONV_WIDTH = 512
CONV_K = 31
SWA_HEADS = 8
SWA_KV_HEADS = 2
SWA_GROUP = SWA_HEADS // SWA_KV_HEADS
SWA_HEAD_DIM = 64
WINDOW = 128
ROPE_DIM = SWA_HEAD_DIM // 4
ROPE_THETA = 500000.0
N_BRANCH = 4
BRANCH_WIDTH = 512
FFN_HIDDEN = 2816
NORM_EPS = 1e-6
LN_EPS = 1e-5

LANES = 128
SUBLANES = 8
VMEM_LIMIT_BYTES = 56 * 1024 * 1024

QK = GLA_HEADS * GLA_DK
QV = GLA_HEADS * GLA_DV
SWA_Q = SWA_HEADS * SWA_HEAD_DIM
SWA_KV = SWA_KV_HEADS * SWA_HEAD_DIM

C_AQ = 0
C_AK = C_AQ + QK
C_AV = C_AK + QK
C_AR = C_AV + QV
C_BU = C_AR + QV
C_BV = C_BU + GMLP_WIDTH
C_CA = C_BV + GMLP_WIDTH
C_CG = C_CA + CONV_WIDTH
C_DQ = C_CG + CONV_WIDTH
C_DK = C_DQ + SWA_Q
C_DV = C_DK + SWA_KV
C_LR = C_DV + SWA_KV
IN_COLS_PACKED = C_LR + LANES

PROMPT_TILE = 256
FFN_TILE = 512
FFN_CHUNK = 256
CONV_PAD = 32
SAMPLE_ROWS = 8


def _mm(a, b):
    return jnp.dot(a, b, preferred_element_type=F32)


def _mm_nt(a, b):
    return lax.dot_general(a, b, (((1,), (1,)), ((), ())), preferred_element_type=F32)


def _mm_tn(a, b):
    return lax.dot_general(a, b, (((0,), (0,)), ((), ())), preferred_element_type=F32)


def _hi_lo(a):
    hi = a.astype(BF16)
    lo = (a - hi.astype(F32)).astype(BF16)
    return hi, lo


def _mm_exact_lhs(m_bf16, a_f32):
    hi, lo = _hi_lo(a_f32)
    return _mm(m_bf16, hi) + _mm(m_bf16, lo)


def _rms_rows(x, g):
    return x * lax.rsqrt(jnp.mean(x * x, axis=-1, keepdims=True) + NORM_EPS) * g


def _ln_rows(x, g, b):
    mu = jnp.mean(x, axis=-1, keepdims=True)
    xc = x - mu
    var = jnp.mean(xc * xc, axis=-1, keepdims=True)
    return xc * lax.rsqrt(var + LN_EPS) * g + b


def _gelu(x):
    return 0.5 * x * (1.0 + lax.erf(x * (1.0 / math.sqrt(2.0))))


def _silu(x):
    return x * jax.nn.sigmoid(x)


def _log_sigmoid(x):
    return jnp.minimum(x, 0.0) - jnp.log1p(jnp.exp(-jnp.abs(x)))


def _iota(shape, dim):
    return lax.broadcasted_iota(jnp.int32, shape, dim)


def _group_ones(n, group, dtype=BF16):
    return (_iota((n, n), 0) // group == _iota((n, n), 1) // group).astype(dtype)


def _group_mean_sq(x, group):
    n = x.shape[-1]
    return _mm_exact_lhs_rhs(x * x, _group_ones(n, group)) * (1.0 / group)


def _mm_exact_lhs_rhs(a_f32, m_bf16):
    hi, lo = _hi_lo(a_f32)
    return _mm(hi, m_bf16) + _mm(lo, m_bf16)


def _rope_slab(x, c, a, b):
    return x * c + pltpu.roll(x, LANES - ROPE_DIM // 2, 1) * a + pltpu.roll(x, ROPE_DIM // 2, 1) * b


def _rope(x, c, a, b):
    n = x.shape[-1] // LANES
    if n == 1:
        return _rope_slab(x, c, a, b)
    return jnp.concatenate(
        [_rope_slab(x[:, i * LANES:(i + 1) * LANES], c, a, b) for i in range(n)], axis=1)


def _head_rms_gla(o, g):
    outs = []
    for h in range(GLA_HEADS):
        oh = o[:, h * GLA_DV:(h + 1) * GLA_DV]
        outs.append(_rms_rows(oh, g))
    return jnp.concatenate(outs, axis=1)


def _swa_q_for_group(q, g, i):
    hq = g * SWA_GROUP + i
    slab = q[:, (hq // 2) * LANES:(hq // 2 + 1) * LANES]
    if hq % 2 != g:
        slab = pltpu.roll(slab, SWA_HEAD_DIM, 1)
    lane_head = _iota(slab.shape, 1) // SWA_HEAD_DIM
    return jnp.where(lane_head == g, slab, 0.0)


def _swa_assemble(outs, rows):
    lane_half = _iota((rows, LANES), 1) // SWA_HEAD_DIM
    slabs = []
    for s in range(SWA_HEADS // 2):
        pair = []
        for hq in (2 * s, 2 * s + 1):
            g, i = hq // SWA_GROUP, hq % SWA_GROUP
            o = outs[g][i]
            if hq % 2 != g:
                o = pltpu.roll(o, SWA_HEAD_DIM, 1)
            pair.append(o)
        slabs.append(jnp.where(lane_half == 0, pair[0], pair[1]))
    return jnp.concatenate(slabs, axis=1)


def _prompt_mixer_kernel(
        sinks_ref,
        x_ref, rc_ref, ra_ref, rb_ref,
        n1g_ref, win_ref, bin_ref, wal_ref, bal_ref, glag_ref,
        lng_ref, lnb_ref, wsp_ref, bspt_ref,
        cw_ref, cb_ref, clg_ref, clb_ref,
        qg_ref, kg_ref,
        wg_ref, bg_ref, wbr_ref, wout_ref,
        y_ref, gla_ref, conv_ref, ko_ref, vo_ref,
        sbd_ref, hbuf_ref, kfull_ref, vfull_ref, obr_ref):
    T = PROMPT_TILE
    t = pl.program_id(1)
    last = pl.num_programs(1) - 1

    @pl.when(t == 0)
    def _():
        sbd_ref[...] = jnp.zeros_like(sbd_ref)
        hbuf_ref[0:CONV_PAD, :] = jnp.zeros((CONV_PAD, CONV_WIDTH), F32)
        kfull_ref[0:WINDOW, :] = jnp.zeros((WINDOW, SWA_KV), F32)
        vfull_ref[0:WINDOW, :] = jnp.zeros((WINDOW, SWA_KV), F32)

    x = x_ref[0]
    xb = _rms_rows(x, n1g_ref[...]).astype(BF16)

    def proj(c0, width):
        return _mm(xb, win_ref[:, c0:c0 + width]) + bin_ref[:, c0:c0 + width]

    q = proj(C_AQ, QK) * (GLA_DK ** -0.5)
    k = proj(C_AK, QK)
    v = proj(C_AV, QV)
    lr = proj(C_LR, LANES)
    la = _log_sigmoid(_mm(lr.astype(BF16), wal_ref[...]) + bal_ref[...]) * (1.0 / GLA_TAU)

    ri = _iota((T, T), 0)
    ci = _iota((T, T), 1)
    same_chunk = (ri // GLA_CHUNK) == (ci // GLA_CHUNK)
    m_blk = same_chunk.astype(BF16)
    m_tril = jnp.logical_and(same_chunk, ci <= ri).astype(BF16)
    la_hi, la_lo = _hi_lo(la)
    bcum = _mm(m_tril, la_hi) + _mm(m_tril, la_lo)
    btot = _mm(m_blk, la_hi) + _mm(m_blk, la_lo)
    q_in = (q * jnp.exp(bcum)).astype(BF16)
    k_out = (k * jnp.exp(-bcum)).astype(BF16)
    k_dec = (k * jnp.exp(btot - bcum)).astype(BF16)
    v_b = v.astype(BF16)

    lane_head = _iota((GLA_CHUNK, QK), 1) // GLA_DK
    causal = (_iota((QK, GLA_CHUNK), 0) % GLA_CHUNK) >= _iota((QK, GLA_CHUNK), 1)
    bd_mask = (_iota((QK, QV), 0) // GLA_DK) == (_iota((QK, QV), 1) // GLA_DV)
    ones_tn = jnp.ones((GLA_CHUNK, GLA_DV), BF16)
    o_chunks = []
    for c in range(T // GLA_CHUNK):
        rs = slice(c * GLA_CHUNK, (c + 1) * GLA_CHUNK)
        qc, kc, kd, vc = q_in[rs], k_out[rs], k_dec[rs], v_b[rs]
        q_stack = jnp.concatenate(
            [jnp.where(lane_head == h, qc, jnp.zeros_like(qc)) for h in range(GLA_HEADS)], axis=0)
        att = jnp.where(causal, _mm_nt(q_stack, kc), 0.0)
        pv = _mm(att.astype(BF16), vc)
        o_intra = jnp.concatenate(
            [pv[h * GLA_CHUNK:(h + 1) * GLA_CHUNK, h * GLA_DV:(h + 1) * GLA_DV]
             for h in range(GLA_HEADS)], axis=1)
        s_bd = sbd_ref[...]
        o_chunks.append(o_intra + _mm(qc, s_bd.astype(BF16)))
        dsum = _mm_tn(la_hi[rs], ones_tn) + _mm_tn(la_lo[rs], ones_tn)
        dcol = jnp.exp(dsum)
        dcol4 = jnp.concatenate([dcol] * GLA_HEADS, axis=1)
        upd = _mm_tn(kd, vc)
        sbd_ref[...] = dcol4 * s_bd + jnp.where(bd_mask, upd, 0.0)
    o_gla = jnp.concatenate(o_chunks, axis=0)
    r = proj(C_AR, QV)
    obr_ref[0] = (_head_rms_gla(o_gla, glag_ref[...]) * _silu(r)).astype(BF16)

    u = _gelu(proj(C_BU, GMLP_WIDTH))
    gv = _ln_rows(_gelu(proj(C_BV, GMLP_WIDTH)), lng_ref[...], lnb_ref[...])
    gv_b = gv.astype(BF16)
    gw = GMLP_WIDTH // GMLP_GROUPS
    tril = _iota((GMLP_CHUNK, GMLP_CHUNK), 1) <= _iota((GMLP_CHUNK, GMLP_CHUNK), 0)
    bspt = bspt_ref[...]
    s_rows = []
    for n in range(T // GMLP_CHUNK):
        rs = slice(n * GMLP_CHUNK, (n + 1) * GMLP_CHUNK)
        cols = []
        for g in range(GMLP_GROUPS):
            w = jnp.where(tril, wsp_ref[g], 0.0).astype(BF16)
            sg = _mm(w, gv_b[rs, g * gw:(g + 1) * gw])
            cols.append(sg + jnp.broadcast_to(bspt[:, g:g + 1], (GMLP_CHUNK, gw)))
        s_rows.append(jnp.concatenate(cols, axis=1))
    obr_ref[1] = (u * jnp.concatenate(s_rows, axis=0)).astype(BF16)

    h = proj(C_CA, CONV_WIDTH) * jax.nn.sigmoid(proj(C_CG, CONV_WIDTH))
    hbuf_ref[CONV_PAD:CONV_PAD + T, :] = h
    acc = jnp.broadcast_to(cb_ref[...], (T, CONV_WIDTH))
    lead = CONV_PAD - (CONV_K - 1)
    for r in range(SUBLANES):
        taps = [kk for kk in range(CONV_K) if (lead + kk) % SUBLANES == r]
        span = max(lead + kk for kk in taps) - r
        shifted = hbuf_ref[r:r + span + T, :]
        for kk in taps:
            a = lead + kk - r
            acc = acc + cw_ref[kk:kk + 1, :] * shifted[a:a + T]
    obr_ref[2] = _silu(_ln_rows(acc, clg_ref[...], clb_ref[...])).astype(BF16)
    hbuf_ref[0:CONV_PAD, :] = hbuf_ref[T:T + CONV_PAD, :]

    zq = proj(C_DQ, SWA_Q)
    zk = proj(C_DK, SWA_KV)
    zv = proj(C_DV, SWA_KV)
    rc, ra, rb = rc_ref[...], ra_ref[...], rb_ref[...]
    qd = _rope(zq * lax.rsqrt(_group_mean_sq(zq, SWA_HEAD_DIM) + NORM_EPS) * qg_ref[...], rc, ra, rb)
    kd_new = _rope(zk * lax.rsqrt(_group_mean_sq(zk, SWA_HEAD_DIM) + NORM_EPS) * kg_ref[...], rc, ra, rb)
    kfull_ref[WINDOW:WINDOW + T, :] = kd_new
    vfull_ref[WINDOW:WINDOW + T, :] = zv

    rows = SWA_GROUP * WINDOW
    qi = _iota((rows, 2 * WINDOW), 0) % WINDOW
    kj = _iota((rows, 2 * WINDOW), 1)
    own_ok = jnp.logical_and(kj >= WINDOW, kj - WINDOW <= qi)
    prev_ok = jnp.logical_and(kj < WINDOW, kj > qi)
    row_head = _iota((rows, 1), 0) // WINDOW
    scale = SWA_HEAD_DIM ** -0.5
    o_blocks = []
    for qb in range(T // WINDOW):
        rs = slice(qb * WINDOW, (qb + 1) * WINDOW)
        kblk = kfull_ref[qb * WINDOW:(qb + 2) * WINDOW, :].astype(BF16)
        vblk = vfull_ref[qb * WINDOW:(qb + 2) * WINDOW, :].astype(BF16)
        if qb == 0:
            valid = jnp.logical_or(own_ok, jnp.logical_and(prev_ok, t > 0))
        else:
            valid = jnp.logical_or(own_ok, prev_ok)
        outs = []
        for g in range(SWA_KV_HEADS):
            q_stack = jnp.concatenate(
                [_swa_q_for_group(qd[rs], g, i) for i in range(SWA_GROUP)], axis=0).astype(BF16)
            s = jnp.where(valid, _mm_nt(q_stack, kblk) * scale, -jnp.inf)
            sink = jnp.zeros((rows, 1), F32)
            for i in range(SWA_GROUP):
                sink = jnp.where(row_head == i, sinks_ref[g * SWA_GROUP + i], sink)
            m = jnp.maximum(jnp.max(s, axis=-1, keepdims=True), sink)
            p = jnp.exp(s - m)
            den = jnp.sum(p, axis=-1, keepdims=True) + jnp.exp(sink - m)
            o = _mm(p.astype(BF16), vblk) / den
            outs.append([o[i * WINDOW:(i + 1) * WINDOW] for i in range(SWA_GROUP)])
        o_blocks.append(_swa_assemble(outs, WINDOW))
    obr_ref[3] = jnp.concatenate(o_blocks, axis=0).astype(BF16)

    kfull_ref[0:WINDOW, :] = kfull_ref[T:T + WINDOW, :]
    vfull_ref[0:WINDOW, :] = vfull_ref[T:T + WINDOW, :]

    merged = jnp.zeros((T, D_MODEL), F32)
    for i in range(N_BRANCH):
        gate = jax.nn.sigmoid(
            _mm(xb, wg_ref[:, i * D_MODEL:(i + 1) * D_MODEL]) + bg_ref[:, i * D_MODEL:(i + 1) * D_MODEL])
        merged = merged + gate * _mm(obr_ref[i], wbr_ref[i])
    y_ref[0] = x + _mm(merged.astype(BF16), wout_ref[...])

    @pl.when(t == last)
    def _():
        conv_ref[0] = hbuf_ref[lead:CONV_PAD, :]
        ko_ref[0] = kfull_ref[0:WINDOW, :]
        vo_ref[0] = vfull_ref[0:WINDOW, :]
        sb = sbd_ref[...]
        gla_ref[0] = jnp.concatenate(
            [sb[hh * GLA_DK:(hh + 1) * GLA_DK, hh * GLA_DV:(hh + 1) * GLA_DV]
             for hh in range(GLA_HEADS)], axis=0)


def _const_spec(shape):
    nd = len(shape)
    return pl.BlockSpec(shape, lambda *_: (0,) * nd, pipeline_mode=pl.Buffered(1))


def _full_spec(shape):
    nd = len(shape)
    return pl.BlockSpec(shape, lambda *_: (0,) * nd)


def _prompt_mixer(x, p, tables):
    B, L, _ = x.shape
    T = PROMPT_TILE
    weights = (p["n1g"], p["win"], p["bin"], p["wal"], p["bal"], p["glag"],
               p["lng"], p["lnb"], p["wsp"], p["bspt"],
               p["cw"], p["cb"], p["clg"], p["clb"],
               p["qg"], p["kg"],
               p["wg"], p["bg"], p["wbr"], p["wout"])
    tab_spec = pl.BlockSpec((T, LANES), lambda b, t, *_: (t, 0))
    grid_spec = pltpu.PrefetchScalarGridSpec(
        num_scalar_prefetch=1,
        grid=(B, L // T),
        in_specs=[pl.BlockSpec((1, T, D_MODEL), lambda b, t, *_: (b, t, 0)),
                  tab_spec, tab_spec, tab_spec]
                 + [_const_spec(w.shape) for w in weights],
        out_specs=[pl.BlockSpec((1, T, D_MODEL), lambda b, t, *_: (b, t, 0)),
                   pl.BlockSpec((1, QK, GLA_DV), lambda b, t, *_: (b, 0, 0)),
                   pl.BlockSpec((1, CONV_K - 1, CONV_WIDTH), lambda b, t, *_: (b, 0, 0)),
                   pl.BlockSpec((1, WINDOW, SWA_KV), lambda b, t, *_: (b, 0, 0)),
                   pl.BlockSpec((1, WINDOW, SWA_KV), lambda b, t, *_: (b, 0, 0))],
        scratch_shapes=[pltpu.VMEM((QK, QV), F32),
                        pltpu.VMEM((T + CONV_PAD, CONV_WIDTH), F32),
                        pltpu.VMEM((T + WINDOW, SWA_KV), F32),
                        pltpu.VMEM((T + WINDOW, SWA_KV), F32),
                        pltpu.VMEM((N_BRANCH, T, BRANCH_WIDTH), BF16)])
    return pl.pallas_call(
        _prompt_mixer_kernel,
        grid_spec=grid_spec,
        out_shape=[jax.ShapeDtypeStruct((B, L, D_MODEL), F32),
                   jax.ShapeDtypeStruct((B, QK, GLA_DV), F32),
                   jax.ShapeDtypeStruct((B, CONV_K - 1, CONV_WIDTH), F32),
                   jax.ShapeDtypeStruct((B, WINDOW, SWA_KV), F32),
                   jax.ShapeDtypeStruct((B, WINDOW, SWA_KV), F32)],
        compiler_params=pltpu.CompilerParams(
            dimension_semantics=("arbitrary", "arbitrary"),
            vmem_limit_bytes=VMEM_LIMIT_BYTES),
        name="prompt_mixer",
    )(p["sinks"], x, *tables, *weights)


def _ffn_kernel(x_ref, g_ref, wgu_ref, wd_ref, y_ref, h_ref):
    x = x_ref[...]
    xb = _rms_rows(x, g_ref[...]).astype(BF16)
    for c in range(FFN_HIDDEN // FFN_CHUNK):
        c0 = c * FFN_CHUNK
        gate = _mm(xb, wgu_ref[:, c0:c0 + FFN_CHUNK])
        up = _mm(xb, wgu_ref[:, FFN_HIDDEN + c0:FFN_HIDDEN + c0 + FFN_CHUNK])
        h_ref[:, c0:c0 + FFN_CHUNK] = (_silu(gate) * up).astype(BF16)
    y_ref[...] = x + _mm(h_ref[...], wd_ref[...])


def _ffn(x2d, p):
    n = x2d.shape[0]
    tile = min(FFN_TILE, n)
    return pl.pallas_call(
        _ffn_kernel,
        grid=(n // tile,),
        in_specs=[pl.BlockSpec((tile, D_MODEL), lambda i: (i, 0)),
                  _const_spec(p["n2g"].shape), _const_spec(p["wgu"].shape), _const_spec(p["wd"].shape)],
        out_specs=pl.BlockSpec((tile, D_MODEL), lambda i: (i, 0)),
        out_shape=jax.ShapeDtypeStruct((n, D_MODEL), F32),
        scratch_shapes=[pltpu.VMEM((tile, FFN_HIDDEN), BF16)],
        compiler_params=pltpu.CompilerParams(
            dimension_semantics=("arbitrary",), vmem_limit_bytes=VMEM_LIMIT_BYTES),
        name="swiglu",
    )(x2d, p["n2g"], p["wgu"], p["wd"])


def _sample_proj_kernel(x_ref, n1g_ref, win_ref, bin_ref, wal_ref, bal_ref, z_ref, ya_ref):
    xb = _rms_rows(x_ref[...], n1g_ref[...]).astype(BF16)
    z = _mm(xb, win_ref[...]) + bin_ref[...]
    z_ref[...] = z
    ya_ref[...] = _mm(z[:, C_LR:C_LR + LANES].astype(BF16), wal_ref[...]) + bal_ref[...]


def _sample_proj(x2d, p):
    n = x2d.shape[0]
    args = (x2d, p["n1g"], p["win"], p["bin"], p["wal"], p["bal"])
    return pl.pallas_call(
        _sample_proj_kernel,
        grid=(1,),
        in_specs=[_const_spec(a.shape) for a in args],
        out_specs=[_full_spec((n, IN_COLS_PACKED)), _full_spec((n, QK))],
        out_shape=[jax.ShapeDtypeStruct((n, IN_COLS_PACKED), F32),
                   jax.ShapeDtypeStruct((n, QK), F32)],
        compiler_params=pltpu.CompilerParams(
            dimension_semantics=("arbitrary",), vmem_limit_bytes=VMEM_LIMIT_BYTES),
        name="sample_proj",
    )(*args)


def _sample_mixer_kernel(
        sinks_ref,
        z_ref, ya_ref, s0_ref, hist_ref, kc_ref, vc_ref,
        rc_ref, ra_ref, rb_ref,
        glag_ref, lng_ref, lnb_ref, wsc_ref, bsc_ref,
        wsh_ref, cb_ref, clg_ref, clb_ref, qg_ref, kg_ref,
        obr_ref, s1_ref, convo_ref, ko_ref, vo_ref, gvo_ref,
        hext_ref):
    R = SAMPLE_ROWS
    nt = 4
    z = z_ref[0]
    row = _iota((R, 1), 0)

    q = z[:, C_AQ:C_AQ + QK] * (GLA_DK ** -0.5)
    k = z[:, C_AK:C_AK + QK]
    v = z[:, C_AV:C_AV + QV]
    r = z[:, C_AR:C_AR + QV]
    la = jnp.where(row < nt, _log_sigmoid(ya_ref[0]) * (1.0 / GLA_TAU), 0.0)
    bcum = jnp.zeros_like(la)
    for s in range(nt):
        bcum = bcum + jnp.where(row >= s, jnp.broadcast_to(la[s:s + 1, :], la.shape), 0.0)
    btot = jnp.broadcast_to(bcum[nt - 1:nt, :], la.shape)
    q_in = q * jnp.exp(bcum)
    k_out = k * jnp.exp(-bcum)
    k_dec = k * jnp.exp(btot - bcum)
    s0 = s0_ref[0]
    s0_b = s0.astype(BF16)

    head_sel = (_iota((QK, LANES), 0) // GLA_DK == _iota((QK, LANES), 1)).astype(BF16)
    head_exp = (_iota((LANES, QV), 0) == _iota((LANES, QV), 1) // GLA_DV).astype(BF16)
    o_intra = jnp.zeros((R, QV), F32)
    for s in range(nt):
        prod = q_in * jnp.broadcast_to(k_out[s:s + 1, :], q_in.shape)
        att_s = _mm_exact_lhs_rhs(prod, head_sel)
        att_e = _mm(att_s.astype(BF16), head_exp)
        o_intra = o_intra + jnp.where(row >= s, att_e, 0.0) * jnp.broadcast_to(v[s:s + 1, :], att_e.shape)
    lane_head = _iota((R, QK), 1) // GLA_DK
    o_state = jnp.concatenate(
        [_mm(jnp.where(lane_head == h, q_in, 0.0).astype(BF16), s0_b) for h in range(GLA_HEADS)], axis=1)
    o_gla = o_intra + o_state
    o_a = _head_rms_gla(o_gla, glag_ref[...]) * _silu(r)

    la_hi = la.astype(BF16).astype(F32)
    la_lo = la - la_hi
    ones_tn = jnp.ones((R, GLA_DV), F32)
    dcol = jnp.exp(_mm_tn(la_hi, ones_tn) + _mm_tn(la_lo, ones_tn))
    upd = _mm_tn(k_dec.astype(BF16).astype(F32), v.astype(BF16).astype(F32))
    s1_ref[0] = dcol * s0 + jnp.concatenate(
        [upd[h * GLA_DK:(h + 1) * GLA_DK, h * GLA_DV:(h + 1) * GLA_DV] for h in range(GLA_HEADS)], axis=0)

    u = _gelu(z[:, C_BU:C_BU + GMLP_WIDTH])
    gv = _ln_rows(_gelu(z[:, C_BV:C_BV + GMLP_WIDTH]), lng_ref[...], lnb_ref[...])
    gvo_ref[0] = gv
    sp = bsc_ref[...]
    for s in range(nt):
        sp = sp + jnp.where(row >= s, wsc_ref[s], 0.0) * jnp.broadcast_to(gv[s:s + 1, :], gv.shape)
    o_b = u * sp

    h = z[:, C_CA:C_CA + CONV_WIDTH] * jax.nn.sigmoid(z[:, C_CG:C_CG + CONV_WIDTH])
    hext_ref[0:CONV_PAD, :] = hist_ref[0]
    hext_ref[CONV_PAD:CONV_PAD + R, :] = h
    hext = hext_ref[...]
    conv = jnp.zeros((R, CONV_WIDTH), F32)
    for tt in range(nt):
        c_t = jnp.sum(hext * wsh_ref[tt], axis=0, keepdims=True)
        conv = jnp.where(row == tt, jnp.broadcast_to(c_t, conv.shape), conv)
    o_c = _silu(_ln_rows(conv + cb_ref[...], clg_ref[...], clb_ref[...]))
    convo_ref[0] = hext_ref[CONV_PAD + nt - (CONV_K - 1):CONV_PAD + nt, :]

    zq = z[:, C_DQ:C_DQ + SWA_Q]
    zk = z[:, C_DK:C_DK + SWA_KV]
    zv = z[:, C_DV:C_DV + SWA_KV]
    rc, ra, rb = rc_ref[...], ra_ref[...], rb_ref[...]
    qd = _rope(zq * lax.rsqrt(_group_mean_sq(zq, SWA_HEAD_DIM) + NORM_EPS) * qg_ref[...], rc, ra, rb)
    k_new = _rope(zk * lax.rsqrt(_group_mean_sq(zk, SWA_HEAD_DIM) + NORM_EPS) * kg_ref[...], rc, ra, rb)
    k_hist = kc_ref[0]
    v_hist = vc_ref[0]
    k_hist_b = k_hist.astype(BF16)
    v_hist_b = v_hist.astype(BF16)

    rows = SWA_GROUP * R
    qt = _iota((rows, WINDOW), 0) % R
    hist_ok = _iota((rows, WINDOW), 1) > qt
    qt1 = _iota((rows, 1), 0) % R
    row_head = _iota((rows, 1), 0) // R
    scale = SWA_HEAD_DIM ** -0.5
    outs = []
    for g in range(SWA_KV_HEADS):
        q_stack = jnp.concatenate([_swa_q_for_group(qd, g, i) for i in range(SWA_GROUP)], axis=0)
        s_hist = jnp.where(hist_ok, _mm_nt(q_stack.astype(BF16), k_hist_b) * scale, -jnp.inf)
        s_new = []
        for s in range(nt):
            sn = jnp.sum(q_stack * jnp.broadcast_to(k_new[s:s + 1, :], q_stack.shape),
                         axis=-1, keepdims=True) * scale
            s_new.append(jnp.where(qt1 >= s, sn, -jnp.inf))
        sink = jnp.zeros((rows, 1), F32)
        for i in range(SWA_GROUP):
            sink = jnp.where(row_head == i, sinks_ref[g * SWA_GROUP + i], sink)
        m = jnp.maximum(jnp.max(s_hist, axis=-1, keepdims=True), sink)
        for s in range(nt):
            m = jnp.maximum(m, s_new[s])
        p_hist = jnp.exp(s_hist - m)
        den = jnp.sum(p_hist, axis=-1, keepdims=True) + jnp.exp(sink - m)
        o = _mm(p_hist.astype(BF16), v_hist_b)
        for s in range(nt):
            p_s = jnp.exp(s_new[s] - m)
            den = den + p_s
            o = o + p_s * jnp.broadcast_to(zv[s:s + 1, :], o.shape)
        o = o / den
        outs.append([o[i * R:(i + 1) * R] for i in range(SWA_GROUP)])
    o_d = _swa_assemble(outs, R)

    ko_ref[0, 0:WINDOW - nt, :] = kc_ref[0, nt:WINDOW, :]
    ko_ref[0, WINDOW - nt:WINDOW, :] = k_new[0:nt, :]
    vo_ref[0, 0:WINDOW - nt, :] = vc_ref[0, nt:WINDOW, :]
    vo_ref[0, WINDOW - nt:WINDOW, :] = zv[0:nt, :]

    obr_ref[0] = jnp.concatenate([o_a, o_b, o_c, o_d], axis=1)


def _sample_mixer(z3, ya3, s0, hist, kc, vc, p, tables):
    nb = z3.shape[0]
    R = SAMPLE_ROWS
    consts = (*tables, p["glag"], p["lng"], p["lnb"], p["wsc"], p["bsc"],
              p["wsh"], p["cb"], p["clg"], p["clb"], p["qg"], p["kg"])

    def per_seq(shape):
        nd = len(shape)
        return pl.BlockSpec((1,) + tuple(shape[1:]), lambda b, *_: (b,) + (0,) * (nd - 1))

    def const(shape):
        nd = len(shape)
        return pl.BlockSpec(shape, lambda b, *_: (0,) * nd)

    out_shapes = [jax.ShapeDtypeStruct((nb, R, N_BRANCH * BRANCH_WIDTH), F32),
                  jax.ShapeDtypeStruct((nb, QK, GLA_DV), F32),
                  jax.ShapeDtypeStruct((nb, CONV_K - 1, CONV_WIDTH), F32),
                  jax.ShapeDtypeStruct((nb, WINDOW, SWA_KV), F32),
                  jax.ShapeDtypeStruct((nb, WINDOW, SWA_KV), F32),
                  jax.ShapeDtypeStruct((nb, R, GMLP_WIDTH), F32)]
    grid_spec = pltpu.PrefetchScalarGridSpec(
        num_scalar_prefetch=1,
        grid=(nb,),
        in_specs=[per_seq(a.shape) for a in (z3, ya3, s0, hist, kc, vc)]
                 + [const(a.shape) for a in consts],
        out_specs=[per_seq(s.shape) for s in out_shapes],
        scratch_shapes=[pltpu.VMEM((CONV_PAD + R, CONV_WIDTH), F32)])
    return pl.pallas_call(
        _sample_mixer_kernel,
        grid_spec=grid_spec,
        out_shape=out_shapes,
        compiler_params=pltpu.CompilerParams(
            dimension_semantics=("arbitrary",), vmem_limit_bytes=VMEM_LIMIT_BYTES),
        name="sample_mixer",
    )(p["sinks"], z3, ya3, s0, hist, kc, vc, *consts)


def _sample_merge_kernel(x_ref, obr_ref, n1g_ref, wg_ref, bg_ref, wbr_ref, wout_ref, y_ref):
    x = x_ref[...]
    xb = _rms_rows(x, n1g_ref[...]).astype(BF16)
    merged = jnp.zeros(x.shape, F32)
    for i in range(N_BRANCH):
        gate = jax.nn.sigmoid(
            _mm(xb, wg_ref[:, i * D_MODEL:(i + 1) * D_MODEL]) + bg_ref[:, i * D_MODEL:(i + 1) * D_MODEL])
        o_i = obr_ref[:, i * BRANCH_WIDTH:(i + 1) * BRANCH_WIDTH].astype(BF16)
        merged = merged + gate * _mm(o_i, wbr_ref[i])
    y_ref[...] = x + _mm(merged.astype(BF16), wout_ref[...])


def _sample_merge(x2d, obr, p):
    args = (x2d, obr, p["n1g"], p["wg"], p["bg"], p["wbr"], p["wout"])
    return pl.pallas_call(
        _sample_merge_kernel,
        grid=(1,),
        in_specs=[_const_spec(a.shape) for a in args],
        out_specs=_full_spec(x2d.shape),
        out_shape=jax.ShapeDtypeStruct(x2d.shape, F32),
        compiler_params=pltpu.CompilerParams(
            dimension_semantics=("arbitrary",), vmem_limit_bytes=VMEM_LIMIT_BYTES),
        name="sample_merge",
    )(*args)


def _rope_tables(pos):
    half = ROPE_DIM // 2
    inv = jnp.exp(-math.log(ROPE_THETA) * jnp.arange(half, dtype=F32) * (2.0 / ROPE_DIM))
    ang = pos.astype(F32)[:, None] * inv[None, :]
    cos, sin = jnp.cos(ang), jnp.sin(ang)
    n = pos.shape[0]
    pad = jnp.zeros((n, SWA_HEAD_DIM - ROPE_DIM), F32)
    zero = jnp.zeros((n, half), F32)
    c_head = jnp.concatenate([cos, cos, pad + 1.0], axis=1)
    a_head = jnp.concatenate([-sin, zero, pad], axis=1)
    b_head = jnp.concatenate([zero, sin, pad], axis=1)
    rep = LANES // SWA_HEAD_DIM
    return tuple(jnp.tile(tb, (1, rep)) for tb in (c_head, a_head, b_head))


def _row(v):
    return v.reshape(1, -1).astype(F32)


def _layer_params(l, norm1_g, w_in, b_in, w_alpha2, b_alpha, gla_norm_g, gmlp_ln_g, gmlp_ln_b,
                  w_spatial, b_spatial, conv_w, conv_b, conv_ln_g, conv_ln_b, q_norm_g, k_norm_g,
                  sinks, w_gate, b_gate, w_branch, w_out, norm2_g, w_gate_up, w_down):
    n_aq = QK
    o_lr = 2 * QK + 2 * QV
    o_rest = o_lr + GLA_RANK

    def repack(m):
        pad = jnp.zeros(m.shape[:-1] + (LANES - GLA_RANK,), m.dtype)
        return jnp.concatenate([m[..., :o_lr], m[..., o_rest:], m[..., o_lr:o_rest], pad], axis=-1)

    wi = w_in[l]
    nt = 4
    ws = w_spatial[l]
    gw = GMLP_WIDTH // GMLP_GROUPS
    wsc = jnp.stack([
        jnp.pad(jnp.repeat(ws[:, :nt, s].T, gw, axis=1), ((0, SAMPLE_ROWS - nt), (0, 0)))
        for s in range(nt)], axis=0)
    bsc = jnp.pad(jnp.repeat(b_spatial[l][:, :nt].T, gw, axis=1), ((0, SAMPLE_ROWS - nt), (0, 0)))
    cwl = conv_w[l]
    lead = CONV_PAD - (CONV_K - 1)
    wsh = jnp.stack([
        jnp.pad(cwl, ((lead + t, CONV_PAD + SAMPLE_ROWS - lead - t - CONV_K), (0, 0)))
        for t in range(nt)], axis=0)
    return {
        "n1g": _row(norm1_g[l]),
        "win": repack(wi).astype(BF16),
        "bin": repack(b_in[l]).reshape(1, -1).astype(F32),
        "wal": jnp.pad(w_alpha2[l], ((0, LANES - GLA_RANK), (0, 0))).astype(BF16),
        "bal": _row(b_alpha[l]),
        "glag": _row(gla_norm_g[l]),
        "lng": _row(gmlp_ln_g[l]),
        "lnb": _row(gmlp_ln_b[l]),
        "wsp": ws.astype(F32),
        "bspt": b_spatial[l].T.astype(F32),
        "wsc": wsc.astype(F32),
        "bsc": bsc.astype(F32),
        "cw": jnp.pad(cwl, ((0, CONV_PAD - CONV_K), (0, 0))).astype(F32),
        "wsh": wsh.astype(F32),
        "cb": _row(conv_b[l]),
        "clg": _row(conv_ln_g[l]),
        "clb": _row(conv_ln_b[l]),
        "qg": _row(jnp.tile(q_norm_g[l], SWA_HEADS)),
        "kg": _row(jnp.tile(k_norm_g[l], SWA_KV_HEADS)),
        "sinks": sinks[l].astype(F32),
        "wg": w_gate[l].astype(BF16),
        "bg": _row(b_gate[l]),
        "wbr": w_branch[l].astype(BF16),
        "wout": w_out[l].astype(BF16),
        "n2g": _row(norm2_g[l]),
        "wgu": w_gate_up[l].astype(BF16),
        "wd": w_down[l].astype(BF16),
    }


def kernel(x_prompt, x_sample, state_gla, state_conv, cache_swa_k, cache_swa_v, norm1_g, w_in, b_in, w_alpha2, b_alpha, gla_norm_g, gmlp_ln_g, gmlp_ln_b, w_spatial, b_spatial, conv_w, conv_b, conv_ln_g, conv_ln_b, q_norm_g, k_norm_g, sinks, w_gate, b_gate, w_branch, w_out, norm2_g, w_gate_up, w_down):
    bp, lp, _ = x_prompt.shape
    bs, ls, _ = x_sample.shape
    R = SAMPLE_ROWS
    prompt_tables = _rope_tables(jnp.arange(lp))
    sample_tables = tuple(
        jnp.pad(tb, ((0, R - ls), (0, 0))) for tb in _rope_tables(PAST_LEN + jnp.arange(ls)))

    yp = x_prompt
    ys = x_sample.reshape(bs * ls, D_MODEL)
    outs = {k: [] for k in ("gla_p", "gla_s", "conv_p", "conv_s", "kp", "vp", "ks", "vs", "gm")}
    for l in range(DEPTH):
        p = _layer_params(l, norm1_g, w_in, b_in, w_alpha2, b_alpha, gla_norm_g, gmlp_ln_g, gmlp_ln_b,
                          w_spatial, b_spatial, conv_w, conv_b, conv_ln_g, conv_ln_b, q_norm_g,
                          k_norm_g, sinks, w_gate, b_gate, w_branch, w_out, norm2_g, w_gate_up, w_down)
        yp, g1, c1, k1, v1 = _prompt_mixer(yp, p, prompt_tables)
        yp = _ffn(yp.reshape(bp * lp, D_MODEL), p).reshape(bp, lp, D_MODEL)
        outs["gla_p"].append(g1.reshape(bp, GLA_HEADS, GLA_DK, GLA_DV))
        outs["conv_p"].append(c1)
        outs["kp"].append(k1.reshape(bp, WINDOW, SWA_KV_HEADS, SWA_HEAD_DIM))
        outs["vp"].append(v1.reshape(bp, WINDOW, SWA_KV_HEADS, SWA_HEAD_DIM))
        z, ya = _sample_proj(ys, p)
        pad_rows = ((0, 0), (0, R - ls), (0, 0))
        z3 = jnp.pad(z.reshape(bs, ls, IN_COLS_PACKED), pad_rows)
        ya3 = jnp.pad(ya.reshape(bs, ls, QK), pad_rows)
        hist = jnp.pad(state_conv[l], ((0, 0), (CONV_PAD - (CONV_K - 1), 0), (0, 0)))
        obr, g2, c2, k2, v2, gv2 = _sample_mixer(
            z3, ya3, state_gla[l].reshape(bs, QK, GLA_DV), hist,
            cache_swa_k[l].reshape(bs, WINDOW, SWA_KV), cache_swa_v[l].reshape(bs, WINDOW, SWA_KV),
            p, sample_tables)
        ys = _sample_merge(ys, obr[:, :ls].reshape(bs * ls, N_BRANCH * BRANCH_WIDTH), p)
        ys = _ffn(ys, p)
        outs["gla_s"].append(g2.reshape(bs, GLA_HEADS, GLA_DK, GLA_DV))
        outs["conv_s"].append(c2)
        outs["ks"].append(k2.reshape(bs, WINDOW, SWA_KV_HEADS, SWA_HEAD_DIM))
        outs["vs"].append(v2.reshape(bs, WINDOW, SWA_KV_HEADS, SWA_HEAD_DIM))
        outs["gm"].append(gv2[:, :ls])
    st = lambda name: jnp.stack(outs[name], 0)
    return (yp, ys.reshape(bs, ls, D_MODEL), st("gla_p"), st("gla_s"), st("conv_p"), st("conv_s"),
            st("kp"), st("vp"), st("ks"), st("vs"), st("gm"))
```

```python
import functools
import math

import jax
import jax.numpy as jnp
from jax import lax
from jax.experimental import pallas as pl
from jax.experimental.pallas import tpu as pltpu

F32 = jnp.float32
BF16 = jnp.bfloat16

D_MODEL = 1024
DEPTH = 2
PAST_LEN = 16384
GLA_HEADS = 4
GLA_DK = 64
GLA_DV = 128
GLA_RANK = 16
GLA_TAU = 16.0
GLA_CHUNK = 64
GMLP_GROUPS = 4
GMLP_WIDTH = 512
GMLP_CHUNK = 128
CONV_WIDTH = 512
CONV_K = 31
SWA_HEADS = 8
SWA_KV_HEADS = 2
SWA_GROUP = SWA_HEADS // SWA_KV_HEADS
SWA_HEAD_DIM = 64
WINDOW = 128
ROPE_DIM = SWA_HEAD_DIM // 4
ROPE_THETA = 500000.0
N_BRANCH = 4
BRANCH_WIDTH = 512
FFN_HIDDEN = 2816
NORM_EPS = 1e-6
LN_EPS = 1e-5

LANES = 128
SUBLANES = 8
VMEM_LIMIT_BYTES = 56 * 1024 * 1024

QK = GLA_HEADS * GLA_DK
QV = GLA_HEADS * GLA_DV
SWA_Q = SWA_HEADS * SWA_HEAD_DIM
SWA_KV = SWA_KV_HEADS * SWA_HEAD_DIM

C_AQ = 0
C_AK = C_AQ + QK
C_AV = C_AK + QK
C_AR = C_AV + QV
C_BU = C_AR + QV
C_BV = C_BU + GMLP_WIDTH
C_CA = C_BV + GMLP_WIDTH
C_CG = C_CA + CONV_WIDTH
C_DQ = C_CG + CONV_WIDTH
C_DK = C_DQ + SWA_Q
C_DV = C_DK + SWA_KV
C_LR = C_DV + SWA_KV
IN_COLS_PACKED = C_LR + LANES

PROMPT_TILE = 256
FFN_TILE = 512
FFN_CHUNK = 256
CONV_PAD = 32
CONV_LEAD = CONV_PAD - (CONV_K - 1)
SAMPLE_ROWS = 8
SAMPLE_TOKENS = 4
SAMPLE_GROUP = 4


def _mm(a, b):
    return jnp.dot(a, b, preferred_element_type=F32)


def _mm_nt(a, b):
    return lax.dot_general(a, b, (((1,), (1,)), ((), ())), preferred_element_type=F32)


def _mm_tn(a, b):
    return lax.dot_general(a, b, (((0,), (0,)), ((), ())), preferred_element_type=F32)


def _hi_lo(a):
    hi = a.astype(BF16)
    lo = (a - hi.astype(F32)).astype(BF16)
    return hi, lo


def _mm_split_lhs(a_f32, m_bf16):
    hi, lo = _hi_lo(a_f32)
    return _mm(hi, m_bf16) + _mm(lo, m_bf16)


def _rms_rows(x, g):
    return x * lax.rsqrt(jnp.mean(x * x, axis=-1, keepdims=True) + NORM_EPS) * g


def _ln_rows(x, g, b):
    mu = jnp.mean(x, axis=-1, keepdims=True)
    xc = x - mu
    var = jnp.mean(xc * xc, axis=-1, keepdims=True)
    return xc * lax.rsqrt(var + LN_EPS) * g + b


def _gelu(x):
    return 0.5 * x * (1.0 + lax.erf(x * (1.0 / math.sqrt(2.0))))


def _silu(x):
    return x * jax.nn.sigmoid(x)


def _log_sigmoid(x):
    return jnp.minimum(x, 0.0) - jnp.log1p(jnp.exp(-jnp.abs(x)))


def _iota(shape, dim):
    return lax.broadcasted_iota(jnp.int32, shape, dim)


def _group_ones(n, group, dtype=BF16):
    return (_iota((n, n), 0) // group == _iota((n, n), 1) // group).astype(dtype)


def _group_mean_sq(x, group):
    return _mm_split_lhs(x * x, _group_ones(x.shape[-1], group)) * (1.0 / group)


def _rope_slab(x, c, a, b):
    return x * c + pltpu.roll(x, LANES - ROPE_DIM // 2, 1) * a + pltpu.roll(x, ROPE_DIM // 2, 1) * b


def _rope(x, c, a, b):
    n = x.shape[-1] // LANES
    if n == 1:
        return _rope_slab(x, c, a, b)
    return jnp.concatenate(
        [_rope_slab(x[:, i * LANES:(i + 1) * LANES], c, a, b) for i in range(n)], axis=1)


def _qk_norm_rope(z, g, c, a, b):
    return _rope(z * lax.rsqrt(_group_mean_sq(z, SWA_HEAD_DIM) + NORM_EPS) * g, c, a, b)


def _head_rms_gla(o, g):
    return jnp.concatenate(
        [_rms_rows(o[:, h * GLA_DV:(h + 1) * GLA_DV], g) for h in range(GLA_HEADS)], axis=1)


def _swa_q_for_group(q, g, i):
    hq = g * SWA_GROUP + i
    slab = q[:, (hq // 2) * LANES:(hq // 2 + 1) * LANES]
    if hq % 2 != g:
        slab = pltpu.roll(slab, SWA_HEAD_DIM, 1)
    lane_head = _iota(slab.shape, 1) // SWA_HEAD_DIM
    return jnp.where(lane_head == g, slab, 0.0)


def _swa_assemble(outs, rows):
    lane_half = _iota((rows, LANES), 1) // SWA_HEAD_DIM
    slabs = []
    for s in range(SWA_HEADS // 2):
        pair = []
        for hq in (2 * s, 2 * s + 1):
            g, i = hq // SWA_GROUP, hq % SWA_GROUP
            o = outs[g][i]
            if hq % 2 != g:
                o = pltpu.roll(o, SWA_HEAD_DIM, 1)
            pair.append(o)
        slabs.append(jnp.where(lane_half == 0, pair[0], pair[1]))
    return jnp.concatenate(slabs, axis=1)


def _sink_column(sinks_ref, base, rows, rows_per_head):
    row_head = _iota((rows, 1), 0) // rows_per_head
    sink = jnp.zeros((rows, 1), F32)
    for i in range(SWA_GROUP):
        sink = jnp.where(row_head == i, sinks_ref[base + i], sink)
    return sink


def _prompt_mixer_kernel(
        sinks_ref,
        x_ref, rc_ref, ra_ref, rb_ref,
        n1g_ref, win_ref, bin_ref, wal_ref, bal_ref, glag_ref,
        lng_ref, lnb_ref, wsp_ref, bspt_ref,
        cw_ref, cb_ref, clg_ref, clb_ref,
        qg_ref, kg_ref,
        wg_ref, bg_ref, wbr_ref, wout_ref,
        y_ref, gla_ref, conv_ref, ko_ref, vo_ref,
        sbd_ref, hbuf_ref, hsh_ref, kfull_ref, vfull_ref,
        *, layer):
    T = PROMPT_TILE
    t = pl.program_id(1)
    last = pl.num_programs(1) - 1

    @pl.when(t == 0)
    def _():
        sbd_ref[...] = jnp.zeros_like(sbd_ref)
        hbuf_ref[0:CONV_PAD, :] = jnp.zeros((CONV_PAD, CONV_WIDTH), F32)
        kfull_ref[0:WINDOW, :] = jnp.zeros((WINDOW, SWA_KV), F32)
        vfull_ref[0:WINDOW, :] = jnp.zeros((WINDOW, SWA_KV), F32)

    x = x_ref[0]
    xb = _rms_rows(x, n1g_ref[...]).astype(BF16)

    def proj(c0, width):
        return _mm(xb, win_ref[:, c0:c0 + width]) + bin_ref[:, c0:c0 + width]

    def gated(i, o_b16):
        cols = slice(i * D_MODEL, (i + 1) * D_MODEL)
        gate = jax.nn.sigmoid(_mm(xb, wg_ref[:, cols]) + bg_ref[:, cols])
        return gate * _mm(o_b16, wbr_ref[i])

    h = proj(C_CA, CONV_WIDTH) * jax.nn.sigmoid(proj(C_CG, CONV_WIDTH))
    hbuf_ref[CONV_PAD:CONV_PAD + T, :] = h
    taps_of = {r: [kk for kk in range(CONV_K) if (CONV_LEAD + kk) % SUBLANES == r] for r in range(SUBLANES)}
    for r in range(1, SUBLANES):
        span = max(CONV_LEAD + kk for kk in taps_of[r]) - r
        hsh_ref[r - 1, 0:span + T, :] = hbuf_ref[r:r + span + T, :]

    q = proj(C_AQ, QK) * (GLA_DK ** -0.5)
    k = proj(C_AK, QK)
    v = proj(C_AV, QV)
    lr = proj(C_LR, LANES)
    la = _log_sigmoid(_mm(lr.astype(BF16), wal_ref[...]) + bal_ref[...]) * (1.0 / GLA_TAU)

    ri = _iota((T, T), 0)
    ci = _iota((T, T), 1)
    same_chunk = (ri // GLA_CHUNK) == (ci // GLA_CHUNK)
    m_blk = same_chunk.astype(BF16)
    m_tril = jnp.logical_and(same_chunk, ci <= ri).astype(BF16)
    la_hi, la_lo = _hi_lo(la)
    bcum = _mm(m_tril, la_hi) + _mm(m_tril, la_lo)
    btot = _mm(m_blk, la_hi) + _mm(m_blk, la_lo)
    q_in = (q * jnp.exp(bcum)).astype(BF16)
    k_out = (k * jnp.exp(-bcum)).astype(BF16)
    k_dec = (k * jnp.exp(btot - bcum)).astype(BF16)
    v_b = v.astype(BF16)

    lane_head = _iota((GLA_CHUNK, QK), 1) // GLA_DK
    causal = (_iota((QK, GLA_CHUNK), 0) % GLA_CHUNK) >= _iota((QK, GLA_CHUNK), 1)
    bd_mask = (_iota((QK, QV), 0) // GLA_DK) == (_iota((QK, QV), 1) // GLA_DV)
    ones_tn = jnp.ones((GLA_CHUNK, GLA_DV), BF16)
    o_chunks = []
    for c in range(T // GLA_CHUNK):
        rs = slice(c * GLA_CHUNK, (c + 1) * GLA_CHUNK)
        qc, kc, kd, vc = q_in[rs], k_out[rs], k_dec[rs], v_b[rs]
        q_stack = jnp.concatenate(
            [jnp.where(lane_head == hh, qc, jnp.zeros_like(qc)) for hh in range(GLA_HEADS)], axis=0)
        att = jnp.where(causal, _mm_nt(q_stack, kc), 0.0)
        pv = _mm(att.astype(BF16), vc)
        o_intra = jnp.concatenate(
            [pv[hh * GLA_CHUNK:(hh + 1) * GLA_CHUNK, hh * GLA_DV:(hh + 1) * GLA_DV]
             for hh in range(GLA_HEADS)], axis=1)
        s_bd = sbd_ref[...]
        o_chunks.append(o_intra + _mm(qc, s_bd.astype(BF16)))
        dsum = _mm_tn(la_hi[rs], ones_tn) + _mm_tn(la_lo[rs], ones_tn)
        dcol = jnp.exp(dsum)
        dcol4 = jnp.concatenate([dcol] * GLA_HEADS, axis=1)
        upd = _mm_tn(kd, vc)
        sbd_ref[...] = dcol4 * s_bd + jnp.where(bd_mask, upd, 0.0)
    o_gla = jnp.concatenate(o_chunks, axis=0)
    r = proj(C_AR, QV)
    merged = gated(0, (_head_rms_gla(o_gla, glag_ref[...]) * _silu(r)).astype(BF16))

    acc = jnp.broadcast_to(cb_ref[...], (T, CONV_WIDTH))
    for r_ in range(SUBLANES):
        for kk in taps_of[r_]:
            a = CONV_LEAD + kk - r_
            rows = hbuf_ref[a:a + T, :] if r_ == 0 else hsh_ref[r_ - 1, a:a + T, :]
            acc = acc + cw_ref[kk:kk + 1, :] * rows
    merged = merged + gated(2, _silu(_ln_rows(acc, clg_ref[...], clb_ref[...])).astype(BF16))
    hbuf_ref[0:CONV_PAD, :] = hbuf_ref[T:T + CONV_PAD, :]

    u = _gelu(proj(C_BU, GMLP_WIDTH))
    gv = _ln_rows(_gelu(proj(C_BV, GMLP_WIDTH)), lng_ref[...], lnb_ref[...])
    gv_b = gv.astype(BF16)
    gw = GMLP_WIDTH // GMLP_GROUPS
    tril = _iota((GMLP_CHUNK, GMLP_CHUNK), 1) <= _iota((GMLP_CHUNK, GMLP_CHUNK), 0)
    bspt = bspt_ref[...]
    s_rows = []
    for n in range(T // GMLP_CHUNK):
        rs = slice(n * GMLP_CHUNK, (n + 1) * GMLP_CHUNK)
        cols = []
        for g in range(GMLP_GROUPS):
            w = jnp.where(tril, wsp_ref[g], 0.0).astype(BF16)
            sg = _mm(w, gv_b[rs, g * gw:(g + 1) * gw])
            cols.append(sg + jnp.broadcast_to(bspt[:, g:g + 1], (GMLP_CHUNK, gw)))
        s_rows.append(jnp.concatenate(cols, axis=1))
    merged = merged + gated(1, (u * jnp.concatenate(s_rows, axis=0)).astype(BF16))

    rc, ra, rb = rc_ref[...], ra_ref[...], rb_ref[...]
    qd = _qk_norm_rope(proj(C_DQ, SWA_Q), qg_ref[...], rc, ra, rb)
    kfull_ref[WINDOW:WINDOW + T, :] = _qk_norm_rope(proj(C_DK, SWA_KV), kg_ref[...], rc, ra, rb)
    vfull_ref[WINDOW:WINDOW + T, :] = proj(C_DV, SWA_KV)

    rows = SWA_GROUP * WINDOW
    qi = _iota((rows, 2 * WINDOW), 0) % WINDOW
    kj = _iota((rows, 2 * WINDOW), 1)
    own_ok = jnp.logical_and(kj >= WINDOW, kj - WINDOW <= qi)
    prev_ok = jnp.logical_and(kj < WINDOW, kj > qi)
    scale = SWA_HEAD_DIM ** -0.5
    o_blocks = []
    for qb in range(T // WINDOW):
        rs = slice(qb * WINDOW, (qb + 1) * WINDOW)
        kblk = kfull_ref[qb * WINDOW:(qb + 2) * WINDOW, :].astype(BF16)
        vblk = vfull_ref[qb * WINDOW:(qb + 2) * WINDOW, :].astype(BF16)
        if qb == 0:
            valid = jnp.logical_or(own_ok, jnp.logical_and(prev_ok, t > 0))
        else:
            valid = jnp.logical_or(own_ok, prev_ok)
        outs = []
        for g in range(SWA_KV_HEADS):
            q_stack = jnp.concatenate(
                [_swa_q_for_group(qd[rs], g, i) for i in range(SWA_GROUP)], axis=0).astype(BF16)
            s = jnp.where(valid, _mm_nt(q_stack, kblk) * scale, -jnp.inf)
            sink = _sink_column(sinks_ref, layer * SWA_HEADS + g * SWA_GROUP, rows, WINDOW)
            m = jnp.maximum(jnp.max(s, axis=-1, keepdims=True), sink)
            p = jnp.exp(s - m)
            den = jnp.sum(p, axis=-1, keepdims=True) + jnp.exp(sink - m)
            o = _mm(p.astype(BF16), vblk) / den
            outs.append([o[i * WINDOW:(i + 1) * WINDOW] for i in range(SWA_GROUP)])
        o_blocks.append(_swa_assemble(outs, WINDOW))
    merged = merged + gated(3, jnp.concatenate(o_blocks, axis=0).astype(BF16))
    kfull_ref[0:WINDOW, :] = kfull_ref[T:T + WINDOW, :]
    vfull_ref[0:WINDOW, :] = vfull_ref[T:T + WINDOW, :]

    y_ref[0] = x + _mm(merged.astype(BF16), wout_ref[...])

    @pl.when(t == last)
    def _():
        conv_ref[0] = hbuf_ref[CONV_LEAD:CONV_PAD, :]
        ko_ref[0] = kfull_ref[0:WINDOW, :]
        vo_ref[0] = vfull_ref[0:WINDOW, :]
        sb = sbd_ref[...]
        gla_ref[0] = jnp.concatenate(
            [sb[hh * GLA_DK:(hh + 1) * GLA_DK, hh * GLA_DV:(hh + 1) * GLA_DV]
             for hh in range(GLA_HEADS)], axis=0)


def _layer_spec(arr, layer):
    nd = arr.ndim
    return pl.BlockSpec((None,) + tuple(arr.shape[1:]), lambda *_: (layer,) + (0,) * (nd - 1),
                        pipeline_mode=pl.Buffered(1))


def _full_spec(shape):
    nd = len(shape)
    return pl.BlockSpec(shape, lambda *_: (0,) * nd)


def _compiler_params(n_axes):
    return pltpu.CompilerParams(
        dimension_semantics=("arbitrary",) * n_axes, vmem_limit_bytes=VMEM_LIMIT_BYTES)


def _prompt_mixer(x, p, tables, layer):
    B, L, _ = x.shape
    T = PROMPT_TILE
    weights = tuple(p[n] for n in (
        "n1g", "win", "bin", "wal", "bal", "glag", "lng", "lnb", "wsp", "bspt",
        "cw", "cb", "clg", "clb", "qg", "kg", "wg", "bg", "wbr", "wout"))
    tab_spec = pl.BlockSpec((T, LANES), lambda b, t, *_: (t, 0))
    grid_spec = pltpu.PrefetchScalarGridSpec(
        num_scalar_prefetch=1,
        grid=(B, L // T),
        in_specs=[pl.BlockSpec((1, T, D_MODEL), lambda b, t, *_: (b, t, 0)),
                  tab_spec, tab_spec, tab_spec]
                 + [_layer_spec(w, layer) for w in weights],
        out_specs=[pl.BlockSpec((1, T, D_MODEL), lambda b, t, *_: (b, t, 0)),
                   pl.BlockSpec((1, QK, GLA_DV), lambda b, t, *_: (b, 0, 0)),
                   pl.BlockSpec((1, CONV_K - 1, CONV_WIDTH), lambda b, t, *_: (b, 0, 0)),
                   pl.BlockSpec((1, WINDOW, SWA_KV), lambda b, t, *_: (b, 0, 0)),
                   pl.BlockSpec((1, WINDOW, SWA_KV), lambda b, t, *_: (b, 0, 0))],
        scratch_shapes=[pltpu.VMEM((QK, QV), F32),
                        pltpu.VMEM((T + CONV_PAD, CONV_WIDTH), F32),
                        pltpu.VMEM((SUBLANES - 1, T + CONV_PAD, CONV_WIDTH), F32),
                        pltpu.VMEM((T + WINDOW, SWA_KV), F32),
                        pltpu.VMEM((T + WINDOW, SWA_KV), F32)])
    return pl.pallas_call(
        functools.partial(_prompt_mixer_kernel, layer=layer),
        grid_spec=grid_spec,
        out_shape=[jax.ShapeDtypeStruct((B, L, D_MODEL), F32),
                   jax.ShapeDtypeStruct((B, QK, GLA_DV), F32),
                   jax.ShapeDtypeStruct((B, CONV_K - 1, CONV_WIDTH), F32),
                   jax.ShapeDtypeStruct((B, WINDOW, SWA_KV), F32),
                   jax.ShapeDtypeStruct((B, WINDOW, SWA_KV), F32)],
        compiler_params=_compiler_params(2),
        name="prompt_mixer",
    )(p["sinks"], x, *tables, *weights)


def _ffn_kernel(x_ref, g_ref, wgu_ref, wd_ref, y_ref, h_ref):
    x = x_ref[...]
    xb = _rms_rows(x, g_ref[...]).astype(BF16)
    for c in range(FFN_HIDDEN // FFN_CHUNK):
        c0 = c * FFN_CHUNK
        gate = _mm(xb, wgu_ref[:, c0:c0 + FFN_CHUNK])
        up = _mm(xb, wgu_ref[:, FFN_HIDDEN + c0:FFN_HIDDEN + c0 + FFN_CHUNK])
        h_ref[:, c0:c0 + FFN_CHUNK] = (_silu(gate) * up).astype(BF16)
    y_ref[...] = x + _mm(h_ref[...], wd_ref[...])


def _ffn(x2d, p, layer):
    n = x2d.shape[0]
    tile = min(FFN_TILE, n)
    return pl.pallas_call(
        _ffn_kernel,
        grid=(n // tile,),
        in_specs=[pl.BlockSpec((tile, D_MODEL), lambda i: (i, 0)),
                  _layer_spec(p["n2g"], layer), _layer_spec(p["wgu"], layer), _layer_spec(p["wd"], layer)],
        out_specs=pl.BlockSpec((tile, D_MODEL), lambda i: (i, 0)),
        out_shape=jax.ShapeDtypeStruct((n, D_MODEL), F32),
        scratch_shapes=[pltpu.VMEM((tile, FFN_HIDDEN), BF16)],
        compiler_params=_compiler_params(1),
        name="swiglu",
    )(x2d, p["n2g"], p["wgu"], p["wd"])


def _sample_proj_kernel(x_ref, n1g_ref, win_ref, bin_ref, wal_ref, bal_ref, z_ref, ya_ref):
    xb = _rms_rows(x_ref[...], n1g_ref[...]).astype(BF16)
    z = _mm(xb, win_ref[...]) + bin_ref[...]
    z_ref[...] = z
    ya_ref[...] = _mm(z[:, C_LR:C_LR + LANES].astype(BF16), wal_ref[...]) + bal_ref[...]


def _sample_proj(x2d, p, layer):
    n = x2d.shape[0]
    weights = tuple(p[k] for k in ("n1g", "win", "bin", "wal", "bal"))
    return pl.pallas_call(
        _sample_proj_kernel,
        grid=(1,),
        in_specs=[_full_spec(x2d.shape)] + [_layer_spec(w, layer) for w in weights],
        out_specs=[_full_spec((n, IN_COLS_PACKED)), _full_spec((n, QK))],
        out_shape=[jax.ShapeDtypeStruct((n, IN_COLS_PACKED), F32),
                   jax.ShapeDtypeStruct((n, QK), F32)],
        compiler_params=_compiler_params(1),
        name="sample_proj",
    )(x2d, *weights)


def _sample_sequence(j, layer, sinks_ref, z_ref, ya_ref, s0_ref, hist_ref, kc_ref, vc_ref,
                     rc, ra, rb, glag, lng, lnb, wsc_ref, bsc, wsh_ref, cb, clg, clb, qg, kg,
                     obr_ref, s1_ref, convo_ref, ko_ref, vo_ref, gvo_ref, hext_ref):
    R = SAMPLE_ROWS
    nt = SAMPLE_TOKENS
    z = z_ref[j]
    row = _iota((R, 1), 0)

    q = z[:, C_AQ:C_AQ + QK] * (GLA_DK ** -0.5)
    k = z[:, C_AK:C_AK + QK]
    v = z[:, C_AV:C_AV + QV]
    r = z[:, C_AR:C_AR + QV]
    la = jnp.where(row < nt, _log_sigmoid(ya_ref[j]) * (1.0 / GLA_TAU), 0.0)
    bcum = jnp.zeros_like(la)
    for s in range(nt):
        bcum = bcum + jnp.where(row >= s, jnp.broadcast_to(la[s:s + 1, :], la.shape), 0.0)
    btot = jnp.broadcast_to(bcum[nt - 1:nt, :], la.shape)
    q_in = q * jnp.exp(bcum)
    k_out = k * jnp.exp(-bcum)
    k_dec = k * jnp.exp(btot - bcum)
    s0 = s0_ref[j]
    s0_b = s0.astype(BF16)

    head_sel = (_iota((QK, LANES), 0) // GLA_DK == _iota((QK, LANES), 1)).astype(BF16)
    head_exp = (_iota((LANES, QV), 0) == _iota((LANES, QV), 1) // GLA_DV).astype(BF16)
    o_intra = jnp.zeros((R, QV), F32)
    for s in range(nt):
        prod = q_in * jnp.broadcast_to(k_out[s:s + 1, :], q_in.shape)
        att_s = _mm_split_lhs(prod, head_sel)
        att_e = _mm(att_s.astype(BF16), head_exp)
        o_intra = o_intra + jnp.where(row >= s, att_e, 0.0) * jnp.broadcast_to(v[s:s + 1, :], att_e.shape)
    lane_head = _iota((R, QK), 1) // GLA_DK
    o_state = jnp.concatenate(
        [_mm(jnp.where(lane_head == hh, q_in, 0.0).astype(BF16), s0_b) for hh in range(GLA_HEADS)], axis=1)
    o_a = _head_rms_gla(o_intra + o_state, glag) * _silu(r)

    la_hi = la.astype(BF16).astype(F32)
    la_lo = la - la_hi
    ones_tn = jnp.ones((R, GLA_DV), F32)
    dcol = jnp.exp(_mm_tn(la_hi, ones_tn) + _mm_tn(la_lo, ones_tn))
    upd = _mm_tn(k_dec.astype(BF16).astype(F32), v.astype(BF16).astype(F32))
    s1_ref[j] = dcol * s0 + jnp.concatenate(
        [upd[hh * GLA_DK:(hh + 1) * GLA_DK, hh * GLA_DV:(hh + 1) * GLA_DV] for hh in range(GLA_HEADS)],
        axis=0)

    u = _gelu(z[:, C_BU:C_BU + GMLP_WIDTH])
    gv = _ln_rows(_gelu(z[:, C_BV:C_BV + GMLP_WIDTH]), lng, lnb)
    gvo_ref[j] = gv
    sp = bsc
    for s in range(nt):
        sp = sp + jnp.where(row >= s, wsc_ref[s], 0.0) * jnp.broadcast_to(gv[s:s + 1, :], gv.shape)
    o_b = u * sp

    h = z[:, C_CA:C_CA + CONV_WIDTH] * jax.nn.sigmoid(z[:, C_CG:C_CG + CONV_WIDTH])
    hext_ref[j, 0:CONV_PAD, :] = hist_ref[j]
    hext_ref[j, CONV_PAD:CONV_PAD + R, :] = h
    hext = hext_ref[j]
    conv = jnp.zeros((R, CONV_WIDTH), F32)
    for tt in range(nt):
        c_t = jnp.sum(hext * wsh_ref[tt], axis=0, keepdims=True)
        conv = jnp.where(row == tt, jnp.broadcast_to(c_t, conv.shape), conv)
    o_c = _silu(_ln_rows(conv + cb, clg, clb))
    convo_ref[j] = hext_ref[j, CONV_PAD + nt - (CONV_K - 1):CONV_PAD + nt, :]

    zv = z[:, C_DV:C_DV + SWA_KV]
    qd = _qk_norm_rope(z[:, C_DQ:C_DQ + SWA_Q], qg, rc, ra, rb)
    k_new = _qk_norm_rope(z[:, C_DK:C_DK + SWA_KV], kg, rc, ra, rb)
    k_hist_b = kc_ref[j].astype(BF16)
    v_hist_b = vc_ref[j].astype(BF16)

    rows = SWA_GROUP * R
    qt = _iota((rows, WINDOW), 0) % R
    hist_ok = _iota((rows, WINDOW), 1) > qt
    qt1 = _iota((rows, 1), 0) % R
    scale = SWA_HEAD_DIM ** -0.5
    outs = []
    for g in range(SWA_KV_HEADS):
        q_stack = jnp.concatenate([_swa_q_for_group(qd, g, i) for i in range(SWA_GROUP)], axis=0)
        s_hist = jnp.where(hist_ok, _mm_nt(q_stack.astype(BF16), k_hist_b) * scale, -jnp.inf)
        s_new = []
        for s in range(nt):
            sn = jnp.sum(q_stack * jnp.broadcast_to(k_new[s:s + 1, :], q_stack.shape),
                         axis=-1, keepdims=True) * scale
            s_new.append(jnp.where(qt1 >= s, sn, -jnp.inf))
        sink = _sink_column(sinks_ref, layer * SWA_HEADS + g * SWA_GROUP, rows, R)
        m = jnp.maximum(jnp.max(s_hist, axis=-1, keepdims=True), sink)
        for s in range(nt):
            m = jnp.maximum(m, s_new[s])
        p_hist = jnp.exp(s_hist - m)
        den = jnp.sum(p_hist, axis=-1, keepdims=True) + jnp.exp(sink - m)
        o = _mm(p_hist.astype(BF16), v_hist_b)
        for s in range(nt):
            p_s = jnp.exp(s_new[s] - m)
            den = den + p_s
            o = o + p_s * jnp.broadcast_to(zv[s:s + 1, :], o.shape)
        o = o / den
        outs.append([o[i * R:(i + 1) * R] for i in range(SWA_GROUP)])
    o_d = _swa_assemble(outs, R)

    ko_ref[j, 0:WINDOW - nt, :] = kc_ref[j, nt:WINDOW, :]
    ko_ref[j, WINDOW - nt:WINDOW, :] = k_new[0:nt, :]
    vo_ref[j, 0:WINDOW - nt, :] = vc_ref[j, nt:WINDOW, :]
    vo_ref[j, WINDOW - nt:WINDOW, :] = zv[0:nt, :]

    obr_ref[j] = jnp.concatenate([o_a, o_b, o_c, o_d], axis=1)


def _sample_mixer_kernel(
        sinks_ref,
        z_ref, ya_ref, s0_ref, hist_ref, kc_ref, vc_ref,
        rc_ref, ra_ref, rb_ref,
        glag_ref, lng_ref, lnb_ref, wsc_ref, bsc_ref,
        wsh_ref, cb_ref, clg_ref, clb_ref, qg_ref, kg_ref,
        obr_ref, s1_ref, convo_ref, ko_ref, vo_ref, gvo_ref,
        hext_ref, *, layer):
    consts = (rc_ref[...], ra_ref[...], rb_ref[...], glag_ref[...], lng_ref[...], lnb_ref[...])
    for j in range(SAMPLE_GROUP):
        _sample_sequence(
            j, layer, sinks_ref, z_ref, ya_ref, s0_ref, hist_ref, kc_ref, vc_ref,
            *consts, wsc_ref, bsc_ref[...], wsh_ref, cb_ref[...], clg_ref[...], clb_ref[...],
            qg_ref[...], kg_ref[...],
            obr_ref, s1_ref, convo_ref, ko_ref, vo_ref, gvo_ref, hext_ref)


def _sample_mixer(z3, ya3, s0, hist, kc, vc, p, tables, layer):
    nb = z3.shape[0]
    R = SAMPLE_ROWS
    G = SAMPLE_GROUP
    weights = tuple(p[n] for n in ("glag", "lng", "lnb", "wsc", "bsc", "wsh", "cb", "clg", "clb", "qg", "kg"))

    def per_group(shape):
        nd = len(shape)
        return pl.BlockSpec((G,) + tuple(shape[1:]), lambda b, *_: (b,) + (0,) * (nd - 1))

    def layer_block(arr):
        nd = arr.ndim
        return pl.BlockSpec((None,) + tuple(arr.shape[1:]), lambda b, *_: (layer,) + (0,) * (nd - 1))

    out_shapes = [jax.ShapeDtypeStruct((nb, R, N_BRANCH * BRANCH_WIDTH), F32),
                  jax.ShapeDtypeStruct((nb, QK, GLA_DV), F32),
                  jax.ShapeDtypeStruct((nb, CONV_K - 1, CONV_WIDTH), F32),
                  jax.ShapeDtypeStruct((nb, WINDOW, SWA_KV), F32),
                  jax.ShapeDtypeStruct((nb, WINDOW, SWA_KV), F32),
                  jax.ShapeDtypeStruct((nb, R, GMLP_WIDTH), F32)]
    grid_spec = pltpu.PrefetchScalarGridSpec(
        num_scalar_prefetch=1,
        grid=(nb // G,),
        in_specs=[per_group(a.shape) for a in (z3, ya3, s0, hist, kc, vc)]
                 + [_full_spec(tb.shape) for tb in tables]
                 + [layer_block(w) for w in weights],
        out_specs=[per_group(s.shape) for s in out_shapes],
        scratch_shapes=[pltpu.VMEM((G, CONV_PAD + R, CONV_WIDTH), F32)])
    return pl.pallas_call(
        functools.partial(_sample_mixer_kernel, layer=layer),
        grid_spec=grid_spec,
        out_shape=out_shapes,
        compiler_params=_compiler_params(1),
        name="sample_mixer",
    )(p["sinks"], z3, ya3, s0, hist, kc, vc, *tables, *weights)


def _sample_merge_kernel(x_ref, obr_ref, n1g_ref, wg_ref, bg_ref, wbr_ref, wout_ref, y_ref):
    x = x_ref[...]
    xb = _rms_rows(x, n1g_ref[...]).astype(BF16)
    merged = jnp.zeros(x.shape, F32)
    for i in range(N_BRANCH):
        gate = jax.nn.sigmoid(
            _mm(xb, wg_ref[:, i * D_MODEL:(i + 1) * D_MODEL]) + bg_ref[:, i * D_MODEL:(i + 1) * D_MODEL])
        o_i = obr_ref[:, i * BRANCH_WIDTH:(i + 1) * BRANCH_WIDTH].astype(BF16)
        merged = merged + gate * _mm(o_i, wbr_ref[i])
    y_ref[...] = x + _mm(merged.astype(BF16), wout_ref[...])


def _sample_merge(x2d, obr, p, layer):
    weights = tuple(p[k] for k in ("n1g", "wg", "bg", "wbr", "wout"))
    return pl.pallas_call(
        _sample_merge_kernel,
        grid=(1,),
        in_specs=[_full_spec(x2d.shape), _full_spec(obr.shape)] + [_layer_spec(w, layer) for w in weights],
        out_specs=_full_spec(x2d.shape),
        out_shape=jax.ShapeDtypeStruct(x2d.shape, F32),
        compiler_params=_compiler_params(1),
        name="sample_merge",
    )(x2d, obr, *weights)


def _rope_tables(pos):
    half = ROPE_DIM // 2
    inv = jnp.exp(-math.log(ROPE_THETA) * jnp.arange(half, dtype=F32) * (2.0 / ROPE_DIM))
    ang = pos.astype(F32)[:, None] * inv[None, :]
    cos, sin = jnp.cos(ang), jnp.sin(ang)
    n = pos.shape[0]
    pad = jnp.zeros((n, SWA_HEAD_DIM - ROPE_DIM), F32)
    zero = jnp.zeros((n, half), F32)
    c_head = jnp.concatenate([cos, cos, pad + 1.0], axis=1)
    a_head = jnp.concatenate([-sin, zero, pad], axis=1)
    b_head = jnp.concatenate([zero, sin, pad], axis=1)
    rep = LANES // SWA_HEAD_DIM
    return tuple(jnp.tile(tb, (1, rep)) for tb in (c_head, a_head, b_head))


def _rows(v):
    return v.reshape(v.shape[0], 1, -1).astype(F32)


def _stacked_params(norm1_g, w_in, b_in, w_alpha2, b_alpha, gla_norm_g, gmlp_ln_g, gmlp_ln_b,
                    w_spatial, b_spatial, conv_w, conv_b, conv_ln_g, conv_ln_b, q_norm_g, k_norm_g,
                    sinks, w_gate, b_gate, w_branch, w_out, norm2_g, w_gate_up, w_down):
    o_lr = 2 * QK + 2 * QV
    o_rest = o_lr + GLA_RANK

    def repack(m, dtype):
        pad = jnp.zeros(m.shape[:-1] + (LANES - GLA_RANK,), dtype)
        return jnp.concatenate(
            [m[..., :o_lr].astype(dtype), m[..., o_rest:].astype(dtype),
             m[..., o_lr:o_rest].astype(dtype), pad], axis=-1)

    nt = SAMPLE_TOKENS
    gw = GMLP_WIDTH // GMLP_GROUPS
    pad_rows = ((0, 0), (0, SAMPLE_ROWS - nt), (0, 0))
    wsc = jnp.stack([
        jnp.pad(jnp.repeat(jnp.swapaxes(w_spatial[:, :, :nt, s], 1, 2), gw, axis=2), pad_rows)
        for s in range(nt)], axis=1)
    bsc = jnp.pad(jnp.repeat(jnp.swapaxes(b_spatial[:, :, :nt], 1, 2), gw, axis=2), pad_rows)
    wsh = jnp.stack([
        jnp.pad(conv_w, ((0, 0), (CONV_LEAD + t, CONV_PAD + SAMPLE_ROWS - CONV_LEAD - t - CONV_K), (0, 0)))
        for t in range(nt)], axis=1)
    return {
        "n1g": _rows(norm1_g),
        "win": repack(w_in, BF16),
        "bin": repack(b_in, F32).reshape(DEPTH, 1, -1),
        "wal": jnp.pad(w_alpha2, ((0, 0), (0, LANES - GLA_RANK), (0, 0))).astype(BF16),
        "bal": _rows(b_alpha),
        "glag": _rows(gla_norm_g),
        "lng": _rows(gmlp_ln_g),
        "lnb": _rows(gmlp_ln_b),
        "wsp": w_spatial.astype(F32),
        "bspt": jnp.swapaxes(b_spatial, 1, 2).astype(F32),
        "wsc": wsc.astype(F32),
        "bsc": bsc.astype(F32),
        "cw": jnp.pad(conv_w, ((0, 0), (0, CONV_PAD - CONV_K), (0, 0))).astype(F32),
        "wsh": wsh.astype(F32),
        "cb": _rows(conv_b),
        "clg": _rows(conv_ln_g),
        "clb": _rows(conv_ln_b),
        "qg": _rows(jnp.tile(q_norm_g, (1, SWA_HEADS))),
        "kg": _rows(jnp.tile(k_norm_g, (1, SWA_KV_HEADS))),
        "sinks": sinks.reshape(-1).astype(F32),
        "wg": w_gate.astype(BF16),
        "bg": _rows(b_gate),
        "wbr": w_branch.astype(BF16),
        "wout": w_out.astype(BF16),
        "n2g": _rows(norm2_g),
        "wgu": w_gate_up.astype(BF16),
        "wd": w_down.astype(BF16),
    }


def kernel(x_prompt, x_sample, state_gla, state_conv, cache_swa_k, cache_swa_v, norm1_g, w_in, b_in, w_alpha2, b_alpha, gla_norm_g, gmlp_ln_g, gmlp_ln_b, w_spatial, b_spatial, conv_w, conv_b, conv_ln_g, conv_ln_b, q_norm_g, k_norm_g, sinks, w_gate, b_gate, w_branch, w_out, norm2_g, w_gate_up, w_down):
    bp, lp, _ = x_prompt.shape
    bs, ls, _ = x_sample.shape
    R = SAMPLE_ROWS
    prompt_tables = _rope_tables(jnp.arange(lp))
    sample_tables = tuple(
        jnp.pad(tb, ((0, R - ls), (0, 0))) for tb in _rope_tables(PAST_LEN + jnp.arange(ls)))
    p = _stacked_params(norm1_g, w_in, b_in, w_alpha2, b_alpha, gla_norm_g, gmlp_ln_g, gmlp_ln_b,
                        w_spatial, b_spatial, conv_w, conv_b, conv_ln_g, conv_ln_b, q_norm_g,
                        k_norm_g, sinks, w_gate, b_gate, w_branch, w_out, norm2_g, w_gate_up, w_down)

    yp = x_prompt
    ys = x_sample.reshape(bs * ls, D_MODEL)
    outs = {k: [] for k in ("gla_p", "gla_s", "conv_p", "conv_s", "kp", "vp", "ks", "vs", "gm")}
    for l in range(DEPTH):
        yp, g1, c1, k1, v1 = _prompt_mixer(yp, p, prompt_tables, l)
        yp = _ffn(yp.reshape(bp * lp, D_MODEL), p, l).reshape(bp, lp, D_MODEL)
        outs["gla_p"].append(g1.reshape(bp, GLA_HEADS, GLA_DK, GLA_DV))
        outs["conv_p"].append(c1)
        outs["kp"].append(k1.reshape(bp, WINDOW, SWA_KV_HEADS, SWA_HEAD_DIM))
        outs["vp"].append(v1.reshape(bp, WINDOW, SWA_KV_HEADS, SWA_HEAD_DIM))
        z, ya = _sample_proj(ys, p, l)
        pad_rows = ((0, 0), (0, R - ls), (0, 0))
        z3 = jnp.pad(z.reshape(bs, ls, IN_COLS_PACKED), pad_rows)
        ya3 = jnp.pad(ya.reshape(bs, ls, QK), pad_rows)
        hist = jnp.pad(state_conv[l], ((0, 0), (CONV_LEAD, 0), (0, 0)))
        obr, g2, c2, k2, v2, gv2 = _sample_mixer(
            z3, ya3, state_gla[l].reshape(bs, QK, GLA_DV), hist,
            cache_swa_k[l].reshape(bs, WINDOW, SWA_KV), cache_swa_v[l].reshape(bs, WINDOW, SWA_KV),
            p, sample_tables, l)
        ys = _sample_merge(ys, obr[:, :ls].reshape(bs * ls, N_BRANCH * BRANCH_WIDTH), p, l)
        ys = _ffn(ys, p, l)
        outs["gla_s"].append(g2.reshape(bs, GLA_HEADS, GLA_DK, GLA_DV))
        outs["conv_s"].append(c2)
        outs["ks"].append(k2.reshape(bs, WINDOW, SWA_KV_HEADS, SWA_HEAD_DIM))
        outs["vs"].append(v2.reshape(bs, WINDOW, SWA_KV_HEADS, SWA_HEAD_DIM))
        outs["gm"].append(gv2[:, :ls])
    st = lambda name: jnp.stack(outs[name], 0)
    return (yp, ys.reshape(bs, ls, D_MODEL), st("gla_p"), st("gla_s"), st("conv_p"), st("conv_s"),
            st("kp"), st("vp"), st("ks"), st("vs"), st("gm"))
```

```python
import functools
import math

import jax
import jax.numpy as jnp
from jax import lax
from jax.experimental import pallas as pl
from jax.experimental.pallas import tpu as pltpu

F32 = jnp.float32
BF16 = jnp.bfloat16

D_MODEL = 1024
DEPTH = 2
PAST_LEN = 16384
GLA_HEADS = 4
GLA_DK = 64
GLA_DV = 128
GLA_RANK = 16
GLA_TAU = 16.0
GLA_CHUNK = 64
GMLP_GROUPS = 4
GMLP_WIDTH = 512
GMLP_CHUNK = 128
CONV_WIDTH = 512
CONV_K = 31
SWA_HEADS = 8
SWA_KV_HEADS = 2
SWA_GROUP = SWA_HEADS // SWA_KV_HEADS
SWA_HEAD_DIM = 64
WINDOW = 128
ROPE_DIM = SWA_HEAD_DIM // 4
ROPE_THETA = 500000.0
N_BRANCH = 4
BRANCH_WIDTH = 512
FFN_HIDDEN = 2816
NORM_EPS = 1e-6
LN_EPS = 1e-5

LANES = 128
SUBLANES = 8
VMEM_LIMIT_BYTES = 56 * 1024 * 1024

QK = GLA_HEADS * GLA_DK
QV = GLA_HEADS * GLA_DV
SWA_Q = SWA_HEADS * SWA_HEAD_DIM
SWA_KV = SWA_KV_HEADS * SWA_HEAD_DIM

C_AQ = 0
C_AK = C_AQ + QK
C_AV = C_AK + QK
C_AR = C_AV + QV
C_BU = C_AR + QV
C_BV = C_BU + GMLP_WIDTH
C_CA = C_BV + GMLP_WIDTH
C_CG = C_CA + CONV_WIDTH
C_DQ = C_CG + CONV_WIDTH
C_DK = C_DQ + SWA_Q
C_DV = C_DK + SWA_KV
C_LR = C_DV + SWA_KV
IN_COLS_PACKED = C_LR + LANES

PROMPT_TILE = 256
FFN_TILE = 512
FFN_CHUNK = 256
CONV_PAD = 32
CONV_LEAD = CONV_PAD - (CONV_K - 1)
SAMPLE_ROWS = 8
SAMPLE_TOKENS = 4
SAMPLE_GROUP = 4


def _mm(a, b):
    return jnp.dot(a, b, preferred_element_type=F32)


def _mm_nt(a, b):
    return lax.dot_general(a, b, (((1,), (1,)), ((), ())), preferred_element_type=F32)


def _mm_tn(a, b):
    return lax.dot_general(a, b, (((0,), (0,)), ((), ())), preferred_element_type=F32)


def _hi_lo(a):
    hi = a.astype(BF16)
    lo = (a - hi.astype(F32)).astype(BF16)
    return hi, lo


def _mm_split_lhs(a_f32, m_bf16):
    hi, lo = _hi_lo(a_f32)
    return _mm(hi, m_bf16) + _mm(lo, m_bf16)


def _rms_rows(x, g):
    return x * lax.rsqrt(jnp.mean(x * x, axis=-1, keepdims=True) + NORM_EPS) * g


def _ln_rows(x, g, b):
    mu = jnp.mean(x, axis=-1, keepdims=True)
    xc = x - mu
    var = jnp.mean(xc * xc, axis=-1, keepdims=True)
    return xc * lax.rsqrt(var + LN_EPS) * g + b


def _gelu(x):
    return 0.5 * x * (1.0 + lax.erf(x * (1.0 / math.sqrt(2.0))))


def _silu(x):
    return x * jax.nn.sigmoid(x)


def _log_sigmoid(x):
    return jnp.minimum(x, 0.0) - jnp.log1p(jnp.exp(-jnp.abs(x)))


def _iota(shape, dim):
    return lax.broadcasted_iota(jnp.int32, shape, dim)


def _group_ones(n, group, dtype=BF16):
    return (_iota((n, n), 0) // group == _iota((n, n), 1) // group).astype(dtype)


def _group_mean_sq(x, group):
    return _mm_split_lhs(x * x, _group_ones(x.shape[-1], group)) * (1.0 / group)


def _rope_slab(x, c, a, b):
    return x * c + pltpu.roll(x, LANES - ROPE_DIM // 2, 1) * a + pltpu.roll(x, ROPE_DIM // 2, 1) * b


def _rope(x, c, a, b):
    n = x.shape[-1] // LANES
    if n == 1:
        return _rope_slab(x, c, a, b)
    return jnp.concatenate(
        [_rope_slab(x[:, i * LANES:(i + 1) * LANES], c, a, b) for i in range(n)], axis=1)


def _qk_norm_rope(z, g, c, a, b, split=True):
    if split:
        ms = _group_mean_sq(z, SWA_HEAD_DIM)
    else:
        ms = _mm((z * z).astype(BF16), _group_ones(z.shape[-1], SWA_HEAD_DIM)) * (1.0 / SWA_HEAD_DIM)
    return _rope(z * lax.rsqrt(ms + NORM_EPS) * g, c, a, b)


def _head_rms_gla(o, g):
    return jnp.concatenate(
        [_rms_rows(o[:, h * GLA_DV:(h + 1) * GLA_DV], g) for h in range(GLA_HEADS)], axis=1)


def _swa_q_for_group(q, g, i):
    hq = g * SWA_GROUP + i
    slab = q[:, (hq // 2) * LANES:(hq // 2 + 1) * LANES]
    if hq % 2 != g:
        slab = pltpu.roll(slab, SWA_HEAD_DIM, 1)
    lane_head = _iota(slab.shape, 1) // SWA_HEAD_DIM
    return jnp.where(lane_head == g, slab, 0.0)


def _swa_assemble(outs, rows):
    lane_half = _iota((rows, LANES), 1) // SWA_HEAD_DIM
    slabs = []
    for s in range(SWA_HEADS // 2):
        pair = []
        for hq in (2 * s, 2 * s + 1):
            g, i = hq // SWA_GROUP, hq % SWA_GROUP
            o = outs[g][i]
            if hq % 2 != g:
                o = pltpu.roll(o, SWA_HEAD_DIM, 1)
            pair.append(o)
        slabs.append(jnp.where(lane_half == 0, pair[0], pair[1]))
    return jnp.concatenate(slabs, axis=1)


def _sink_column(sinks_ref, base, rows, rows_per_head):
    row_head = _iota((rows, 1), 0) // rows_per_head
    sink = jnp.zeros((rows, 1), F32)
    for i in range(SWA_GROUP):
        sink = jnp.where(row_head == i, sinks_ref[base + i], sink)
    return sink


def _prompt_mixer_kernel(
        sinks_ref,
        x_ref, rc_ref, ra_ref, rb_ref,
        n1g_ref, win_ref, bin_ref, wal_ref, bal_ref, glag_ref,
        lng_ref, lnb_ref, wsp_ref, bspt_ref,
        cw_ref, cb_ref, clg_ref, clb_ref,
        qg_ref, kg_ref,
        wg_ref, bg_ref, wbr_ref, wout_ref,
        y_ref, gla_ref, conv_ref, ko_ref, vo_ref,
        gstate_ref, hbuf_ref, hsh_ref, kfull_ref, vfull_ref,
        *, layer):
    T = PROMPT_TILE
    t = pl.program_id(1)
    last = pl.num_programs(1) - 1

    @pl.when(t == 0)
    def _():
        gstate_ref[...] = jnp.zeros_like(gstate_ref)
        hbuf_ref[0:CONV_PAD, :] = jnp.zeros((CONV_PAD, CONV_WIDTH), F32)
        kfull_ref[0:WINDOW, :] = jnp.zeros((WINDOW, SWA_KV), F32)
        vfull_ref[0:WINDOW, :] = jnp.zeros((WINDOW, SWA_KV), F32)

    x = x_ref[0]
    xb = _rms_rows(x, n1g_ref[...]).astype(BF16)

    def proj(c0, width):
        return _mm(xb, win_ref[:, c0:c0 + width]) + bin_ref[:, c0:c0 + width]

    def gated(i, o_b16):
        cols = slice(i * D_MODEL, (i + 1) * D_MODEL)
        gate = jax.nn.sigmoid(_mm(xb, wg_ref[:, cols]) + bg_ref[:, cols])
        return gate * _mm(o_b16, wbr_ref[i])

    h = proj(C_CA, CONV_WIDTH) * jax.nn.sigmoid(proj(C_CG, CONV_WIDTH))
    hbuf_ref[CONV_PAD:CONV_PAD + T, :] = h
    taps_of = {r: [kk for kk in range(CONV_K) if (CONV_LEAD + kk) % SUBLANES == r] for r in range(SUBLANES)}
    for r in range(1, SUBLANES):
        span = max(CONV_LEAD + kk for kk in taps_of[r]) - r
        hsh_ref[r - 1, 0:span + T, :] = hbuf_ref[r:r + span + T, :]

    q = proj(C_AQ, QK) * (GLA_DK ** -0.5)
    k = proj(C_AK, QK)
    v = proj(C_AV, QV)
    lr = proj(C_LR, LANES)
    la = _log_sigmoid(_mm(lr.astype(BF16), wal_ref[...]) + bal_ref[...]) * (1.0 / GLA_TAU)

    n_chunks = T // GLA_CHUNK
    ri = _iota((T, T), 0)
    ci = _iota((T, T), 1)
    chunk_causal = jnp.logical_and((ri // GLA_CHUNK) == (ci // GLA_CHUNK), ci <= ri)
    la_hi, la_lo = _hi_lo(la)
    m_tril = chunk_causal.astype(BF16)
    bcum = _mm(m_tril, la_hi) + _mm(m_tril, la_lo)
    tot_rows = [bcum[(c + 1) * GLA_CHUNK - 1:(c + 1) * GLA_CHUNK, :] for c in range(n_chunks)]
    btot = jnp.concatenate([jnp.broadcast_to(tr, (GLA_CHUNK, QK)) for tr in tot_rows], axis=0)
    q_in = (q * jnp.exp(bcum)).astype(BF16)
    k_out = (k * jnp.exp(-bcum)).astype(BF16)
    k_dec_t = (k * jnp.exp(btot - bcum)).T.astype(BF16)
    v_b = v.astype(BF16)
    dec_t = jnp.exp(jnp.concatenate(
        [jnp.broadcast_to(tr, (GLA_DV, QK)) for tr in tot_rows], axis=0).T)

    lane_head = _iota((T, QK), 1) // GLA_DK
    o_heads = []
    for hh in range(GLA_HEADS):
        qh = jnp.where(lane_head == hh, q_in, jnp.zeros_like(q_in))
        att = jnp.where(chunk_causal, _mm_nt(qh, k_out), 0.0).astype(BF16)
        o_heads.append(_mm(att, v_b[:, hh * GLA_DV:(hh + 1) * GLA_DV]))
    o_intra = jnp.concatenate(o_heads, axis=1)

    bd_mask = (_iota((QK, QV), 0) // GLA_DK) == (_iota((QK, QV), 1) // GLA_DV)
    tok_chunk = _iota((GLA_DK, T), 1) // GLA_CHUNK
    o_state = []
    for c in range(n_chunks):
        rs = slice(c * GLA_CHUNK, (c + 1) * GLA_CHUNK)
        s_c = gstate_ref[...]
        s_b = s_c.astype(BF16)
        s_bd = jnp.where(bd_mask, jnp.concatenate([s_b] * GLA_HEADS, axis=1), jnp.zeros((QK, QV), BF16))
        o_state.append(_mm(q_in[rs], s_bd))
        upd = []
        for hh in range(GLA_HEADS):
            kt = k_dec_t[hh * GLA_DK:(hh + 1) * GLA_DK, :]
            kt = jnp.where(tok_chunk == c, kt, jnp.zeros_like(kt))
            upd.append(_mm(kt, v_b[:, hh * GLA_DV:(hh + 1) * GLA_DV]))
        gstate_ref[...] = dec_t[:, c * GLA_DV:(c + 1) * GLA_DV] * s_c + jnp.concatenate(upd, axis=0)
    o_gla = o_intra + jnp.concatenate(o_state, axis=0)
    r = proj(C_AR, QV)
    merged = gated(0, (_head_rms_gla(o_gla, glag_ref[...]) * _silu(r)).astype(BF16))

    acc = jnp.broadcast_to(cb_ref[...], (T, CONV_WIDTH))
    for r_ in range(SUBLANES):
        for kk in taps_of[r_]:
            a = CONV_LEAD + kk - r_
            rows = hbuf_ref[a:a + T, :] if r_ == 0 else hsh_ref[r_ - 1, a:a + T, :]
            acc = acc + cw_ref[kk:kk + 1, :] * rows
    merged = merged + gated(2, _silu(_ln_rows(acc, clg_ref[...], clb_ref[...])).astype(BF16))
    hbuf_ref[0:CONV_PAD, :] = hbuf_ref[T:T + CONV_PAD, :]

    u = _gelu(proj(C_BU, GMLP_WIDTH))
    gv = _ln_rows(_gelu(proj(C_BV, GMLP_WIDTH)), lng_ref[...], lnb_ref[...])
    gv_b = gv.astype(BF16)
    gw = GMLP_WIDTH // GMLP_GROUPS
    tril = _iota((GMLP_CHUNK, GMLP_CHUNK), 1) <= _iota((GMLP_CHUNK, GMLP_CHUNK), 0)
    bspt = bspt_ref[...]
    s_rows = []
    for n in range(T // GMLP_CHUNK):
        rs = slice(n * GMLP_CHUNK, (n + 1) * GMLP_CHUNK)
        cols = []
        for g in range(GMLP_GROUPS):
            w = jnp.where(tril, wsp_ref[g], 0.0).astype(BF16)
            sg = _mm(w, gv_b[rs, g * gw:(g + 1) * gw])
            cols.append(sg + jnp.broadcast_to(bspt[:, g:g + 1], (GMLP_CHUNK, gw)))
        s_rows.append(jnp.concatenate(cols, axis=1))
    merged = merged + gated(1, (u * jnp.concatenate(s_rows, axis=0)).astype(BF16))

    rc, ra, rb = rc_ref[...], ra_ref[...], rb_ref[...]
    qd = _qk_norm_rope(proj(C_DQ, SWA_Q), qg_ref[...], rc, ra, rb, split=False)
    kfull_ref[WINDOW:WINDOW + T, :] = _qk_norm_rope(proj(C_DK, SWA_KV), kg_ref[...], rc, ra, rb)
    vfull_ref[WINDOW:WINDOW + T, :] = proj(C_DV, SWA_KV)

    rows = SWA_GROUP * WINDOW
    qi = _iota((rows, 2 * WINDOW), 0) % WINDOW
    kj = _iota((rows, 2 * WINDOW), 1)
    own_ok = jnp.logical_and(kj >= WINDOW, kj - WINDOW <= qi)
    prev_ok = jnp.logical_and(kj < WINDOW, kj > qi)
    scale = SWA_HEAD_DIM ** -0.5
    o_blocks = []
    for qb in range(T // WINDOW):
        rs = slice(qb * WINDOW, (qb + 1) * WINDOW)
        kblk = kfull_ref[qb * WINDOW:(qb + 2) * WINDOW, :].astype(BF16)
        vblk = vfull_ref[qb * WINDOW:(qb + 2) * WINDOW, :].astype(BF16)
        if qb == 0:
            valid = jnp.logical_or(own_ok, jnp.logical_and(prev_ok, t > 0))
        else:
            valid = jnp.logical_or(own_ok, prev_ok)
        outs = []
        for g in range(SWA_KV_HEADS):
            q_stack = jnp.concatenate(
                [_swa_q_for_group(qd[rs], g, i) for i in range(SWA_GROUP)], axis=0).astype(BF16)
            s = jnp.where(valid, _mm_nt(q_stack, kblk) * scale, -jnp.inf)
            sink = _sink_column(sinks_ref, layer * SWA_HEADS + g * SWA_GROUP, rows, WINDOW)
            m = jnp.maximum(jnp.max(s, axis=-1, keepdims=True), sink)
            p = jnp.exp(s - m)
            den = jnp.sum(p, axis=-1, keepdims=True) + jnp.exp(sink - m)
            o = _mm(p.astype(BF16), vblk) / den
            outs.append([o[i * WINDOW:(i + 1) * WINDOW] for i in range(SWA_GROUP)])
        o_blocks.append(_swa_assemble(outs, WINDOW))
    merged = merged + gated(3, jnp.concatenate(o_blocks, axis=0).astype(BF16))
    kfull_ref[0:WINDOW, :] = kfull_ref[T:T + WINDOW, :]
    vfull_ref[0:WINDOW, :] = vfull_ref[T:T + WINDOW, :]

    y_ref[0] = x + _mm(merged.astype(BF16), wout_ref[...])

    @pl.when(t == last)
    def _():
        conv_ref[0] = hbuf_ref[CONV_LEAD:CONV_PAD, :]
        ko_ref[0] = kfull_ref[0:WINDOW, :]
        vo_ref[0] = vfull_ref[0:WINDOW, :]
        gla_ref[0] = gstate_ref[...]


def _layer_spec(arr, layer):
    nd = arr.ndim
    return pl.BlockSpec((None,) + tuple(arr.shape[1:]), lambda *_: (layer,) + (0,) * (nd - 1),
                        pipeline_mode=pl.Buffered(1))


def _full_spec(shape):
    nd = len(shape)
    return pl.BlockSpec(shape, lambda *_: (0,) * nd)


def _compiler_params(n_axes):
    return pltpu.CompilerParams(
        dimension_semantics=("arbitrary",) * n_axes, vmem_limit_bytes=VMEM_LIMIT_BYTES)


def _prompt_mixer(x, p, tables, layer):
    B, L, _ = x.shape
    T = PROMPT_TILE
    weights = tuple(p[n] for n in (
        "n1g", "win", "bin", "wal", "bal", "glag", "lng", "lnb", "wsp", "bspt",
        "cw", "cb", "clg", "clb", "qg", "kg", "wg", "bg", "wbr", "wout"))
    tab_spec = pl.BlockSpec((T, LANES), lambda b, t, *_: (t, 0))
    grid_spec = pltpu.PrefetchScalarGridSpec(
        num_scalar_prefetch=1,
        grid=(B, L // T),
        in_specs=[pl.BlockSpec((1, T, D_MODEL), lambda b, t, *_: (b, t, 0)),
                  tab_spec, tab_spec, tab_spec]
                 + [_layer_spec(w, layer) for w in weights],
        out_specs=[pl.BlockSpec((1, T, D_MODEL), lambda b, t, *_: (b, t, 0)),
                   pl.BlockSpec((1, QK, GLA_DV), lambda b, t, *_: (b, 0, 0)),
                   pl.BlockSpec((1, CONV_K - 1, CONV_WIDTH), lambda b, t, *_: (b, 0, 0)),
                   pl.BlockSpec((1, WINDOW, SWA_KV), lambda b, t, *_: (b, 0, 0)),
                   pl.BlockSpec((1, WINDOW, SWA_KV), lambda b, t, *_: (b, 0, 0))],
        scratch_shapes=[pltpu.VMEM((QK, GLA_DV), F32),
                        pltpu.VMEM((T + CONV_PAD, CONV_WIDTH), F32),
                        pltpu.VMEM((SUBLANES - 1, T + CONV_PAD, CONV_WIDTH), F32),
                        pltpu.VMEM((T + WINDOW, SWA_KV), F32),
                        pltpu.VMEM((T + WINDOW, SWA_KV), F32)])
    return pl.pallas_call(
        functools.partial(_prompt_mixer_kernel, layer=layer),
        grid_spec=grid_spec,
        out_shape=[jax.ShapeDtypeStruct((B, L, D_MODEL), F32),
                   jax.ShapeDtypeStruct((B, QK, GLA_DV), F32),
                   jax.ShapeDtypeStruct((B, CONV_K - 1, CONV_WIDTH), F32),
                   jax.ShapeDtypeStruct((B, WINDOW, SWA_KV), F32),
                   jax.ShapeDtypeStruct((B, WINDOW, SWA_KV), F32)],
        compiler_params=_compiler_params(2),
        name="prompt_mixer",
    )(p["sinks"], x, *tables, *weights)


def _ffn_kernel(x_ref, g_ref, wgu_ref, wd_ref, y_ref, h_ref):
    x = x_ref[...]
    xb = _rms_rows(x, g_ref[...]).astype(BF16)
    for c in range(FFN_HIDDEN // FFN_CHUNK):
        c0 = c * FFN_CHUNK
        gate = _mm(xb, wgu_ref[:, c0:c0 + FFN_CHUNK])
        up = _mm(xb, wgu_ref[:, FFN_HIDDEN + c0:FFN_HIDDEN + c0 + FFN_CHUNK])
        h_ref[:, c0:c0 + FFN_CHUNK] = (_silu(gate) * up).astype(BF16)
    y_ref[...] = x + _mm(h_ref[...], wd_ref[...])


def _ffn(x2d, p, layer):
    n = x2d.shape[0]
    tile = min(FFN_TILE, n)
    return pl.pallas_call(
        _ffn_kernel,
        grid=(n // tile,),
        in_specs=[pl.BlockSpec((tile, D_MODEL), lambda i: (i, 0)),
                  _layer_spec(p["n2g"], layer), _layer_spec(p["wgu"], layer), _layer_spec(p["wd"], layer)],
        out_specs=pl.BlockSpec((tile, D_MODEL), lambda i: (i, 0)),
        out_shape=jax.ShapeDtypeStruct((n, D_MODEL), F32),
        scratch_shapes=[pltpu.VMEM((tile, FFN_HIDDEN), BF16)],
        compiler_params=_compiler_params(1),
        name="swiglu",
    )(x2d, p["n2g"], p["wgu"], p["wd"])


def _sample_proj_kernel(x_ref, n1g_ref, win_ref, bin_ref, wal_ref, bal_ref, z_ref, ya_ref):
    xb = _rms_rows(x_ref[...], n1g_ref[...]).astype(BF16)
    z = _mm(xb, win_ref[...]) + bin_ref[...]
    z_ref[...] = z
    ya_ref[...] = _mm(z[:, C_LR:C_LR + LANES].astype(BF16), wal_ref[...]) + bal_ref[...]


def _sample_proj(x2d, p, layer):
    n = x2d.shape[0]
    weights = tuple(p[k] for k in ("n1g", "win", "bin", "wal", "bal"))
    return pl.pallas_call(
        _sample_proj_kernel,
        grid=(1,),
        in_specs=[_full_spec(x2d.shape)] + [_layer_spec(w, layer) for w in weights],
        out_specs=[_full_spec((n, IN_COLS_PACKED)), _full_spec((n, QK))],
        out_shape=[jax.ShapeDtypeStruct((n, IN_COLS_PACKED), F32),
                   jax.ShapeDtypeStruct((n, QK), F32)],
        compiler_params=_compiler_params(1),
        name="sample_proj",
    )(x2d, *weights)


def _sample_sequence(j, layer, sinks_ref, z_ref, ya_ref, s0_ref, hist_ref, kc_ref, vc_ref,
                     rc, ra, rb, glag, lng, lnb, wsc_ref, bsc, wsh_ref, cb, clg, clb, qg, kg,
                     obr_ref, s1_ref, convo_ref, ko_ref, vo_ref, gvo_ref, hext_ref):
    R = SAMPLE_ROWS
    nt = SAMPLE_TOKENS
    z = z_ref[j]
    row = _iota((R, 1), 0)

    q = z[:, C_AQ:C_AQ + QK] * (GLA_DK ** -0.5)
    k = z[:, C_AK:C_AK + QK]
    v = z[:, C_AV:C_AV + QV]
    r = z[:, C_AR:C_AR + QV]
    la = jnp.where(row < nt, _log_sigmoid(ya_ref[j]) * (1.0 / GLA_TAU), 0.0)
    bcum = jnp.zeros_like(la)
    for s in range(nt):
        bcum = bcum + jnp.where(row >= s, jnp.broadcast_to(la[s:s + 1, :], la.shape), 0.0)
    btot = jnp.broadcast_to(bcum[nt - 1:nt, :], la.shape)
    q_in = q * jnp.exp(bcum)
    k_out = k * jnp.exp(-bcum)
    k_dec = k * jnp.exp(btot - bcum)
    s0 = s0_ref[j]
    s0_b = s0.astype(BF16)

    head_sel = (_iota((QK, LANES), 0) // GLA_DK == _iota((QK, LANES), 1)).astype(BF16)
    head_exp = (_iota((LANES, QV), 0) == _iota((LANES, QV), 1) // GLA_DV).astype(BF16)
    o_intra = jnp.zeros((R, QV), F32)
    for s in range(nt):
        prod = q_in * jnp.broadcast_to(k_out[s:s + 1, :], q_in.shape)
        att_s = _mm_split_lhs(prod, head_sel)
        att_e = _mm(att_s.astype(BF16), head_exp)
        o_intra = o_intra + jnp.where(row >= s, att_e, 0.0) * jnp.broadcast_to(v[s:s + 1, :], att_e.shape)
    lane_head = _iota((R, QK), 1) // GLA_DK
    o_state = jnp.concatenate(
        [_mm(jnp.where(lane_head == hh, q_in, 0.0).astype(BF16), s0_b) for hh in range(GLA_HEADS)], axis=1)
    o_a = _head_rms_gla(o_intra + o_state, glag) * _silu(r)

    la_hi = la.astype(BF16).astype(F32)
    la_lo = la - la_hi
    ones_tn = jnp.ones((R, GLA_DV), F32)
    dcol = jnp.exp(_mm_tn(la_hi, ones_tn) + _mm_tn(la_lo, ones_tn))
    upd = _mm_tn(k_dec.astype(BF16).astype(F32), v.astype(BF16).astype(F32))
    s1_ref[j] = dcol * s0 + jnp.concatenate(
        [upd[hh * GLA_DK:(hh + 1) * GLA_DK, hh * GLA_DV:(hh + 1) * GLA_DV] for hh in range(GLA_HEADS)],
        axis=0)

    u = _gelu(z[:, C_BU:C_BU + GMLP_WIDTH])
    gv = _ln_rows(_gelu(z[:, C_BV:C_BV + GMLP_WIDTH]), lng, lnb)
    gvo_ref[j] = gv
    sp = bsc
    for s in range(nt):
        sp = sp + jnp.where(row >= s, wsc_ref[s], 0.0) * jnp.broadcast_to(gv[s:s + 1, :], gv.shape)
    o_b = u * sp

    h = z[:, C_CA:C_CA + CONV_WIDTH] * jax.nn.sigmoid(z[:, C_CG:C_CG + CONV_WIDTH])
    hext_ref[j, 0:SUBLANES, :] = jnp.zeros((SUBLANES, CONV_WIDTH), F32)
    hext_ref[j, CONV_LEAD:CONV_PAD, :] = hist_ref[j]
    hext_ref[j, CONV_PAD:CONV_PAD + R, :] = h
    hext = hext_ref[j]
    conv = jnp.zeros((R, CONV_WIDTH), F32)
    for tt in range(nt):
        c_t = jnp.sum(hext * wsh_ref[tt], axis=0, keepdims=True)
        conv = jnp.where(row == tt, jnp.broadcast_to(c_t, conv.shape), conv)
    o_c = _silu(_ln_rows(conv + cb, clg, clb))
    convo_ref[j] = hext_ref[j, CONV_PAD + nt - (CONV_K - 1):CONV_PAD + nt, :]

    zv = z[:, C_DV:C_DV + SWA_KV]
    qd = _qk_norm_rope(z[:, C_DQ:C_DQ + SWA_Q], qg, rc, ra, rb)
    k_new = _qk_norm_rope(z[:, C_DK:C_DK + SWA_KV], kg, rc, ra, rb)
    k_hist_b = kc_ref[j].astype(BF16)
    v_hist_b = vc_ref[j].astype(BF16)

    rows = SWA_GROUP * R
    qt = _iota((rows, WINDOW), 0) % R
    hist_ok = _iota((rows, WINDOW), 1) > qt
    qt1 = _iota((rows, 1), 0) % R
    scale = SWA_HEAD_DIM ** -0.5
    outs = []
    for g in range(SWA_KV_HEADS):
        q_stack = jnp.concatenate([_swa_q_for_group(qd, g, i) for i in range(SWA_GROUP)], axis=0)
        s_hist = jnp.where(hist_ok, _mm_nt(q_stack.astype(BF16), k_hist_b) * scale, -jnp.inf)
        s_new = []
        for s in range(nt):
            sn = jnp.sum(q_stack * jnp.broadcast_to(k_new[s:s + 1, :], q_stack.shape),
                         axis=-1, keepdims=True) * scale
            s_new.append(jnp.where(qt1 >= s, sn, -jnp.inf))
        sink = _sink_column(sinks_ref, layer * SWA_HEADS + g * SWA_GROUP, rows, R)
        m = jnp.maximum(jnp.max(s_hist, axis=-1, keepdims=True), sink)
        for s in range(nt):
            m = jnp.maximum(m, s_new[s])
        p_hist = jnp.exp(s_hist - m)
        den = jnp.sum(p_hist, axis=-1, keepdims=True) + jnp.exp(sink - m)
        o = _mm(p_hist.astype(BF16), v_hist_b)
        for s in range(nt):
            p_s = jnp.exp(s_new[s] - m)
            den = den + p_s
            o = o + p_s * jnp.broadcast_to(zv[s:s + 1, :], o.shape)
        o = o / den
        outs.append([o[i * R:(i + 1) * R] for i in range(SWA_GROUP)])
    o_d = _swa_assemble(outs, R)

    ko_ref[j, 0:WINDOW - nt, :] = kc_ref[j, nt:WINDOW, :]
    ko_ref[j, WINDOW - nt:WINDOW, :] = k_new[0:nt, :]
    vo_ref[j, 0:WINDOW - nt, :] = vc_ref[j, nt:WINDOW, :]
    vo_ref[j, WINDOW - nt:WINDOW, :] = zv[0:nt, :]

    obr_ref[j] = jnp.concatenate([o_a, o_b, o_c, o_d], axis=1)


def _sample_mixer_kernel(
        sinks_ref,
        z_ref, ya_ref, s0_ref, hist_ref, kc_ref, vc_ref,
        rc_ref, ra_ref, rb_ref,
        glag_ref, lng_ref, lnb_ref, wsc_ref, bsc_ref,
        wsh_ref, cb_ref, clg_ref, clb_ref, qg_ref, kg_ref,
        obr_ref, s1_ref, convo_ref, ko_ref, vo_ref, gvo_ref,
        hext_ref, *, layer):
    consts = (rc_ref[...], ra_ref[...], rb_ref[...], glag_ref[...], lng_ref[...], lnb_ref[...])
    for j in range(SAMPLE_GROUP):
        _sample_sequence(
            j, layer, sinks_ref, z_ref, ya_ref, s0_ref, hist_ref, kc_ref, vc_ref,
            *consts, wsc_ref, bsc_ref[...], wsh_ref, cb_ref[...], clg_ref[...], clb_ref[...],
            qg_ref[...], kg_ref[...],
            obr_ref, s1_ref, convo_ref, ko_ref, vo_ref, gvo_ref, hext_ref)


def _sample_mixer(z3, ya3, s0, hist, kc, vc, p, tables, layer):
    nb = z3.shape[0]
    R = SAMPLE_ROWS
    G = SAMPLE_GROUP
    weights = tuple(p[n] for n in ("glag", "lng", "lnb", "wsc", "bsc", "wsh", "cb", "clg", "clb", "qg", "kg"))

    def per_group(shape):
        nd = len(shape)
        return pl.BlockSpec((G,) + tuple(shape[1:]), lambda b, *_: (b,) + (0,) * (nd - 1))

    def per_group_of_layer(arr):
        nd = arr.ndim
        return pl.BlockSpec((None, G) + tuple(arr.shape[2:]), lambda b, *_: (layer, b) + (0,) * (nd - 2))

    def layer_block(arr):
        nd = arr.ndim
        return pl.BlockSpec((None,) + tuple(arr.shape[1:]), lambda b, *_: (layer,) + (0,) * (nd - 1))

    out_shapes = [jax.ShapeDtypeStruct((nb, R, N_BRANCH * BRANCH_WIDTH), F32),
                  jax.ShapeDtypeStruct((nb, QK, GLA_DV), F32),
                  jax.ShapeDtypeStruct((nb, CONV_K - 1, CONV_WIDTH), F32),
                  jax.ShapeDtypeStruct((nb, WINDOW, SWA_KV), F32),
                  jax.ShapeDtypeStruct((nb, WINDOW, SWA_KV), F32),
                  jax.ShapeDtypeStruct((nb, R, GMLP_WIDTH), F32)]
    grid_spec = pltpu.PrefetchScalarGridSpec(
        num_scalar_prefetch=1,
        grid=(nb // G,),
        in_specs=[per_group(a.shape) for a in (z3, ya3)]
                 + [per_group_of_layer(a) for a in (s0, hist, kc, vc)]
                 + [_full_spec(tb.shape) for tb in tables]
                 + [layer_block(w) for w in weights],
        out_specs=[per_group(s.shape) for s in out_shapes],
        scratch_shapes=[pltpu.VMEM((G, CONV_PAD + R, CONV_WIDTH), F32)])
    return pl.pallas_call(
        functools.partial(_sample_mixer_kernel, layer=layer),
        grid_spec=grid_spec,
        out_shape=out_shapes,
        compiler_params=_compiler_params(1),
        name="sample_mixer",
    )(p["sinks"], z3, ya3, s0, hist, kc, vc, *tables, *weights)


def _sample_merge_kernel(x_ref, obr_ref, n1g_ref, wg_ref, bg_ref, wbr_ref, wout_ref, y_ref):
    x = x_ref[...]
    xb = _rms_rows(x, n1g_ref[...]).astype(BF16)
    merged = jnp.zeros(x.shape, F32)
    for i in range(N_BRANCH):
        gate = jax.nn.sigmoid(
            _mm(xb, wg_ref[:, i * D_MODEL:(i + 1) * D_MODEL]) + bg_ref[:, i * D_MODEL:(i + 1) * D_MODEL])
        o_i = obr_ref[:, i * BRANCH_WIDTH:(i + 1) * BRANCH_WIDTH].astype(BF16)
        merged = merged + gate * _mm(o_i, wbr_ref[i])
    y_ref[...] = x + _mm(merged.astype(BF16), wout_ref[...])


def _sample_merge(x2d, obr, p, layer):
    weights = tuple(p[k] for k in ("n1g", "wg", "bg", "wbr", "wout"))
    return pl.pallas_call(
        _sample_merge_kernel,
        grid=(1,),
        in_specs=[_full_spec(x2d.shape), _full_spec(obr.shape)] + [_layer_spec(w, layer) for w in weights],
        out_specs=_full_spec(x2d.shape),
        out_shape=jax.ShapeDtypeStruct(x2d.shape, F32),
        compiler_params=_compiler_params(1),
        name="sample_merge",
    )(x2d, obr, *weights)


def _rope_tables(pos):
    half = ROPE_DIM // 2
    inv = jnp.exp(-math.log(ROPE_THETA) * jnp.arange(half, dtype=F32) * (2.0 / ROPE_DIM))
    ang = pos.astype(F32)[:, None] * inv[None, :]
    cos, sin = jnp.cos(ang), jnp.sin(ang)
    n = pos.shape[0]
    pad = jnp.zeros((n, SWA_HEAD_DIM - ROPE_DIM), F32)
    zero = jnp.zeros((n, half), F32)
    c_head = jnp.concatenate([cos, cos, pad + 1.0], axis=1)
    a_head = jnp.concatenate([-sin, zero, pad], axis=1)
    b_head = jnp.concatenate([zero, sin, pad], axis=1)
    rep = LANES // SWA_HEAD_DIM
    return tuple(jnp.tile(tb, (1, rep)) for tb in (c_head, a_head, b_head))


def _rows(v):
    return v.reshape(v.shape[0], 1, -1).astype(F32)


def _stacked_params(norm1_g, w_in, b_in, w_alpha2, b_alpha, gla_norm_g, gmlp_ln_g, gmlp_ln_b,
                    w_spatial, b_spatial, conv_w, conv_b, conv_ln_g, conv_ln_b, q_norm_g, k_norm_g,
                    sinks, w_gate, b_gate, w_branch, w_out, norm2_g, w_gate_up, w_down):
    o_lr = 2 * QK + 2 * QV
    o_rest = o_lr + GLA_RANK

    def repack(m, dtype):
        pad = jnp.zeros(m.shape[:-1] + (LANES - GLA_RANK,), dtype)
        return jnp.concatenate(
            [m[..., :o_lr].astype(dtype), m[..., o_rest:].astype(dtype),
             m[..., o_lr:o_rest].astype(dtype), pad], axis=-1)

    nt = SAMPLE_TOKENS
    gw = GMLP_WIDTH // GMLP_GROUPS
    pad_rows = ((0, 0), (0, SAMPLE_ROWS - nt), (0, 0))
    wsc = jnp.stack([
        jnp.pad(jnp.repeat(jnp.swapaxes(w_spatial[:, :, :nt, s], 1, 2), gw, axis=2), pad_rows)
        for s in range(nt)], axis=1)
    bsc = jnp.pad(jnp.repeat(jnp.swapaxes(b_spatial[:, :, :nt], 1, 2), gw, axis=2), pad_rows)
    wsh = jnp.stack([
        jnp.pad(conv_w, ((0, 0), (CONV_LEAD + t, CONV_PAD + SAMPLE_ROWS - CONV_LEAD - t - CONV_K), (0, 0)))
        for t in range(nt)], axis=1)
    return {
        "n1g": _rows(norm1_g),
        "win": repack(w_in, BF16),
        "bin": repack(b_in, F32).reshape(DEPTH, 1, -1),
        "wal": jnp.pad(w_alpha2, ((0, 0), (0, LANES - GLA_RANK), (0, 0))).astype(BF16),
        "bal": _rows(b_alpha),
        "glag": _rows(gla_norm_g),
        "lng": _rows(gmlp_ln_g),
        "lnb": _rows(gmlp_ln_b),
        "wsp": w_spatial.astype(F32),
        "bspt": jnp.swapaxes(b_spatial, 1, 2).astype(F32),
        "wsc": wsc.astype(F32),
        "bsc": bsc.astype(F32),
        "cw": jnp.pad(conv_w, ((0, 0), (0, CONV_PAD - CONV_K), (0, 0))).astype(F32),
        "wsh": wsh.astype(F32),
        "cb": _rows(conv_b),
        "clg": _rows(conv_ln_g),
        "clb": _rows(conv_ln_b),
        "qg": _rows(jnp.tile(q_norm_g, (1, SWA_HEADS))),
        "kg": _rows(jnp.tile(k_norm_g, (1, SWA_KV_HEADS))),
        "sinks": sinks.reshape(-1).astype(F32),
        "wg": w_gate.astype(BF16),
        "bg": _rows(b_gate),
        "wbr": w_branch.astype(BF16),
        "wout": w_out.astype(BF16),
        "n2g": _rows(norm2_g),
        "wgu": w_gate_up.astype(BF16),
        "wd": w_down.astype(BF16),
    }


def kernel(x_prompt, x_sample, state_gla, state_conv, cache_swa_k, cache_swa_v, norm1_g, w_in, b_in, w_alpha2, b_alpha, gla_norm_g, gmlp_ln_g, gmlp_ln_b, w_spatial, b_spatial, conv_w, conv_b, conv_ln_g, conv_ln_b, q_norm_g, k_norm_g, sinks, w_gate, b_gate, w_branch, w_out, norm2_g, w_gate_up, w_down):
    bp, lp, _ = x_prompt.shape
    bs, ls, _ = x_sample.shape
    R = SAMPLE_ROWS
    prompt_tables = _rope_tables(jnp.arange(lp))
    sample_tables = tuple(
        jnp.pad(tb, ((0, R - ls), (0, 0))) for tb in _rope_tables(PAST_LEN + jnp.arange(ls)))
    p = _stacked_params(norm1_g, w_in, b_in, w_alpha2, b_alpha, gla_norm_g, gmlp_ln_g, gmlp_ln_b,
                        w_spatial, b_spatial, conv_w, conv_b, conv_ln_g, conv_ln_b, q_norm_g,
                        k_norm_g, sinks, w_gate, b_gate, w_branch, w_out, norm2_g, w_gate_up, w_down)

    gla_in = state_gla.reshape(DEPTH, bs, QK, GLA_DV)
    swa_k_in = cache_swa_k.reshape(DEPTH, bs, WINDOW, SWA_KV)
    swa_v_in = cache_swa_v.reshape(DEPTH, bs, WINDOW, SWA_KV)

    yp = x_prompt
    ys = x_sample.reshape(bs * ls, D_MODEL)
    outs = {k: [] for k in ("gla_p", "gla_s", "conv_p", "conv_s", "kp", "vp", "ks", "vs", "gm")}
    for l in range(DEPTH):
        yp, g1, c1, k1, v1 = _prompt_mixer(yp, p, prompt_tables, l)
        yp = _ffn(yp.reshape(bp * lp, D_MODEL), p, l).reshape(bp, lp, D_MODEL)
        outs["gla_p"].append(g1.reshape(bp, GLA_HEADS, GLA_DK, GLA_DV))
        outs["conv_p"].append(c1)
        outs["kp"].append(k1.reshape(bp, WINDOW, SWA_KV_HEADS, SWA_HEAD_DIM))
        outs["vp"].append(v1.reshape(bp, WINDOW, SWA_KV_HEADS, SWA_HEAD_DIM))
        z, ya = _sample_proj(ys, p, l)
        pad_rows = ((0, 0), (0, R - ls), (0, 0))
        z3 = jnp.pad(z.reshape(bs, ls, IN_COLS_PACKED), pad_rows)
        ya3 = jnp.pad(ya.reshape(bs, ls, QK), pad_rows)
        obr, g2, c2, k2, v2, gv2 = _sample_mixer(
            z3, ya3, gla_in, state_conv, swa_k_in, swa_v_in, p, sample_tables, l)
        ys = _sample_merge(ys, obr[:, :ls].reshape(bs * ls, N_BRANCH * BRANCH_WIDTH), p, l)
        ys = _ffn(ys, p, l)
        outs["gla_s"].append(g2.reshape(bs, GLA_HEADS, GLA_DK, GLA_DV))
        outs["conv_s"].append(c2)
        outs["ks"].append(k2.reshape(bs, WINDOW, SWA_KV_HEADS, SWA_HEAD_DIM))
        outs["vs"].append(v2.reshape(bs, WINDOW, SWA_KV_HEADS, SWA_HEAD_DIM))
        outs["gm"].append(gv2[:, :ls])
    st = lambda name: jnp.stack(outs[name], 0)
    return (yp, ys.reshape(bs, ls, D_MODEL), st("gla_p"), st("gla_s"), st("conv_p"), st("conv_s"),
            st("kp"), st("vp"), st("ks"), st("vs"), st("gm"))
```

```python
import functools
import math

import jax
import jax.numpy as jnp
from jax import lax
from jax.experimental import pallas as pl
from jax.experimental.pallas import tpu as pltpu

F32 = jnp.float32
BF16 = jnp.bfloat16

D_MODEL = 1024
DEPTH = 2
PAST_LEN = 16384
GLA_HEADS = 4
GLA_DK = 64
GLA_DV = 128
GLA_RANK = 16
GLA_TAU = 16.0
GLA_CHUNK = 64
GMLP_GROUPS = 4
GMLP_WIDTH = 512
GMLP_CHUNK = 128
CONV_WIDTH = 512
CONV_K = 31
SWA_HEADS = 8
SWA_KV_HEADS = 2
SWA_GROUP = SWA_HEADS // SWA_KV_HEADS
SWA_HEAD_DIM = 64
WINDOW = 128
ROPE_DIM = SWA_HEAD_DIM // 4
ROPE_THETA = 500000.0
N_BRANCH = 4
BRANCH_WIDTH = 512
FFN_HIDDEN = 2816
NORM_EPS = 1e-6
LN_EPS = 1e-5

LANES = 128
SUBLANES = 8
VMEM_LIMIT_BYTES = 56 * 1024 * 1024

QK = GLA_HEADS * GLA_DK
QV = GLA_HEADS * GLA_DV
SWA_Q = SWA_HEADS * SWA_HEAD_DIM
SWA_KV = SWA_KV_HEADS * SWA_HEAD_DIM

C_AQ = 0
C_AK = C_AQ + QK
C_AV = C_AK + QK
C_AR = C_AV + QV
C_BU = C_AR + QV
C_BV = C_BU + GMLP_WIDTH
C_CA = C_BV + GMLP_WIDTH
C_CG = C_CA + CONV_WIDTH
C_DQ = C_CG + CONV_WIDTH
C_DK = C_DQ + SWA_Q
C_DV = C_DK + SWA_KV
C_LR = C_DV + SWA_KV
IN_COLS_PACKED = C_LR + LANES

PROMPT_TILE = 256
FFN_TILE = 512
FFN_CHUNK = 256
GATE_PART = 256
CONV_PAD = 32
CONV_LEAD = CONV_PAD - (CONV_K - 1)
SAMPLE_ROWS = 8
SAMPLE_TOKENS = 4
SAMPLE_GROUP = 4


def _mm(a, b):
    return jnp.dot(a, b, preferred_element_type=F32)


def _mm_nt(a, b):
    return lax.dot_general(a, b, (((1,), (1,)), ((), ())), preferred_element_type=F32)


def _mm_tn(a, b):
    return lax.dot_general(a, b, (((0,), (0,)), ((), ())), preferred_element_type=F32)


def _hi_lo(a):
    hi = a.astype(BF16)
    lo = (a - hi.astype(F32)).astype(BF16)
    return hi, lo


def _mm_split_lhs(a_f32, m_bf16):
    hi, lo = _hi_lo(a_f32)
    return _mm(hi, m_bf16) + _mm(lo, m_bf16)


def _rms_rows(x, g):
    return x * lax.rsqrt(jnp.mean(x * x, axis=-1, keepdims=True) + NORM_EPS) * g


def _ln_rows(x, g, b):
    mu = jnp.mean(x, axis=-1, keepdims=True)
    xc = x - mu
    var = jnp.mean(xc * xc, axis=-1, keepdims=True)
    return xc * lax.rsqrt(var + LN_EPS) * g + b


def _gelu(x):
    return 0.5 * x * (1.0 + lax.erf(x * (1.0 / math.sqrt(2.0))))


def _silu(x):
    return x * jax.nn.sigmoid(x)


def _log_sigmoid(x):
    return jnp.minimum(x, 0.0) - jnp.log(1.0 + jnp.exp(-jnp.abs(x)))


def _iota(shape, dim):
    return lax.broadcasted_iota(jnp.int32, shape, dim)


def _group_ones(n, group, dtype=BF16):
    return (_iota((n, n), 0) // group == _iota((n, n), 1) // group).astype(dtype)


def _group_mean_sq(x, group):
    return _mm_split_lhs(x * x, _group_ones(x.shape[-1], group)) * (1.0 / group)


def _rope_slab(x, c, a, b):
    return x * c + pltpu.roll(x, LANES - ROPE_DIM // 2, 1) * a + pltpu.roll(x, ROPE_DIM // 2, 1) * b


def _rope(x, c, a, b):
    n = x.shape[-1] // LANES
    if n == 1:
        return _rope_slab(x, c, a, b)
    return jnp.concatenate(
        [_rope_slab(x[:, i * LANES:(i + 1) * LANES], c, a, b) for i in range(n)], axis=1)


def _qk_norm_rope(z, g, c, a, b, split=True):
    if split:
        ms = _group_mean_sq(z, SWA_HEAD_DIM)
    else:
        ms = _mm((z * z).astype(BF16), _group_ones(z.shape[-1], SWA_HEAD_DIM)) * (1.0 / SWA_HEAD_DIM)
    return _rope(z * lax.rsqrt(ms + NORM_EPS) * g, c, a, b)


def _head_rms_gla(o, g):
    return jnp.concatenate(
        [_rms_rows(o[:, h * GLA_DV:(h + 1) * GLA_DV], g) for h in range(GLA_HEADS)], axis=1)


def _swa_q_for_group(q, g, i):
    hq = g * SWA_GROUP + i
    slab = q[:, (hq // 2) * LANES:(hq // 2 + 1) * LANES]
    if hq % 2 != g:
        slab = pltpu.roll(slab, SWA_HEAD_DIM, 1)
    lane_head = _iota(slab.shape, 1) // SWA_HEAD_DIM
    return jnp.where(lane_head == g, slab, 0.0)


def _swa_assemble(outs, rows):
    lane_half = _iota((rows, LANES), 1) // SWA_HEAD_DIM
    slabs = []
    for s in range(SWA_HEADS // 2):
        pair = []
        for hq in (2 * s, 2 * s + 1):
            g, i = hq // SWA_GROUP, hq % SWA_GROUP
            o = outs[g][i]
            if hq % 2 != g:
                o = pltpu.roll(o, SWA_HEAD_DIM, 1)
            pair.append(o)
        slabs.append(jnp.where(lane_half == 0, pair[0], pair[1]))
    return jnp.concatenate(slabs, axis=1)


def _sink_column(sinks_ref, base, rows, rows_per_head):
    row_head = _iota((rows, 1), 0) // rows_per_head
    sink = jnp.zeros((rows, 1), F32)
    for i in range(SWA_GROUP):
        sink = jnp.where(row_head == i, sinks_ref[base + i], sink)
    return sink


def _prompt_mixer_kernel(
        sinks_ref,
        x_ref, rc_ref, ra_ref, rb_ref,
        n1g_ref, win_ref, bin_ref, wal_ref, bal_ref, glag_ref,
        lng_ref, lnb_ref, wsp_ref, bspt_ref,
        cw_ref, cb_ref, clg_ref, clb_ref,
        qg_ref, kg_ref,
        wg_ref, bg_ref, wbr_ref, wout_ref,
        y_ref, gla_ref, conv_ref, ko_ref, vo_ref,
        gstate_ref, hbuf_ref, hsh_ref, kfull_ref, vfull_ref, gates_ref,
        *, layer):
    T = PROMPT_TILE
    t = pl.program_id(1)
    last = pl.num_programs(1) - 1

    @pl.when(t == 0)
    def _():
        gstate_ref[...] = jnp.zeros_like(gstate_ref)
        hbuf_ref[0:CONV_PAD, :] = jnp.zeros((CONV_PAD, CONV_WIDTH), F32)
        kfull_ref[0:WINDOW, :] = jnp.zeros((WINDOW, SWA_KV), F32)
        vfull_ref[0:WINDOW, :] = jnp.zeros((WINDOW, SWA_KV), F32)

    x = x_ref[0]
    xb = _rms_rows(x, n1g_ref[...]).astype(BF16)

    def proj(c0, width):
        return _mm(xb, win_ref[:, c0:c0 + width]) + bin_ref[:, c0:c0 + width]

    def gate_part(i, j):
        c0 = i * D_MODEL + j * GATE_PART
        gates_ref[i, :, j * GATE_PART:(j + 1) * GATE_PART] = jax.nn.sigmoid(
            _mm(xb, wg_ref[:, c0:c0 + GATE_PART]) + bg_ref[:, c0:c0 + GATE_PART])

    def gate_all(i):
        for j in range(D_MODEL // GATE_PART):
            gate_part(i, j)

    def branch(i, o_b16):
        return _mm(o_b16, wbr_ref[i])


    h = proj(C_CA, CONV_WIDTH) * jax.nn.sigmoid(proj(C_CG, CONV_WIDTH))
    q = proj(C_AQ, QK) * (GLA_DK ** -0.5)
    k = proj(C_AK, QK)
    v = proj(C_AV, QV)
    lr = proj(C_LR, LANES)
    hbuf_ref[CONV_PAD:CONV_PAD + T, :] = h
    taps_of = {r: [kk for kk in range(CONV_K) if (CONV_LEAD + kk) % SUBLANES == r] for r in range(SUBLANES)}
    for r in range(1, SUBLANES):
        span = max(CONV_LEAD + kk for kk in taps_of[r]) - r
        hsh_ref[r - 1, 0:span + T, :] = hbuf_ref[r:r + span + T, :]

    z = {}
    conv_fill = [
        lambda: z.update(u=proj(C_BU, GMLP_WIDTH)),
        lambda: z.update(gv=proj(C_BV, GMLP_WIDTH)),
        lambda: z.update(q=proj(C_DQ, SWA_Q)),
        lambda: z.update(k=proj(C_DK, SWA_KV), v=proj(C_DV, SWA_KV)),
        lambda: z.update(r=proj(C_AR, QV)),
        lambda: (gate_part(0, 0), gate_part(0, 1)),
        lambda: (gate_part(0, 2), gate_part(0, 3)),
        lambda: (gate_part(1, 0), gate_part(1, 1)),
    ]
    acc = jnp.broadcast_to(cb_ref[...], (T, CONV_WIDTH))
    for r_ in range(SUBLANES):
        for kk in taps_of[r_]:
            a = CONV_LEAD + kk - r_
            rows_k = hbuf_ref[a:a + T, :] if r_ == 0 else hsh_ref[r_ - 1, a:a + T, :]
            acc = acc + cw_ref[kk:kk + 1, :] * rows_k
        conv_fill[r_]()
    hbuf_ref[0:CONV_PAD, :] = hbuf_ref[T:T + CONV_PAD, :]
    o_c = _silu(_ln_rows(acc, clg_ref[...], clb_ref[...])).astype(BF16)
    zu, zgv, zq, zk, zvd, r = z["u"], z["gv"], z["q"], z["k"], z["v"], z["r"]

    la = _log_sigmoid(_mm(lr.astype(BF16), wal_ref[...]) + bal_ref[...]) * (1.0 / GLA_TAU)
    gate_part(1, 2)
    gate_part(1, 3)

    n_chunks = T // GLA_CHUNK
    ri = _iota((T, T), 0)
    ci = _iota((T, T), 1)
    chunk_causal = jnp.logical_and((ri // GLA_CHUNK) == (ci // GLA_CHUNK), ci <= ri)
    la_hi, la_lo = _hi_lo(la)
    m_tril = chunk_causal.astype(BF16)
    bcum = _mm(m_tril, la_hi) + _mm(m_tril, la_lo)
    tot_rows = [bcum[(c + 1) * GLA_CHUNK - 1:(c + 1) * GLA_CHUNK, :] for c in range(n_chunks)]
    btot = jnp.concatenate([jnp.broadcast_to(tr, (GLA_CHUNK, QK)) for tr in tot_rows], axis=0)
    q_in = (q * jnp.exp(bcum)).astype(BF16)
    k_out = (k * jnp.exp(-bcum)).astype(BF16)
    k_dec_t = (k * jnp.exp(btot - bcum)).T.astype(BF16)
    v_b = v.astype(BF16)
    dec_t = jnp.exp(jnp.concatenate(
        [jnp.broadcast_to(tr, (GLA_DV, QK)) for tr in tot_rows], axis=0).T)

    lane_head = _iota((T, QK), 1) // GLA_DK
    o_heads = []
    for hh in range(GLA_HEADS):
        qh = jnp.where(lane_head == hh, q_in, jnp.zeros_like(q_in))
        att = jnp.where(chunk_causal, _mm_nt(qh, k_out), 0.0).astype(BF16)
        o_heads.append(_mm(att, v_b[:, hh * GLA_DV:(hh + 1) * GLA_DV]))
    o_intra = jnp.concatenate(o_heads, axis=1)
    gate_part(2, 0)
    gate_part(2, 1)

    bd_mask = (_iota((QK, QV), 0) // GLA_DK) == (_iota((QK, QV), 1) // GLA_DV)
    tok_chunk = _iota((GLA_DK, T), 1) // GLA_CHUNK
    o_state = []
    for c in range(n_chunks):
        rs = slice(c * GLA_CHUNK, (c + 1) * GLA_CHUNK)
        s_c = gstate_ref[...]
        s_b = s_c.astype(BF16)
        s_bd = jnp.where(bd_mask, jnp.concatenate([s_b] * GLA_HEADS, axis=1), jnp.zeros((QK, QV), BF16))
        o_state.append(_mm(q_in[rs], s_bd))
        upd = []
        for hh in range(GLA_HEADS):
            kt = k_dec_t[hh * GLA_DK:(hh + 1) * GLA_DK, :]
            kt = jnp.where(tok_chunk == c, kt, jnp.zeros_like(kt))
            upd.append(_mm(kt, v_b[:, hh * GLA_DV:(hh + 1) * GLA_DV]))
        gstate_ref[...] = dec_t[:, c * GLA_DV:(c + 1) * GLA_DV] * s_c + jnp.concatenate(upd, axis=0)
    o_gla = o_intra + jnp.concatenate(o_state, axis=0)
    gate_part(2, 2)
    gate_part(2, 3)
    o_a = (_head_rms_gla(o_gla, glag_ref[...]) * _silu(r)).astype(BF16)

    u = _gelu(zu)
    gv = _ln_rows(_gelu(zgv), lng_ref[...], lnb_ref[...])
    gate_all(3)
    gv_b = gv.astype(BF16)
    gw = GMLP_WIDTH // GMLP_GROUPS
    tril = _iota((GMLP_CHUNK, GMLP_CHUNK), 1) <= _iota((GMLP_CHUNK, GMLP_CHUNK), 0)
    bspt = bspt_ref[...]
    s_rows = []
    for n in range(T // GMLP_CHUNK):
        rs = slice(n * GMLP_CHUNK, (n + 1) * GMLP_CHUNK)
        cols = []
        for g in range(GMLP_GROUPS):
            w = jnp.where(tril, wsp_ref[g], 0.0).astype(BF16)
            sg = _mm(w, gv_b[rs, g * gw:(g + 1) * gw])
            cols.append(sg + jnp.broadcast_to(bspt[:, g:g + 1], (GMLP_CHUNK, gw)))
        s_rows.append(jnp.concatenate(cols, axis=1))
    o_b = (u * jnp.concatenate(s_rows, axis=0)).astype(BF16)

    rc, ra, rb = rc_ref[...], ra_ref[...], rb_ref[...]
    qd = _qk_norm_rope(zq, qg_ref[...], rc, ra, rb, split=False)
    kfull_ref[WINDOW:WINDOW + T, :] = _qk_norm_rope(zk, kg_ref[...], rc, ra, rb)
    vfull_ref[WINDOW:WINDOW + T, :] = zvd
    half = D_MODEL // 2
    partial = {}

    def branch_half(i, o_b16, j):
        partial[(i, j)] = _mm(o_b16, wbr_ref[i, :, j * half:(j + 1) * half])

    swa_fill = [functools.partial(branch_half, i, o, j) for i, o in ((2, o_c), (0, o_a)) for j in range(2)]

    rows = SWA_GROUP * WINDOW
    qi = _iota((rows, 2 * WINDOW), 0) % WINDOW
    kj = _iota((rows, 2 * WINDOW), 1)
    own_ok = jnp.logical_and(kj >= WINDOW, kj - WINDOW <= qi)
    prev_ok = jnp.logical_and(kj < WINDOW, kj > qi)
    scale = SWA_HEAD_DIM ** -0.5
    o_blocks = []
    for qb in range(T // WINDOW):
        rs = slice(qb * WINDOW, (qb + 1) * WINDOW)
        kblk = kfull_ref[qb * WINDOW:(qb + 2) * WINDOW, :].astype(BF16)
        vblk = vfull_ref[qb * WINDOW:(qb + 2) * WINDOW, :].astype(BF16)
        if qb == 0:
            valid = jnp.logical_or(own_ok, jnp.logical_and(prev_ok, t > 0))
        else:
            valid = jnp.logical_or(own_ok, prev_ok)
        outs = []
        for g in range(SWA_KV_HEADS):
            q_stack = jnp.concatenate(
                [_swa_q_for_group(qd[rs], g, i) for i in range(SWA_GROUP)], axis=0).astype(BF16)
            s = jnp.where(valid, _mm_nt(q_stack, kblk) * scale, -jnp.inf)
            sink = _sink_column(sinks_ref, layer * SWA_HEADS + g * SWA_GROUP, rows, WINDOW)
            m = jnp.maximum(jnp.max(s, axis=-1, keepdims=True), sink)
            p = jnp.exp(s - m)
            den = jnp.sum(p, axis=-1, keepdims=True) + jnp.exp(sink - m)
            o = _mm(p.astype(BF16), vblk) / den
            outs.append([o[i * WINDOW:(i + 1) * WINDOW] for i in range(SWA_GROUP)])
            if swa_fill:
                swa_fill.pop(0)()
        o_blocks.append(_swa_assemble(outs, WINDOW))
    for fill in swa_fill:
        fill()
    o_d = jnp.concatenate(o_blocks, axis=0).astype(BF16)
    kfull_ref[0:WINDOW, :] = kfull_ref[T:T + WINDOW, :]
    vfull_ref[0:WINDOW, :] = vfull_ref[T:T + WINDOW, :]

    merged = (gates_ref[0] * jnp.concatenate([partial[(0, 0)], partial[(0, 1)]], axis=1)
              + gates_ref[2] * jnp.concatenate([partial[(2, 0)], partial[(2, 1)]], axis=1))
    merged = merged + gates_ref[1] * branch(1, o_b)
    merged = merged + gates_ref[3] * branch(3, o_d)
    y_ref[0] = x + _mm(merged.astype(BF16), wout_ref[...])

    @pl.when(t == last)
    def _():
        conv_ref[0] = hbuf_ref[CONV_LEAD:CONV_PAD, :]
        ko_ref[0] = kfull_ref[0:WINDOW, :]
        vo_ref[0] = vfull_ref[0:WINDOW, :]
        gla_ref[0] = gstate_ref[...]


def _layer_spec(arr, layer):
    nd = arr.ndim
    return pl.BlockSpec((None,) + tuple(arr.shape[1:]), lambda *_: (layer,) + (0,) * (nd - 1),
                        pipeline_mode=pl.Buffered(1))


def _full_spec(shape):
    nd = len(shape)
    return pl.BlockSpec(shape, lambda *_: (0,) * nd)


def _compiler_params(n_axes):
    return pltpu.CompilerParams(
        dimension_semantics=("arbitrary",) * n_axes, vmem_limit_bytes=VMEM_LIMIT_BYTES)


def _prompt_mixer(x, p, tables, layer):
    B, L, _ = x.shape
    T = PROMPT_TILE
    weights = tuple(p[n] for n in (
        "n1g", "win", "bin", "wal", "bal", "glag", "lng", "lnb", "wsp", "bspt",
        "cw", "cb", "clg", "clb", "qg", "kg", "wg", "bg", "wbr", "wout"))
    tab_spec = pl.BlockSpec((T, LANES), lambda b, t, *_: (t, 0))
    grid_spec = pltpu.PrefetchScalarGridSpec(
        num_scalar_prefetch=1,
        grid=(B, L // T),
        in_specs=[pl.BlockSpec((1, T, D_MODEL), lambda b, t, *_: (b, t, 0)),
                  tab_spec, tab_spec, tab_spec]
                 + [_layer_spec(w, layer) for w in weights],
        out_specs=[pl.BlockSpec((1, T, D_MODEL), lambda b, t, *_: (b, t, 0)),
                   pl.BlockSpec((1, QK, GLA_DV), lambda b, t, *_: (b, 0, 0)),
                   pl.BlockSpec((1, CONV_K - 1, CONV_WIDTH), lambda b, t, *_: (b, 0, 0)),
                   pl.BlockSpec((1, WINDOW, SWA_KV), lambda b, t, *_: (b, 0, 0)),
                   pl.BlockSpec((1, WINDOW, SWA_KV), lambda b, t, *_: (b, 0, 0))],
        scratch_shapes=[pltpu.VMEM((QK, GLA_DV), F32),
                        pltpu.VMEM((T + CONV_PAD, CONV_WIDTH), F32),
                        pltpu.VMEM((SUBLANES - 1, T + CONV_PAD, CONV_WIDTH), F32),
                        pltpu.VMEM((T + WINDOW, SWA_KV), F32),
                        pltpu.VMEM((T + WINDOW, SWA_KV), F32),
                        pltpu.VMEM((N_BRANCH, T, D_MODEL), F32)])
    return pl.pallas_call(
        functools.partial(_prompt_mixer_kernel, layer=layer),
        grid_spec=grid_spec,
        out_shape=[jax.ShapeDtypeStruct((B, L, D_MODEL), F32),
                   jax.ShapeDtypeStruct((B, QK, GLA_DV), F32),
                   jax.ShapeDtypeStruct((B, CONV_K - 1, CONV_WIDTH), F32),
                   jax.ShapeDtypeStruct((B, WINDOW, SWA_KV), F32),
                   jax.ShapeDtypeStruct((B, WINDOW, SWA_KV), F32)],
        compiler_params=_compiler_params(2),
        name="prompt_mixer",
    )(p["sinks"], x, *tables, *weights)


def _ffn_kernel(x_ref, g_ref, wgu_ref, wd_ref, y_ref, h_ref):
    x = x_ref[...]
    xb = _rms_rows(x, g_ref[...]).astype(BF16)
    for c in range(FFN_HIDDEN // FFN_CHUNK):
        c0 = c * FFN_CHUNK
        gate = _mm(xb, wgu_ref[:, c0:c0 + FFN_CHUNK])
        up = _mm(xb, wgu_ref[:, FFN_HIDDEN + c0:FFN_HIDDEN + c0 + FFN_CHUNK])
        h_ref[:, c0:c0 + FFN_CHUNK] = (_silu(gate) * up).astype(BF16)
    y_ref[...] = x + _mm(h_ref[...], wd_ref[...])


def _ffn(x2d, p, layer):
    n = x2d.shape[0]
    tile = min(FFN_TILE, n)
    return pl.pallas_call(
        _ffn_kernel,
        grid=(n // tile,),
        in_specs=[pl.BlockSpec((tile, D_MODEL), lambda i: (i, 0)),
                  _layer_spec(p["n2g"], layer), _layer_spec(p["wgu"], layer), _layer_spec(p["wd"], layer)],
        out_specs=pl.BlockSpec((tile, D_MODEL), lambda i: (i, 0)),
        out_shape=jax.ShapeDtypeStruct((n, D_MODEL), F32),
        scratch_shapes=[pltpu.VMEM((tile, FFN_HIDDEN), BF16)],
        compiler_params=_compiler_params(1),
        name="swiglu",
    )(x2d, p["n2g"], p["wgu"], p["wd"])


def _sample_proj_kernel(x_ref, n1g_ref, win_ref, bin_ref, wal_ref, bal_ref, z_ref, ya_ref):
    xb = _rms_rows(x_ref[...], n1g_ref[...]).astype(BF16)
    z = _mm(xb, win_ref[...]) + bin_ref[...]
    z_ref[...] = z
    ya_ref[...] = _mm(z[:, C_LR:C_LR + LANES].astype(BF16), wal_ref[...]) + bal_ref[...]


def _sample_proj(x2d, p, layer):
    n = x2d.shape[0]
    weights = tuple(p[k] for k in ("n1g", "win", "bin", "wal", "bal"))
    return pl.pallas_call(
        _sample_proj_kernel,
        grid=(1,),
        in_specs=[_full_spec(x2d.shape)] + [_layer_spec(w, layer) for w in weights],
        out_specs=[_full_spec((n, IN_COLS_PACKED)), _full_spec((n, QK))],
        out_shape=[jax.ShapeDtypeStruct((n, IN_COLS_PACKED), F32),
                   jax.ShapeDtypeStruct((n, QK), F32)],
        compiler_params=_compiler_params(1),
        name="sample_proj",
    )(x2d, *weights)


def _sample_sequence(j, layer, sinks_ref, z_ref, ya_ref, s0_ref, hist_ref, kc_ref, vc_ref,
                     rc, ra, rb, glag, lng, lnb, wsc_ref, bsc, wsh_ref, cb, clg, clb, qg, kg,
                     obr_ref, s1_ref, convo_ref, ko_ref, vo_ref, gvo_ref, hext_ref):
    R = SAMPLE_ROWS
    nt = SAMPLE_TOKENS
    z = z_ref[j]
    row = _iota((R, 1), 0)

    q = z[:, C_AQ:C_AQ + QK] * (GLA_DK ** -0.5)
    k = z[:, C_AK:C_AK + QK]
    v = z[:, C_AV:C_AV + QV]
    r = z[:, C_AR:C_AR + QV]
    la = jnp.where(row < nt, _log_sigmoid(ya_ref[j]) * (1.0 / GLA_TAU), 0.0)
    bcum = jnp.zeros_like(la)
    for s in range(nt):
        bcum = bcum + jnp.where(row >= s, jnp.broadcast_to(la[s:s + 1, :], la.shape), 0.0)
    btot = jnp.broadcast_to(bcum[nt - 1:nt, :], la.shape)
    q_in = q * jnp.exp(bcum)
    k_out = k * jnp.exp(-bcum)
    k_dec = k * jnp.exp(btot - bcum)
    s0 = s0_ref[j]
    s0_b = s0.astype(BF16)

    head_sel = (_iota((QK, LANES), 0) // GLA_DK == _iota((QK, LANES), 1)).astype(BF16)
    head_exp = (_iota((LANES, QV), 0) == _iota((LANES, QV), 1) // GLA_DV).astype(BF16)
    o_intra = jnp.zeros((R, QV), F32)
    for s in range(nt):
        prod = q_in * jnp.broadcast_to(k_out[s:s + 1, :], q_in.shape)
        att_s = _mm_split_lhs(prod, head_sel)
        att_e = _mm(att_s.astype(BF16), head_exp)
        o_intra = o_intra + jnp.where(row >= s, att_e, 0.0) * jnp.broadcast_to(v[s:s + 1, :], att_e.shape)
    lane_head = _iota((R, QK), 1) // GLA_DK
    o_state = jnp.concatenate(
        [_mm(jnp.where(lane_head == hh, q_in, 0.0).astype(BF16), s0_b) for hh in range(GLA_HEADS)], axis=1)
    o_a = _head_rms_gla(o_intra + o_state, glag) * _silu(r)

    la_hi = la.astype(BF16).astype(F32)
    la_lo = la - la_hi
    ones_tn = jnp.ones((R, GLA_DV), F32)
    dcol = jnp.exp(_mm_tn(la_hi, ones_tn) + _mm_tn(la_lo, ones_tn))
    upd = _mm_tn(k_dec.astype(BF16).astype(F32), v.astype(BF16).astype(F32))
    s1_ref[j] = dcol * s0 + jnp.concatenate(
        [upd[hh * GLA_DK:(hh + 1) * GLA_DK, hh * GLA_DV:(hh + 1) * GLA_DV] for hh in range(GLA_HEADS)],
        axis=0)

    u = _gelu(z[:, C_BU:C_BU + GMLP_WIDTH])
    gv = _ln_rows(_gelu(z[:, C_BV:C_BV + GMLP_WIDTH]), lng, lnb)
    gvo_ref[j] = gv
    sp = bsc
    for s in range(nt):
        sp = sp + jnp.where(row >= s, wsc_ref[s], 0.0) * jnp.broadcast_to(gv[s:s + 1, :], gv.shape)
    o_b = u * sp

    h = z[:, C_CA:C_CA + CONV_WIDTH] * jax.nn.sigmoid(z[:, C_CG:C_CG + CONV_WIDTH])
    hext_ref[j, 0:SUBLANES, :] = jnp.zeros((SUBLANES, CONV_WIDTH), F32)
    hext_ref[j, CONV_LEAD:CONV_PAD, :] = hist_ref[j]
    hext_ref[j, CONV_PAD:CONV_PAD + R, :] = h
    hext = hext_ref[j]
    conv = jnp.zeros((R, CONV_WIDTH), F32)
    for tt in range(nt):
        c_t = jnp.sum(hext * wsh_ref[tt], axis=0, keepdims=True)
        conv = jnp.where(row == tt, jnp.broadcast_to(c_t, conv.shape), conv)
    o_c = _silu(_ln_rows(conv + cb, clg, clb))
    convo_ref[j] = hext_ref[j, CONV_PAD + nt - (CONV_K - 1):CONV_PAD + nt, :]

    zv = z[:, C_DV:C_DV + SWA_KV]
    qd = _qk_norm_rope(z[:, C_DQ:C_DQ + SWA_Q], qg, rc, ra, rb)
    k_new = _qk_norm_rope(z[:, C_DK:C_DK + SWA_KV], kg, rc, ra, rb)
    k_hist_b = kc_ref[j].astype(BF16)
    v_hist_b = vc_ref[j].astype(BF16)

    rows = SWA_GROUP * R
    qt = _iota((rows, WINDOW), 0) % R
    hist_ok = _iota((rows, WINDOW), 1) > qt
    qt1 = _iota((rows, 1), 0) % R
    scale = SWA_HEAD_DIM ** -0.5
    outs = []
    for g in range(SWA_KV_HEADS):
        q_stack = jnp.concatenate([_swa_q_for_group(qd, g, i) for i in range(SWA_GROUP)], axis=0)
        s_hist = jnp.where(hist_ok, _mm_nt(q_stack.astype(BF16), k_hist_b) * scale, -jnp.inf)
        s_new = []
        for s in range(nt):
            sn = jnp.sum(q_stack * jnp.broadcast_to(k_new[s:s + 1, :], q_stack.shape),
                         axis=-1, keepdims=True) * scale
            s_new.append(jnp.where(qt1 >= s, sn, -jnp.inf))
        sink = _sink_column(sinks_ref, layer * SWA_HEADS + g * SWA_GROUP, rows, R)
        m = jnp.maximum(jnp.max(s_hist, axis=-1, keepdims=True), sink)
        for s in range(nt):
            m = jnp.maximum(m, s_new[s])
        p_hist = jnp.exp(s_hist - m)
        den = jnp.sum(p_hist, axis=-1, keepdims=True) + jnp.exp(sink - m)
        o = _mm(p_hist.astype(BF16), v_hist_b)
        for s in range(nt):
            p_s = jnp.exp(s_new[s] - m)
            den = den + p_s
            o = o + p_s * jnp.broadcast_to(zv[s:s + 1, :], o.shape)
        o = o / den
        outs.append([o[i * R:(i + 1) * R] for i in range(SWA_GROUP)])
    o_d = _swa_assemble(outs, R)

    ko_ref[j, 0:WINDOW - nt, :] = kc_ref[j, nt:WINDOW, :]
    ko_ref[j, WINDOW - nt:WINDOW, :] = k_new[0:nt, :]
    vo_ref[j, 0:WINDOW - nt, :] = vc_ref[j, nt:WINDOW, :]
    vo_ref[j, WINDOW - nt:WINDOW, :] = zv[0:nt, :]

    obr_ref[j] = jnp.concatenate([o_a, o_b, o_c, o_d], axis=1)


def _sample_mixer_kernel(
        sinks_ref,
        z_ref, ya_ref, s0_ref, hist_ref, kc_ref, vc_ref,
        rc_ref, ra_ref, rb_ref,
        glag_ref, lng_ref, lnb_ref, wsc_ref, bsc_ref,
        wsh_ref, cb_ref, clg_ref, clb_ref, qg_ref, kg_ref,
        obr_ref, s1_ref, convo_ref, ko_ref, vo_ref, gvo_ref,
        hext_ref, *, layer):
    consts = (rc_ref[...], ra_ref[...], rb_ref[...], glag_ref[...], lng_ref[...], lnb_ref[...])
    for j in range(SAMPLE_GROUP):
        _sample_sequence(
            j, layer, sinks_ref, z_ref, ya_ref, s0_ref, hist_ref, kc_ref, vc_ref,
            *consts, wsc_ref, bsc_ref[...], wsh_ref, cb_ref[...], clg_ref[...], clb_ref[...],
            qg_ref[...], kg_ref[...],
            obr_ref, s1_ref, convo_ref, ko_ref, vo_ref, gvo_ref, hext_ref)


def _sample_mixer(z3, ya3, s0, hist, kc, vc, p, tables, layer):
    nb = z3.shape[0]
    R = SAMPLE_ROWS
    G = SAMPLE_GROUP
    weights = tuple(p[n] for n in ("glag", "lng", "lnb", "wsc", "bsc", "wsh", "cb", "clg", "clb", "qg", "kg"))

    def per_group(shape):
        nd = len(shape)
        return pl.BlockSpec((G,) + tuple(shape[1:]), lambda b, *_: (b,) + (0,) * (nd - 1))

    def per_group_of_layer(arr):
        nd = arr.ndim
        return pl.BlockSpec((None, G) + tuple(arr.shape[2:]), lambda b, *_: (layer, b) + (0,) * (nd - 2))

    def layer_block(arr):
        nd = arr.ndim
        return pl.BlockSpec((None,) + tuple(arr.shape[1:]), lambda b, *_: (layer,) + (0,) * (nd - 1))

    out_shapes = [jax.ShapeDtypeStruct((nb, R, N_BRANCH * BRANCH_WIDTH), F32),
                  jax.ShapeDtypeStruct((nb, QK, GLA_DV), F32),
                  jax.ShapeDtypeStruct((nb, CONV_K - 1, CONV_WIDTH), F32),
                  jax.ShapeDtypeStruct((nb, WINDOW, SWA_KV), F32),
                  jax.ShapeDtypeStruct((nb, WINDOW, SWA_KV), F32),
                  jax.ShapeDtypeStruct((nb, R, GMLP_WIDTH), F32)]
    grid_spec = pltpu.PrefetchScalarGridSpec(
        num_scalar_prefetch=1,
        grid=(nb // G,),
        in_specs=[per_group(a.shape) for a in (z3, ya3)]
                 + [per_group_of_layer(a) for a in (s0, hist, kc, vc)]
                 + [_full_spec(tb.shape) for tb in tables]
                 + [layer_block(w) for w in weights],
        out_specs=[per_group(s.shape) for s in out_shapes],
        scratch_shapes=[pltpu.VMEM((G, CONV_PAD + R, CONV_WIDTH), F32)])
    return pl.pallas_call(
        functools.partial(_sample_mixer_kernel, layer=layer),
        grid_spec=grid_spec,
        out_shape=out_shapes,
        compiler_params=_compiler_params(1),
        name="sample_mixer",
    )(p["sinks"], z3, ya3, s0, hist, kc, vc, *tables, *weights)


def _sample_merge_kernel(x_ref, obr_ref, n1g_ref, wg_ref, bg_ref, wbr_ref, wout_ref, y_ref):
    x = x_ref[...]
    xb = _rms_rows(x, n1g_ref[...]).astype(BF16)
    merged = jnp.zeros(x.shape, F32)
    for i in range(N_BRANCH):
        gate = jax.nn.sigmoid(
            _mm(xb, wg_ref[:, i * D_MODEL:(i + 1) * D_MODEL]) + bg_ref[:, i * D_MODEL:(i + 1) * D_MODEL])
        o_i = obr_ref[:, i * BRANCH_WIDTH:(i + 1) * BRANCH_WIDTH].astype(BF16)
        merged = merged + gate * _mm(o_i, wbr_ref[i])
    y_ref[...] = x + _mm(merged.astype(BF16), wout_ref[...])


def _sample_merge(x2d, obr, p, layer):
    weights = tuple(p[k] for k in ("n1g", "wg", "bg", "wbr", "wout"))
    return pl.pallas_call(
        _sample_merge_kernel,
        grid=(1,),
        in_specs=[_full_spec(x2d.shape), _full_spec(obr.shape)] + [_layer_spec(w, layer) for w in weights],
        out_specs=_full_spec(x2d.shape),
        out_shape=jax.ShapeDtypeStruct(x2d.shape, F32),
        compiler_params=_compiler_params(1),
        name="sample_merge",
    )(x2d, obr, *weights)


def _rope_tables(pos):
    half = ROPE_DIM // 2
    inv = jnp.exp(-math.log(ROPE_THETA) * jnp.arange(half, dtype=F32) * (2.0 / ROPE_DIM))
    ang = pos.astype(F32)[:, None] * inv[None, :]
    cos, sin = jnp.cos(ang), jnp.sin(ang)
    n = pos.shape[0]
    pad = jnp.zeros((n, SWA_HEAD_DIM - ROPE_DIM), F32)
    zero = jnp.zeros((n, half), F32)
    c_head = jnp.concatenate([cos, cos, pad + 1.0], axis=1)
    a_head = jnp.concatenate([-sin, zero, pad], axis=1)
    b_head = jnp.concatenate([zero, sin, pad], axis=1)
    rep = LANES // SWA_HEAD_DIM
    return tuple(jnp.tile(tb, (1, rep)) for tb in (c_head, a_head, b_head))


def _rows(v):
    return v.reshape(v.shape[0], 1, -1).astype(F32)


def _stacked_params(norm1_g, w_in, b_in, w_alpha2, b_alpha, gla_norm_g, gmlp_ln_g, gmlp_ln_b,
                    w_spatial, b_spatial, conv_w, conv_b, conv_ln_g, conv_ln_b, q_norm_g, k_norm_g,
                    sinks, w_gate, b_gate, w_branch, w_out, norm2_g, w_gate_up, w_down):
    o_lr = 2 * QK + 2 * QV
    o_rest = o_lr + GLA_RANK

    def repack(m, dtype):
        pad = jnp.zeros(m.shape[:-1] + (LANES - GLA_RANK,), dtype)
        return jnp.concatenate(
            [m[..., :o_lr].astype(dtype), m[..., o_rest:].astype(dtype),
             m[..., o_lr:o_rest].astype(dtype), pad], axis=-1)

    nt = SAMPLE_TOKENS
    gw = GMLP_WIDTH // GMLP_GROUPS
    pad_rows = ((0, 0), (0, SAMPLE_ROWS - nt), (0, 0))
    wsc = jnp.stack([
        jnp.pad(jnp.repeat(jnp.swapaxes(w_spatial[:, :, :nt, s], 1, 2), gw, axis=2), pad_rows)
        for s in range(nt)], axis=1)
    bsc = jnp.pad(jnp.repeat(jnp.swapaxes(b_spatial[:, :, :nt], 1, 2), gw, axis=2), pad_rows)
    wsh = jnp.stack([
        jnp.pad(conv_w, ((0, 0), (CONV_LEAD + t, CONV_PAD + SAMPLE_ROWS - CONV_LEAD - t - CONV_K), (0, 0)))
        for t in range(nt)], axis=1)
    return {
        "n1g": _rows(norm1_g),
        "win": repack(w_in, BF16),
        "bin": repack(b_in, F32).reshape(DEPTH, 1, -1),
        "wal": jnp.pad(w_alpha2, ((0, 0), (0, LANES - GLA_RANK), (0, 0))).astype(BF16),
        "bal": _rows(b_alpha),
        "glag": _rows(gla_norm_g),
        "lng": _rows(gmlp_ln_g),
        "lnb": _rows(gmlp_ln_b),
        "wsp": w_spatial.astype(F32),
        "bspt": jnp.swapaxes(b_spatial, 1, 2).astype(F32),
        "wsc": wsc.astype(F32),
        "bsc": bsc.astype(F32),
        "cw": jnp.pad(conv_w, ((0, 0), (0, CONV_PAD - CONV_K), (0, 0))).astype(F32),
        "wsh": wsh.astype(F32),
        "cb": _rows(conv_b),
        "clg": _rows(conv_ln_g),
        "clb": _rows(conv_ln_b),
        "qg": _rows(jnp.tile(q_norm_g, (1, SWA_HEADS))),
        "kg": _rows(jnp.tile(k_norm_g, (1, SWA_KV_HEADS))),
        "sinks": sinks.reshape(-1).astype(F32),
        "wg": w_gate.astype(BF16),
        "bg": _rows(b_gate),
        "wbr": w_branch.astype(BF16),
        "wout": w_out.astype(BF16),
        "n2g": _rows(norm2_g),
        "wgu": w_gate_up.astype(BF16),
        "wd": w_down.astype(BF16),
    }


def kernel(x_prompt, x_sample, state_gla, state_conv, cache_swa_k, cache_swa_v, norm1_g, w_in, b_in, w_alpha2, b_alpha, gla_norm_g, gmlp_ln_g, gmlp_ln_b, w_spatial, b_spatial, conv_w, conv_b, conv_ln_g, conv_ln_b, q_norm_g, k_norm_g, sinks, w_gate, b_gate, w_branch, w_out, norm2_g, w_gate_up, w_down):
    bp, lp, _ = x_prompt.shape
    bs, ls, _ = x_sample.shape
    R = SAMPLE_ROWS
    prompt_tables = _rope_tables(jnp.arange(lp))
    sample_tables = tuple(
        jnp.pad(tb, ((0, R - ls), (0, 0))) for tb in _rope_tables(PAST_LEN + jnp.arange(ls)))
    p = _stacked_params(norm1_g, w_in, b_in, w_alpha2, b_alpha, gla_norm_g, gmlp_ln_g, gmlp_ln_b,
                        w_spatial, b_spatial, conv_w, conv_b, conv_ln_g, conv_ln_b, q_norm_g,
                        k_norm_g, sinks, w_gate, b_gate, w_branch, w_out, norm2_g, w_gate_up, w_down)

    gla_in = state_gla.reshape(DEPTH, bs, QK, GLA_DV)
    swa_k_in = cache_swa_k.reshape(DEPTH, bs, WINDOW, SWA_KV)
    swa_v_in = cache_swa_v.reshape(DEPTH, bs, WINDOW, SWA_KV)

    yp = x_prompt
    ys = x_sample.reshape(bs * ls, D_MODEL)
    outs = {k: [] for k in ("gla_p", "gla_s", "conv_p", "conv_s", "kp", "vp", "ks", "vs", "gm")}
    for l in range(DEPTH):
        yp, g1, c1, k1, v1 = _prompt_mixer(yp, p, prompt_tables, l)
        yp = _ffn(yp.reshape(bp * lp, D_MODEL), p, l).reshape(bp, lp, D_MODEL)
        outs["gla_p"].append(g1.reshape(bp, GLA_HEADS, GLA_DK, GLA_DV))
        outs["conv_p"].append(c1)
        outs["kp"].append(k1.reshape(bp, WINDOW, SWA_KV_HEADS, SWA_HEAD_DIM))
        outs["vp"].append(v1.reshape(bp, WINDOW, SWA_KV_HEADS, SWA_HEAD_DIM))
        z, ya = _sample_proj(ys, p, l)
        pad_rows = ((0, 0), (0, R - ls), (0, 0))
        z3 = jnp.pad(z.reshape(bs, ls, IN_COLS_PACKED), pad_rows)
        ya3 = jnp.pad(ya.reshape(bs, ls, QK), pad_rows)
        obr, g2, c2, k2, v2, gv2 = _sample_mixer(
            z3, ya3, gla_in, state_conv, swa_k_in, swa_v_in, p, sample_tables, l)
        ys = _sample_merge(ys, obr[:, :ls].reshape(bs * ls, N_BRANCH * BRANCH_WIDTH), p, l)
        ys = _ffn(ys, p, l)
        outs["gla_s"].append(g2.reshape(bs, GLA_HEADS, GLA_DK, GLA_DV))
        outs["conv_s"].append(c2)
        outs["ks"].append(k2.reshape(bs, WINDOW, SWA_KV_HEADS, SWA_HEAD_DIM))
        outs["vs"].append(v2.reshape(bs, WINDOW, SWA_KV_HEADS, SWA_HEAD_DIM))
        outs["gm"].append(gv2[:, :ls])
    st = lambda name: jnp.stack(outs[name], 0)
    return (yp, ys.reshape(bs, ls, D_MODEL), st("gla_p"), st("gla_s"), st("conv_p"), st("conv_s"),
            st("kp"), st("vp"), st("ks"), st("vs"), st("gm"))
```

```python
import functools
import math

import jax
import jax.numpy as jnp
from jax import lax
from jax.experimental import pallas as pl
from jax.experimental.pallas import tpu as pltpu

F32 = jnp.float32
BF16 = jnp.bfloat16

D_MODEL = 1024
DEPTH = 2
PAST_LEN = 16384
GLA_HEADS = 4
GLA_DK = 64
GLA_DV = 128
GLA_RANK = 16
GLA_TAU = 16.0
GLA_CHUNK = 64
GMLP_GROUPS = 4
GMLP_WIDTH = 512
GMLP_CHUNK = 128
CONV_WIDTH = 512
CONV_K = 31
SWA_HEADS = 8
SWA_KV_HEADS = 2
SWA_GROUP = SWA_HEADS // SWA_KV_HEADS
SWA_HEAD_DIM = 64
WINDOW = 128
ROPE_DIM = SWA_HEAD_DIM // 4
ROPE_THETA = 500000.0
N_BRANCH = 4
BRANCH_WIDTH = 512
FFN_HIDDEN = 2816
NORM_EPS = 1e-6
LN_EPS = 1e-5

LANES = 128
SUBLANES = 8
VMEM_LIMIT_BYTES = 56 * 1024 * 1024

QK = GLA_HEADS * GLA_DK
QV = GLA_HEADS * GLA_DV
SWA_Q = SWA_HEADS * SWA_HEAD_DIM
SWA_KV = SWA_KV_HEADS * SWA_HEAD_DIM

C_AQ = 0
C_AK = C_AQ + QK
C_AV = C_AK + QK
C_AR = C_AV + QV
C_BU = C_AR + QV
C_BV = C_BU + GMLP_WIDTH
C_CA = C_BV + GMLP_WIDTH
C_CG = C_CA + CONV_WIDTH
C_DQ = C_CG + CONV_WIDTH
C_DK = C_DQ + SWA_Q
C_DV = C_DK + SWA_KV
C_LR = C_DV + SWA_KV
IN_COLS_PACKED = C_LR + LANES

PROMPT_TILE = 256
FFN_TILE = 512
FFN_CHUNK = 256
GATE_PART = 256
CONV_PAD = 32
CONV_LEAD = CONV_PAD - (CONV_K - 1)
SAMPLE_ROWS = 8
SAMPLE_TOKENS = 4
SAMPLE_GROUP = 8


def _mm(a, b):
    return jnp.dot(a, b, preferred_element_type=F32)


def _mm_nt(a, b):
    return lax.dot_general(a, b, (((1,), (1,)), ((), ())), preferred_element_type=F32)


def _mm_tn(a, b):
    return lax.dot_general(a, b, (((0,), (0,)), ((), ())), preferred_element_type=F32)


def _hi_lo(a):
    hi = a.astype(BF16)
    lo = (a - hi.astype(F32)).astype(BF16)
    return hi, lo


def _mm_split_lhs(a_f32, m_bf16):
    hi, lo = _hi_lo(a_f32)
    return _mm(hi, m_bf16) + _mm(lo, m_bf16)


def _rms_rows(x, g):
    return x * lax.rsqrt(jnp.mean(x * x, axis=-1, keepdims=True) + NORM_EPS) * g


def _ln_rows(x, g, b):
    mu = jnp.mean(x, axis=-1, keepdims=True)
    xc = x - mu
    var = jnp.mean(xc * xc, axis=-1, keepdims=True)
    return xc * lax.rsqrt(var + LN_EPS) * g + b


def _gelu(x):
    return 0.5 * x * (1.0 + lax.erf(x * (1.0 / math.sqrt(2.0))))


def _silu(x):
    return x * jax.nn.sigmoid(x)


def _log_sigmoid(x):
    return jnp.minimum(x, 0.0) - jnp.log(1.0 + jnp.exp(-jnp.abs(x)))


def _iota(shape, dim):
    return lax.broadcasted_iota(jnp.int32, shape, dim)


def _group_ones(n, group, dtype=BF16):
    return (_iota((n, n), 0) // group == _iota((n, n), 1) // group).astype(dtype)


def _group_mean_sq(x, group):
    return _mm_split_lhs(x * x, _group_ones(x.shape[-1], group)) * (1.0 / group)


def _rope_slab(x, c, a, b):
    return x * c + pltpu.roll(x, LANES - ROPE_DIM // 2, 1) * a + pltpu.roll(x, ROPE_DIM // 2, 1) * b


def _rope(x, c, a, b):
    n = x.shape[-1] // LANES
    if n == 1:
        return _rope_slab(x, c, a, b)
    return jnp.concatenate(
        [_rope_slab(x[:, i * LANES:(i + 1) * LANES], c, a, b) for i in range(n)], axis=1)


def _qk_norm_rope(z, g, c, a, b, split=True):
    if split:
        ms = _group_mean_sq(z, SWA_HEAD_DIM)
    else:
        ms = _mm((z * z).astype(BF16), _group_ones(z.shape[-1], SWA_HEAD_DIM)) * (1.0 / SWA_HEAD_DIM)
    return _rope(z * lax.rsqrt(ms + NORM_EPS) * g, c, a, b)


def _head_rms_gla(o, g):
    return jnp.concatenate(
        [_rms_rows(o[:, h * GLA_DV:(h + 1) * GLA_DV], g) for h in range(GLA_HEADS)], axis=1)


def _swa_q_for_group(q, g, i):
    hq = g * SWA_GROUP + i
    slab = q[:, (hq // 2) * LANES:(hq // 2 + 1) * LANES]
    if hq % 2 != g:
        slab = pltpu.roll(slab, SWA_HEAD_DIM, 1)
    lane_head = _iota(slab.shape, 1) // SWA_HEAD_DIM
    return jnp.where(lane_head == g, slab, 0.0)


def _swa_assemble(outs, rows):
    lane_half = _iota((rows, LANES), 1) // SWA_HEAD_DIM
    slabs = []
    for s in range(SWA_HEADS // 2):
        pair = []
        for hq in (2 * s, 2 * s + 1):
            g, i = hq // SWA_GROUP, hq % SWA_GROUP
            o = outs[g][i]
            if hq % 2 != g:
                o = pltpu.roll(o, SWA_HEAD_DIM, 1)
            pair.append(o)
        slabs.append(jnp.where(lane_half == 0, pair[0], pair[1]))
    return jnp.concatenate(slabs, axis=1)


def _sink_column(sinks_ref, base, rows, rows_per_head):
    row_head = _iota((rows, 1), 0) // rows_per_head
    sink = jnp.zeros((rows, 1), F32)
    for i in range(SWA_GROUP):
        sink = jnp.where(row_head == i, sinks_ref[base + i], sink)
    return sink


def _prompt_mixer_kernel(
        sinks_ref,
        x_ref, rc_ref, ra_ref, rb_ref,
        n1g_ref, win_ref, bin_ref, wal_ref, bal_ref, glag_ref,
        lng_ref, lnb_ref, wsp_ref, bspt_ref,
        cw_ref, cb_ref, clg_ref, clb_ref,
        qg_ref, kg_ref,
        wg_ref, bg_ref, wbr_ref, wout_ref,
        y_ref, gla_ref, conv_ref, ko_ref, vo_ref,
        gstate_ref, hbuf_ref, hsh_ref, kfull_ref, vfull_ref, gates_ref,
        *, layer):
    T = PROMPT_TILE
    t = pl.program_id(1)
    last = pl.num_programs(1) - 1

    @pl.when(t == 0)
    def _():
        gstate_ref[...] = jnp.zeros_like(gstate_ref)
        hbuf_ref[0:CONV_PAD, :] = jnp.zeros((CONV_PAD, CONV_WIDTH), F32)
        kfull_ref[0:WINDOW, :] = jnp.zeros((WINDOW, SWA_KV), F32)
        vfull_ref[0:WINDOW, :] = jnp.zeros((WINDOW, SWA_KV), F32)

    x = x_ref[0]
    xb = _rms_rows(x, n1g_ref[...]).astype(BF16)

    def proj(c0, width):
        return _mm(xb, win_ref[:, c0:c0 + width]) + bin_ref[:, c0:c0 + width]

    def gate_part(i, j):
        c0 = i * D_MODEL + j * GATE_PART
        gates_ref[i, :, j * GATE_PART:(j + 1) * GATE_PART] = jax.nn.sigmoid(
            _mm(xb, wg_ref[:, c0:c0 + GATE_PART]) + bg_ref[:, c0:c0 + GATE_PART])

    def branch(i, o_b16):
        return _mm(o_b16, wbr_ref[i])


    h = proj(C_CA, CONV_WIDTH) * jax.nn.sigmoid(proj(C_CG, CONV_WIDTH))
    q = proj(C_AQ, QK) * (GLA_DK ** -0.5)
    k = proj(C_AK, QK)
    v = proj(C_AV, QV)
    kvl = proj(C_DK, 2 * SWA_KV + LANES)
    zk, zvd, lr = kvl[:, :SWA_KV], kvl[:, SWA_KV:2 * SWA_KV], kvl[:, 2 * SWA_KV:]
    hbuf_ref[CONV_PAD:CONV_PAD + T, :] = h
    taps_of = {r: [kk for kk in range(CONV_K) if (CONV_LEAD + kk) % SUBLANES == r] for r in range(SUBLANES)}
    for r in range(1, SUBLANES):
        span = max(CONV_LEAD + kk for kk in taps_of[r]) - r
        hsh_ref[r - 1, 0:span + T, :] = hbuf_ref[r:r + span + T, :]

    z = {}
    conv_fill = [
        lambda: z.update(u=proj(C_BU, GMLP_WIDTH)),
        lambda: z.update(gv=proj(C_BV, GMLP_WIDTH)),
        lambda: z.update(q=proj(C_DQ, SWA_Q)),
        lambda: z.update(r=proj(C_AR, QV)),
        lambda: (gate_part(0, 0), gate_part(0, 1)),
        lambda: (gate_part(0, 2), gate_part(0, 3)),
        lambda: (gate_part(1, 0), gate_part(1, 1)),
        lambda: (gate_part(1, 2), gate_part(1, 3)),
    ]
    acc = jnp.broadcast_to(cb_ref[...], (T, CONV_WIDTH))
    for r_ in range(SUBLANES):
        for kk in taps_of[r_]:
            a = CONV_LEAD + kk - r_
            rows_k = hbuf_ref[a:a + T, :] if r_ == 0 else hsh_ref[r_ - 1, a:a + T, :]
            acc = acc + cw_ref[kk:kk + 1, :] * rows_k
        conv_fill[r_]()
    hbuf_ref[0:CONV_PAD, :] = hbuf_ref[T:T + CONV_PAD, :]
    o_c = _silu(_ln_rows(acc, clg_ref[...], clb_ref[...])).astype(BF16)
    zu, zgv, zq, r = z["u"], z["gv"], z["q"], z["r"]

    la = _log_sigmoid(_mm(lr.astype(BF16), wal_ref[...]) + bal_ref[...]) * (1.0 / GLA_TAU)
    gate_part(2, 0)
    gate_part(2, 1)

    n_chunks = T // GLA_CHUNK
    ri = _iota((T, T), 0)
    ci = _iota((T, T), 1)
    chunk_causal = jnp.logical_and((ri // GLA_CHUNK) == (ci // GLA_CHUNK), ci <= ri)
    la_hi, la_lo = _hi_lo(la)
    m_tril = chunk_causal.astype(BF16)
    bcum = _mm(m_tril, la_hi) + _mm(m_tril, la_lo)
    tot_rows = [bcum[(c + 1) * GLA_CHUNK - 1:(c + 1) * GLA_CHUNK, :] for c in range(n_chunks)]
    btot = jnp.concatenate([jnp.broadcast_to(tr, (GLA_CHUNK, QK)) for tr in tot_rows], axis=0)
    q_in = (q * jnp.exp(bcum)).astype(BF16)
    k_out = (k * jnp.exp(-bcum)).astype(BF16)
    k_dec_t = (k * jnp.exp(btot - bcum)).T.astype(BF16)
    v_b = v.astype(BF16)
    dec_t = jnp.exp(jnp.concatenate(
        [jnp.broadcast_to(tr, (GLA_DV, QK)) for tr in tot_rows], axis=0).T)

    lane_head = _iota((T, QK), 1) // GLA_DK
    o_heads = []
    for hh in range(GLA_HEADS):
        qh = jnp.where(lane_head == hh, q_in, jnp.zeros_like(q_in))
        att = jnp.where(chunk_causal, _mm_nt(qh, k_out), 0.0).astype(BF16)
        o_heads.append(_mm(att, v_b[:, hh * GLA_DV:(hh + 1) * GLA_DV]))
    o_intra = jnp.concatenate(o_heads, axis=1)
    gate_part(2, 2)
    gate_part(2, 3)

    bd_mask = (_iota((QK, QV), 0) // GLA_DK) == (_iota((QK, QV), 1) // GLA_DV)
    tok_chunk = _iota((GLA_DK, T), 1) // GLA_CHUNK
    o_state = []
    for c in range(n_chunks):
        rs = slice(c * GLA_CHUNK, (c + 1) * GLA_CHUNK)
        s_c = gstate_ref[...]
        s_b = s_c.astype(BF16)
        s_bd = jnp.where(bd_mask, jnp.concatenate([s_b] * GLA_HEADS, axis=1), jnp.zeros((QK, QV), BF16))
        o_state.append(_mm(q_in[rs], s_bd))
        upd = []
        for hh in range(GLA_HEADS):
            kt = k_dec_t[hh * GLA_DK:(hh + 1) * GLA_DK, :]
            kt = jnp.where(tok_chunk == c, kt, jnp.zeros_like(kt))
            upd.append(_mm(kt, v_b[:, hh * GLA_DV:(hh + 1) * GLA_DV]))
        gstate_ref[...] = dec_t[:, c * GLA_DV:(c + 1) * GLA_DV] * s_c + jnp.concatenate(upd, axis=0)
    o_gla = o_intra + jnp.concatenate(o_state, axis=0)
    gate_part(3, 0)
    gate_part(3, 1)
    o_a = (_head_rms_gla(o_gla, glag_ref[...]) * _silu(r)).astype(BF16)

    u = _gelu(zu)
    gv = _ln_rows(_gelu(zgv), lng_ref[...], lnb_ref[...])
    gate_part(3, 2)
    gate_part(3, 3)
    gv_b = gv.astype(BF16)
    gw = GMLP_WIDTH // GMLP_GROUPS
    tril = _iota((GMLP_CHUNK, GMLP_CHUNK), 1) <= _iota((GMLP_CHUNK, GMLP_CHUNK), 0)
    bspt = bspt_ref[...]
    s_rows = []
    for n in range(T // GMLP_CHUNK):
        rs = slice(n * GMLP_CHUNK, (n + 1) * GMLP_CHUNK)
        cols = []
        for g in range(GMLP_GROUPS):
            w = jnp.where(tril, wsp_ref[g], 0.0).astype(BF16)
            sg = _mm(w, gv_b[rs, g * gw:(g + 1) * gw])
            cols.append(sg + jnp.broadcast_to(bspt[:, g:g + 1], (GMLP_CHUNK, gw)))
        s_rows.append(jnp.concatenate(cols, axis=1))
    o_b = (u * jnp.concatenate(s_rows, axis=0)).astype(BF16)

    rc, ra, rb = rc_ref[...], ra_ref[...], rb_ref[...]
    qd = _qk_norm_rope(zq, qg_ref[...], rc, ra, rb, split=False)
    kfull_ref[WINDOW:WINDOW + T, :] = _qk_norm_rope(zk, kg_ref[...], rc, ra, rb)
    vfull_ref[WINDOW:WINDOW + T, :] = zvd
    half = D_MODEL // 2
    partial = {}

    def branch_half(i, o_b16, j):
        partial[(i, j)] = _mm(o_b16, wbr_ref[i, :, j * half:(j + 1) * half])

    swa_fill = [functools.partial(branch_half, i, o, j) for i, o in ((2, o_c), (0, o_a)) for j in range(2)]

    rows = SWA_GROUP * WINDOW
    qi = _iota((rows, 2 * WINDOW), 0) % WINDOW
    kj = _iota((rows, 2 * WINDOW), 1)
    own_ok = jnp.logical_and(kj >= WINDOW, kj - WINDOW <= qi)
    prev_ok = jnp.logical_and(kj < WINDOW, kj > qi)
    scale = SWA_HEAD_DIM ** -0.5
    o_blocks = []
    for qb in range(T // WINDOW):
        rs = slice(qb * WINDOW, (qb + 1) * WINDOW)
        kblk = kfull_ref[qb * WINDOW:(qb + 2) * WINDOW, :].astype(BF16)
        vblk = vfull_ref[qb * WINDOW:(qb + 2) * WINDOW, :].astype(BF16)
        if qb == 0:
            valid = jnp.logical_or(own_ok, jnp.logical_and(prev_ok, t > 0))
        else:
            valid = jnp.logical_or(own_ok, prev_ok)
        outs = []
        for g in range(SWA_KV_HEADS):
            q_stack = jnp.concatenate(
                [_swa_q_for_group(qd[rs], g, i) for i in range(SWA_GROUP)], axis=0).astype(BF16)
            s = jnp.where(valid, _mm_nt(q_stack, kblk) * scale, -jnp.inf)
            sink = _sink_column(sinks_ref, layer * SWA_HEADS + g * SWA_GROUP, rows, WINDOW)
            m = jnp.maximum(jnp.max(s, axis=-1, keepdims=True), sink)
            p = jnp.exp(s - m)
            den = jnp.sum(p, axis=-1, keepdims=True) + jnp.exp(sink - m)
            o = _mm(p.astype(BF16), vblk) / den
            outs.append([o[i * WINDOW:(i + 1) * WINDOW] for i in range(SWA_GROUP)])
            if swa_fill:
                swa_fill.pop(0)()
        o_blocks.append(_swa_assemble(outs, WINDOW))
    for fill in swa_fill:
        fill()
    o_d = jnp.concatenate(o_blocks, axis=0).astype(BF16)
    kfull_ref[0:WINDOW, :] = kfull_ref[T:T + WINDOW, :]
    vfull_ref[0:WINDOW, :] = vfull_ref[T:T + WINDOW, :]

    merged = (gates_ref[0] * jnp.concatenate([partial[(0, 0)], partial[(0, 1)]], axis=1)
              + gates_ref[2] * jnp.concatenate([partial[(2, 0)], partial[(2, 1)]], axis=1))
    merged = merged + gates_ref[1] * branch(1, o_b)
    merged = merged + gates_ref[3] * branch(3, o_d)
    y_ref[0] = x + _mm(merged.astype(BF16), wout_ref[...])

    @pl.when(t == last)
    def _():
        conv_ref[0] = hbuf_ref[CONV_LEAD:CONV_PAD, :]
        ko_ref[0] = kfull_ref[0:WINDOW, :]
        vo_ref[0] = vfull_ref[0:WINDOW, :]
        gla_ref[0] = gstate_ref[...]


def _layer_spec(arr, layer):
    nd = arr.ndim
    return pl.BlockSpec((None,) + tuple(arr.shape[1:]), lambda *_: (layer,) + (0,) * (nd - 1),
                        pipeline_mode=pl.Buffered(1))


def _full_spec(shape):
    nd = len(shape)
    return pl.BlockSpec(shape, lambda *_: (0,) * nd)


def _compiler_params(n_axes):
    return pltpu.CompilerParams(
        dimension_semantics=("arbitrary",) * n_axes, vmem_limit_bytes=VMEM_LIMIT_BYTES)


def _prompt_mixer(x, p, tables, layer):
    B, L, _ = x.shape
    T = PROMPT_TILE
    weights = tuple(p[n] for n in (
        "n1g", "win", "bin", "wal", "bal", "glag", "lng", "lnb", "wsp", "bspt",
        "cw", "cb", "clg", "clb", "qg", "kg", "wg", "bg", "wbr", "wout"))
    tab_spec = pl.BlockSpec((T, LANES), lambda b, t, *_: (t, 0))
    grid_spec = pltpu.PrefetchScalarGridSpec(
        num_scalar_prefetch=1,
        grid=(B, L // T),
        in_specs=[pl.BlockSpec((1, T, D_MODEL), lambda b, t, *_: (b, t, 0)),
                  tab_spec, tab_spec, tab_spec]
                 + [_layer_spec(w, layer) for w in weights],
        out_specs=[pl.BlockSpec((1, T, D_MODEL), lambda b, t, *_: (b, t, 0)),
                   pl.BlockSpec((1, QK, GLA_DV), lambda b, t, *_: (b, 0, 0)),
                   pl.BlockSpec((1, CONV_K - 1, CONV_WIDTH), lambda b, t, *_: (b, 0, 0)),
                   pl.BlockSpec((1, WINDOW, SWA_KV), lambda b, t, *_: (b, 0, 0)),
                   pl.BlockSpec((1, WINDOW, SWA_KV), lambda b, t, *_: (b, 0, 0))],
        scratch_shapes=[pltpu.VMEM((QK, GLA_DV), F32),
                        pltpu.VMEM((T + CONV_PAD, CONV_WIDTH), F32),
                        pltpu.VMEM((SUBLANES - 1, T + CONV_PAD, CONV_WIDTH), F32),
                        pltpu.VMEM((T + WINDOW, SWA_KV), F32),
                        pltpu.VMEM((T + WINDOW, SWA_KV), F32),
                        pltpu.VMEM((N_BRANCH, T, D_MODEL), F32)])
    return pl.pallas_call(
        functools.partial(_prompt_mixer_kernel, layer=layer),
        grid_spec=grid_spec,
        out_shape=[jax.ShapeDtypeStruct((B, L, D_MODEL), F32),
                   jax.ShapeDtypeStruct((B, QK, GLA_DV), F32),
                   jax.ShapeDtypeStruct((B, CONV_K - 1, CONV_WIDTH), F32),
                   jax.ShapeDtypeStruct((B, WINDOW, SWA_KV), F32),
                   jax.ShapeDtypeStruct((B, WINDOW, SWA_KV), F32)],
        compiler_params=_compiler_params(2),
        name="prompt_mixer",
    )(p["sinks"], x, *tables, *weights)


def _ffn_kernel(x_ref, g_ref, wgu_ref, wd_ref, y_ref, h_ref):
    x = x_ref[...]
    xb = _rms_rows(x, g_ref[...]).astype(BF16)
    for c in range(FFN_HIDDEN // FFN_CHUNK):
        c0 = c * FFN_CHUNK
        gate = _mm(xb, wgu_ref[:, c0:c0 + FFN_CHUNK])
        up = _mm(xb, wgu_ref[:, FFN_HIDDEN + c0:FFN_HIDDEN + c0 + FFN_CHUNK])
        h_ref[:, c0:c0 + FFN_CHUNK] = (_silu(gate) * up).astype(BF16)
    y_ref[...] = x + _mm(h_ref[...], wd_ref[...])


def _ffn(x2d, p, layer):
    n = x2d.shape[0]
    tile = min(FFN_TILE, n)
    return pl.pallas_call(
        _ffn_kernel,
        grid=(n // tile,),
        in_specs=[pl.BlockSpec((tile, D_MODEL), lambda i: (i, 0)),
                  _layer_spec(p["n2g"], layer), _layer_spec(p["wgu"], layer), _layer_spec(p["wd"], layer)],
        out_specs=pl.BlockSpec((tile, D_MODEL), lambda i: (i, 0)),
        out_shape=jax.ShapeDtypeStruct((n, D_MODEL), F32),
        scratch_shapes=[pltpu.VMEM((tile, FFN_HIDDEN), BF16)],
        compiler_params=_compiler_params(1),
        name="swiglu",
    )(x2d, p["n2g"], p["wgu"], p["wd"])


def _sample_proj_kernel(x_ref, n1g_ref, win_ref, bin_ref, wal_ref, bal_ref, z_ref, ya_ref):
    xb = _rms_rows(x_ref[...], n1g_ref[...]).astype(BF16)
    z = _mm(xb, win_ref[...]) + bin_ref[...]
    z_ref[...] = z
    ya_ref[...] = _mm(z[:, C_LR:C_LR + LANES].astype(BF16), wal_ref[...]) + bal_ref[...]


def _sample_proj(x2d, p, layer):
    n = x2d.shape[0]
    weights = tuple(p[k] for k in ("n1g", "win", "bin", "wal", "bal"))
    return pl.pallas_call(
        _sample_proj_kernel,
        grid=(1,),
        in_specs=[_full_spec(x2d.shape)] + [_layer_spec(w, layer) for w in weights],
        out_specs=[_full_spec((n, IN_COLS_PACKED)), _full_spec((n, QK))],
        out_shape=[jax.ShapeDtypeStruct((n, IN_COLS_PACKED), F32),
                   jax.ShapeDtypeStruct((n, QK), F32)],
        compiler_params=_compiler_params(1),
        name="sample_proj",
    )(x2d, *weights)


def _sample_sequence(j, layer, sinks_ref, z_ref, ya_ref, s0_ref, hist_ref, kc_ref, vc_ref,
                     rc, ra, rb, glag, lng, lnb, wsc_ref, bsc, wsh_ref, cb, clg, clb, qg, kg,
                     obr_ref, s1_ref, convo_ref, ko_ref, vo_ref, gvo_ref, hext_ref):
    R = SAMPLE_ROWS
    nt = SAMPLE_TOKENS
    z = z_ref[j]
    row = _iota((R, 1), 0)
    head_ones = _group_ones(SWA_Q, SWA_HEAD_DIM)

    q = z[:, C_AQ:C_AQ + QK] * (GLA_DK ** -0.5)
    k = z[:, C_AK:C_AK + QK]
    v = z[:, C_AV:C_AV + QV]
    r = z[:, C_AR:C_AR + QV]
    zq = z[:, C_DQ:C_DQ + SWA_Q]
    zk = z[:, C_DK:C_DK + SWA_KV]
    zv = z[:, C_DV:C_DV + SWA_KV]
    ms_q = _mm_split_lhs(zq * zq, head_ones) * (1.0 / SWA_HEAD_DIM)
    ms_k = _mm_split_lhs(zk * zk, head_ones[:SWA_KV, :SWA_KV]) * (1.0 / SWA_HEAD_DIM)
    la = jnp.where(row < nt, _log_sigmoid(ya_ref[j]) * (1.0 / GLA_TAU), 0.0)
    bcum = jnp.zeros_like(la)
    for s in range(nt):
        bcum = bcum + jnp.where(row >= s, jnp.broadcast_to(la[s:s + 1, :], la.shape), 0.0)
    btot_row = bcum[nt - 1:nt, :]
    q_in = q * jnp.exp(bcum)
    k_out = k * jnp.exp(-bcum)
    k_dec = k * jnp.exp(jnp.broadcast_to(btot_row, la.shape) - bcum)
    s0 = s0_ref[j]
    s0_b = s0.astype(BF16)
    head_sel = (_iota((QK, LANES), 0) // GLA_DK == _iota((QK, LANES), 1)).astype(BF16)
    head_exp = (_iota((LANES, QV), 0) == _iota((LANES, QV), 1) // GLA_DV).astype(BF16)
    lane_head = _iota((R, QK), 1) // GLA_DK
    att_s = [_mm_split_lhs(q_in * jnp.broadcast_to(k_out[s:s + 1, :], q_in.shape), head_sel)
             for s in range(nt)]
    o_state = [_mm(jnp.where(lane_head == hh, q_in, 0.0).astype(BF16), s0_b) for hh in range(GLA_HEADS)]
    upd = _mm_tn(k_dec.astype(BF16).astype(F32), v.astype(BF16).astype(F32))
    dcol = jnp.exp(jnp.broadcast_to(btot_row, (GLA_DV, QK)).T)

    u = _gelu(z[:, C_BU:C_BU + GMLP_WIDTH])
    gv = _ln_rows(_gelu(z[:, C_BV:C_BV + GMLP_WIDTH]), lng, lnb)
    gvo_ref[j] = gv
    sp = bsc
    for s in range(nt):
        sp = sp + jnp.where(row >= s, wsc_ref[s], 0.0) * jnp.broadcast_to(gv[s:s + 1, :], gv.shape)
    o_b = u * sp

    h = z[:, C_CA:C_CA + CONV_WIDTH] * jax.nn.sigmoid(z[:, C_CG:C_CG + CONV_WIDTH])
    hext_ref[j, 0:SUBLANES, :] = jnp.zeros((SUBLANES, CONV_WIDTH), F32)
    hext_ref[j, CONV_LEAD:CONV_PAD, :] = hist_ref[j]
    hext_ref[j, CONV_PAD:CONV_PAD + R, :] = h
    yield

    hext = hext_ref[j]
    conv = jnp.zeros((R, CONV_WIDTH), F32)
    for tt in range(nt):
        c_t = jnp.sum(hext * wsh_ref[tt], axis=0, keepdims=True)
        conv = jnp.where(row == tt, jnp.broadcast_to(c_t, conv.shape), conv)
    o_c = _silu(_ln_rows(conv + cb, clg, clb))
    convo_ref[j] = hext_ref[j, CONV_PAD + nt - (CONV_K - 1):CONV_PAD + nt, :]

    att_e = [_mm(a.astype(BF16), head_exp) for a in att_s]
    s1_ref[j] = dcol * s0 + jnp.concatenate(
        [upd[hh * GLA_DK:(hh + 1) * GLA_DK, hh * GLA_DV:(hh + 1) * GLA_DV] for hh in range(GLA_HEADS)],
        axis=0)

    qd = _rope(zq * lax.rsqrt(ms_q + NORM_EPS) * qg, rc, ra, rb)
    k_new = _rope(zk * lax.rsqrt(ms_k + NORM_EPS) * kg, rc, ra, rb)
    k_hist_b = kc_ref[j].astype(BF16)
    v_hist_b = vc_ref[j].astype(BF16)
    rows = SWA_GROUP * R
    qt = _iota((rows, WINDOW), 0) % R
    hist_ok = _iota((rows, WINDOW), 1) > qt
    qt1 = _iota((rows, 1), 0) % R
    scale = SWA_HEAD_DIM ** -0.5
    q_stacks, s_hists, s_news = [], [], []
    for g in range(SWA_KV_HEADS):
        q_stack = jnp.concatenate([_swa_q_for_group(qd, g, i) for i in range(SWA_GROUP)], axis=0)
        q_stacks.append(q_stack)
        s_hists.append(_mm_nt(q_stack.astype(BF16), k_hist_b))
        s_news.append([jnp.sum(q_stack * jnp.broadcast_to(k_new[s:s + 1, :], q_stack.shape),
                               axis=-1, keepdims=True) for s in range(nt)])
    yield

    o_intra = jnp.zeros((R, QV), F32)
    for s in range(nt):
        o_intra = o_intra + jnp.where(row >= s, att_e[s], 0.0) * jnp.broadcast_to(v[s:s + 1, :], (R, QV))
    o_a = _head_rms_gla(o_intra + jnp.concatenate(o_state, axis=1), glag) * _silu(r)

    sinks, ms, s_masked, n_masked = [], [], [], []
    for g in range(SWA_KV_HEADS):
        s_hist = jnp.where(hist_ok, s_hists[g] * scale, -jnp.inf)
        s_new = [jnp.where(qt1 >= s, s_news[g][s] * scale, -jnp.inf) for s in range(nt)]
        sink = _sink_column(sinks_ref, layer * SWA_HEADS + g * SWA_GROUP, rows, R)
        m = jnp.maximum(jnp.max(s_hist, axis=-1, keepdims=True), sink)
        for s in range(nt):
            m = jnp.maximum(m, s_new[s])
        sinks.append(sink)
        ms.append(m)
        s_masked.append(s_hist)
        n_masked.append(s_new)
    yield

    dens, pvs, p_news = [], [], []
    for g in range(SWA_KV_HEADS):
        p_hist = jnp.exp(s_masked[g] - ms[g])
        dens.append(jnp.sum(p_hist, axis=-1, keepdims=True) + jnp.exp(sinks[g] - ms[g]))
        pvs.append(_mm(p_hist.astype(BF16), v_hist_b))
        p_news.append([jnp.exp(n_masked[g][s] - ms[g]) for s in range(nt)])
    yield

    outs = []
    for g in range(SWA_KV_HEADS):
        o, den = pvs[g], dens[g]
        for s in range(nt):
            den = den + p_news[g][s]
            o = o + p_news[g][s] * jnp.broadcast_to(zv[s:s + 1, :], o.shape)
        o = o / den
        outs.append([o[i * R:(i + 1) * R] for i in range(SWA_GROUP)])
    o_d = _swa_assemble(outs, R)

    ko_ref[j, 0:WINDOW - nt, :] = kc_ref[j, nt:WINDOW, :]
    ko_ref[j, WINDOW - nt:WINDOW, :] = k_new[0:nt, :]
    vo_ref[j, 0:WINDOW - nt, :] = vc_ref[j, nt:WINDOW, :]
    vo_ref[j, WINDOW - nt:WINDOW, :] = zv[0:nt, :]

    obr_ref[j] = jnp.concatenate([o_a, o_b, o_c, o_d], axis=1)


def _sample_mixer_kernel(
        sinks_ref,
        z_ref, ya_ref, s0_ref, hist_ref, kc_ref, vc_ref,
        rc_ref, ra_ref, rb_ref,
        glag_ref, lng_ref, lnb_ref, wsc_ref, bsc_ref,
        wsh_ref, cb_ref, clg_ref, clb_ref, qg_ref, kg_ref,
        s1_all_ref, convo_all_ref, ko_all_ref, vo_all_ref,
        obr_ref, s1_ref, convo_ref, ko_ref, vo_ref, gvo_ref,
        hext_ref, *, layer):
    del s1_all_ref, convo_all_ref, ko_all_ref, vo_all_ref
    consts = (rc_ref[...], ra_ref[...], rb_ref[...], glag_ref[...], lng_ref[...], lnb_ref[...])
    sequences = [
        _sample_sequence(
            j, layer, sinks_ref, z_ref, ya_ref, s0_ref, hist_ref, kc_ref, vc_ref,
            *consts, wsc_ref, bsc_ref[...], wsh_ref, cb_ref[...], clg_ref[...], clb_ref[...],
            qg_ref[...], kg_ref[...],
            obr_ref, s1_ref, convo_ref, ko_ref, vo_ref, gvo_ref, hext_ref)
        for j in range(SAMPLE_GROUP)]
    while sequences:
        sequences = [seq for seq in sequences if next(seq, StopIteration) is not StopIteration]


def _sample_mixer(z3, ya3, s0, hist, kc, vc, p, tables, stacked_states, layer):
    nb = z3.shape[0]
    R = SAMPLE_ROWS
    G = SAMPLE_GROUP
    weights = tuple(p[n] for n in ("glag", "lng", "lnb", "wsc", "bsc", "wsh", "cb", "clg", "clb", "qg", "kg"))

    def per_group(shape):
        nd = len(shape)
        return pl.BlockSpec((G,) + tuple(shape[1:]), lambda b, *_: (b,) + (0,) * (nd - 1))

    def per_group_of_layer(arr):
        nd = arr.ndim
        return pl.BlockSpec((None, G) + tuple(arr.shape[2:]), lambda b, *_: (layer, b) + (0,) * (nd - 2))

    def layer_block(arr):
        nd = arr.ndim
        return pl.BlockSpec((None,) + tuple(arr.shape[1:]), lambda b, *_: (layer,) + (0,) * (nd - 1))

    out_shapes = [jax.ShapeDtypeStruct((nb, R, N_BRANCH * BRANCH_WIDTH), F32)] \
        + [jax.ShapeDtypeStruct(a.shape, F32) for a in stacked_states] \
        + [jax.ShapeDtypeStruct((nb, R, GMLP_WIDTH), F32)]
    operands = (z3, ya3, s0, hist, kc, vc, *tables, *weights)
    n_in = 1 + len(operands)
    grid_spec = pltpu.PrefetchScalarGridSpec(
        num_scalar_prefetch=1,
        grid=(nb // G,),
        in_specs=[per_group(a.shape) for a in (z3, ya3)]
                 + [per_group_of_layer(a) for a in (s0, hist, kc, vc)]
                 + [_full_spec(tb.shape) for tb in tables]
                 + [layer_block(w) for w in weights]
                 + [pl.BlockSpec(memory_space=pl.ANY) for _ in stacked_states],
        out_specs=[per_group(out_shapes[0].shape)]
                  + [per_group_of_layer(a) for a in stacked_states]
                  + [per_group(out_shapes[-1].shape)],
        scratch_shapes=[pltpu.VMEM((G, CONV_PAD + R, CONV_WIDTH), F32)])
    return pl.pallas_call(
        functools.partial(_sample_mixer_kernel, layer=layer),
        grid_spec=grid_spec,
        out_shape=out_shapes,
        input_output_aliases={n_in + i: 1 + i for i in range(len(stacked_states))},
        compiler_params=_compiler_params(1),
        name="sample_mixer",
    )(p["sinks"], *operands, *stacked_states)


def _sample_merge_kernel(x_ref, obr_ref, n1g_ref, wg_ref, bg_ref, wbr_ref, wout_ref, y_ref):
    x = x_ref[...]
    xb = _rms_rows(x, n1g_ref[...]).astype(BF16)
    merged = jnp.zeros(x.shape, F32)
    for i in range(N_BRANCH):
        gate = jax.nn.sigmoid(
            _mm(xb, wg_ref[:, i * D_MODEL:(i + 1) * D_MODEL]) + bg_ref[:, i * D_MODEL:(i + 1) * D_MODEL])
        o_i = obr_ref[:, i * BRANCH_WIDTH:(i + 1) * BRANCH_WIDTH].astype(BF16)
        merged = merged + gate * _mm(o_i, wbr_ref[i])
    y_ref[...] = x + _mm(merged.astype(BF16), wout_ref[...])


def _sample_merge(x2d, obr, p, layer):
    weights = tuple(p[k] for k in ("n1g", "wg", "bg", "wbr", "wout"))
    return pl.pallas_call(
        _sample_merge_kernel,
        grid=(1,),
        in_specs=[_full_spec(x2d.shape), _full_spec(obr.shape)] + [_layer_spec(w, layer) for w in weights],
        out_specs=_full_spec(x2d.shape),
        out_shape=jax.ShapeDtypeStruct(x2d.shape, F32),
        compiler_params=_compiler_params(1),
        name="sample_merge",
    )(x2d, obr, *weights)


def _rope_tables(pos):
    half = ROPE_DIM // 2
    inv = jnp.exp(-math.log(ROPE_THETA) * jnp.arange(half, dtype=F32) * (2.0 / ROPE_DIM))
    ang = pos.astype(F32)[:, None] * inv[None, :]
    cos, sin = jnp.cos(ang), jnp.sin(ang)
    n = pos.shape[0]
    pad = jnp.zeros((n, SWA_HEAD_DIM - ROPE_DIM), F32)
    zero = jnp.zeros((n, half), F32)
    c_head = jnp.concatenate([cos, cos, pad + 1.0], axis=1)
    a_head = jnp.concatenate([-sin, zero, pad], axis=1)
    b_head = jnp.concatenate([zero, sin, pad], axis=1)
    rep = LANES // SWA_HEAD_DIM
    return tuple(jnp.tile(tb, (1, rep)) for tb in (c_head, a_head, b_head))


def _rows(v):
    return v.reshape(v.shape[0], 1, -1).astype(F32)


def _stacked_params(norm1_g, w_in, b_in, w_alpha2, b_alpha, gla_norm_g, gmlp_ln_g, gmlp_ln_b,
                    w_spatial, b_spatial, conv_w, conv_b, conv_ln_g, conv_ln_b, q_norm_g, k_norm_g,
                    sinks, w_gate, b_gate, w_branch, w_out, norm2_g, w_gate_up, w_down):
    o_lr = 2 * QK + 2 * QV
    o_rest = o_lr + GLA_RANK

    def repack(m, dtype):
        pad = jnp.zeros(m.shape[:-1] + (LANES - GLA_RANK,), dtype)
        return jnp.concatenate(
            [m[..., :o_lr].astype(dtype), m[..., o_rest:].astype(dtype),
             m[..., o_lr:o_rest].astype(dtype), pad], axis=-1)

    nt = SAMPLE_TOKENS
    gw = GMLP_WIDTH // GMLP_GROUPS
    pad_rows = ((0, 0), (0, SAMPLE_ROWS - nt), (0, 0))
    wsc = jnp.stack([
        jnp.pad(jnp.repeat(jnp.swapaxes(w_spatial[:, :, :nt, s], 1, 2), gw, axis=2), pad_rows)
        for s in range(nt)], axis=1)
    bsc = jnp.pad(jnp.repeat(jnp.swapaxes(b_spatial[:, :, :nt], 1, 2), gw, axis=2), pad_rows)
    wsh = jnp.stack([
        jnp.pad(conv_w, ((0, 0), (CONV_LEAD + t, CONV_PAD + SAMPLE_ROWS - CONV_LEAD - t - CONV_K), (0, 0)))
        for t in range(nt)], axis=1)
    return {
        "n1g": _rows(norm1_g),
        "win": repack(w_in, BF16),
        "bin": repack(b_in, F32).reshape(DEPTH, 1, -1),
        "wal": jnp.pad(w_alpha2, ((0, 0), (0, LANES - GLA_RANK), (0, 0))).astype(BF16),
        "bal": _rows(b_alpha),
        "glag": _rows(gla_norm_g),
        "lng": _rows(gmlp_ln_g),
        "lnb": _rows(gmlp_ln_b),
        "wsp": w_spatial.astype(F32),
        "bspt": jnp.swapaxes(b_spatial, 1, 2).astype(F32),
        "wsc": wsc.astype(F32),
        "bsc": bsc.astype(F32),
        "cw": jnp.pad(conv_w, ((0, 0), (0, CONV_PAD - CONV_K), (0, 0))).astype(F32),
        "wsh": wsh.astype(F32),
        "cb": _rows(conv_b),
        "clg": _rows(conv_ln_g),
        "clb": _rows(conv_ln_b),
        "qg": _rows(jnp.tile(q_norm_g, (1, SWA_HEADS))),
        "kg": _rows(jnp.tile(k_norm_g, (1, SWA_KV_HEADS))),
        "sinks": sinks.reshape(-1).astype(F32),
        "wg": w_gate.astype(BF16),
        "bg": _rows(b_gate),
        "wbr": w_branch.astype(BF16),
        "wout": w_out.astype(BF16),
        "n2g": _rows(norm2_g),
        "wgu": w_gate_up.astype(BF16),
        "wd": w_down.astype(BF16),
    }


def kernel(x_prompt, x_sample, state_gla, state_conv, cache_swa_k, cache_swa_v, norm1_g, w_in, b_in, w_alpha2, b_alpha, gla_norm_g, gmlp_ln_g, gmlp_ln_b, w_spatial, b_spatial, conv_w, conv_b, conv_ln_g, conv_ln_b, q_norm_g, k_norm_g, sinks, w_gate, b_gate, w_branch, w_out, norm2_g, w_gate_up, w_down):
    bp, lp, _ = x_prompt.shape
    bs, ls, _ = x_sample.shape
    R = SAMPLE_ROWS
    prompt_tables = _rope_tables(jnp.arange(lp))
    sample_tables = tuple(
        jnp.pad(tb, ((0, R - ls), (0, 0))) for tb in _rope_tables(PAST_LEN + jnp.arange(ls)))
    p = _stacked_params(norm1_g, w_in, b_in, w_alpha2, b_alpha, gla_norm_g, gmlp_ln_g, gmlp_ln_b,
                        w_spatial, b_spatial, conv_w, conv_b, conv_ln_g, conv_ln_b, q_norm_g,
                        k_norm_g, sinks, w_gate, b_gate, w_branch, w_out, norm2_g, w_gate_up, w_down)

    gla_in = state_gla.reshape(DEPTH, bs, QK, GLA_DV)
    swa_k_in = cache_swa_k.reshape(DEPTH, bs, WINDOW, SWA_KV)
    swa_v_in = cache_swa_v.reshape(DEPTH, bs, WINDOW, SWA_KV)

    sample_states = [jnp.zeros(a.shape, F32) for a in (gla_in, state_conv, swa_k_in, swa_v_in)]

    yp = x_prompt
    ys = x_sample.reshape(bs * ls, D_MODEL)
    outs = {k: [] for k in ("gla_p", "conv_p", "kp", "vp", "gm")}
    for l in range(DEPTH):
        yp, g1, c1, k1, v1 = _prompt_mixer(yp, p, prompt_tables, l)
        yp = _ffn(yp.reshape(bp * lp, D_MODEL), p, l).reshape(bp, lp, D_MODEL)
        outs["gla_p"].append(g1.reshape(bp, GLA_HEADS, GLA_DK, GLA_DV))
        outs["conv_p"].append(c1)
        outs["kp"].append(k1.reshape(bp, WINDOW, SWA_KV_HEADS, SWA_HEAD_DIM))
        outs["vp"].append(v1.reshape(bp, WINDOW, SWA_KV_HEADS, SWA_HEAD_DIM))
        z, ya = _sample_proj(ys, p, l)
        pad_rows = ((0, 0), (0, R - ls), (0, 0))
        z3 = jnp.pad(z.reshape(bs, ls, IN_COLS_PACKED), pad_rows)
        ya3 = jnp.pad(ya.reshape(bs, ls, QK), pad_rows)
        obr, *sample_states, gv2 = _sample_mixer(
            z3, ya3, gla_in, state_conv, swa_k_in, swa_v_in, p, sample_tables, sample_states, l)
        ys = _sample_merge(ys, obr[:, :ls].reshape(bs * ls, N_BRANCH * BRANCH_WIDTH), p, l)
        ys = _ffn(ys, p, l)
        outs["gm"].append(gv2[:, :ls])
    st = lambda name: jnp.stack(outs[name], 0)
    gla_s, conv_s, ks, vs = sample_states
    kv_shape = (DEPTH, bs, WINDOW, SWA_KV_HEADS, SWA_HEAD_DIM)
    return (yp, ys.reshape(bs, ls, D_MODEL), st("gla_p"), gla_s.reshape(state_gla.shape), st("conv_p"),
            conv_s, st("kp"), st("vp"), ks.reshape(kv_shape), vs.reshape(kv_shape), st("gm"))
```

```python
import functools
import math

import jax
import jax.numpy as jnp
from jax import lax
from jax.experimental import pallas as pl
from jax.experimental.pallas import tpu as pltpu

F32 = jnp.float32
BF16 = jnp.bfloat16

D_MODEL = 1024
DEPTH = 2
PAST_LEN = 16384
GLA_HEADS = 4
GLA_DK = 64
GLA_DV = 128
GLA_RANK = 16
GLA_TAU = 16.0
GLA_CHUNK = 64
GMLP_GROUPS = 4
GMLP_WIDTH = 512
GMLP_CHUNK = 128
CONV_WIDTH = 512
CONV_K = 31
SWA_HEADS = 8
SWA_KV_HEADS = 2
SWA_GROUP = SWA_HEADS // SWA_KV_HEADS
SWA_HEAD_DIM = 64
WINDOW = 128
ROPE_DIM = SWA_HEAD_DIM // 4
ROPE_THETA = 500000.0
N_BRANCH = 4
BRANCH_WIDTH = 512
FFN_HIDDEN = 2816
NORM_EPS = 1e-6
LN_EPS = 1e-5

LANES = 128
SUBLANES = 8
VMEM_LIMIT_BYTES = 56 * 1024 * 1024

QK = GLA_HEADS * GLA_DK
QV = GLA_HEADS * GLA_DV
SWA_Q = SWA_HEADS * SWA_HEAD_DIM
SWA_KV = SWA_KV_HEADS * SWA_HEAD_DIM

C_AQ = 0
C_AK = C_AQ + QK
C_AV = C_AK + QK
C_AR = C_AV + QV
C_BU = C_AR + QV
C_BV = C_BU + GMLP_WIDTH
C_CA = C_BV + GMLP_WIDTH
C_CG = C_CA + CONV_WIDTH
C_DQ = C_CG + CONV_WIDTH
C_DK = C_DQ + SWA_Q
C_DV = C_DK + SWA_KV
C_LR = C_DV + SWA_KV
IN_COLS_PACKED = C_LR + LANES

PROMPT_TILE = 256
FFN_TILE = 512
FFN_CHUNK = 256
GATE_PART = 256
CONV_PAD = 32
CONV_LEAD = CONV_PAD - (CONV_K - 1)
SAMPLE_ROWS = 8
SAMPLE_TOKENS = 4
SAMPLE_GROUP = 8


def _mm(a, b):
    return jnp.dot(a, b, preferred_element_type=F32)


def _mm_nt(a, b):
    return lax.dot_general(a, b, (((1,), (1,)), ((), ())), preferred_element_type=F32)


def _mm_tn(a, b):
    return lax.dot_general(a, b, (((0,), (0,)), ((), ())), preferred_element_type=F32)


def _hi_lo(a):
    hi = a.astype(BF16)
    lo = (a - hi.astype(F32)).astype(BF16)
    return hi, lo


def _mm_split_lhs(a_f32, m_bf16):
    hi, lo = _hi_lo(a_f32)
    return _mm(hi, m_bf16) + _mm(lo, m_bf16)


def _rms_rows(x, g):
    return x * lax.rsqrt(jnp.mean(x * x, axis=-1, keepdims=True) + NORM_EPS) * g


def _ln_rows(x, g, b):
    mu = jnp.mean(x, axis=-1, keepdims=True)
    xc = x - mu
    var = jnp.mean(xc * xc, axis=-1, keepdims=True)
    return xc * lax.rsqrt(var + LN_EPS) * g + b


def _gelu(x):
    return 0.5 * x * (1.0 + lax.erf(x * (1.0 / math.sqrt(2.0))))


def _silu(x):
    return x * jax.nn.sigmoid(x)


def _log_sigmoid(x):
    return jnp.minimum(x, 0.0) - jnp.log(1.0 + jnp.exp(-jnp.abs(x)))


def _iota(shape, dim):
    return lax.broadcasted_iota(jnp.int32, shape, dim)


def _group_ones(n, group, dtype=BF16):
    return (_iota((n, n), 0) // group == _iota((n, n), 1) // group).astype(dtype)


def _group_mean_sq(x, group):
    return _mm_split_lhs(x * x, _group_ones(x.shape[-1], group)) * (1.0 / group)


def _rope_slab(x, c, a, b):
    return x * c + pltpu.roll(x, LANES - ROPE_DIM // 2, 1) * a + pltpu.roll(x, ROPE_DIM // 2, 1) * b


def _rope(x, c, a, b):
    n = x.shape[-1] // LANES
    if n == 1:
        return _rope_slab(x, c, a, b)
    return jnp.concatenate(
        [_rope_slab(x[:, i * LANES:(i + 1) * LANES], c, a, b) for i in range(n)], axis=1)


def _qk_norm_rope(z, g, c, a, b, split=True):
    if split:
        ms = _group_mean_sq(z, SWA_HEAD_DIM)
    else:
        ms = _mm((z * z).astype(BF16), _group_ones(z.shape[-1], SWA_HEAD_DIM)) * (1.0 / SWA_HEAD_DIM)
    return _rope(z * lax.rsqrt(ms + NORM_EPS) * g, c, a, b)


def _head_rms_gla(o, g):
    return jnp.concatenate(
        [_rms_rows(o[:, h * GLA_DV:(h + 1) * GLA_DV], g) for h in range(GLA_HEADS)], axis=1)


def _swa_q_for_group(q, g, i):
    hq = g * SWA_GROUP + i
    slab = q[:, (hq // 2) * LANES:(hq // 2 + 1) * LANES]
    if hq % 2 != g:
        slab = pltpu.roll(slab, SWA_HEAD_DIM, 1)
    lane_head = _iota(slab.shape, 1) // SWA_HEAD_DIM
    return jnp.where(lane_head == g, slab, 0.0)


def _swa_assemble(outs, rows):
    lane_half = _iota((rows, LANES), 1) // SWA_HEAD_DIM
    slabs = []
    for s in range(SWA_HEADS // 2):
        pair = []
        for hq in (2 * s, 2 * s + 1):
            g, i = hq // SWA_GROUP, hq % SWA_GROUP
            o = outs[g][i]
            if hq % 2 != g:
                o = pltpu.roll(o, SWA_HEAD_DIM, 1)
            pair.append(o)
        slabs.append(jnp.where(lane_half == 0, pair[0], pair[1]))
    return jnp.concatenate(slabs, axis=1)


def _sink_column(sinks_ref, base, rows, rows_per_head):
    row_head = _iota((rows, 1), 0) // rows_per_head
    sink = jnp.zeros((rows, 1), F32)
    for i in range(SWA_GROUP):
        sink = jnp.where(row_head == i, sinks_ref[base + i], sink)
    return sink


def _prompt_mixer_kernel(
        sinks_ref,
        x_ref, rc_ref, ra_ref, rb_ref,
        n1g_ref, win_ref, bin_ref, wal_ref, bal_ref, glag_ref,
        lng_ref, lnb_ref, wsp_ref, bspt_ref,
        cw_ref, cb_ref, clg_ref, clb_ref,
        qg_ref, kg_ref,
        wg_ref, bg_ref, wbr_ref, wout_ref,
        y_ref, gla_ref, conv_ref, ko_ref, vo_ref,
        gstate_ref, hbuf_ref, hsh_ref, kfull_ref, vfull_ref, gates_ref,
        *, layer):
    T = PROMPT_TILE
    t = pl.program_id(1)
    last = pl.num_programs(1) - 1

    @pl.when(t == 0)
    def _():
        gstate_ref[...] = jnp.zeros_like(gstate_ref)
        hbuf_ref[0:CONV_PAD, :] = jnp.zeros((CONV_PAD, CONV_WIDTH), F32)
        kfull_ref[0:WINDOW, :] = jnp.zeros((WINDOW, SWA_KV), F32)
        vfull_ref[0:WINDOW, :] = jnp.zeros((WINDOW, SWA_KV), F32)

    x = x_ref[0]
    xb = _rms_rows(x, n1g_ref[...]).astype(BF16)

    def proj(c0, width):
        return _mm(xb, win_ref[:, c0:c0 + width]) + bin_ref[:, c0:c0 + width]

    def gate_part(i, j):
        c0 = i * D_MODEL + j * GATE_PART
        gates_ref[i, :, j * GATE_PART:(j + 1) * GATE_PART] = jax.nn.sigmoid(
            _mm(xb, wg_ref[:, c0:c0 + GATE_PART]) + bg_ref[:, c0:c0 + GATE_PART])

    def branch(i, o_b16):
        return _mm(o_b16, wbr_ref[i, :, :D_MODEL])


    h = proj(C_CA, CONV_WIDTH) * jax.nn.sigmoid(proj(C_CG, CONV_WIDTH))
    q = proj(C_AQ, QK) * (GLA_DK ** -0.5)
    k = proj(C_AK, QK)
    v = proj(C_AV, QV)
    kvl = proj(C_DK, 2 * SWA_KV + LANES)
    zk, zvd, lr = kvl[:, :SWA_KV], kvl[:, SWA_KV:2 * SWA_KV], kvl[:, 2 * SWA_KV:]
    hbuf_ref[CONV_PAD:CONV_PAD + T, :] = h
    taps_of = {r: [kk for kk in range(CONV_K) if (CONV_LEAD + kk) % SUBLANES == r] for r in range(SUBLANES)}
    for r in range(1, SUBLANES):
        span = max(CONV_LEAD + kk for kk in taps_of[r]) - r
        hsh_ref[r - 1, 0:span + T, :] = hbuf_ref[r:r + span + T, :]

    z = {}
    conv_fill = [
        lambda: z.update(u=proj(C_BU, GMLP_WIDTH)),
        lambda: z.update(gv=proj(C_BV, GMLP_WIDTH)),
        lambda: z.update(q=proj(C_DQ, SWA_Q)),
        lambda: z.update(r=proj(C_AR, QV)),
        lambda: (gate_part(0, 0), gate_part(0, 1)),
        lambda: (gate_part(0, 2), gate_part(0, 3)),
        lambda: (gate_part(1, 0), gate_part(1, 1)),
        lambda: (gate_part(1, 2), gate_part(1, 3)),
    ]
    acc = jnp.broadcast_to(cb_ref[...], (T, CONV_WIDTH))
    for r_ in range(SUBLANES):
        for kk in taps_of[r_]:
            a = CONV_LEAD + kk - r_
            rows_k = hbuf_ref[a:a + T, :] if r_ == 0 else hsh_ref[r_ - 1, a:a + T, :]
            acc = acc + cw_ref[kk:kk + 1, :] * rows_k
        conv_fill[r_]()
    hbuf_ref[0:CONV_PAD, :] = hbuf_ref[T:T + CONV_PAD, :]
    o_c = _silu(_ln_rows(acc, clg_ref[...], clb_ref[...])).astype(BF16)
    zu, zgv, zq, r = z["u"], z["gv"], z["q"], z["r"]

    la = _log_sigmoid(_mm(lr.astype(BF16), wal_ref[...]) + bal_ref[...]) * (1.0 / GLA_TAU)
    gate_part(2, 0)
    gate_part(2, 1)

    n_chunks = T // GLA_CHUNK
    ri = _iota((T, T), 0)
    ci = _iota((T, T), 1)
    chunk_causal = jnp.logical_and((ri // GLA_CHUNK) == (ci // GLA_CHUNK), ci <= ri)
    la_hi, la_lo = _hi_lo(la)
    m_tril = chunk_causal.astype(BF16)
    bcum = _mm(m_tril, la_hi) + _mm(m_tril, la_lo)
    tot_rows = [bcum[(c + 1) * GLA_CHUNK - 1:(c + 1) * GLA_CHUNK, :] for c in range(n_chunks)]
    btot = jnp.concatenate([jnp.broadcast_to(tr, (GLA_CHUNK, QK)) for tr in tot_rows], axis=0)
    q_in = (q * jnp.exp(bcum)).astype(BF16)
    k_out = (k * jnp.exp(-bcum)).astype(BF16)
    k_dec_t = (k * jnp.exp(btot - bcum)).T.astype(BF16)
    v_b = v.astype(BF16)
    dec_t = jnp.exp(jnp.concatenate(
        [jnp.broadcast_to(tr, (GLA_DV, QK)) for tr in tot_rows], axis=0).T)

    lane_head = _iota((T, QK), 1) // GLA_DK
    o_heads = []
    for hh in range(GLA_HEADS):
        qh = jnp.where(lane_head == hh, q_in, jnp.zeros_like(q_in))
        att = jnp.where(chunk_causal, _mm_nt(qh, k_out), 0.0).astype(BF16)
        o_heads.append(_mm(att, v_b[:, hh * GLA_DV:(hh + 1) * GLA_DV]))
    o_intra = jnp.concatenate(o_heads, axis=1)
    gate_part(2, 2)
    gate_part(2, 3)

    bd_mask = (_iota((QK, QV), 0) // GLA_DK) == (_iota((QK, QV), 1) // GLA_DV)
    tok_chunk = _iota((GLA_DK, T), 1) // GLA_CHUNK
    o_state = []
    for c in range(n_chunks):
        rs = slice(c * GLA_CHUNK, (c + 1) * GLA_CHUNK)
        s_c = gstate_ref[...]
        s_b = s_c.astype(BF16)
        s_bd = jnp.where(bd_mask, jnp.concatenate([s_b] * GLA_HEADS, axis=1), jnp.zeros((QK, QV), BF16))
        o_state.append(_mm(q_in[rs], s_bd))
        upd = []
        for hh in range(GLA_HEADS):
            kt = k_dec_t[hh * GLA_DK:(hh + 1) * GLA_DK, :]
            kt = jnp.where(tok_chunk == c, kt, jnp.zeros_like(kt))
            upd.append(_mm(kt, v_b[:, hh * GLA_DV:(hh + 1) * GLA_DV]))
        gstate_ref[...] = dec_t[:, c * GLA_DV:(c + 1) * GLA_DV] * s_c + jnp.concatenate(upd, axis=0)
    o_gla = o_intra + jnp.concatenate(o_state, axis=0)
    gate_part(3, 0)
    gate_part(3, 1)
    o_a = (_head_rms_gla(o_gla, glag_ref[...]) * _silu(r)).astype(BF16)

    u = _gelu(zu)
    gv = _ln_rows(_gelu(zgv), lng_ref[...], lnb_ref[...])
    gate_part(3, 2)
    gate_part(3, 3)
    gv_b = gv.astype(BF16)
    gw = GMLP_WIDTH // GMLP_GROUPS
    tril = _iota((GMLP_CHUNK, GMLP_CHUNK), 1) <= _iota((GMLP_CHUNK, GMLP_CHUNK), 0)
    bspt = bspt_ref[...]
    s_rows = []
    for n in range(T // GMLP_CHUNK):
        rs = slice(n * GMLP_CHUNK, (n + 1) * GMLP_CHUNK)
        cols = []
        for g in range(GMLP_GROUPS):
            w = jnp.where(tril, wsp_ref[g], 0.0).astype(BF16)
            sg = _mm(w, gv_b[rs, g * gw:(g + 1) * gw])
            cols.append(sg + jnp.broadcast_to(bspt[:, g:g + 1], (GMLP_CHUNK, gw)))
        s_rows.append(jnp.concatenate(cols, axis=1))
    o_b = (u * jnp.concatenate(s_rows, axis=0)).astype(BF16)

    rc, ra, rb = rc_ref[...], ra_ref[...], rb_ref[...]
    qd = _qk_norm_rope(zq, qg_ref[...], rc, ra, rb, split=False)
    kfull_ref[WINDOW:WINDOW + T, :] = _qk_norm_rope(zk, kg_ref[...], rc, ra, rb)
    vfull_ref[WINDOW:WINDOW + T, :] = zvd
    half = D_MODEL // 2
    partial = {}

    def branch_half(i, o_b16, j):
        partial[(i, j)] = _mm(o_b16, wbr_ref[i, :, j * half:(j + 1) * half])

    swa_fill = [functools.partial(branch_half, i, o, j) for i, o in ((2, o_c), (0, o_a)) for j in range(2)]

    rows = SWA_GROUP * WINDOW
    qi = _iota((rows, 2 * WINDOW), 0) % WINDOW
    kj = _iota((rows, 2 * WINDOW), 1)
    own_ok = jnp.logical_and(kj >= WINDOW, kj - WINDOW <= qi)
    prev_ok = jnp.logical_and(kj < WINDOW, kj > qi)
    scale = SWA_HEAD_DIM ** -0.5
    o_blocks = []
    for qb in range(T // WINDOW):
        rs = slice(qb * WINDOW, (qb + 1) * WINDOW)
        kblk = kfull_ref[qb * WINDOW:(qb + 2) * WINDOW, :].astype(BF16)
        vblk = vfull_ref[qb * WINDOW:(qb + 2) * WINDOW, :].astype(BF16)
        if qb == 0:
            valid = jnp.logical_or(own_ok, jnp.logical_and(prev_ok, t > 0))
        else:
            valid = jnp.logical_or(own_ok, prev_ok)
        outs = []
        for g in range(SWA_KV_HEADS):
            q_stack = jnp.concatenate(
                [_swa_q_for_group(qd[rs], g, i) for i in range(SWA_GROUP)], axis=0).astype(BF16)
            s = jnp.where(valid, _mm_nt(q_stack, kblk) * scale, -jnp.inf)
            sink = _sink_column(sinks_ref, layer * SWA_HEADS + g * SWA_GROUP, rows, WINDOW)
            m = jnp.maximum(jnp.max(s, axis=-1, keepdims=True), sink)
            p = jnp.exp(s - m)
            den = jnp.sum(p, axis=-1, keepdims=True) + jnp.exp(sink - m)
            o = _mm(p.astype(BF16), vblk) / den
            outs.append([o[i * WINDOW:(i + 1) * WINDOW] for i in range(SWA_GROUP)])
            if swa_fill:
                swa_fill.pop(0)()
        o_blocks.append(_swa_assemble(outs, WINDOW))
    for fill in swa_fill:
        fill()
    o_d = jnp.concatenate(o_blocks, axis=0).astype(BF16)
    kfull_ref[0:WINDOW, :] = kfull_ref[T:T + WINDOW, :]
    vfull_ref[0:WINDOW, :] = vfull_ref[T:T + WINDOW, :]

    merged = (gates_ref[0] * jnp.concatenate([partial[(0, 0)], partial[(0, 1)]], axis=1)
              + gates_ref[2] * jnp.concatenate([partial[(2, 0)], partial[(2, 1)]], axis=1))
    merged = merged + gates_ref[1] * branch(1, o_b)
    merged = merged + gates_ref[3] * branch(3, o_d)
    y_ref[0] = x + _mm(merged.astype(BF16), wout_ref[:, :D_MODEL])

    @pl.when(t == last)
    def _():
        conv_ref[0] = hbuf_ref[CONV_LEAD:CONV_PAD, :]
        ko_ref[0] = kfull_ref[0:WINDOW, :]
        vo_ref[0] = vfull_ref[0:WINDOW, :]
        gla_ref[0] = gstate_ref[...]


def _layer_spec(arr, layer):
    nd = arr.ndim
    return pl.BlockSpec((None,) + tuple(arr.shape[1:]), lambda *_: (layer,) + (0,) * (nd - 1),
                        pipeline_mode=pl.Buffered(1))


def _full_spec(shape):
    nd = len(shape)
    return pl.BlockSpec(shape, lambda *_: (0,) * nd)


def _compiler_params(n_axes):
    return pltpu.CompilerParams(
        dimension_semantics=("arbitrary",) * n_axes, vmem_limit_bytes=VMEM_LIMIT_BYTES)


def _prompt_mixer(x, p, tables, layer):
    B, L, _ = x.shape
    T = PROMPT_TILE
    weights = tuple(p[n] for n in (
        "n1g", "win", "bin", "wal", "bal", "glag", "lng", "lnb", "wsp", "bspt",
        "cw", "cb", "clg", "clb", "qg", "kg", "wg", "bg", "wbr", "wout"))
    tab_spec = pl.BlockSpec((T, LANES), lambda b, t, *_: (t, 0))
    grid_spec = pltpu.PrefetchScalarGridSpec(
        num_scalar_prefetch=1,
        grid=(B, L // T),
        in_specs=[pl.BlockSpec((1, T, D_MODEL), lambda b, t, *_: (b, t, 0)),
                  tab_spec, tab_spec, tab_spec]
                 + [_layer_spec(w, layer) for w in weights],
        out_specs=[pl.BlockSpec((1, T, D_MODEL), lambda b, t, *_: (b, t, 0)),
                   pl.BlockSpec((1, QK, GLA_DV), lambda b, t, *_: (b, 0, 0)),
                   pl.BlockSpec((1, CONV_K - 1, CONV_WIDTH), lambda b, t, *_: (b, 0, 0)),
                   pl.BlockSpec((1, WINDOW, SWA_KV), lambda b, t, *_: (b, 0, 0)),
                   pl.BlockSpec((1, WINDOW, SWA_KV), lambda b, t, *_: (b, 0, 0))],
        scratch_shapes=[pltpu.VMEM((QK, GLA_DV), F32),
                        pltpu.VMEM((T + CONV_PAD, CONV_WIDTH), F32),
                        pltpu.VMEM((SUBLANES - 1, T + CONV_PAD, CONV_WIDTH), F32),
                        pltpu.VMEM((T + WINDOW, SWA_KV), F32),
                        pltpu.VMEM((T + WINDOW, SWA_KV), F32),
                        pltpu.VMEM((N_BRANCH, T, D_MODEL), F32)])
    return pl.pallas_call(
        functools.partial(_prompt_mixer_kernel, layer=layer),
        grid_spec=grid_spec,
        out_shape=[jax.ShapeDtypeStruct((B, L, D_MODEL), F32),
                   jax.ShapeDtypeStruct((B, QK, GLA_DV), F32),
                   jax.ShapeDtypeStruct((B, CONV_K - 1, CONV_WIDTH), F32),
                   jax.ShapeDtypeStruct((B, WINDOW, SWA_KV), F32),
                   jax.ShapeDtypeStruct((B, WINDOW, SWA_KV), F32)],
        compiler_params=_compiler_params(2),
        name="prompt_mixer",
    )(p["sinks"], x, *tables, *weights)


def _ffn_kernel(x_ref, g_ref, wgu_ref, wd_ref, y_ref, h_ref):
    x = x_ref[...]
    xb = _rms_rows(x, g_ref[...]).astype(BF16)
    for c in range(FFN_HIDDEN // FFN_CHUNK):
        c0 = c * FFN_CHUNK
        gate = _mm(xb, wgu_ref[:, c0:c0 + FFN_CHUNK])
        up = _mm(xb, wgu_ref[:, FFN_HIDDEN + c0:FFN_HIDDEN + c0 + FFN_CHUNK])
        h_ref[:, c0:c0 + FFN_CHUNK] = (_silu(gate) * up).astype(BF16)
    y_ref[...] = x + _mm(h_ref[...], wd_ref[:, :D_MODEL])


def _ffn(x2d, p, layer):
    n = x2d.shape[0]
    tile = min(FFN_TILE, n)
    return pl.pallas_call(
        _ffn_kernel,
        grid=(n // tile,),
        in_specs=[pl.BlockSpec((tile, D_MODEL), lambda i: (i, 0)),
                  _layer_spec(p["n2g"], layer), _layer_spec(p["wgu"], layer), _layer_spec(p["wd"], layer)],
        out_specs=pl.BlockSpec((tile, D_MODEL), lambda i: (i, 0)),
        out_shape=jax.ShapeDtypeStruct((n, D_MODEL), F32),
        scratch_shapes=[pltpu.VMEM((tile, FFN_HIDDEN), BF16)],
        compiler_params=_compiler_params(1),
        name="swiglu",
    )(x2d, p["n2g"], p["wgu"], p["wd"])


def _sample_proj_kernel(x_ref, n1g_ref, win_ref, bin_ref, wal_ref, bal_ref, z_ref, ya_ref):
    xb = _rms_rows(x_ref[...], n1g_ref[...]).astype(BF16)
    z = _mm(xb, win_ref[...]) + bin_ref[...]
    z_ref[...] = z
    ya_ref[...] = _mm(z[:, C_LR:C_LR + LANES].astype(BF16), wal_ref[...]) + bal_ref[...]


def _sample_proj(x2d, p, layer):
    n = x2d.shape[0]
    weights = tuple(p[k] for k in ("n1g", "win", "bin", "wal", "bal"))
    return pl.pallas_call(
        _sample_proj_kernel,
        grid=(1,),
        in_specs=[_full_spec(x2d.shape)] + [_layer_spec(w, layer) for w in weights],
        out_specs=[_full_spec((n, IN_COLS_PACKED)), _full_spec((n, QK))],
        out_shape=[jax.ShapeDtypeStruct((n, IN_COLS_PACKED), F32),
                   jax.ShapeDtypeStruct((n, QK), F32)],
        compiler_params=_compiler_params(1),
        name="sample_proj",
    )(x2d, *weights)


def _sample_sequence(j, layer, sinks_ref, z_ref, ya_ref, s0_ref, hist_ref, kc_ref, vc_ref,
                     rc, ra, rb, glag, lng, lnb, wsc_ref, bsc, wsh_ref, cb, clg, clb, qg, kg,
                     obr_ref, s1_ref, convo_ref, ko_ref, vo_ref, gvo_ref, hext_ref):
    R = SAMPLE_ROWS
    nt = SAMPLE_TOKENS
    row = _iota((R, 1), 0)

    def seq_rows(ref):
        x = ref[(j // 2) * R:(j // 2 + 1) * R, :]
        if j % 2:
            x = pltpu.roll(x, R - nt, 0)
        return jnp.where(row < nt, x, 0.0)

    z = seq_rows(z_ref)
    head_ones = _group_ones(SWA_Q, SWA_HEAD_DIM)

    q = z[:, C_AQ:C_AQ + QK] * (GLA_DK ** -0.5)
    k = z[:, C_AK:C_AK + QK]
    v = z[:, C_AV:C_AV + QV]
    r = z[:, C_AR:C_AR + QV]
    zq = z[:, C_DQ:C_DQ + SWA_Q]
    zk = z[:, C_DK:C_DK + SWA_KV]
    zv = z[:, C_DV:C_DV + SWA_KV]
    ms_q = _mm_split_lhs(zq * zq, head_ones) * (1.0 / SWA_HEAD_DIM)
    ms_k = _mm_split_lhs(zk * zk, head_ones[:SWA_KV, :SWA_KV]) * (1.0 / SWA_HEAD_DIM)
    la = jnp.where(row < nt, _log_sigmoid(seq_rows(ya_ref)) * (1.0 / GLA_TAU), 0.0)
    bcum = jnp.zeros_like(la)
    for s in range(nt):
        bcum = bcum + jnp.where(row >= s, jnp.broadcast_to(la[s:s + 1, :], la.shape), 0.0)
    btot_row = bcum[nt - 1:nt, :]
    q_in = q * jnp.exp(bcum)
    k_out = k * jnp.exp(-bcum)
    k_dec = k * jnp.exp(jnp.broadcast_to(btot_row, la.shape) - bcum)
    s0 = s0_ref[j]
    s0_b = s0.astype(BF16)
    head_sel = (_iota((QK, LANES), 0) // GLA_DK == _iota((QK, LANES), 1)).astype(BF16)
    head_exp = (_iota((LANES, QV), 0) == _iota((LANES, QV), 1) // GLA_DV).astype(BF16)
    lane_head = _iota((R, QK), 1) // GLA_DK
    att_s = [_mm_split_lhs(q_in * jnp.broadcast_to(k_out[s:s + 1, :], q_in.shape), head_sel)
             for s in range(nt)]
    o_state = [_mm(jnp.where(lane_head == hh, q_in, 0.0).astype(BF16), s0_b) for hh in range(GLA_HEADS)]
    upd = _mm_tn(k_dec.astype(BF16).astype(F32), v.astype(BF16).astype(F32))
    dcol = jnp.exp(jnp.broadcast_to(btot_row, (GLA_DV, QK)).T)

    u = _gelu(z[:, C_BU:C_BU + GMLP_WIDTH])
    gv = _ln_rows(_gelu(z[:, C_BV:C_BV + GMLP_WIDTH]), lng, lnb)
    gvo_ref[j * nt:(j + 1) * nt, :] = gv[0:nt, :]
    sp = bsc
    for s in range(nt):
        sp = sp + jnp.where(row >= s, wsc_ref[s], 0.0) * jnp.broadcast_to(gv[s:s + 1, :], gv.shape)
    o_b = u * sp

    h = z[:, C_CA:C_CA + CONV_WIDTH] * jax.nn.sigmoid(z[:, C_CG:C_CG + CONV_WIDTH])
    hext_ref[j, 0:SUBLANES, :] = jnp.zeros((SUBLANES, CONV_WIDTH), F32)
    hext_ref[j, CONV_LEAD:CONV_PAD, :] = hist_ref[j]
    hext_ref[j, CONV_PAD:CONV_PAD + R, :] = h
    yield

    hext = hext_ref[j]
    conv = jnp.zeros((R, CONV_WIDTH), F32)
    for tt in range(nt):
        c_t = jnp.sum(hext * wsh_ref[tt], axis=0, keepdims=True)
        conv = jnp.where(row == tt, jnp.broadcast_to(c_t, conv.shape), conv)
    o_c = _silu(_ln_rows(conv + cb, clg, clb))
    convo_ref[j] = hext_ref[j, CONV_PAD + nt - (CONV_K - 1):CONV_PAD + nt, :]

    att_e = [_mm(a.astype(BF16), head_exp) for a in att_s]
    s1_ref[j] = dcol * s0 + jnp.concatenate(
        [upd[hh * GLA_DK:(hh + 1) * GLA_DK, hh * GLA_DV:(hh + 1) * GLA_DV] for hh in range(GLA_HEADS)],
        axis=0)

    qd = _rope(zq * lax.rsqrt(ms_q + NORM_EPS) * qg, rc, ra, rb)
    k_new = _rope(zk * lax.rsqrt(ms_k + NORM_EPS) * kg, rc, ra, rb)
    k_hist_b = kc_ref[j].astype(BF16)
    v_hist_b = vc_ref[j].astype(BF16)
    rows = SWA_GROUP * R
    qt = _iota((rows, WINDOW), 0) % R
    hist_ok = _iota((rows, WINDOW), 1) > qt
    qt1 = _iota((rows, 1), 0) % R
    scale = SWA_HEAD_DIM ** -0.5
    q_stacks, s_hists, s_news = [], [], []
    for g in range(SWA_KV_HEADS):
        q_stack = jnp.concatenate([_swa_q_for_group(qd, g, i) for i in range(SWA_GROUP)], axis=0)
        q_stacks.append(q_stack)
        s_hists.append(_mm_nt(q_stack.astype(BF16), k_hist_b))
        s_news.append([jnp.sum(q_stack * jnp.broadcast_to(k_new[s:s + 1, :], q_stack.shape),
                               axis=-1, keepdims=True) for s in range(nt)])
    yield

    o_intra = jnp.zeros((R, QV), F32)
    for s in range(nt):
        o_intra = o_intra + jnp.where(row >= s, att_e[s], 0.0) * jnp.broadcast_to(v[s:s + 1, :], (R, QV))
    o_a = _head_rms_gla(o_intra + jnp.concatenate(o_state, axis=1), glag) * _silu(r)

    sinks, ms, s_masked, n_masked = [], [], [], []
    for g in range(SWA_KV_HEADS):
        s_hist = jnp.where(hist_ok, s_hists[g] * scale, -jnp.inf)
        s_new = [jnp.where(qt1 >= s, s_news[g][s] * scale, -jnp.inf) for s in range(nt)]
        sink = _sink_column(sinks_ref, layer * SWA_HEADS + g * SWA_GROUP, rows, R)
        m = jnp.maximum(jnp.max(s_hist, axis=-1, keepdims=True), sink)
        for s in range(nt):
            m = jnp.maximum(m, s_new[s])
        sinks.append(sink)
        ms.append(m)
        s_masked.append(s_hist)
        n_masked.append(s_new)
    yield

    dens, pvs, p_news = [], [], []
    for g in range(SWA_KV_HEADS):
        p_hist = jnp.exp(s_masked[g] - ms[g])
        dens.append(jnp.sum(p_hist, axis=-1, keepdims=True) + jnp.exp(sinks[g] - ms[g]))
        pvs.append(_mm(p_hist.astype(BF16), v_hist_b))
        p_news.append([jnp.exp(n_masked[g][s] - ms[g]) for s in range(nt)])
    yield

    outs = []
    for g in range(SWA_KV_HEADS):
        o, den = pvs[g], dens[g]
        for s in range(nt):
            den = den + p_news[g][s]
            o = o + p_news[g][s] * jnp.broadcast_to(zv[s:s + 1, :], o.shape)
        o = o / den
        outs.append([o[i * R:(i + 1) * R] for i in range(SWA_GROUP)])
    o_d = _swa_assemble(outs, R)

    ko_ref[j, 0:WINDOW - nt, :] = kc_ref[j, nt:WINDOW, :]
    ko_ref[j, WINDOW - nt:WINDOW, :] = k_new[0:nt, :]
    vo_ref[j, 0:WINDOW - nt, :] = vc_ref[j, nt:WINDOW, :]
    vo_ref[j, WINDOW - nt:WINDOW, :] = zv[0:nt, :]

    obr_ref[j * nt:(j + 1) * nt, :] = jnp.concatenate([o_a, o_b, o_c, o_d], axis=1)[0:nt, :]


def _sample_mixer_kernel(
        sinks_ref,
        z_ref, ya_ref, s0_ref, hist_ref, kc_ref, vc_ref,
        rc_ref, ra_ref, rb_ref,
        glag_ref, lng_ref, lnb_ref, wsc_ref, bsc_ref,
        wsh_ref, cb_ref, clg_ref, clb_ref, qg_ref, kg_ref,
        s1_all_ref, convo_all_ref, ko_all_ref, vo_all_ref,
        obr_ref, s1_ref, convo_ref, ko_ref, vo_ref, gvo_ref,
        hext_ref, *, layer):
    del s1_all_ref, convo_all_ref, ko_all_ref, vo_all_ref
    consts = (rc_ref[...], ra_ref[...], rb_ref[...], glag_ref[...], lng_ref[...], lnb_ref[...])
    sequences = [
        _sample_sequence(
            j, layer, sinks_ref, z_ref, ya_ref, s0_ref, hist_ref, kc_ref, vc_ref,
            *consts, wsc_ref, bsc_ref[...], wsh_ref, cb_ref[...], clg_ref[...], clb_ref[...],
            qg_ref[...], kg_ref[...],
            obr_ref, s1_ref, convo_ref, ko_ref, vo_ref, gvo_ref, hext_ref)
        for j in range(SAMPLE_GROUP)]
    while sequences:
        sequences = [seq for seq in sequences if next(seq, StopIteration) is not StopIteration]


def _sample_mixer(z, ya, s0, hist, kc, vc, p, tables, stacked_states, layer):
    nt = SAMPLE_TOKENS
    nb = z.shape[0] // nt
    R = SAMPLE_ROWS
    G = SAMPLE_GROUP
    weights = tuple(p[n] for n in ("glag", "lng", "lnb", "wsc", "bsc", "wsh", "cb", "clg", "clb", "qg", "kg"))

    def per_group(shape):
        return pl.BlockSpec((G * nt, shape[1]), lambda b, *_: (b, 0))

    def per_group_of_layer(arr):
        nd = arr.ndim
        return pl.BlockSpec((None, G) + tuple(arr.shape[2:]), lambda b, *_: (layer, b) + (0,) * (nd - 2))

    def layer_block(arr):
        nd = arr.ndim
        return pl.BlockSpec((None,) + tuple(arr.shape[1:]), lambda b, *_: (layer,) + (0,) * (nd - 1))

    out_shapes = [jax.ShapeDtypeStruct((nb * nt, N_BRANCH * BRANCH_WIDTH), F32)] \
        + [jax.ShapeDtypeStruct(a.shape, F32) for a in stacked_states] \
        + [jax.ShapeDtypeStruct((nb * nt, GMLP_WIDTH), F32)]
    operands = (z, ya, s0, hist, kc, vc, *tables, *weights)
    n_in = 1 + len(operands)
    grid_spec = pltpu.PrefetchScalarGridSpec(
        num_scalar_prefetch=1,
        grid=(nb // G,),
        in_specs=[per_group(a.shape) for a in (z, ya)]
                 + [per_group_of_layer(a) for a in (s0, hist, kc, vc)]
                 + [_full_spec(tb.shape) for tb in tables]
                 + [layer_block(w) for w in weights]
                 + [pl.BlockSpec(memory_space=pl.ANY) for _ in stacked_states],
        out_specs=[per_group(out_shapes[0].shape)]
                  + [per_group_of_layer(a) for a in stacked_states]
                  + [per_group(out_shapes[-1].shape)],
        scratch_shapes=[pltpu.VMEM((G, CONV_PAD + R, CONV_WIDTH), F32)])
    return pl.pallas_call(
        functools.partial(_sample_mixer_kernel, layer=layer),
        grid_spec=grid_spec,
        out_shape=out_shapes,
        input_output_aliases={n_in + i: 1 + i for i in range(len(stacked_states))},
        compiler_params=_compiler_params(1),
        name="sample_mixer",
    )(p["sinks"], *operands, *stacked_states)


def _sample_merge_kernel(x_ref, obr_ref, n1g_ref, wg_ref, bg_ref, wbr_ref, wout_ref, y_ref):
    x = x_ref[...]
    xb = _rms_rows(x, n1g_ref[...]).astype(BF16)
    merged = jnp.zeros(x.shape, F32)
    for i in range(N_BRANCH):
        gate = jax.nn.sigmoid(
            _mm(xb, wg_ref[:, i * D_MODEL:(i + 1) * D_MODEL]) + bg_ref[:, i * D_MODEL:(i + 1) * D_MODEL])
        o_i = obr_ref[:, i * BRANCH_WIDTH:(i + 1) * BRANCH_WIDTH].astype(BF16)
        merged = merged + gate * _mm(o_i, wbr_ref[i, :, :D_MODEL])
    y_ref[...] = x + _mm(merged.astype(BF16), wout_ref[:, :D_MODEL])


def _sample_merge(x2d, obr, p, layer):
    weights = tuple(p[k] for k in ("n1g", "wg", "bg", "wbr", "wout"))
    return pl.pallas_call(
        _sample_merge_kernel,
        grid=(1,),
        in_specs=[_full_spec(x2d.shape), _full_spec(obr.shape)] + [_layer_spec(w, layer) for w in weights],
        out_specs=_full_spec(x2d.shape),
        out_shape=jax.ShapeDtypeStruct(x2d.shape, F32),
        compiler_params=_compiler_params(1),
        name="sample_merge",
    )(x2d, obr, *weights)


def _rope_tables(pos):
    half = ROPE_DIM // 2
    inv = jnp.exp(-math.log(ROPE_THETA) * jnp.arange(half, dtype=F32) * (2.0 / ROPE_DIM))
    ang = pos.astype(F32)[:, None] * inv[None, :]
    cos, sin = jnp.cos(ang), jnp.sin(ang)
    n = pos.shape[0]
    pad = jnp.zeros((n, SWA_HEAD_DIM - ROPE_DIM), F32)
    zero = jnp.zeros((n, half), F32)
    c_head = jnp.concatenate([cos, cos, pad + 1.0], axis=1)
    a_head = jnp.concatenate([-sin, zero, pad], axis=1)
    b_head = jnp.concatenate([zero, sin, pad], axis=1)
    rep = LANES // SWA_HEAD_DIM
    return tuple(jnp.tile(tb, (1, rep)) for tb in (c_head, a_head, b_head))


def _lane_pad(w):
    return jnp.pad(w, [(0, 0)] * (w.ndim - 1) + [(0, LANES)])


def _rows(v):
    return v.reshape(v.shape[0], 1, -1).astype(F32)


def _stacked_params(norm1_g, w_in, b_in, w_alpha2, b_alpha, gla_norm_g, gmlp_ln_g, gmlp_ln_b,
                    w_spatial, b_spatial, conv_w, conv_b, conv_ln_g, conv_ln_b, q_norm_g, k_norm_g,
                    sinks, w_gate, b_gate, w_branch, w_out, norm2_g, w_gate_up, w_down):
    o_lr = 2 * QK + 2 * QV
    o_rest = o_lr + GLA_RANK

    def repack(m, dtype):
        pad = jnp.zeros(m.shape[:-1] + (LANES - GLA_RANK,), dtype)
        return jnp.concatenate(
            [m[..., :o_lr].astype(dtype), m[..., o_rest:].astype(dtype),
             m[..., o_lr:o_rest].astype(dtype), pad], axis=-1)

    nt = SAMPLE_TOKENS
    gw = GMLP_WIDTH // GMLP_GROUPS
    pad_rows = ((0, 0), (0, SAMPLE_ROWS - nt), (0, 0))
    wsc = jnp.stack([
        jnp.pad(jnp.repeat(jnp.swapaxes(w_spatial[:, :, :nt, s], 1, 2), gw, axis=2), pad_rows)
        for s in range(nt)], axis=1)
    bsc = jnp.pad(jnp.repeat(jnp.swapaxes(b_spatial[:, :, :nt], 1, 2), gw, axis=2), pad_rows)
    wsh = jnp.stack([
        jnp.pad(conv_w, ((0, 0), (CONV_LEAD + t, CONV_PAD + SAMPLE_ROWS - CONV_LEAD - t - CONV_K), (0, 0)))
        for t in range(nt)], axis=1)
    return {
        "n1g": _rows(norm1_g),
        "win": repack(w_in, BF16),
        "bin": repack(b_in, F32).reshape(DEPTH, 1, -1),
        "wal": jnp.pad(w_alpha2, ((0, 0), (0, LANES - GLA_RANK), (0, 0))).astype(BF16),
        "bal": _rows(b_alpha),
        "glag": _rows(gla_norm_g),
        "lng": _rows(gmlp_ln_g),
        "lnb": _rows(gmlp_ln_b),
        "wsp": w_spatial.astype(F32),
        "bspt": jnp.swapaxes(b_spatial, 1, 2).astype(F32),
        "wsc": wsc.astype(F32),
        "bsc": bsc.astype(F32),
        "cw": jnp.pad(conv_w, ((0, 0), (0, CONV_PAD - CONV_K), (0, 0))).astype(F32),
        "wsh": wsh.astype(F32),
        "cb": _rows(conv_b),
        "clg": _rows(conv_ln_g),
        "clb": _rows(conv_ln_b),
        "qg": _rows(jnp.tile(q_norm_g, (1, SWA_HEADS))),
        "kg": _rows(jnp.tile(k_norm_g, (1, SWA_KV_HEADS))),
        "sinks": sinks.reshape(-1).astype(F32),
        "wg": _lane_pad(w_gate.astype(BF16)),
        "bg": _rows(b_gate),
        "wbr": _lane_pad(w_branch.astype(BF16)),
        "wout": _lane_pad(w_out.astype(BF16)),
        "n2g": _rows(norm2_g),
        "wgu": w_gate_up.astype(BF16),
        "wd": _lane_pad(w_down.astype(BF16)),
    }


def kernel(x_prompt, x_sample, state_gla, state_conv, cache_swa_k, cache_swa_v, norm1_g, w_in, b_in, w_alpha2, b_alpha, gla_norm_g, gmlp_ln_g, gmlp_ln_b, w_spatial, b_spatial, conv_w, conv_b, conv_ln_g, conv_ln_b, q_norm_g, k_norm_g, sinks, w_gate, b_gate, w_branch, w_out, norm2_g, w_gate_up, w_down):
    bp, lp, _ = x_prompt.shape
    bs, ls, _ = x_sample.shape
    R = SAMPLE_ROWS
    prompt_tables = _rope_tables(jnp.arange(lp))
    sample_tables = tuple(
        jnp.pad(tb, ((0, R - ls), (0, 0))) for tb in _rope_tables(PAST_LEN + jnp.arange(ls)))
    p = _stacked_params(norm1_g, w_in, b_in, w_alpha2, b_alpha, gla_norm_g, gmlp_ln_g, gmlp_ln_b,
                        w_spatial, b_spatial, conv_w, conv_b, conv_ln_g, conv_ln_b, q_norm_g,
                        k_norm_g, sinks, w_gate, b_gate, w_branch, w_out, norm2_g, w_gate_up, w_down)

    gla_in = state_gla.reshape(DEPTH, bs, QK, GLA_DV)
    swa_k_in = cache_swa_k.reshape(DEPTH, bs, WINDOW, SWA_KV)
    swa_v_in = cache_swa_v.reshape(DEPTH, bs, WINDOW, SWA_KV)

    sample_states = [jnp.zeros(a.shape, F32) for a in (gla_in, state_conv, swa_k_in, swa_v_in)]

    yp = x_prompt
    ys = x_sample.reshape(bs * ls, D_MODEL)
    outs = {k: [] for k in ("gla_p", "conv_p", "kp", "vp", "gm")}
    for l in range(DEPTH):
        yp, g1, c1, k1, v1 = _prompt_mixer(yp, p, prompt_tables, l)
        yp = _ffn(yp.reshape(bp * lp, D_MODEL), p, l).reshape(bp, lp, D_MODEL)
        outs["gla_p"].append(g1.reshape(bp, GLA_HEADS, GLA_DK, GLA_DV))
        outs["conv_p"].append(c1)
        outs["kp"].append(k1.reshape(bp, WINDOW, SWA_KV_HEADS, SWA_HEAD_DIM))
        outs["vp"].append(v1.reshape(bp, WINDOW, SWA_KV_HEADS, SWA_HEAD_DIM))
        z, ya = _sample_proj(ys, p, l)
        obr, *sample_states, gv2 = _sample_mixer(
            z, ya, gla_in, state_conv, swa_k_in, swa_v_in, p, sample_tables, sample_states, l)
        ys = _sample_merge(ys, obr, p, l)
        ys = _ffn(ys, p, l)
        outs["gm"].append(gv2.reshape(bs, ls, GMLP_WIDTH))
    st = lambda name: jnp.stack(outs[name], 0)
    gla_s, conv_s, ks, vs = sample_states
    kv_shape = (DEPTH, bs, WINDOW, SWA_KV_HEADS, SWA_HEAD_DIM)
    return (yp, ys.reshape(bs, ls, D_MODEL), st("gla_p"), gla_s.reshape(state_gla.shape), st("conv_p"),
            conv_s, st("kp"), st("vp"), ks.reshape(kv_shape), vs.reshape(kv_shape), st("gm"))
```

```python
import functools
import math

import jax
import jax.numpy as jnp
from jax import lax
from jax.experimental import pallas as pl
from jax.experimental.pallas import tpu as pltpu

F32 = jnp.float32
BF16 = jnp.bfloat16

D_MODEL = 1024
DEPTH = 2
PAST_LEN = 16384
GLA_HEADS = 4
GLA_DK = 64
GLA_DV = 128
GLA_RANK = 16
GLA_TAU = 16.0
GLA_CHUNK = 64
GMLP_GROUPS = 4
GMLP_WIDTH = 512
GMLP_CHUNK = 128
CONV_WIDTH = 512
CONV_K = 31
SWA_HEADS = 8
SWA_KV_HEADS = 2
SWA_GROUP = SWA_HEADS // SWA_KV_HEADS
SWA_HEAD_DIM = 64
WINDOW = 128
ROPE_DIM = SWA_HEAD_DIM // 4
ROPE_THETA = 500000.0
N_BRANCH = 4
BRANCH_WIDTH = 512
FFN_HIDDEN = 2816
NORM_EPS = 1e-6
LN_EPS = 1e-5

LANES = 128
SUBLANES = 8
VMEM_LIMIT_BYTES = 56 * 1024 * 1024

QK = GLA_HEADS * GLA_DK
QV = GLA_HEADS * GLA_DV
SWA_Q = SWA_HEADS * SWA_HEAD_DIM
SWA_KV = SWA_KV_HEADS * SWA_HEAD_DIM

C_AQ = 0
C_AK = C_AQ + QK
C_AV = C_AK + QK
C_AR = C_AV + QV
C_BU = C_AR + QV
C_BV = C_BU + GMLP_WIDTH
C_CA = C_BV + GMLP_WIDTH
C_CG = C_CA + CONV_WIDTH
C_DQ = C_CG + CONV_WIDTH
C_DK = C_DQ + SWA_Q
C_DV = C_DK + SWA_KV
C_LR = C_DV + SWA_KV
IN_COLS_PACKED = C_LR + LANES

PROMPT_TILE = 256
FFN_TILE = 512
FFN_CHUNK = 256
GATE_PART = 256
CONV_PAD = 32
CONV_LEAD = CONV_PAD - (CONV_K - 1)
CONV_ROWS = 32
SAMPLE_ROWS = 8
SAMPLE_TOKENS = 4
SAMPLE_GROUP = 8


def _mm(a, b):
    return jnp.dot(a, b, preferred_element_type=F32)


def _mm_nt(a, b):
    return lax.dot_general(a, b, (((1,), (1,)), ((), ())), preferred_element_type=F32)


def _mm_tn(a, b):
    return lax.dot_general(a, b, (((0,), (0,)), ((), ())), preferred_element_type=F32)


def _hi_lo(a):
    hi = a.astype(BF16)
    lo = (a - hi.astype(F32)).astype(BF16)
    return hi, lo


def _mm_split_lhs(a_f32, m_bf16):
    hi, lo = _hi_lo(a_f32)
    return _mm(hi, m_bf16) + _mm(lo, m_bf16)


def _rms_rows(x, g):
    return x * lax.rsqrt(jnp.mean(x * x, axis=-1, keepdims=True) + NORM_EPS) * g


def _ln_rows(x, g, b):
    mu = jnp.mean(x, axis=-1, keepdims=True)
    xc = x - mu
    var = jnp.mean(xc * xc, axis=-1, keepdims=True)
    return xc * lax.rsqrt(var + LN_EPS) * g + b


def _gelu(x):
    return 0.5 * x * (1.0 + lax.erf(x * (1.0 / math.sqrt(2.0))))


def _silu(x):
    return x * jax.nn.sigmoid(x)


def _log_sigmoid(x):
    return jnp.minimum(x, 0.0) - jnp.log(1.0 + jnp.exp(-jnp.abs(x)))


def _iota(shape, dim):
    return lax.broadcasted_iota(jnp.int32, shape, dim)


def _group_ones(n, group, dtype=BF16):
    return (_iota((n, n), 0) // group == _iota((n, n), 1) // group).astype(dtype)


def _group_mean_sq(x, group):
    return _mm_split_lhs(x * x, _group_ones(x.shape[-1], group)) * (1.0 / group)


def _rope_slab(x, c, a, b):
    return x * c + pltpu.roll(x, LANES - ROPE_DIM // 2, 1) * a + pltpu.roll(x, ROPE_DIM // 2, 1) * b


def _rope(x, c, a, b):
    n = x.shape[-1] // LANES
    if n == 1:
        return _rope_slab(x, c, a, b)
    return jnp.concatenate(
        [_rope_slab(x[:, i * LANES:(i + 1) * LANES], c, a, b) for i in range(n)], axis=1)


def _qk_norm_rope(z, g, c, a, b, split=True):
    if split:
        ms = _group_mean_sq(z, SWA_HEAD_DIM)
    else:
        ms = _mm((z * z).astype(BF16), _group_ones(z.shape[-1], SWA_HEAD_DIM)) * (1.0 / SWA_HEAD_DIM)
    return _rope(z * lax.rsqrt(ms + NORM_EPS) * g, c, a, b)


def _head_rms_gla(o, g):
    return jnp.concatenate(
        [_rms_rows(o[:, h * GLA_DV:(h + 1) * GLA_DV], g) for h in range(GLA_HEADS)], axis=1)


def _swa_q_for_group(q, g, i):
    hq = g * SWA_GROUP + i
    slab = q[:, (hq // 2) * LANES:(hq // 2 + 1) * LANES]
    if hq % 2 != g:
        slab = pltpu.roll(slab, SWA_HEAD_DIM, 1)
    lane_head = _iota(slab.shape, 1) // SWA_HEAD_DIM
    return jnp.where(lane_head == g, slab, 0.0)


def _swa_assemble(outs, rows):
    lane_half = _iota((rows, LANES), 1) // SWA_HEAD_DIM
    slabs = []
    for s in range(SWA_HEADS // 2):
        pair = []
        for hq in (2 * s, 2 * s + 1):
            g, i = hq // SWA_GROUP, hq % SWA_GROUP
            o = outs[g][i]
            if hq % 2 != g:
                o = pltpu.roll(o, SWA_HEAD_DIM, 1)
            pair.append(o)
        slabs.append(jnp.where(lane_half == 0, pair[0], pair[1]))
    return jnp.concatenate(slabs, axis=1)


def _sink_column(sinks_ref, base, rows, rows_per_head):
    row_head = _iota((rows, 1), 0) // rows_per_head
    sink = jnp.zeros((rows, 1), F32)
    for i in range(SWA_GROUP):
        sink = jnp.where(row_head == i, sinks_ref[base + i], sink)
    return sink


def _prompt_mixer_kernel(
        sinks_ref,
        x_ref, xn_ref, rc_ref, ra_ref, rb_ref,
        n1g_ref, win_ref, bin_ref, wal_ref, bal_ref, glag_ref,
        lng_ref, lnb_ref, wsp_ref, bspt_ref,
        cw_ref, cb_ref, clg_ref, clb_ref,
        qg_ref, kg_ref,
        wg_ref, bg_ref, wbr_ref, wout_ref,
        y_ref, gla_ref, conv_ref, ko_ref, vo_ref,
        gstate_ref, hbuf_ref, hsh_ref, kfull_ref, vfull_ref, gates_ref, xb_ref,
        *, layer):
    T = PROMPT_TILE
    t = pl.program_id(1)
    last = pl.num_programs(1) - 1

    @pl.when(t == 0)
    def _():
        gstate_ref[...] = jnp.zeros_like(gstate_ref)
        hbuf_ref[0:CONV_PAD, :] = jnp.zeros((CONV_PAD, CONV_WIDTH), F32)
        kfull_ref[0:WINDOW, :] = jnp.zeros((WINDOW, SWA_KV), F32)
        vfull_ref[0:WINDOW, :] = jnp.zeros((WINDOW, SWA_KV), F32)

    @pl.when(jnp.logical_and(pl.program_id(0) == 0, t == 0))
    def _():
        xb_ref[...] = _rms_rows(x_ref[0], n1g_ref[...]).astype(BF16)

    x = x_ref[0]
    xb = xb_ref[...]

    def proj(c0, width):
        return _mm(xb, win_ref[:, c0:c0 + width]) + bin_ref[:, c0:c0 + width]

    def gate_part(i, j):
        c0 = i * D_MODEL + j * GATE_PART
        gates_ref[i, :, j * GATE_PART:(j + 1) * GATE_PART] = jax.nn.sigmoid(
            _mm(xb, wg_ref[:, c0:c0 + GATE_PART]) + bg_ref[:, c0:c0 + GATE_PART])

    def branch(i, o_b16):
        return _mm(o_b16, wbr_ref[i, :, :D_MODEL])


    h = proj(C_CA, CONV_WIDTH) * jax.nn.sigmoid(proj(C_CG, CONV_WIDTH))
    q = proj(C_AQ, QK) * (GLA_DK ** -0.5)
    k = proj(C_AK, QK)
    v = proj(C_AV, QV)
    kvl = proj(C_DK, 2 * SWA_KV + LANES)
    zk, zvd, lr = kvl[:, :SWA_KV], kvl[:, SWA_KV:2 * SWA_KV], kvl[:, 2 * SWA_KV:]
    hbuf_ref[CONV_PAD:CONV_PAD + T, :] = h
    taps_of = {r: [kk for kk in range(CONV_K) if (CONV_LEAD + kk) % SUBLANES == r] for r in range(SUBLANES)}
    for r in range(1, SUBLANES):
        span = max(CONV_LEAD + kk for kk in taps_of[r]) - r
        hsh_ref[r - 1, 0:span + T, :] = hbuf_ref[r:r + span + T, :]

    z = {}
    conv_fill = [
        lambda: z.update(u=proj(C_BU, GMLP_WIDTH)),
        lambda: z.update(gv=proj(C_BV, GMLP_WIDTH)),
        lambda: z.update(q=proj(C_DQ, SWA_Q)),
        lambda: z.update(r=proj(C_AR, QV)),
        lambda: (gate_part(0, 0), gate_part(0, 1)),
        lambda: (gate_part(0, 2), gate_part(0, 3)),
        lambda: (gate_part(1, 0), gate_part(1, 1)),
        lambda: (gate_part(1, 2), gate_part(1, 3)),
    ]
    acc_blocks = []
    for rb in range(T // CONV_ROWS):
        acc = jnp.broadcast_to(cb_ref[...], (CONV_ROWS, CONV_WIDTH))
        for r_ in range(SUBLANES):
            for kk in taps_of[r_]:
                a = CONV_LEAD + kk - r_ + rb * CONV_ROWS
                rows_k = hbuf_ref[a:a + CONV_ROWS, :] if r_ == 0 else hsh_ref[r_ - 1, a:a + CONV_ROWS, :]
                acc = acc + cw_ref[kk:kk + 1, :] * rows_k
        acc_blocks.append(acc)
        conv_fill[rb]()
    hbuf_ref[0:CONV_PAD, :] = hbuf_ref[T:T + CONV_PAD, :]
    o_c = _silu(_ln_rows(jnp.concatenate(acc_blocks, axis=0), clg_ref[...], clb_ref[...])).astype(BF16)
    half = D_MODEL // 2
    partial = {}

    def branch_half(i, o_b16, j):
        partial[(i, j)] = _mm(o_b16, wbr_ref[i, :, j * half:(j + 1) * half])

    zu, zgv, zq, r = z["u"], z["gv"], z["q"], z["r"]

    la = _log_sigmoid(_mm(lr.astype(BF16), wal_ref[...]) + bal_ref[...]) * (1.0 / GLA_TAU)
    gate_part(2, 0)
    gate_part(2, 1)

    n_chunks = T // GLA_CHUNK
    ri = _iota((T, T), 0)
    ci = _iota((T, T), 1)
    chunk_causal = jnp.logical_and((ri // GLA_CHUNK) == (ci // GLA_CHUNK), ci <= ri)
    la_hi, la_lo = _hi_lo(la)
    m_tril = chunk_causal.astype(BF16)
    bcum = _mm(m_tril, la_hi) + _mm(m_tril, la_lo)
    tot_rows = [bcum[(c + 1) * GLA_CHUNK - 1:(c + 1) * GLA_CHUNK, :] for c in range(n_chunks)]
    btot = jnp.concatenate([jnp.broadcast_to(tr, (GLA_CHUNK, QK)) for tr in tot_rows], axis=0)
    q_in = (q * jnp.exp(bcum)).astype(BF16)
    k_out = (k * jnp.exp(-bcum)).astype(BF16)
    k_dec_t = (k * jnp.exp(btot - bcum)).T.astype(BF16)
    v_b = v.astype(BF16)
    dec_t = jnp.exp(jnp.concatenate(
        [jnp.broadcast_to(tr, (GLA_DV, QK)) for tr in tot_rows], axis=0).T)

    lane_head = _iota((T, QK), 1) // GLA_DK
    o_heads = []
    for hh in range(GLA_HEADS):
        qh = jnp.where(lane_head == hh, q_in, jnp.zeros_like(q_in))
        att = jnp.where(chunk_causal, _mm_nt(qh, k_out), 0.0).astype(BF16)
        o_heads.append(_mm(att, v_b[:, hh * GLA_DV:(hh + 1) * GLA_DV]))
    o_intra = jnp.concatenate(o_heads, axis=1)
    gate_part(2, 2)
    gate_part(2, 3)

    bd_mask = (_iota((QK, QV), 0) // GLA_DK) == (_iota((QK, QV), 1) // GLA_DV)
    tok_chunk = _iota((GLA_DK, T), 1) // GLA_CHUNK
    o_state = []
    for c in range(n_chunks):
        rs = slice(c * GLA_CHUNK, (c + 1) * GLA_CHUNK)
        s_c = gstate_ref[...]
        s_b = s_c.astype(BF16)
        s_bd = jnp.where(bd_mask, jnp.concatenate([s_b] * GLA_HEADS, axis=1), jnp.zeros((QK, QV), BF16))
        o_state.append(_mm(q_in[rs], s_bd))
        upd = []
        for hh in range(GLA_HEADS):
            kt = k_dec_t[hh * GLA_DK:(hh + 1) * GLA_DK, :]
            kt = jnp.where(tok_chunk == c, kt, jnp.zeros_like(kt))
            upd.append(_mm(kt, v_b[:, hh * GLA_DV:(hh + 1) * GLA_DV]))
        gstate_ref[...] = dec_t[:, c * GLA_DV:(c + 1) * GLA_DV] * s_c + jnp.concatenate(upd, axis=0)
    o_gla = o_intra + jnp.concatenate(o_state, axis=0)
    gate_part(3, 0)
    gate_part(3, 1)
    o_a = (_head_rms_gla(o_gla, glag_ref[...]) * _silu(r)).astype(BF16)

    u = _gelu(zu)
    gv = _ln_rows(_gelu(zgv), lng_ref[...], lnb_ref[...])
    gate_part(3, 2)
    gate_part(3, 3)
    gv_b = gv.astype(BF16)
    gw = GMLP_WIDTH // GMLP_GROUPS
    tril = _iota((GMLP_CHUNK, GMLP_CHUNK), 1) <= _iota((GMLP_CHUNK, GMLP_CHUNK), 0)
    bspt = bspt_ref[...]
    s_rows = []
    for n in range(T // GMLP_CHUNK):
        rs = slice(n * GMLP_CHUNK, (n + 1) * GMLP_CHUNK)
        cols = []
        for g in range(GMLP_GROUPS):
            w = jnp.where(tril, wsp_ref[g], 0.0).astype(BF16)
            sg = _mm(w, gv_b[rs, g * gw:(g + 1) * gw])
            cols.append(sg + jnp.broadcast_to(bspt[:, g:g + 1], (GMLP_CHUNK, gw)))
        s_rows.append(jnp.concatenate(cols, axis=1))
    o_b = (u * jnp.concatenate(s_rows, axis=0)).astype(BF16)

    rc, ra, rb = rc_ref[...], ra_ref[...], rb_ref[...]
    qd = _qk_norm_rope(zq, qg_ref[...], rc, ra, rb, split=False)
    kfull_ref[WINDOW:WINDOW + T, :] = _qk_norm_rope(zk, kg_ref[...], rc, ra, rb)
    vfull_ref[WINDOW:WINDOW + T, :] = zvd
    swa_fill = [functools.partial(branch_half, i, o, j) for i, o in ((2, o_c), (0, o_a)) for j in range(2)]

    rows = SWA_GROUP * WINDOW
    qi = _iota((rows, 2 * WINDOW), 0) % WINDOW
    kj = _iota((rows, 2 * WINDOW), 1)
    own_ok = jnp.logical_and(kj >= WINDOW, kj - WINDOW <= qi)
    prev_ok = jnp.logical_and(kj < WINDOW, kj > qi)
    scale = SWA_HEAD_DIM ** -0.5
    o_blocks = []
    for qb in range(T // WINDOW):
        rs = slice(qb * WINDOW, (qb + 1) * WINDOW)
        kblk = kfull_ref[qb * WINDOW:(qb + 2) * WINDOW, :].astype(BF16)
        vblk = vfull_ref[qb * WINDOW:(qb + 2) * WINDOW, :].astype(BF16)
        if qb == 0:
            valid = jnp.logical_or(own_ok, jnp.logical_and(prev_ok, t > 0))
        else:
            valid = jnp.logical_or(own_ok, prev_ok)
        outs = []
        for g in range(SWA_KV_HEADS):
            q_stack = jnp.concatenate(
                [_swa_q_for_group(qd[rs], g, i) for i in range(SWA_GROUP)], axis=0).astype(BF16)
            s = jnp.where(valid, _mm_nt(q_stack, kblk) * scale, -jnp.inf)
            sink = _sink_column(sinks_ref, layer * SWA_HEADS + g * SWA_GROUP, rows, WINDOW)
            m = jnp.maximum(jnp.max(s, axis=-1, keepdims=True), sink)
            p = jnp.exp(s - m)
            den = jnp.sum(p, axis=-1, keepdims=True) + jnp.exp(sink - m)
            o = _mm(p.astype(BF16), vblk) / den
            outs.append([o[i * WINDOW:(i + 1) * WINDOW] for i in range(SWA_GROUP)])
            if swa_fill:
                swa_fill.pop(0)()
        o_blocks.append(_swa_assemble(outs, WINDOW))
    for fill in swa_fill:
        fill()
    o_d = jnp.concatenate(o_blocks, axis=0).astype(BF16)
    kfull_ref[0:WINDOW, :] = kfull_ref[T:T + WINDOW, :]
    vfull_ref[0:WINDOW, :] = vfull_ref[T:T + WINDOW, :]

    merged = (gates_ref[0] * jnp.concatenate([partial[(0, 0)], partial[(0, 1)]], axis=1)
              + gates_ref[2] * jnp.concatenate([partial[(2, 0)], partial[(2, 1)]], axis=1))
    merged = merged + gates_ref[1] * branch(1, o_b)
    merged = merged + gates_ref[3] * branch(3, o_d)
    y_ref[0] = x + _mm(merged.astype(BF16), wout_ref[:, :D_MODEL])
    xb_ref[...] = _rms_rows(xn_ref[0], n1g_ref[...]).astype(BF16)

    @pl.when(t == last)
    def _():
        conv_ref[0] = hbuf_ref[CONV_LEAD:CONV_PAD, :]
        ko_ref[0] = kfull_ref[0:WINDOW, :]
        vo_ref[0] = vfull_ref[0:WINDOW, :]
        gla_ref[0] = gstate_ref[...]


def _layer_spec(arr, layer):
    nd = arr.ndim
    return pl.BlockSpec((None,) + tuple(arr.shape[1:]), lambda *_: (layer,) + (0,) * (nd - 1),
                        pipeline_mode=pl.Buffered(1))


def _full_spec(shape):
    nd = len(shape)
    return pl.BlockSpec(shape, lambda *_: (0,) * nd)


def _compiler_params(n_axes):
    return pltpu.CompilerParams(
        dimension_semantics=("arbitrary",) * n_axes, vmem_limit_bytes=VMEM_LIMIT_BYTES)


def _prompt_mixer(x, p, tables, layer):
    B, L, _ = x.shape
    T = PROMPT_TILE
    weights = tuple(p[n] for n in (
        "n1g", "win", "bin", "wal", "bal", "glag", "lng", "lnb", "wsp", "bspt",
        "cw", "cb", "clg", "clb", "qg", "kg", "wg", "bg", "wbr", "wout"))
    tab_spec = pl.BlockSpec((T, LANES), lambda b, t, *_: (t, 0))
    n_tiles = L // T

    def next_tile(b, t, *_):
        s = jnp.minimum(b * n_tiles + t + 1, B * n_tiles - 1)
        return (s // n_tiles, s % n_tiles, 0)

    grid_spec = pltpu.PrefetchScalarGridSpec(
        num_scalar_prefetch=1,
        grid=(B, L // T),
        in_specs=[pl.BlockSpec((1, T, D_MODEL), lambda b, t, *_: (b, t, 0)),
                  pl.BlockSpec((1, T, D_MODEL), next_tile),
                  tab_spec, tab_spec, tab_spec]
                 + [_layer_spec(w, layer) for w in weights],
        out_specs=[pl.BlockSpec((1, T, D_MODEL), lambda b, t, *_: (b, t, 0)),
                   pl.BlockSpec((1, QK, GLA_DV), lambda b, t, *_: (b, 0, 0)),
                   pl.BlockSpec((1, CONV_K - 1, CONV_WIDTH), lambda b, t, *_: (b, 0, 0)),
                   pl.BlockSpec((1, WINDOW, SWA_KV), lambda b, t, *_: (b, 0, 0)),
                   pl.BlockSpec((1, WINDOW, SWA_KV), lambda b, t, *_: (b, 0, 0))],
        scratch_shapes=[pltpu.VMEM((QK, GLA_DV), F32),
                        pltpu.VMEM((T + CONV_PAD, CONV_WIDTH), F32),
                        pltpu.VMEM((SUBLANES - 1, T + CONV_PAD, CONV_WIDTH), F32),
                        pltpu.VMEM((T + WINDOW, SWA_KV), F32),
                        pltpu.VMEM((T + WINDOW, SWA_KV), F32),
                        pltpu.VMEM((N_BRANCH, T, D_MODEL), F32),
                        pltpu.VMEM((T, D_MODEL), BF16)])
    return pl.pallas_call(
        functools.partial(_prompt_mixer_kernel, layer=layer),
        grid_spec=grid_spec,
        out_shape=[jax.ShapeDtypeStruct((B, L, D_MODEL), F32),
                   jax.ShapeDtypeStruct((B, QK, GLA_DV), F32),
                   jax.ShapeDtypeStruct((B, CONV_K - 1, CONV_WIDTH), F32),
                   jax.ShapeDtypeStruct((B, WINDOW, SWA_KV), F32),
                   jax.ShapeDtypeStruct((B, WINDOW, SWA_KV), F32)],
        compiler_params=_compiler_params(2),
        name="prompt_mixer",
    )(p["sinks"], x, x, *tables, *weights)


def _ffn_kernel(x_ref, g_ref, wgu_ref, wd_ref, y_ref, h_ref):
    x = x_ref[...]
    xb = _rms_rows(x, g_ref[...]).astype(BF16)
    for c in range(FFN_HIDDEN // FFN_CHUNK):
        c0 = c * FFN_CHUNK
        gate = _mm(xb, wgu_ref[:, c0:c0 + FFN_CHUNK])
        up = _mm(xb, wgu_ref[:, FFN_HIDDEN + c0:FFN_HIDDEN + c0 + FFN_CHUNK])
        h_ref[:, c0:c0 + FFN_CHUNK] = (_silu(gate) * up).astype(BF16)
    y_ref[...] = x + _mm(h_ref[...], wd_ref[:, :D_MODEL])


def _ffn(x2d, p, layer):
    n = x2d.shape[0]
    tile = min(FFN_TILE, n)
    return pl.pallas_call(
        _ffn_kernel,
        grid=(n // tile,),
        in_specs=[pl.BlockSpec((tile, D_MODEL), lambda i: (i, 0)),
                  _layer_spec(p["n2g"], layer), _layer_spec(p["wgu"], layer), _layer_spec(p["wd"], layer)],
        out_specs=pl.BlockSpec((tile, D_MODEL), lambda i: (i, 0)),
        out_shape=jax.ShapeDtypeStruct((n, D_MODEL), F32),
        scratch_shapes=[pltpu.VMEM((tile, FFN_HIDDEN), BF16)],
        compiler_params=_compiler_params(1),
        name="swiglu",
    )(x2d, p["n2g"], p["wgu"], p["wd"])


def _sample_proj_kernel(x_ref, n1g_ref, win_ref, bin_ref, wal_ref, bal_ref, z_ref, ya_ref):
    xb = _rms_rows(x_ref[...], n1g_ref[...]).astype(BF16)
    z = _mm(xb, win_ref[...]) + bin_ref[...]
    z_ref[...] = z
    ya_ref[...] = _mm(z[:, C_LR:C_LR + LANES].astype(BF16), wal_ref[...]) + bal_ref[...]


def _sample_proj(x2d, p, layer):
    n = x2d.shape[0]
    weights = tuple(p[k] for k in ("n1g", "win", "bin", "wal", "bal"))
    return pl.pallas_call(
        _sample_proj_kernel,
        grid=(1,),
        in_specs=[_full_spec(x2d.shape)] + [_layer_spec(w, layer) for w in weights],
        out_specs=[_full_spec((n, IN_COLS_PACKED)), _full_spec((n, QK))],
        out_shape=[jax.ShapeDtypeStruct((n, IN_COLS_PACKED), F32),
                   jax.ShapeDtypeStruct((n, QK), F32)],
        compiler_params=_compiler_params(1),
        name="sample_proj",
    )(x2d, *weights)


def _sample_sequence(j, layer, sinks_ref, z_ref, ya_ref, s0_ref, hist_ref, kc_ref, vc_ref,
                     rc, ra, rb, glag, lng, lnb, wsc_ref, bsc, wsh_ref, cb, clg, clb, qg, kg,
                     obr_ref, s1_ref, convo_ref, ko_ref, vo_ref, gvo_ref, hext_ref):
    R = SAMPLE_ROWS
    nt = SAMPLE_TOKENS
    row = _iota((R, 1), 0)

    def seq_rows(ref):
        x = ref[(j // 2) * R:(j // 2 + 1) * R, :]
        if j % 2:
            x = pltpu.roll(x, R - nt, 0)
        return jnp.where(row < nt, x, 0.0)

    z = seq_rows(z_ref)
    head_ones = _group_ones(SWA_Q, SWA_HEAD_DIM)

    q = z[:, C_AQ:C_AQ + QK] * (GLA_DK ** -0.5)
    k = z[:, C_AK:C_AK + QK]
    v = z[:, C_AV:C_AV + QV]
    r = z[:, C_AR:C_AR + QV]
    zq = z[:, C_DQ:C_DQ + SWA_Q]
    zk = z[:, C_DK:C_DK + SWA_KV]
    zv = z[:, C_DV:C_DV + SWA_KV]
    ms_q = _mm_split_lhs(zq * zq, head_ones) * (1.0 / SWA_HEAD_DIM)
    ms_k = _mm_split_lhs(zk * zk, head_ones[:SWA_KV, :SWA_KV]) * (1.0 / SWA_HEAD_DIM)
    la = jnp.where(row < nt, _log_sigmoid(seq_rows(ya_ref)) * (1.0 / GLA_TAU), 0.0)
    bcum = jnp.zeros_like(la)
    for s in range(nt):
        bcum = bcum + jnp.where(row >= s, jnp.broadcast_to(la[s:s + 1, :], la.shape), 0.0)
    btot_row = bcum[nt - 1:nt, :]
    q_in = q * jnp.exp(bcum)
    k_out = k * jnp.exp(-bcum)
    k_dec = k * jnp.exp(jnp.broadcast_to(btot_row, la.shape) - bcum)
    s0 = s0_ref[j]
    s0_b = s0.astype(BF16)
    head_sel = (_iota((QK, LANES), 0) // GLA_DK == _iota((QK, LANES), 1)).astype(BF16)
    head_exp = (_iota((LANES, QV), 0) == _iota((LANES, QV), 1) // GLA_DV).astype(BF16)
    lane_head = _iota((R, QK), 1) // GLA_DK
    att_s = [_mm_split_lhs(q_in * jnp.broadcast_to(k_out[s:s + 1, :], q_in.shape), head_sel)
             for s in range(nt)]
    o_state = [_mm(jnp.where(lane_head == hh, q_in, 0.0).astype(BF16), s0_b) for hh in range(GLA_HEADS)]
    upd = _mm_tn(k_dec.astype(BF16).astype(F32), v.astype(BF16).astype(F32))
    dcol = jnp.exp(jnp.broadcast_to(btot_row, (GLA_DV, QK)).T)

    u = _gelu(z[:, C_BU:C_BU + GMLP_WIDTH])
    gv = _ln_rows(_gelu(z[:, C_BV:C_BV + GMLP_WIDTH]), lng, lnb)
    gvo_ref[j * nt:(j + 1) * nt, :] = gv[0:nt, :]
    sp = bsc
    for s in range(nt):
        sp = sp + jnp.where(row >= s, wsc_ref[s], 0.0) * jnp.broadcast_to(gv[s:s + 1, :], gv.shape)
    o_b = u * sp

    h = z[:, C_CA:C_CA + CONV_WIDTH] * jax.nn.sigmoid(z[:, C_CG:C_CG + CONV_WIDTH])
    hext_ref[j, 0:SUBLANES, :] = jnp.zeros((SUBLANES, CONV_WIDTH), F32)
    hext_ref[j, CONV_LEAD:CONV_PAD, :] = hist_ref[j]
    hext_ref[j, CONV_PAD:CONV_PAD + R, :] = h
    yield

    hext = hext_ref[j]
    conv = jnp.zeros((R, CONV_WIDTH), F32)
    for tt in range(nt):
        c_t = jnp.sum(hext * wsh_ref[tt], axis=0, keepdims=True)
        conv = jnp.where(row == tt, jnp.broadcast_to(c_t, conv.shape), conv)
    o_c = _silu(_ln_rows(conv + cb, clg, clb))
    convo_ref[j] = hext_ref[j, CONV_PAD + nt - (CONV_K - 1):CONV_PAD + nt, :]

    att_e = [_mm(a.astype(BF16), head_exp) for a in att_s]
    s1_ref[j] = dcol * s0 + jnp.concatenate(
        [upd[hh * GLA_DK:(hh + 1) * GLA_DK, hh * GLA_DV:(hh + 1) * GLA_DV] for hh in range(GLA_HEADS)],
        axis=0)

    qd = _rope(zq * lax.rsqrt(ms_q + NORM_EPS) * qg, rc, ra, rb)
    k_new = _rope(zk * lax.rsqrt(ms_k + NORM_EPS) * kg, rc, ra, rb)
    k_hist_b = kc_ref[j].astype(BF16)
    v_hist_b = vc_ref[j].astype(BF16)
    rows = SWA_GROUP * R
    qt = _iota((rows, WINDOW), 0) % R
    hist_ok = _iota((rows, WINDOW), 1) > qt
    qt1 = _iota((rows, 1), 0) % R
    scale = SWA_HEAD_DIM ** -0.5
    q_stacks, s_hists, s_news = [], [], []
    for g in range(SWA_KV_HEADS):
        q_stack = jnp.concatenate([_swa_q_for_group(qd, g, i) for i in range(SWA_GROUP)], axis=0)
        q_stacks.append(q_stack)
        s_hists.append(_mm_nt(q_stack.astype(BF16), k_hist_b))
        s_news.append([jnp.sum(q_stack * jnp.broadcast_to(k_new[s:s + 1, :], q_stack.shape),
                               axis=-1, keepdims=True) for s in range(nt)])
    yield

    o_intra = jnp.zeros((R, QV), F32)
    for s in range(nt):
        o_intra = o_intra + jnp.where(row >= s, att_e[s], 0.0) * jnp.broadcast_to(v[s:s + 1, :], (R, QV))
    o_a = _head_rms_gla(o_intra + jnp.concatenate(o_state, axis=1), glag) * _silu(r)

    sinks, ms, s_masked, n_masked = [], [], [], []
    for g in range(SWA_KV_HEADS):
        s_hist = jnp.where(hist_ok, s_hists[g] * scale, -jnp.inf)
        s_new = [jnp.where(qt1 >= s, s_news[g][s] * scale, -jnp.inf) for s in range(nt)]
        sink = _sink_column(sinks_ref, layer * SWA_HEADS + g * SWA_GROUP, rows, R)
        m = jnp.maximum(jnp.max(s_hist, axis=-1, keepdims=True), sink)
        for s in range(nt):
            m = jnp.maximum(m, s_new[s])
        sinks.append(sink)
        ms.append(m)
        s_masked.append(s_hist)
        n_masked.append(s_new)
    yield

    dens, pvs, p_news = [], [], []
    for g in range(SWA_KV_HEADS):
        p_hist = jnp.exp(s_masked[g] - ms[g])
        dens.append(jnp.sum(p_hist, axis=-1, keepdims=True) + jnp.exp(sinks[g] - ms[g]))
        pvs.append(_mm(p_hist.astype(BF16), v_hist_b))
        p_news.append([jnp.exp(n_masked[g][s] - ms[g]) for s in range(nt)])
    yield

    outs = []
    for g in range(SWA_KV_HEADS):
        o, den = pvs[g], dens[g]
        for s in range(nt):
            den = den + p_news[g][s]
            o = o + p_news[g][s] * jnp.broadcast_to(zv[s:s + 1, :], o.shape)
        o = o / den
        outs.append([o[i * R:(i + 1) * R] for i in range(SWA_GROUP)])
    o_d = _swa_assemble(outs, R)

    ko_ref[j, 0:WINDOW - nt, :] = kc_ref[j, nt:WINDOW, :]
    ko_ref[j, WINDOW - nt:WINDOW, :] = k_new[0:nt, :]
    vo_ref[j, 0:WINDOW - nt, :] = vc_ref[j, nt:WINDOW, :]
    vo_ref[j, WINDOW - nt:WINDOW, :] = zv[0:nt, :]

    obr_ref[j * nt:(j + 1) * nt, :] = jnp.concatenate([o_a, o_b, o_c, o_d], axis=1)[0:nt, :]


def _sample_mixer_kernel(
        sinks_ref,
        z_ref, ya_ref, s0_ref, hist_ref, kc_ref, vc_ref,
        rc_ref, ra_ref, rb_ref,
        glag_ref, lng_ref, lnb_ref, wsc_ref, bsc_ref,
        wsh_ref, cb_ref, clg_ref, clb_ref, qg_ref, kg_ref,
        s1_all_ref, convo_all_ref, ko_all_ref, vo_all_ref,
        obr_ref, s1_ref, convo_ref, ko_ref, vo_ref, gvo_ref,
        hext_ref, *, layer):
    del s1_all_ref, convo_all_ref, ko_all_ref, vo_all_ref
    consts = (rc_ref[...], ra_ref[...], rb_ref[...], glag_ref[...], lng_ref[...], lnb_ref[...])
    sequences = [
        _sample_sequence(
            j, layer, sinks_ref, z_ref, ya_ref, s0_ref, hist_ref, kc_ref, vc_ref,
            *consts, wsc_ref, bsc_ref[...], wsh_ref, cb_ref[...], clg_ref[...], clb_ref[...],
            qg_ref[...], kg_ref[...],
            obr_ref, s1_ref, convo_ref, ko_ref, vo_ref, gvo_ref, hext_ref)
        for j in range(SAMPLE_GROUP)]
    while sequences:
        sequences = [seq for seq in sequences if next(seq, StopIteration) is not StopIteration]


def _sample_mixer(z, ya, s0, hist, kc, vc, p, tables, stacked_states, layer):
    nt = SAMPLE_TOKENS
    nb = z.shape[0] // nt
    R = SAMPLE_ROWS
    G = SAMPLE_GROUP
    weights = tuple(p[n] for n in ("glag", "lng", "lnb", "wsc", "bsc", "wsh", "cb", "clg", "clb", "qg", "kg"))

    def per_group(shape):
        return pl.BlockSpec((G * nt, shape[1]), lambda b, *_: (b, 0))

    def per_group_of_layer(arr):
        nd = arr.ndim
        return pl.BlockSpec((None, G) + tuple(arr.shape[2:]), lambda b, *_: (layer, b) + (0,) * (nd - 2))

    def layer_block(arr):
        nd = arr.ndim
        return pl.BlockSpec((None,) + tuple(arr.shape[1:]), lambda b, *_: (layer,) + (0,) * (nd - 1))

    out_shapes = [jax.ShapeDtypeStruct((nb * nt, N_BRANCH * BRANCH_WIDTH), F32)] \
        + [jax.ShapeDtypeStruct(a.shape, F32) for a in stacked_states] \
        + [jax.ShapeDtypeStruct((nb * nt, GMLP_WIDTH), F32)]
    operands = (z, ya, s0, hist, kc, vc, *tables, *weights)
    n_in = 1 + len(operands)
    grid_spec = pltpu.PrefetchScalarGridSpec(
        num_scalar_prefetch=1,
        grid=(nb // G,),
        in_specs=[per_group(a.shape) for a in (z, ya)]
                 + [per_group_of_layer(a) for a in (s0, hist, kc, vc)]
                 + [_full_spec(tb.shape) for tb in tables]
                 + [layer_block(w) for w in weights]
                 + [pl.BlockSpec(memory_space=pl.ANY) for _ in stacked_states],
        out_specs=[per_group(out_shapes[0].shape)]
                  + [per_group_of_layer(a) for a in stacked_states]
                  + [per_group(out_shapes[-1].shape)],
        scratch_shapes=[pltpu.VMEM((G, CONV_PAD + R, CONV_WIDTH), F32)])
    return pl.pallas_call(
        functools.partial(_sample_mixer_kernel, layer=layer),
        grid_spec=grid_spec,
        out_shape=out_shapes,
        input_output_aliases={n_in + i: 1 + i for i in range(len(stacked_states))},
        compiler_params=_compiler_params(1),
        name="sample_mixer",
    )(p["sinks"], *operands, *stacked_states)


def _sample_merge_kernel(x_ref, obr_ref, n1g_ref, wg_ref, bg_ref, wbr_ref, wout_ref, y_ref):
    x = x_ref[...]
    xb = _rms_rows(x, n1g_ref[...]).astype(BF16)
    merged = jnp.zeros(x.shape, F32)
    for i in range(N_BRANCH):
        gate = jax.nn.sigmoid(
            _mm(xb, wg_ref[:, i * D_MODEL:(i + 1) * D_MODEL]) + bg_ref[:, i * D_MODEL:(i + 1) * D_MODEL])
        o_i = obr_ref[:, i * BRANCH_WIDTH:(i + 1) * BRANCH_WIDTH].astype(BF16)
        merged = merged + gate * _mm(o_i, wbr_ref[i, :, :D_MODEL])
    y_ref[...] = x + _mm(merged.astype(BF16), wout_ref[:, :D_MODEL])


def _sample_merge(x2d, obr, p, layer):
    weights = tuple(p[k] for k in ("n1g", "wg", "bg", "wbr", "wout"))
    return pl.pallas_call(
        _sample_merge_kernel,
        grid=(1,),
        in_specs=[_full_spec(x2d.shape), _full_spec(obr.shape)] + [_layer_spec(w, layer) for w in weights],
        out_specs=_full_spec(x2d.shape),
        out_shape=jax.ShapeDtypeStruct(x2d.shape, F32),
        compiler_params=_compiler_params(1),
        name="sample_merge",
    )(x2d, obr, *weights)


def _rope_tables(pos):
    half = ROPE_DIM // 2
    inv = jnp.exp(-math.log(ROPE_THETA) * jnp.arange(half, dtype=F32) * (2.0 / ROPE_DIM))
    ang = pos.astype(F32)[:, None] * inv[None, :]
    cos, sin = jnp.cos(ang), jnp.sin(ang)
    n = pos.shape[0]
    pad = jnp.zeros((n, SWA_HEAD_DIM - ROPE_DIM), F32)
    zero = jnp.zeros((n, half), F32)
    c_head = jnp.concatenate([cos, cos, pad + 1.0], axis=1)
    a_head = jnp.concatenate([-sin, zero, pad], axis=1)
    b_head = jnp.concatenate([zero, sin, pad], axis=1)
    rep = LANES // SWA_HEAD_DIM
    return tuple(jnp.tile(tb, (1, rep)) for tb in (c_head, a_head, b_head))


def _lane_pad(w):
    return jnp.pad(w, [(0, 0)] * (w.ndim - 1) + [(0, LANES)])


def _rows(v):
    return v.reshape(v.shape[0], 1, -1).astype(F32)


def _stacked_params(norm1_g, w_in, b_in, w_alpha2, b_alpha, gla_norm_g, gmlp_ln_g, gmlp_ln_b,
                    w_spatial, b_spatial, conv_w, conv_b, conv_ln_g, conv_ln_b, q_norm_g, k_norm_g,
                    sinks, w_gate, b_gate, w_branch, w_out, norm2_g, w_gate_up, w_down):
    o_lr = 2 * QK + 2 * QV
    o_rest = o_lr + GLA_RANK

    def repack(m, dtype):
        pad = jnp.zeros(m.shape[:-1] + (LANES - GLA_RANK,), m.dtype)
        return jnp.concatenate(
            [m[..., :o_lr], m[..., o_rest:], m[..., o_lr:o_rest], pad], axis=-1).astype(dtype)

    nt = SAMPLE_TOKENS
    gw = GMLP_WIDTH // GMLP_GROUPS
    pad_rows = ((0, 0), (0, SAMPLE_ROWS - nt), (0, 0))
    wsc = jnp.stack([
        jnp.pad(jnp.repeat(jnp.swapaxes(w_spatial[:, :, :nt, s], 1, 2), gw, axis=2), pad_rows)
        for s in range(nt)], axis=1)
    bsc = jnp.pad(jnp.repeat(jnp.swapaxes(b_spatial[:, :, :nt], 1, 2), gw, axis=2), pad_rows)
    wsh = jnp.stack([
        jnp.pad(conv_w, ((0, 0), (CONV_LEAD + t, CONV_PAD + SAMPLE_ROWS - CONV_LEAD - t - CONV_K), (0, 0)))
        for t in range(nt)], axis=1)
    return {
        "n1g": _rows(norm1_g),
        "win": repack(w_in, BF16),
        "bin": repack(b_in, F32).reshape(DEPTH, 1, -1),
        "wal": jnp.pad(w_alpha2, ((0, 0), (0, LANES - GLA_RANK), (0, 0))).astype(BF16),
        "bal": _rows(b_alpha),
        "glag": _rows(gla_norm_g),
        "lng": _rows(gmlp_ln_g),
        "lnb": _rows(gmlp_ln_b),
        "wsp": w_spatial.astype(F32),
        "bspt": jnp.swapaxes(b_spatial, 1, 2).astype(F32),
        "wsc": wsc.astype(F32),
        "bsc": bsc.astype(F32),
        "cw": jnp.pad(conv_w, ((0, 0), (0, CONV_PAD - CONV_K), (0, 0))).astype(F32),
        "wsh": wsh.astype(F32),
        "cb": _rows(conv_b),
        "clg": _rows(conv_ln_g),
        "clb": _rows(conv_ln_b),
        "qg": _rows(jnp.tile(q_norm_g, (1, SWA_HEADS))),
        "kg": _rows(jnp.tile(k_norm_g, (1, SWA_KV_HEADS))),
        "sinks": sinks.reshape(-1).astype(F32),
        "wg": _lane_pad(w_gate.astype(BF16)),
        "bg": _rows(b_gate),
        "wbr": _lane_pad(w_branch.astype(BF16)),
        "wout": _lane_pad(w_out.astype(BF16)),
        "n2g": _rows(norm2_g),
        "wgu": w_gate_up.astype(BF16),
        "wd": _lane_pad(w_down.astype(BF16)),
    }


def kernel(x_prompt, x_sample, state_gla, state_conv, cache_swa_k, cache_swa_v, norm1_g, w_in, b_in, w_alpha2, b_alpha, gla_norm_g, gmlp_ln_g, gmlp_ln_b, w_spatial, b_spatial, conv_w, conv_b, conv_ln_g, conv_ln_b, q_norm_g, k_norm_g, sinks, w_gate, b_gate, w_branch, w_out, norm2_g, w_gate_up, w_down):
    bp, lp, _ = x_prompt.shape
    bs, ls, _ = x_sample.shape
    R = SAMPLE_ROWS
    prompt_tables = _rope_tables(jnp.arange(lp))
    sample_tables = tuple(
        jnp.pad(tb, ((0, R - ls), (0, 0))) for tb in _rope_tables(PAST_LEN + jnp.arange(ls)))
    p = _stacked_params(norm1_g, w_in, b_in, w_alpha2, b_alpha, gla_norm_g, gmlp_ln_g, gmlp_ln_b,
                        w_spatial, b_spatial, conv_w, conv_b, conv_ln_g, conv_ln_b, q_norm_g,
                        k_norm_g, sinks, w_gate, b_gate, w_branch, w_out, norm2_g, w_gate_up, w_down)

    gla_in = state_gla.reshape(DEPTH, bs, QK, GLA_DV)
    swa_k_in = cache_swa_k.reshape(DEPTH, bs, WINDOW, SWA_KV)
    swa_v_in = cache_swa_v.reshape(DEPTH, bs, WINDOW, SWA_KV)

    sample_states = [jnp.zeros(a.shape, F32) for a in (gla_in, state_conv, swa_k_in, swa_v_in)]

    yp = x_prompt
    ys = x_sample.reshape(bs * ls, D_MODEL)
    outs = {k: [] for k in ("gla_p", "conv_p", "kp", "vp", "gm")}
    for l in range(DEPTH):
        yp, g1, c1, k1, v1 = _prompt_mixer(yp, p, prompt_tables, l)
        yp = _ffn(yp.reshape(bp * lp, D_MODEL), p, l).reshape(bp, lp, D_MODEL)
        outs["gla_p"].append(g1.reshape(bp, GLA_HEADS, GLA_DK, GLA_DV))
        outs["conv_p"].append(c1)
        outs["kp"].append(k1.reshape(bp, WINDOW, SWA_KV_HEADS, SWA_HEAD_DIM))
        outs["vp"].append(v1.reshape(bp, WINDOW, SWA_KV_HEADS, SWA_HEAD_DIM))
        z, ya = _sample_proj(ys, p, l)
        obr, *sample_states, gv2 = _sample_mixer(
            z, ya, gla_in, state_conv, swa_k_in, swa_v_in, p, sample_tables, sample_states, l)
        ys = _sample_merge(ys, obr, p, l)
        ys = _ffn(ys, p, l)
        outs["gm"].append(gv2.reshape(bs, ls, GMLP_WIDTH))
    st = lambda name: jnp.stack(outs[name], 0)
    gla_s, conv_s, ks, vs = sample_states
    kv_shape = (DEPTH, bs, WINDOW, SWA_KV_HEADS, SWA_HEAD_DIM)
    return (yp, ys.reshape(bs, ls, D_MODEL), st("gla_p"), gla_s.reshape(state_gla.shape), st("conv_p"),
            conv_s, st("kp"), st("vp"), ks.reshape(kv_shape), vs.reshape(kv_shape), st("gm"))
```

```python
import functools
import math

import jax
import jax.numpy as jnp
from jax import lax
from jax.experimental import pallas as pl
from jax.experimental.pallas import tpu as pltpu

F32 = jnp.float32
BF16 = jnp.bfloat16

D_MODEL = 1024
DEPTH = 2
PAST_LEN = 16384
GLA_HEADS = 4
GLA_DK = 64
GLA_DV = 128
GLA_RANK = 16
GLA_TAU = 16.0
GLA_CHUNK = 64
GMLP_GROUPS = 4
GMLP_WIDTH = 512
GMLP_CHUNK = 128
CONV_WIDTH = 512
CONV_K = 31
SWA_HEADS = 8
SWA_KV_HEADS = 2
SWA_GROUP = SWA_HEADS // SWA_KV_HEADS
SWA_HEAD_DIM = 64
WINDOW = 128
ROPE_DIM = SWA_HEAD_DIM // 4
ROPE_THETA = 500000.0
N_BRANCH = 4
BRANCH_WIDTH = 512
FFN_HIDDEN = 2816
NORM_EPS = 1e-6
LN_EPS = 1e-5

LANES = 128
SUBLANES = 8
VMEM_LIMIT_BYTES = 56 * 1024 * 1024

QK = GLA_HEADS * GLA_DK
QV = GLA_HEADS * GLA_DV
SWA_Q = SWA_HEADS * SWA_HEAD_DIM
SWA_KV = SWA_KV_HEADS * SWA_HEAD_DIM

C_AQ = 0
C_AK = C_AQ + QK
C_AV = C_AK + QK
C_AR = C_AV + QV
C_BU = C_AR + QV
C_BV = C_BU + GMLP_WIDTH
C_CA = C_BV + GMLP_WIDTH
C_CG = C_CA + CONV_WIDTH
C_DQ = C_CG + CONV_WIDTH
C_DK = C_DQ + SWA_Q
C_DV = C_DK + SWA_KV
C_LR = C_DV + SWA_KV
IN_COLS_PACKED = C_LR + LANES

PROMPT_TILE = 256
FFN_TILE = 512
FFN_CHUNK = 256
GATE_PART = 256
CONV_PAD = 32
CONV_LEAD = CONV_PAD - (CONV_K - 1)
CONV_ROWS = 32
SAMPLE_ROWS = 8
SAMPLE_TOKENS = 4
SAMPLE_GROUP = 8


def _mm(a, b):
    return jnp.dot(a, b, preferred_element_type=F32)


def _mm_nt(a, b):
    return lax.dot_general(a, b, (((1,), (1,)), ((), ())), preferred_element_type=F32)


def _mm_tn(a, b):
    return lax.dot_general(a, b, (((0,), (0,)), ((), ())), preferred_element_type=F32)


def _hi_lo(a):
    hi = a.astype(BF16)
    lo = (a - hi.astype(F32)).astype(BF16)
    return hi, lo


def _mm_split_lhs(a_f32, m_bf16):
    hi, lo = _hi_lo(a_f32)
    return _mm(hi, m_bf16) + _mm(lo, m_bf16)


def _rms_rows(x, g):
    return x * lax.rsqrt(jnp.mean(x * x, axis=-1, keepdims=True) + NORM_EPS) * g


def _ln_rows(x, g, b):
    mu = jnp.mean(x, axis=-1, keepdims=True)
    xc = x - mu
    var = jnp.mean(xc * xc, axis=-1, keepdims=True)
    return xc * lax.rsqrt(var + LN_EPS) * g + b


def _gelu(x):
    return 0.5 * x * (1.0 + lax.erf(x * (1.0 / math.sqrt(2.0))))


def _silu(x):
    return x * jax.nn.sigmoid(x)


def _log_sigmoid(x):
    return jnp.minimum(x, 0.0) - jnp.log(1.0 + jnp.exp(-jnp.abs(x)))


def _iota(shape, dim):
    return lax.broadcasted_iota(jnp.int32, shape, dim)


def _group_ones(n, group, dtype=BF16):
    return (_iota((n, n), 0) // group == _iota((n, n), 1) // group).astype(dtype)


def _group_mean_sq(x, group):
    return _mm_split_lhs(x * x, _group_ones(x.shape[-1], group)) * (1.0 / group)


def _rope_slab(x, c, a, b):
    return x * c + pltpu.roll(x, LANES - ROPE_DIM // 2, 1) * a + pltpu.roll(x, ROPE_DIM // 2, 1) * b


def _rope(x, c, a, b):
    n = x.shape[-1] // LANES
    if n == 1:
        return _rope_slab(x, c, a, b)
    return jnp.concatenate(
        [_rope_slab(x[:, i * LANES:(i + 1) * LANES], c, a, b) for i in range(n)], axis=1)


def _qk_norm_rope(z, g, c, a, b, split=True):
    if split:
        ms = _group_mean_sq(z, SWA_HEAD_DIM)
    else:
        ms = _mm((z * z).astype(BF16), _group_ones(z.shape[-1], SWA_HEAD_DIM)) * (1.0 / SWA_HEAD_DIM)
    return _rope(z * lax.rsqrt(ms + NORM_EPS) * g, c, a, b)


def _head_rms_gla(o, g):
    return jnp.concatenate(
        [_rms_rows(o[:, h * GLA_DV:(h + 1) * GLA_DV], g) for h in range(GLA_HEADS)], axis=1)


def _swa_q_for_group(q, g, i):
    hq = g * SWA_GROUP + i
    slab = q[:, (hq // 2) * LANES:(hq // 2 + 1) * LANES]
    if hq % 2 != g:
        slab = pltpu.roll(slab, SWA_HEAD_DIM, 1)
    lane_head = _iota(slab.shape, 1) // SWA_HEAD_DIM
    return jnp.where(lane_head == g, slab, 0.0)


def _swa_assemble(outs, rows):
    lane_half = _iota((rows, LANES), 1) // SWA_HEAD_DIM
    slabs = []
    for s in range(SWA_HEADS // 2):
        pair = []
        for hq in (2 * s, 2 * s + 1):
            g, i = hq // SWA_GROUP, hq % SWA_GROUP
            o = outs[g][i]
            if hq % 2 != g:
                o = pltpu.roll(o, SWA_HEAD_DIM, 1)
            pair.append(o)
        slabs.append(jnp.where(lane_half == 0, pair[0], pair[1]))
    return jnp.concatenate(slabs, axis=1)


def _sink_column(sinks_ref, base, rows, rows_per_head):
    row_head = _iota((rows, 1), 0) // rows_per_head
    sink = jnp.zeros((rows, 1), F32)
    for i in range(SWA_GROUP):
        sink = jnp.where(row_head == i, sinks_ref[base + i], sink)
    return sink


def _prompt_mixer_kernel(
        sinks_ref,
        x_ref, rope_ref,
        n1g_ref, win_ref, bin_ref, wal_ref, bal_ref, glag_ref,
        lng_ref, lnb_ref, wsp_ref, bspt_ref,
        cw_ref, cb_ref, clg_ref, clb_ref,
        qg_ref, kg_ref,
        wg_ref, bg_ref, wbr_ref, wout_ref,
        y_ref, gla_ref, conv_ref, ko_ref, vo_ref,
        gstate_ref, hbuf_ref, hsh_ref, kfull_ref, vfull_ref, gates_ref,
        *, layer):
    T = PROMPT_TILE
    t = pl.program_id(1)
    last = pl.num_programs(1) - 1

    @pl.when(t == 0)
    def _():
        gstate_ref[...] = jnp.zeros_like(gstate_ref)
        hbuf_ref[0:CONV_PAD, :] = jnp.zeros((CONV_PAD, CONV_WIDTH), F32)
        kfull_ref[0:WINDOW, :] = jnp.zeros((WINDOW, SWA_KV), F32)
        vfull_ref[0:WINDOW, :] = jnp.zeros((WINDOW, SWA_KV), F32)

    x = x_ref[0]
    xb = _rms_rows(x, n1g_ref[...]).astype(BF16)

    def proj(c0, width):
        return _mm(xb, win_ref[:, c0:c0 + width]) + bin_ref[:, c0:c0 + width]

    def gate_part(i, j):
        c0 = i * D_MODEL + j * GATE_PART
        gates_ref[i, :, j * GATE_PART:(j + 1) * GATE_PART] = jax.nn.sigmoid(
            _mm(xb, wg_ref[:, c0:c0 + GATE_PART]) + bg_ref[:, c0:c0 + GATE_PART])

    def branch(i, o_b16):
        return _mm(o_b16, wbr_ref[i, :, :D_MODEL])


    h = proj(C_CA, CONV_WIDTH) * jax.nn.sigmoid(proj(C_CG, CONV_WIDTH))
    q = proj(C_AQ, QK) * (GLA_DK ** -0.5)
    k = proj(C_AK, QK)
    v = proj(C_AV, QV)
    kvl = proj(C_DK, 2 * SWA_KV + LANES)
    zk, zvd, lr = kvl[:, :SWA_KV], kvl[:, SWA_KV:2 * SWA_KV], kvl[:, 2 * SWA_KV:]
    hbuf_ref[CONV_PAD:CONV_PAD + T, :] = h
    taps_of = {r: [kk for kk in range(CONV_K) if (CONV_LEAD + kk) % SUBLANES == r] for r in range(SUBLANES)}
    for r in range(1, SUBLANES):
        span = max(CONV_LEAD + kk for kk in taps_of[r]) - r
        hsh_ref[r - 1, 0:span + T, :] = hbuf_ref[r:r + span + T, :]

    z = {}
    conv_fill = [
        lambda: z.update(u=proj(C_BU, GMLP_WIDTH)),
        lambda: z.update(gv=proj(C_BV, GMLP_WIDTH)),
        lambda: z.update(q=proj(C_DQ, SWA_Q)),
        lambda: z.update(r=proj(C_AR, QV)),
        lambda: (gate_part(0, 0), gate_part(0, 1)),
        lambda: (gate_part(0, 2), gate_part(0, 3)),
        lambda: (gate_part(1, 0), gate_part(1, 1)),
        lambda: (gate_part(1, 2), gate_part(1, 3)),
    ]
    acc_blocks = []
    for rb in range(T // CONV_ROWS):
        acc = jnp.broadcast_to(cb_ref[...], (CONV_ROWS, CONV_WIDTH))
        for r_ in range(SUBLANES):
            for kk in taps_of[r_]:
                a = CONV_LEAD + kk - r_ + rb * CONV_ROWS
                rows_k = hbuf_ref[a:a + CONV_ROWS, :] if r_ == 0 else hsh_ref[r_ - 1, a:a + CONV_ROWS, :]
                acc = acc + cw_ref[kk:kk + 1, :] * rows_k
        acc_blocks.append(acc)
        conv_fill[rb]()
    hbuf_ref[0:CONV_PAD, :] = hbuf_ref[T:T + CONV_PAD, :]
    o_c = _silu(_ln_rows(jnp.concatenate(acc_blocks, axis=0), clg_ref[...], clb_ref[...])).astype(BF16)
    half = D_MODEL // 2
    partial = {}

    def branch_half(i, o_b16, j):
        partial[(i, j)] = _mm(o_b16, wbr_ref[i, :, j * half:(j + 1) * half])

    zu, zgv, zq, r = z["u"], z["gv"], z["q"], z["r"]

    la = _log_sigmoid(_mm(lr.astype(BF16), wal_ref[...]) + bal_ref[...]) * (1.0 / GLA_TAU)
    gate_part(2, 0)
    gate_part(2, 1)

    n_chunks = T // GLA_CHUNK
    ri = _iota((T, T), 0)
    ci = _iota((T, T), 1)
    chunk_causal = jnp.logical_and((ri // GLA_CHUNK) == (ci // GLA_CHUNK), ci <= ri)
    la_hi, la_lo = _hi_lo(la)
    m_tril = chunk_causal.astype(BF16)
    bcum = _mm(m_tril, la_hi) + _mm(m_tril, la_lo)
    tot_rows = [bcum[(c + 1) * GLA_CHUNK - 1:(c + 1) * GLA_CHUNK, :] for c in range(n_chunks)]
    btot = jnp.concatenate([jnp.broadcast_to(tr, (GLA_CHUNK, QK)) for tr in tot_rows], axis=0)
    q_in = (q * jnp.exp(bcum)).astype(BF16)
    k_out = (k * jnp.exp(-bcum)).astype(BF16)
    k_dec_t = (k * jnp.exp(btot - bcum)).T.astype(BF16)
    v_b = v.astype(BF16)
    dec_t = jnp.exp(jnp.concatenate(
        [jnp.broadcast_to(tr, (GLA_DV, QK)) for tr in tot_rows], axis=0).T)

    lane_head = _iota((T, QK), 1) // GLA_DK
    o_heads = []
    for hh in range(GLA_HEADS):
        qh = jnp.where(lane_head == hh, q_in, jnp.zeros_like(q_in))
        att = jnp.where(chunk_causal, _mm_nt(qh, k_out), 0.0).astype(BF16)
        o_heads.append(_mm(att, v_b[:, hh * GLA_DV:(hh + 1) * GLA_DV]))
    o_intra = jnp.concatenate(o_heads, axis=1)
    gate_part(2, 2)
    gate_part(2, 3)

    bd_mask = (_iota((QK, QV), 0) // GLA_DK) == (_iota((QK, QV), 1) // GLA_DV)
    tok_chunk = _iota((GLA_DK, T), 1) // GLA_CHUNK
    o_state = []
    for c in range(n_chunks):
        rs = slice(c * GLA_CHUNK, (c + 1) * GLA_CHUNK)
        s_c = gstate_ref[...]
        s_b = s_c.astype(BF16)
        s_bd = jnp.where(bd_mask, jnp.concatenate([s_b] * GLA_HEADS, axis=1), jnp.zeros((QK, QV), BF16))
        o_state.append(_mm(q_in[rs], s_bd))
        upd = []
        for hh in range(GLA_HEADS):
            kt = k_dec_t[hh * GLA_DK:(hh + 1) * GLA_DK, :]
            kt = jnp.where(tok_chunk == c, kt, jnp.zeros_like(kt))
            upd.append(_mm(kt, v_b[:, hh * GLA_DV:(hh + 1) * GLA_DV]))
        gstate_ref[...] = dec_t[:, c * GLA_DV:(c + 1) * GLA_DV] * s_c + jnp.concatenate(upd, axis=0)
    o_gla = o_intra + jnp.concatenate(o_state, axis=0)
    gate_part(3, 0)
    gate_part(3, 1)
    o_a = (_head_rms_gla(o_gla, glag_ref[...]) * _silu(r)).astype(BF16)

    u = _gelu(zu)
    gv = _ln_rows(_gelu(zgv), lng_ref[...], lnb_ref[...])
    gate_part(3, 2)
    gate_part(3, 3)
    gv_b = gv.astype(BF16)
    gw = GMLP_WIDTH // GMLP_GROUPS
    tril = _iota((GMLP_CHUNK, GMLP_CHUNK), 1) <= _iota((GMLP_CHUNK, GMLP_CHUNK), 0)
    bspt = bspt_ref[...]
    s_rows = []
    for n in range(T // GMLP_CHUNK):
        rs = slice(n * GMLP_CHUNK, (n + 1) * GMLP_CHUNK)
        cols = []
        for g in range(GMLP_GROUPS):
            w = jnp.where(tril, wsp_ref[g], 0.0).astype(BF16)
            sg = _mm(w, gv_b[rs, g * gw:(g + 1) * gw])
            cols.append(sg + jnp.broadcast_to(bspt[:, g:g + 1], (GMLP_CHUNK, gw)))
        s_rows.append(jnp.concatenate(cols, axis=1))
    o_b = (u * jnp.concatenate(s_rows, axis=0)).astype(BF16)

    rc, ra, rb = (rope_ref[:, i * LANES:(i + 1) * LANES] for i in range(3))
    qd = _qk_norm_rope(zq, qg_ref[...], rc, ra, rb, split=False) * (SWA_HEAD_DIM ** -0.5)
    kfull_ref[WINDOW:WINDOW + T, :] = _qk_norm_rope(zk, kg_ref[...], rc, ra, rb)
    vfull_ref[WINDOW:WINDOW + T, :] = zvd
    swa_fill = [functools.partial(branch_half, i, o, j) for i, o in ((2, o_c), (0, o_a)) for j in range(2)]

    rows = SWA_GROUP * WINDOW
    qi = _iota((rows, 2 * WINDOW), 0) % WINDOW
    kj = _iota((rows, 2 * WINDOW), 1)
    own_ok = jnp.logical_and(kj >= WINDOW, kj - WINDOW <= qi)
    prev_ok = jnp.logical_and(kj < WINDOW, kj > qi)
    o_blocks = []
    for qb in range(T // WINDOW):
        rs = slice(qb * WINDOW, (qb + 1) * WINDOW)
        kblk = kfull_ref[qb * WINDOW:(qb + 2) * WINDOW, :].astype(BF16)
        vblk = vfull_ref[qb * WINDOW:(qb + 2) * WINDOW, :].astype(BF16)
        if qb == 0:
            valid = jnp.logical_or(own_ok, jnp.logical_and(prev_ok, t > 0))
        else:
            valid = jnp.logical_or(own_ok, prev_ok)
        outs = []
        for g in range(SWA_KV_HEADS):
            q_stack = jnp.concatenate(
                [_swa_q_for_group(qd[rs], g, i) for i in range(SWA_GROUP)], axis=0).astype(BF16)
            s = jnp.where(valid, _mm_nt(q_stack, kblk), -jnp.inf)
            sink = _sink_column(sinks_ref, layer * SWA_HEADS + g * SWA_GROUP, rows, WINDOW)
            m = jnp.maximum(jnp.max(s, axis=-1, keepdims=True), sink)
            p = jnp.exp(s - m)
            den = jnp.sum(p, axis=-1, keepdims=True) + jnp.exp(sink - m)
            o = _mm(p.astype(BF16), vblk) / den
            outs.append([o[i * WINDOW:(i + 1) * WINDOW] for i in range(SWA_GROUP)])
            if swa_fill:
                swa_fill.pop(0)()
        o_blocks.append(_swa_assemble(outs, WINDOW))
    for fill in swa_fill:
        fill()
    o_d = jnp.concatenate(o_blocks, axis=0).astype(BF16)
    kfull_ref[0:WINDOW, :] = kfull_ref[T:T + WINDOW, :]
    vfull_ref[0:WINDOW, :] = vfull_ref[T:T + WINDOW, :]

    merged = (gates_ref[0] * jnp.concatenate([partial[(0, 0)], partial[(0, 1)]], axis=1)
              + gates_ref[2] * jnp.concatenate([partial[(2, 0)], partial[(2, 1)]], axis=1))
    merged = merged + gates_ref[1] * branch(1, o_b)
    merged = merged + gates_ref[3] * branch(3, o_d)
    y_ref[0] = x + _mm(merged.astype(BF16), wout_ref[:, :D_MODEL])

    @pl.when(t == last)
    def _():
        conv_ref[0] = hbuf_ref[CONV_LEAD:CONV_PAD, :]
        ko_ref[0] = kfull_ref[0:WINDOW, :]
        vo_ref[0] = vfull_ref[0:WINDOW, :]
        gla_ref[0] = gstate_ref[...]


def _layer_spec(arr, layer):
    nd = arr.ndim
    return pl.BlockSpec((None,) + tuple(arr.shape[1:]), lambda *_: (layer,) + (0,) * (nd - 1),
                        pipeline_mode=pl.Buffered(1))


def _full_spec(shape):
    nd = len(shape)
    return pl.BlockSpec(shape, lambda *_: (0,) * nd)


def _compiler_params(n_axes):
    return pltpu.CompilerParams(
        dimension_semantics=("arbitrary",) * n_axes, vmem_limit_bytes=VMEM_LIMIT_BYTES)


def _prompt_mixer(x, p, tables, layer):
    B, L, _ = x.shape
    T = PROMPT_TILE
    weights = tuple(p[n] for n in (
        "n1g", "win", "bin", "wal", "bal", "glag", "lng", "lnb", "wsp", "bspt",
        "cw", "cb", "clg", "clb", "qg", "kg", "wg", "bg", "wbr", "wout"))
    tab_spec = pl.BlockSpec((T, 3 * LANES), lambda b, t, *_: (t, 0))

    grid_spec = pltpu.PrefetchScalarGridSpec(
        num_scalar_prefetch=1,
        grid=(B, L // T),
        in_specs=[pl.BlockSpec((1, T, D_MODEL), lambda b, t, *_: (b, t, 0)),
                  tab_spec]
                 + [_layer_spec(w, layer) for w in weights],
        out_specs=[pl.BlockSpec((1, T, D_MODEL), lambda b, t, *_: (b, t, 0)),
                   pl.BlockSpec((1, QK, GLA_DV), lambda b, t, *_: (b, 0, 0)),
                   pl.BlockSpec((1, CONV_K - 1, CONV_WIDTH), lambda b, t, *_: (b, 0, 0)),
                   pl.BlockSpec((1, WINDOW, SWA_KV), lambda b, t, *_: (b, 0, 0)),
                   pl.BlockSpec((1, WINDOW, SWA_KV), lambda b, t, *_: (b, 0, 0))],
        scratch_shapes=[pltpu.VMEM((QK, GLA_DV), F32),
                        pltpu.VMEM((T + CONV_PAD, CONV_WIDTH), F32),
                        pltpu.VMEM((SUBLANES - 1, T + CONV_PAD, CONV_WIDTH), F32),
                        pltpu.VMEM((T + WINDOW, SWA_KV), F32),
                        pltpu.VMEM((T + WINDOW, SWA_KV), F32),
                        pltpu.VMEM((N_BRANCH, T, D_MODEL), F32)])
    return pl.pallas_call(
        functools.partial(_prompt_mixer_kernel, layer=layer),
        grid_spec=grid_spec,
        out_shape=[jax.ShapeDtypeStruct((B, L, D_MODEL), F32),
                   jax.ShapeDtypeStruct((B, QK, GLA_DV), F32),
                   jax.ShapeDtypeStruct((B, CONV_K - 1, CONV_WIDTH), F32),
                   jax.ShapeDtypeStruct((B, WINDOW, SWA_KV), F32),
                   jax.ShapeDtypeStruct((B, WINDOW, SWA_KV), F32)],
        compiler_params=_compiler_params(2),
        name="prompt_mixer",
    )(p["sinks"], x, jnp.concatenate(tables, axis=1), *weights)


def _ffn_kernel(x_ref, g_ref, wgu_ref, wd_ref, y_ref, h_ref):
    x = x_ref[...]
    xb = _rms_rows(x, g_ref[...]).astype(BF16)
    for c in range(FFN_HIDDEN // FFN_CHUNK):
        c0 = c * FFN_CHUNK
        gate = _mm(xb, wgu_ref[:, c0:c0 + FFN_CHUNK])
        up = _mm(xb, wgu_ref[:, FFN_HIDDEN + c0:FFN_HIDDEN + c0 + FFN_CHUNK])
        h_ref[:, c0:c0 + FFN_CHUNK] = (_silu(gate) * up).astype(BF16)
    y_ref[...] = x + _mm(h_ref[...], wd_ref[...])


def _ffn(x2d, p, layer):
    n = x2d.shape[0]
    tile = min(FFN_TILE, n)
    return pl.pallas_call(
        _ffn_kernel,
        grid=(n // tile,),
        in_specs=[pl.BlockSpec((tile, D_MODEL), lambda i: (i, 0)),
                  _layer_spec(p["n2g"], layer), _layer_spec(p["wgu"], layer), _layer_spec(p["wd"], layer)],
        out_specs=pl.BlockSpec((tile, D_MODEL), lambda i: (i, 0)),
        out_shape=jax.ShapeDtypeStruct((n, D_MODEL), F32),
        scratch_shapes=[pltpu.VMEM((tile, FFN_HIDDEN), BF16)],
        compiler_params=_compiler_params(1),
        name="swiglu",
    )(x2d, p["n2g"], p["wgu"], p["wd"])


def _sample_proj_kernel(x_ref, n1g_ref, win_ref, bin_ref, wal_ref, bal_ref, z_ref, ya_ref):
    xb = _rms_rows(x_ref[...], n1g_ref[...]).astype(BF16)
    z = _mm(xb, win_ref[...]) + bin_ref[...]
    z_ref[...] = z
    ya_ref[...] = _mm(z[:, C_LR:C_LR + LANES].astype(BF16), wal_ref[...]) + bal_ref[...]


def _sample_proj(x2d, p, layer):
    n = x2d.shape[0]
    weights = tuple(p[k] for k in ("n1g", "win", "bin", "wal", "bal"))
    return pl.pallas_call(
        _sample_proj_kernel,
        grid=(1,),
        in_specs=[_full_spec(x2d.shape)] + [_layer_spec(w, layer) for w in weights],
        out_specs=[_full_spec((n, IN_COLS_PACKED)), _full_spec((n, QK))],
        out_shape=[jax.ShapeDtypeStruct((n, IN_COLS_PACKED), F32),
                   jax.ShapeDtypeStruct((n, QK), F32)],
        compiler_params=_compiler_params(1),
        name="sample_proj",
    )(x2d, *weights)


def _sample_sequence(j, layer, sinks_ref, z_ref, ya_ref, s0_ref, hist_ref, kc_ref, vc_ref,
                     rc, ra, rb, glag, lng, lnb, wsc_ref, bsc, wsh_ref, cb, clg, clb, qg, kg,
                     obr_ref, s1_ref, convo_ref, ko_ref, vo_ref, gvo_ref, hext_ref):
    R = SAMPLE_ROWS
    nt = SAMPLE_TOKENS
    row = _iota((R, 1), 0)

    def seq_rows(ref):
        x = ref[(j // 2) * R:(j // 2 + 1) * R, :]
        if j % 2:
            x = pltpu.roll(x, R - nt, 0)
        return jnp.where(row < nt, x, 0.0)

    z = seq_rows(z_ref)
    head_ones = _group_ones(SWA_Q, SWA_HEAD_DIM)

    q = z[:, C_AQ:C_AQ + QK] * (GLA_DK ** -0.5)
    k = z[:, C_AK:C_AK + QK]
    v = z[:, C_AV:C_AV + QV]
    r = z[:, C_AR:C_AR + QV]
    zq = z[:, C_DQ:C_DQ + SWA_Q]
    zk = z[:, C_DK:C_DK + SWA_KV]
    zv = z[:, C_DV:C_DV + SWA_KV]
    ms_q = _mm_split_lhs(zq * zq, head_ones) * (1.0 / SWA_HEAD_DIM)
    ms_k = _mm_split_lhs(zk * zk, head_ones[:SWA_KV, :SWA_KV]) * (1.0 / SWA_HEAD_DIM)
    la = jnp.where(row < nt, _log_sigmoid(seq_rows(ya_ref)) * (1.0 / GLA_TAU), 0.0)
    bcum = jnp.zeros_like(la)
    for s in range(nt):
        bcum = bcum + jnp.where(row >= s, jnp.broadcast_to(la[s:s + 1, :], la.shape), 0.0)
    btot_row = bcum[nt - 1:nt, :]
    q_in = q * jnp.exp(bcum)
    k_out = k * jnp.exp(-bcum)
    k_dec = k * jnp.exp(jnp.broadcast_to(btot_row, la.shape) - bcum)
    s0 = s0_ref[j]
    s0_b = s0.astype(BF16)
    head_sel = (_iota((QK, LANES), 0) // GLA_DK == _iota((QK, LANES), 1)).astype(BF16)
    head_exp = (_iota((LANES, QV), 0) == _iota((LANES, QV), 1) // GLA_DV).astype(BF16)
    lane_head = _iota((R, QK), 1) // GLA_DK
    att_s = [_mm_split_lhs(q_in * jnp.broadcast_to(k_out[s:s + 1, :], q_in.shape), head_sel)
             for s in range(nt)]
    o_state = [_mm(jnp.where(lane_head == hh, q_in, 0.0).astype(BF16), s0_b) for hh in range(GLA_HEADS)]
    upd = _mm_tn(k_dec.astype(BF16).astype(F32), v.astype(BF16).astype(F32))
    dcol = jnp.exp(jnp.broadcast_to(btot_row, (GLA_DV, QK)).T)

    u = _gelu(z[:, C_BU:C_BU + GMLP_WIDTH])
    gv = _ln_rows(_gelu(z[:, C_BV:C_BV + GMLP_WIDTH]), lng, lnb)
    gvo_ref[j * nt:(j + 1) * nt, :] = gv[0:nt, :]
    sp = bsc
    for s in range(nt):
        sp = sp + jnp.where(row >= s, wsc_ref[s], 0.0) * jnp.broadcast_to(gv[s:s + 1, :], gv.shape)
    o_b = u * sp

    h = z[:, C_CA:C_CA + CONV_WIDTH] * jax.nn.sigmoid(z[:, C_CG:C_CG + CONV_WIDTH])
    hext_ref[j, 0:SUBLANES, :] = jnp.zeros((SUBLANES, CONV_WIDTH), F32)
    hext_ref[j, CONV_LEAD:CONV_PAD, :] = hist_ref[j]
    hext_ref[j, CONV_PAD:CONV_PAD + R, :] = h
    yield

    hext = hext_ref[j]
    conv = jnp.zeros((R, CONV_WIDTH), F32)
    for tt in range(nt):
        c_t = jnp.sum(hext * wsh_ref[tt], axis=0, keepdims=True)
        conv = jnp.where(row == tt, jnp.broadcast_to(c_t, conv.shape), conv)
    o_c = _silu(_ln_rows(conv + cb, clg, clb))
    convo_ref[j] = hext_ref[j, CONV_PAD + nt - (CONV_K - 1):CONV_PAD + nt, :]

    att_e = [_mm(a.astype(BF16), head_exp) for a in att_s]
    s1_ref[j] = dcol * s0 + jnp.concatenate(
        [upd[hh * GLA_DK:(hh + 1) * GLA_DK, hh * GLA_DV:(hh + 1) * GLA_DV] for hh in range(GLA_HEADS)],
        axis=0)

    qd = _rope(zq * lax.rsqrt(ms_q + NORM_EPS) * qg, rc, ra, rb)
    k_new = _rope(zk * lax.rsqrt(ms_k + NORM_EPS) * kg, rc, ra, rb)
    k_hist_b = kc_ref[j].astype(BF16)
    v_hist_b = vc_ref[j].astype(BF16)
    rows = SWA_GROUP * R
    qt = _iota((rows, WINDOW), 0) % R
    hist_ok = _iota((rows, WINDOW), 1) > qt
    qt1 = _iota((rows, 1), 0) % R
    scale = SWA_HEAD_DIM ** -0.5
    q_stacks, s_hists, s_news = [], [], []
    for g in range(SWA_KV_HEADS):
        q_stack = jnp.concatenate([_swa_q_for_group(qd, g, i) for i in range(SWA_GROUP)], axis=0)
        q_stacks.append(q_stack)
        s_hists.append(_mm_nt(q_stack.astype(BF16), k_hist_b))
        s_news.append([jnp.sum(q_stack * jnp.broadcast_to(k_new[s:s + 1, :], q_stack.shape),
                               axis=-1, keepdims=True) for s in range(nt)])
    yield

    o_intra = jnp.zeros((R, QV), F32)
    for s in range(nt):
        o_intra = o_intra + jnp.where(row >= s, att_e[s], 0.0) * jnp.broadcast_to(v[s:s + 1, :], (R, QV))
    o_a = _head_rms_gla(o_intra + jnp.concatenate(o_state, axis=1), glag) * _silu(r)

    sinks, ms, s_masked, n_masked = [], [], [], []
    for g in range(SWA_KV_HEADS):
        s_hist = jnp.where(hist_ok, s_hists[g] * scale, -jnp.inf)
        s_new = [jnp.where(qt1 >= s, s_news[g][s] * scale, -jnp.inf) for s in range(nt)]
        sink = _sink_column(sinks_ref, layer * SWA_HEADS + g * SWA_GROUP, rows, R)
        m = jnp.maximum(jnp.max(s_hist, axis=-1, keepdims=True), sink)
        for s in range(nt):
            m = jnp.maximum(m, s_new[s])
        sinks.append(sink)
        ms.append(m)
        s_masked.append(s_hist)
        n_masked.append(s_new)
    yield

    dens, pvs, p_news = [], [], []
    for g in range(SWA_KV_HEADS):
        p_hist = jnp.exp(s_masked[g] - ms[g])
        dens.append(jnp.sum(p_hist, axis=-1, keepdims=True) + jnp.exp(sinks[g] - ms[g]))
        pvs.append(_mm(p_hist.astype(BF16), v_hist_b))
        p_news.append([jnp.exp(n_masked[g][s] - ms[g]) for s in range(nt)])
    yield

    outs = []
    for g in range(SWA_KV_HEADS):
        o, den = pvs[g], dens[g]
        for s in range(nt):
            den = den + p_news[g][s]
            o = o + p_news[g][s] * jnp.broadcast_to(zv[s:s + 1, :], o.shape)
        o = o / den
        outs.append([o[i * R:(i + 1) * R] for i in range(SWA_GROUP)])
    o_d = _swa_assemble(outs, R)

    ko_ref[j, 0:WINDOW - nt, :] = kc_ref[j, nt:WINDOW, :]
    ko_ref[j, WINDOW - nt:WINDOW, :] = k_new[0:nt, :]
    vo_ref[j, 0:WINDOW - nt, :] = vc_ref[j, nt:WINDOW, :]
    vo_ref[j, WINDOW - nt:WINDOW, :] = zv[0:nt, :]

    obr_ref[j * nt:(j + 1) * nt, :] = jnp.concatenate([o_a, o_b, o_c, o_d], axis=1)[0:nt, :]


def _sample_mixer_kernel(
        sinks_ref,
        z_ref, ya_ref, s0_ref, hist_ref, kc_ref, vc_ref,
        rc_ref, ra_ref, rb_ref,
        glag_ref, lng_ref, lnb_ref, wsc_ref, bsc_ref,
        wsh_ref, cb_ref, clg_ref, clb_ref, qg_ref, kg_ref,
        s1_all_ref, convo_all_ref, ko_all_ref, vo_all_ref,
        obr_ref, s1_ref, convo_ref, ko_ref, vo_ref, gvo_ref,
        hext_ref, *, layer):
    del s1_all_ref, convo_all_ref, ko_all_ref, vo_all_ref
    consts = (rc_ref[...], ra_ref[...], rb_ref[...], glag_ref[...], lng_ref[...], lnb_ref[...])
    sequences = [
        _sample_sequence(
            j, layer, sinks_ref, z_ref, ya_ref, s0_ref, hist_ref, kc_ref, vc_ref,
            *consts, wsc_ref, bsc_ref[...], wsh_ref, cb_ref[...], clg_ref[...], clb_ref[...],
            qg_ref[...], kg_ref[...],
            obr_ref, s1_ref, convo_ref, ko_ref, vo_ref, gvo_ref, hext_ref)
        for j in range(SAMPLE_GROUP)]
    while sequences:
        sequences = [seq for seq in sequences if next(seq, StopIteration) is not StopIteration]


def _sample_mixer(z, ya, s0, hist, kc, vc, p, tables, stacked_states, layer):
    nt = SAMPLE_TOKENS
    nb = z.shape[0] // nt
    R = SAMPLE_ROWS
    G = SAMPLE_GROUP
    weights = tuple(p[n] for n in ("glag", "lng", "lnb", "wsc", "bsc", "wsh", "cb", "clg", "clb", "qg", "kg"))

    def per_group(shape):
        return pl.BlockSpec((G * nt, shape[1]), lambda b, *_: (b, 0))

    def per_group_of_layer(arr):
        nd = arr.ndim
        return pl.BlockSpec((None, G) + tuple(arr.shape[2:]), lambda b, *_: (layer, b) + (0,) * (nd - 2))

    def layer_block(arr):
        nd = arr.ndim
        return pl.BlockSpec((None,) + tuple(arr.shape[1:]), lambda b, *_: (layer,) + (0,) * (nd - 1))

    out_shapes = [jax.ShapeDtypeStruct((nb * nt, N_BRANCH * BRANCH_WIDTH), F32)] \
        + [jax.ShapeDtypeStruct(a.shape, F32) for a in stacked_states] \
        + [jax.ShapeDtypeStruct((nb * nt, GMLP_WIDTH), F32)]
    operands = (z, ya, s0, hist, kc, vc, *tables, *weights)
    n_in = 1 + len(operands)
    grid_spec = pltpu.PrefetchScalarGridSpec(
        num_scalar_prefetch=1,
        grid=(nb // G,),
        in_specs=[per_group(a.shape) for a in (z, ya)]
                 + [per_group_of_layer(a) for a in (s0, hist, kc, vc)]
                 + [_full_spec(tb.shape) for tb in tables]
                 + [layer_block(w) for w in weights]
                 + [pl.BlockSpec(memory_space=pl.ANY) for _ in stacked_states],
        out_specs=[per_group(out_shapes[0].shape)]
                  + [per_group_of_layer(a) for a in stacked_states]
                  + [per_group(out_shapes[-1].shape)],
        scratch_shapes=[pltpu.VMEM((G, CONV_PAD + R, CONV_WIDTH), F32)])
    return pl.pallas_call(
        functools.partial(_sample_mixer_kernel, layer=layer),
        grid_spec=grid_spec,
        out_shape=out_shapes,
        input_output_aliases={n_in + i: 1 + i for i in range(len(stacked_states))},
        compiler_params=_compiler_params(1),
        name="sample_mixer",
    )(p["sinks"], *operands, *stacked_states)


def _sample_merge_kernel(x_ref, obr_ref, n1g_ref, wg_ref, bg_ref, wbr_ref, wout_ref, y_ref):
    x = x_ref[...]
    xb = _rms_rows(x, n1g_ref[...]).astype(BF16)
    merged = jnp.zeros(x.shape, F32)
    for i in range(N_BRANCH):
        gate = jax.nn.sigmoid(
            _mm(xb, wg_ref[:, i * D_MODEL:(i + 1) * D_MODEL]) + bg_ref[:, i * D_MODEL:(i + 1) * D_MODEL])
        o_i = obr_ref[:, i * BRANCH_WIDTH:(i + 1) * BRANCH_WIDTH].astype(BF16)
        merged = merged + gate * _mm(o_i, wbr_ref[i, :, :D_MODEL])
    y_ref[...] = x + _mm(merged.astype(BF16), wout_ref[:, :D_MODEL])


def _sample_merge(x2d, obr, p, layer):
    weights = tuple(p[k] for k in ("n1g", "wg", "bg", "wbr", "wout"))
    return pl.pallas_call(
        _sample_merge_kernel,
        grid=(1,),
        in_specs=[_full_spec(x2d.shape), _full_spec(obr.shape)] + [_layer_spec(w, layer) for w in weights],
        out_specs=_full_spec(x2d.shape),
        out_shape=jax.ShapeDtypeStruct(x2d.shape, F32),
        compiler_params=_compiler_params(1),
        name="sample_merge",
    )(x2d, obr, *weights)


def _rope_tables(pos):
    half = ROPE_DIM // 2
    inv = jnp.exp(-math.log(ROPE_THETA) * jnp.arange(half, dtype=F32) * (2.0 / ROPE_DIM))
    ang = pos.astype(F32)[:, None] * inv[None, :]
    cos, sin = jnp.cos(ang), jnp.sin(ang)
    n = pos.shape[0]
    pad = jnp.zeros((n, SWA_HEAD_DIM - ROPE_DIM), F32)
    zero = jnp.zeros((n, half), F32)
    c_head = jnp.concatenate([cos, cos, pad + 1.0], axis=1)
    a_head = jnp.concatenate([-sin, zero, pad], axis=1)
    b_head = jnp.concatenate([zero, sin, pad], axis=1)
    rep = LANES // SWA_HEAD_DIM
    return tuple(jnp.tile(tb, (1, rep)) for tb in (c_head, a_head, b_head))


def _lane_pad(w):
    return jnp.pad(w, [(0, 0)] * (w.ndim - 1) + [(0, LANES)]).astype(BF16)


def _rows(v):
    return v.reshape(v.shape[0], 1, -1).astype(F32)


def _stacked_params(norm1_g, w_in, b_in, w_alpha2, b_alpha, gla_norm_g, gmlp_ln_g, gmlp_ln_b,
                    w_spatial, b_spatial, conv_w, conv_b, conv_ln_g, conv_ln_b, q_norm_g, k_norm_g,
                    sinks, w_gate, b_gate, w_branch, w_out, norm2_g, w_gate_up, w_down):
    o_lr = 2 * QK + 2 * QV
    o_rest = o_lr + GLA_RANK

    def repack(m, dtype):
        pad = jnp.zeros(m.shape[:-1] + (LANES - GLA_RANK,), m.dtype)
        return jnp.concatenate(
            [m[..., :o_lr], m[..., o_rest:], m[..., o_lr:o_rest], pad], axis=-1).astype(dtype)

    nt = SAMPLE_TOKENS
    gw = GMLP_WIDTH // GMLP_GROUPS
    pad_rows = ((0, 0), (0, SAMPLE_ROWS - nt), (0, 0))
    wsc = jnp.stack([
        jnp.pad(jnp.repeat(jnp.swapaxes(w_spatial[:, :, :nt, s], 1, 2), gw, axis=2), pad_rows)
        for s in range(nt)], axis=1)
    bsc = jnp.pad(jnp.repeat(jnp.swapaxes(b_spatial[:, :, :nt], 1, 2), gw, axis=2), pad_rows)
    wsh = jnp.stack([
        jnp.pad(conv_w, ((0, 0), (CONV_LEAD + t, CONV_PAD + SAMPLE_ROWS - CONV_LEAD - t - CONV_K), (0, 0)))
        for t in range(nt)], axis=1)
    return {
        "n1g": _rows(norm1_g),
        "win": repack(w_in, BF16),
        "bin": repack(b_in, F32).reshape(DEPTH, 1, -1),
        "wal": jnp.pad(w_alpha2, ((0, 0), (0, LANES - GLA_RANK), (0, 0))).astype(BF16),
        "bal": _rows(b_alpha),
        "glag": _rows(gla_norm_g),
        "lng": _rows(gmlp_ln_g),
        "lnb": _rows(gmlp_ln_b),
        "wsp": w_spatial.astype(F32),
        "bspt": jnp.swapaxes(b_spatial, 1, 2).astype(F32),
        "wsc": wsc.astype(F32),
        "bsc": bsc.astype(F32),
        "cw": jnp.pad(conv_w, ((0, 0), (0, CONV_PAD - CONV_K), (0, 0))).astype(F32),
        "wsh": wsh.astype(F32),
        "cb": _rows(conv_b),
        "clg": _rows(conv_ln_g),
        "clb": _rows(conv_ln_b),
        "qg": _rows(jnp.tile(q_norm_g, (1, SWA_HEADS))),
        "kg": _rows(jnp.tile(k_norm_g, (1, SWA_KV_HEADS))),
        "sinks": sinks.reshape(-1).astype(F32),
        "wg": _lane_pad(w_gate),
        "bg": _rows(b_gate),
        "wbr": _lane_pad(w_branch),
        "wout": _lane_pad(w_out),
        "n2g": _rows(norm2_g),
        "wgu": w_gate_up.astype(BF16),
        "wd": w_down.astype(BF16),
    }


def kernel(x_prompt, x_sample, state_gla, state_conv, cache_swa_k, cache_swa_v, norm1_g, w_in, b_in, w_alpha2, b_alpha, gla_norm_g, gmlp_ln_g, gmlp_ln_b, w_spatial, b_spatial, conv_w, conv_b, conv_ln_g, conv_ln_b, q_norm_g, k_norm_g, sinks, w_gate, b_gate, w_branch, w_out, norm2_g, w_gate_up, w_down):
    bp, lp, _ = x_prompt.shape
    bs, ls, _ = x_sample.shape
    R = SAMPLE_ROWS
    prompt_tables = _rope_tables(jnp.arange(lp))
    sample_tables = tuple(
        jnp.pad(tb, ((0, R - ls), (0, 0))) for tb in _rope_tables(PAST_LEN + jnp.arange(ls)))
    p = _stacked_params(norm1_g, w_in, b_in, w_alpha2, b_alpha, gla_norm_g, gmlp_ln_g, gmlp_ln_b,
                        w_spatial, b_spatial, conv_w, conv_b, conv_ln_g, conv_ln_b, q_norm_g,
                        k_norm_g, sinks, w_gate, b_gate, w_branch, w_out, norm2_g, w_gate_up, w_down)

    gla_in = state_gla.reshape(DEPTH, bs, QK, GLA_DV)
    swa_k_in = cache_swa_k.reshape(DEPTH, bs, WINDOW, SWA_KV)
    swa_v_in = cache_swa_v.reshape(DEPTH, bs, WINDOW, SWA_KV)

    sample_states = [jnp.zeros(a.shape, F32) for a in (gla_in, state_conv, swa_k_in, swa_v_in)]

    yp = x_prompt
    ys = x_sample.reshape(bs * ls, D_MODEL)
    outs = {k: [] for k in ("gla_p", "conv_p", "kp", "vp", "gm")}
    for l in range(DEPTH):
        yp, g1, c1, k1, v1 = _prompt_mixer(yp, p, prompt_tables, l)
        yp = _ffn(yp.reshape(bp * lp, D_MODEL), p, l).reshape(bp, lp, D_MODEL)
        outs["gla_p"].append(g1.reshape(bp, GLA_HEADS, GLA_DK, GLA_DV))
        outs["conv_p"].append(c1)
        outs["kp"].append(k1.reshape(bp, WINDOW, SWA_KV_HEADS, SWA_HEAD_DIM))
        outs["vp"].append(v1.reshape(bp, WINDOW, SWA_KV_HEADS, SWA_HEAD_DIM))
        z, ya = _sample_proj(ys, p, l)
        obr, *sample_states, gv2 = _sample_mixer(
            z, ya, gla_in, state_conv, swa_k_in, swa_v_in, p, sample_tables, sample_states, l)
        ys = _sample_merge(ys, obr, p, l)
        ys = _ffn(ys, p, l)
        outs["gm"].append(gv2.reshape(bs, ls, GMLP_WIDTH))
    st = lambda name: jnp.stack(outs[name], 0)
    gla_s, conv_s, ks, vs = sample_states
    kv_shape = (DEPTH, bs, WINDOW, SWA_KV_HEADS, SWA_HEAD_DIM)
    return (yp, ys.reshape(bs, ls, D_MODEL), st("gla_p"), gla_s.reshape(state_gla.shape), st("conv_p"),
            conv_s, st("kp"), st("vp"), ks.reshape(kv_shape), vs.reshape(kv_shape), st("gm"))
```

```python
import functools
import math

import jax
import jax.numpy as jnp
from jax import lax
from jax.experimental import pallas as pl
from jax.experimental.pallas import tpu as pltpu

F32 = jnp.float32
BF16 = jnp.bfloat16

D_MODEL = 1024
DEPTH = 2
PAST_LEN = 16384
GLA_HEADS = 4
GLA_DK = 64
GLA_DV = 128
GLA_RANK = 16
GLA_TAU = 16.0
GLA_CHUNK = 64
GMLP_GROUPS = 4
GMLP_WIDTH = 512
GMLP_CHUNK = 128
CONV_WIDTH = 512
CONV_K = 31
SWA_HEADS = 8
SWA_KV_HEADS = 2
SWA_GROUP = SWA_HEADS // SWA_KV_HEADS
SWA_HEAD_DIM = 64
WINDOW = 128
ROPE_DIM = SWA_HEAD_DIM // 4
ROPE_THETA = 500000.0
N_BRANCH = 4
BRANCH_WIDTH = 512
FFN_HIDDEN = 2816
NORM_EPS = 1e-6
LN_EPS = 1e-5

LANES = 128
SUBLANES = 8
VMEM_LIMIT_BYTES = 56 * 1024 * 1024

QK = GLA_HEADS * GLA_DK
QV = GLA_HEADS * GLA_DV
SWA_Q = SWA_HEADS * SWA_HEAD_DIM
SWA_KV = SWA_KV_HEADS * SWA_HEAD_DIM

C_AQ = 0
C_AK = C_AQ + QK
C_AV = C_AK + QK
C_AR = C_AV + QV
C_BU = C_AR + QV
C_BV = C_BU + GMLP_WIDTH
C_CA = C_BV + GMLP_WIDTH
C_CG = C_CA + CONV_WIDTH
C_DQ = C_CG + CONV_WIDTH
C_DK = C_DQ + SWA_Q
C_DV = C_DK + SWA_KV
C_LR = C_DV + SWA_KV
IN_COLS_PACKED = C_LR + LANES

PROMPT_TILE = 256
FFN_TILE = 512
FFN_CHUNK = 256
GATE_PART = 256
CONV_PAD = 32
CONV_LEAD = CONV_PAD - (CONV_K - 1)
CONV_ROWS = 32
SAMPLE_ROWS = 8
SAMPLE_TOKENS = 4
SAMPLE_GROUP = 8


def _mm(a, b):
    return jnp.dot(a, b, preferred_element_type=F32)


def _mm_nt(a, b):
    return lax.dot_general(a, b, (((1,), (1,)), ((), ())), preferred_element_type=F32)


def _mm_tn(a, b):
    return lax.dot_general(a, b, (((0,), (0,)), ((), ())), preferred_element_type=F32)


def _hi_lo(a):
    hi = a.astype(BF16)
    lo = (a - hi.astype(F32)).astype(BF16)
    return hi, lo


def _mm_split_lhs(a_f32, m_bf16):
    hi, lo = _hi_lo(a_f32)
    return _mm(hi, m_bf16) + _mm(lo, m_bf16)


def _rms_rows(x, g):
    return x * lax.rsqrt(jnp.mean(x * x, axis=-1, keepdims=True) + NORM_EPS) * g


def _ln_rows(x, g, b):
    mu = jnp.mean(x, axis=-1, keepdims=True)
    xc = x - mu
    var = jnp.mean(xc * xc, axis=-1, keepdims=True)
    return xc * lax.rsqrt(var + LN_EPS) * g + b


def _gelu(x):
    return 0.5 * x * (1.0 + lax.erf(x * (1.0 / math.sqrt(2.0))))


def _silu(x):
    return x * jax.nn.sigmoid(x)


def _log_sigmoid(x):
    return jnp.minimum(x, 0.0) - jnp.log(1.0 + jnp.exp(-jnp.abs(x)))


def _iota(shape, dim):
    return lax.broadcasted_iota(jnp.int32, shape, dim)


def _group_ones(n, group, dtype=BF16):
    return (_iota((n, n), 0) // group == _iota((n, n), 1) // group).astype(dtype)


def _group_mean_sq(x, group):
    return _mm_split_lhs(x * x, _group_ones(x.shape[-1], group)) * (1.0 / group)


def _rope_slab(x, c, a, b):
    return x * c + pltpu.roll(x, LANES - ROPE_DIM // 2, 1) * a + pltpu.roll(x, ROPE_DIM // 2, 1) * b


def _rope(x, c, a, b):
    n = x.shape[-1] // LANES
    if n == 1:
        return _rope_slab(x, c, a, b)
    return jnp.concatenate(
        [_rope_slab(x[:, i * LANES:(i + 1) * LANES], c, a, b) for i in range(n)], axis=1)


def _qk_norm_rope(z, g, c, a, b, split=True):
    if split:
        ms = _group_mean_sq(z, SWA_HEAD_DIM)
    else:
        ms = _mm((z * z).astype(BF16), _group_ones(z.shape[-1], SWA_HEAD_DIM)) * (1.0 / SWA_HEAD_DIM)
    return _rope(z * lax.rsqrt(ms + NORM_EPS) * g, c, a, b)


def _head_rms_gla(o, g):
    return jnp.concatenate(
        [_rms_rows(o[:, h * GLA_DV:(h + 1) * GLA_DV], g) for h in range(GLA_HEADS)], axis=1)


def _swa_q_for_group(q, g, i):
    hq = g * SWA_GROUP + i
    slab = q[:, (hq // 2) * LANES:(hq // 2 + 1) * LANES]
    if hq % 2 != g:
        slab = pltpu.roll(slab, SWA_HEAD_DIM, 1)
    lane_head = _iota(slab.shape, 1) // SWA_HEAD_DIM
    return jnp.where(lane_head == g, slab, 0.0)


def _swa_assemble(outs, rows):
    lane_half = _iota((rows, LANES), 1) // SWA_HEAD_DIM
    slabs = []
    for s in range(SWA_HEADS // 2):
        pair = []
        for hq in (2 * s, 2 * s + 1):
            g, i = hq // SWA_GROUP, hq % SWA_GROUP
            o = outs[g][i]
            if hq % 2 != g:
                o = pltpu.roll(o, SWA_HEAD_DIM, 1)
            pair.append(o)
        slabs.append(jnp.where(lane_half == 0, pair[0], pair[1]))
    return jnp.concatenate(slabs, axis=1)


def _sink_column(sinks_ref, base, rows, rows_per_head):
    row_head = _iota((rows, 1), 0) // rows_per_head
    sink = jnp.zeros((rows, 1), F32)
    for i in range(SWA_GROUP):
        sink = jnp.where(row_head == i, sinks_ref[base + i], sink)
    return sink


def _prompt_mixer_kernel(
        sinks_ref,
        x_ref, rope_ref,
        n1g_ref, win_ref, bin_ref, wal_ref, bal_ref, glag_ref,
        lng_ref, lnb_ref, wsp_ref, bspt_ref,
        cw_ref, cb_ref, clg_ref, clb_ref,
        qg_ref, kg_ref,
        wg_ref, bg_ref, wbr_ref, wout_ref,
        y_ref, gla_ref, conv_ref, ko_ref, vo_ref,
        gstate_ref, hbuf_ref, hsh_ref, kfull_ref, vfull_ref, gates_ref,
        *, layer):
    T = PROMPT_TILE
    t = pl.program_id(1)
    last = pl.num_programs(1) - 1

    @pl.when(t == 0)
    def _():
        gstate_ref[...] = jnp.zeros_like(gstate_ref)
        hbuf_ref[0:CONV_PAD, :] = jnp.zeros((CONV_PAD, CONV_WIDTH), F32)
        kfull_ref[0:WINDOW, :] = jnp.zeros((WINDOW, SWA_KV), F32)
        vfull_ref[0:WINDOW, :] = jnp.zeros((WINDOW, SWA_KV), F32)

    x = x_ref[0]
    xb = _rms_rows(x, n1g_ref[...]).astype(BF16)

    def proj(c0, width):
        return _mm(xb, win_ref[:, c0:c0 + width]) + bin_ref[:, c0:c0 + width]

    def gate_part(i, j):
        c0 = i * D_MODEL + j * GATE_PART
        gates_ref[i, :, j * GATE_PART:(j + 1) * GATE_PART] = jax.nn.sigmoid(
            _mm(xb, wg_ref[:, c0:c0 + GATE_PART]) + bg_ref[:, c0:c0 + GATE_PART])

    def branch(i, o_b16):
        return _mm(o_b16, wbr_ref[i, :, :D_MODEL])


    h = proj(C_CA, CONV_WIDTH) * jax.nn.sigmoid(proj(C_CG, CONV_WIDTH))
    q = proj(C_AQ, QK) * (GLA_DK ** -0.5)
    k = proj(C_AK, QK)
    v = proj(C_AV, QV)
    kvl = proj(C_DK, 2 * SWA_KV + LANES)
    zk, zvd, lr = kvl[:, :SWA_KV], kvl[:, SWA_KV:2 * SWA_KV], kvl[:, 2 * SWA_KV:]
    hbuf_ref[CONV_PAD:CONV_PAD + T, :] = h
    taps_of = {r: [kk for kk in range(CONV_K) if (CONV_LEAD + kk) % SUBLANES == r] for r in range(SUBLANES)}
    for r in range(1, SUBLANES):
        span = max(CONV_LEAD + kk for kk in taps_of[r]) - r
        hsh_ref[r - 1, 0:span + T, :] = hbuf_ref[r:r + span + T, :]

    z = {}
    conv_fill = [
        lambda: z.update(u=proj(C_BU, GMLP_WIDTH)),
        lambda: z.update(gv=proj(C_BV, GMLP_WIDTH)),
        lambda: z.update(q=proj(C_DQ, SWA_Q)),
        lambda: z.update(r=proj(C_AR, QV)),
        lambda: (gate_part(0, 0), gate_part(0, 1)),
        lambda: (gate_part(0, 2), gate_part(0, 3)),
        lambda: (gate_part(1, 0), gate_part(1, 1)),
        lambda: (gate_part(1, 2), gate_part(1, 3)),
    ]
    acc_blocks = []
    for rb in range(T // CONV_ROWS):
        acc = jnp.broadcast_to(cb_ref[...], (CONV_ROWS, CONV_WIDTH))
        for r_ in range(SUBLANES):
            for kk in taps_of[r_]:
                a = CONV_LEAD + kk - r_ + rb * CONV_ROWS
                rows_k = hbuf_ref[a:a + CONV_ROWS, :] if r_ == 0 else hsh_ref[r_ - 1, a:a + CONV_ROWS, :]
                acc = acc + cw_ref[kk:kk + 1, :] * rows_k
        acc_blocks.append(acc)
        conv_fill[rb]()
    hbuf_ref[0:CONV_PAD, :] = hbuf_ref[T:T + CONV_PAD, :]
    o_c = _silu(_ln_rows(jnp.concatenate(acc_blocks, axis=0), clg_ref[...], clb_ref[...])).astype(BF16)
    half = D_MODEL // 2
    partial = {}

    def branch_half(i, o_b16, j):
        partial[(i, j)] = _mm(o_b16, wbr_ref[i, :, j * half:(j + 1) * half])

    zu, zgv, zq, r = z["u"], z["gv"], z["q"], z["r"]

    la = _log_sigmoid(_mm(lr.astype(BF16), wal_ref[...]) + bal_ref[...]) * (1.0 / GLA_TAU)
    gate_part(2, 0)
    gate_part(2, 1)

    n_chunks = T // GLA_CHUNK
    ri = _iota((T, T), 0)
    ci = _iota((T, T), 1)
    chunk_causal = jnp.logical_and((ri // GLA_CHUNK) == (ci // GLA_CHUNK), ci <= ri)
    la_hi, la_lo = _hi_lo(la)
    m_tril = chunk_causal.astype(BF16)
    bcum = _mm(m_tril, la_hi) + _mm(m_tril, la_lo)
    tot_rows = [bcum[(c + 1) * GLA_CHUNK - 1:(c + 1) * GLA_CHUNK, :] for c in range(n_chunks)]
    btot = jnp.concatenate([jnp.broadcast_to(tr, (GLA_CHUNK, QK)) for tr in tot_rows], axis=0)
    q_in = (q * jnp.exp(bcum)).astype(BF16)
    k_out = (k * jnp.exp(-bcum)).astype(BF16)
    k_dec_t = (k * jnp.exp(btot - bcum)).T.astype(BF16)
    v_b = v.astype(BF16)
    dec_t = jnp.exp(jnp.concatenate(
        [jnp.broadcast_to(tr, (GLA_DV, QK)) for tr in tot_rows], axis=0).T)

    lane_head = _iota((T, QK), 1) // GLA_DK
    o_heads = []
    for hh in range(GLA_HEADS):
        qh = jnp.where(lane_head == hh, q_in, jnp.zeros_like(q_in))
        att = jnp.where(chunk_causal, _mm_nt(qh, k_out), 0.0).astype(BF16)
        o_heads.append(_mm(att, v_b[:, hh * GLA_DV:(hh + 1) * GLA_DV]))
    o_intra = jnp.concatenate(o_heads, axis=1)
    gate_part(2, 2)
    gate_part(2, 3)

    bd_mask = (_iota((QK, QV), 0) // GLA_DK) == (_iota((QK, QV), 1) // GLA_DV)
    tok_chunk = _iota((GLA_DK, T), 1) // GLA_CHUNK
    o_state = []
    for c in range(n_chunks):
        rs = slice(c * GLA_CHUNK, (c + 1) * GLA_CHUNK)
        s_c = gstate_ref[...]
        s_b = s_c.astype(BF16)
        s_bd = jnp.where(bd_mask, jnp.concatenate([s_b] * GLA_HEADS, axis=1), jnp.zeros((QK, QV), BF16))
        o_state.append(_mm(q_in[rs], s_bd))
        upd = []
        for hh in range(GLA_HEADS):
            kt = k_dec_t[hh * GLA_DK:(hh + 1) * GLA_DK, :]
            kt = jnp.where(tok_chunk == c, kt, jnp.zeros_like(kt))
            upd.append(_mm(kt, v_b[:, hh * GLA_DV:(hh + 1) * GLA_DV]))
        gstate_ref[...] = dec_t[:, c * GLA_DV:(c + 1) * GLA_DV] * s_c + jnp.concatenate(upd, axis=0)
    o_gla = o_intra + jnp.concatenate(o_state, axis=0)
    gate_part(3, 0)
    gate_part(3, 1)
    o_a = (_head_rms_gla(o_gla, glag_ref[...]) * _silu(r)).astype(BF16)

    u = _gelu(zu)
    gv = _ln_rows(_gelu(zgv), lng_ref[...], lnb_ref[...])
    gate_part(3, 2)
    gate_part(3, 3)
    gv_b = gv.astype(BF16)
    gw = GMLP_WIDTH // GMLP_GROUPS
    tril = _iota((GMLP_CHUNK, GMLP_CHUNK), 1) <= _iota((GMLP_CHUNK, GMLP_CHUNK), 0)
    bspt = bspt_ref[...]
    s_rows = []
    for n in range(T // GMLP_CHUNK):
        rs = slice(n * GMLP_CHUNK, (n + 1) * GMLP_CHUNK)
        cols = []
        for g in range(GMLP_GROUPS):
            w = jnp.where(tril, wsp_ref[g], 0.0).astype(BF16)
            sg = _mm(w, gv_b[rs, g * gw:(g + 1) * gw])
            cols.append(sg + jnp.broadcast_to(bspt[:, g:g + 1], (GMLP_CHUNK, gw)))
        s_rows.append(jnp.concatenate(cols, axis=1))
    o_b = (u * jnp.concatenate(s_rows, axis=0)).astype(BF16)

    rc, ra, rb = (rope_ref[:, i * LANES:(i + 1) * LANES] for i in range(3))
    qd = _qk_norm_rope(zq, qg_ref[...], rc, ra, rb, split=False) * (SWA_HEAD_DIM ** -0.5)
    kfull_ref[WINDOW:WINDOW + T, :] = _qk_norm_rope(zk, kg_ref[...], rc, ra, rb)
    vfull_ref[WINDOW:WINDOW + T, :] = zvd
    swa_fill = [functools.partial(branch_half, i, o, j) for i, o in ((2, o_c), (0, o_a)) for j in range(2)]

    rows = SWA_GROUP * WINDOW
    own_key = _iota((rows, WINDOW), 1) <= _iota((rows, WINDOW), 0) % WINDOW
    o_blocks = []
    for qb in range(T // WINDOW):
        rs = slice(qb * WINDOW, (qb + 1) * WINDOW)
        kblk = kfull_ref[qb * WINDOW:(qb + 2) * WINDOW, :].astype(BF16)
        vblk = vfull_ref[qb * WINDOW:(qb + 2) * WINDOW, :].astype(BF16)
        outs = []
        for g in range(SWA_KV_HEADS):
            q_stack = jnp.concatenate(
                [_swa_q_for_group(qd[rs], g, i) for i in range(SWA_GROUP)], axis=0).astype(BF16)
            s2 = _mm_nt(q_stack, kblk)
            s = jnp.where(own_key, s2[:, WINDOW:], s2[:, :WINDOW])
            if qb == 0:
                s = jnp.where(jnp.logical_or(own_key, t > 0), s, -jnp.inf)
            sink = _sink_column(sinks_ref, layer * SWA_HEADS + g * SWA_GROUP, rows, WINDOW)
            m = jnp.maximum(jnp.max(s, axis=-1, keepdims=True), sink)
            p = jnp.exp(s - m)
            den = jnp.sum(p, axis=-1, keepdims=True) + jnp.exp(sink - m)
            p2 = jnp.concatenate([jnp.where(own_key, 0.0, p), jnp.where(own_key, p, 0.0)], axis=1)
            o = _mm(p2.astype(BF16), vblk) / den
            outs.append([o[i * WINDOW:(i + 1) * WINDOW] for i in range(SWA_GROUP)])
            if swa_fill:
                swa_fill.pop(0)()
        o_blocks.append(_swa_assemble(outs, WINDOW))
    for fill in swa_fill:
        fill()
    o_d = jnp.concatenate(o_blocks, axis=0).astype(BF16)
    kfull_ref[0:WINDOW, :] = kfull_ref[T:T + WINDOW, :]
    vfull_ref[0:WINDOW, :] = vfull_ref[T:T + WINDOW, :]

    merged = (gates_ref[0] * jnp.concatenate([partial[(0, 0)], partial[(0, 1)]], axis=1)
              + gates_ref[2] * jnp.concatenate([partial[(2, 0)], partial[(2, 1)]], axis=1))
    merged = merged + gates_ref[1] * branch(1, o_b)
    merged = merged + gates_ref[3] * branch(3, o_d)
    y_ref[0] = x + _mm(merged.astype(BF16), wout_ref[:, :D_MODEL])

    @pl.when(t == last)
    def _():
        conv_ref[0] = hbuf_ref[CONV_LEAD:CONV_PAD, :]
        ko_ref[0] = kfull_ref[0:WINDOW, :]
        vo_ref[0] = vfull_ref[0:WINDOW, :]
        gla_ref[0] = gstate_ref[...]


def _layer_spec(arr, layer):
    nd = arr.ndim
    return pl.BlockSpec((None,) + tuple(arr.shape[1:]), lambda *_: (layer,) + (0,) * (nd - 1),
                        pipeline_mode=pl.Buffered(1))


def _full_spec(shape):
    nd = len(shape)
    return pl.BlockSpec(shape, lambda *_: (0,) * nd)


def _compiler_params(n_axes):
    return pltpu.CompilerParams(
        dimension_semantics=("arbitrary",) * n_axes, vmem_limit_bytes=VMEM_LIMIT_BYTES)


def _prompt_mixer(x, p, tables, layer):
    B, L, _ = x.shape
    T = PROMPT_TILE
    weights = tuple(p[n] for n in (
        "n1g", "win", "bin", "wal", "bal", "glag", "lng", "lnb", "wsp", "bspt",
        "cw", "cb", "clg", "clb", "qg", "kg", "wg", "bg", "wbr", "wout"))
    tab_spec = pl.BlockSpec((T, 3 * LANES), lambda b, t, *_: (t, 0))

    grid_spec = pltpu.PrefetchScalarGridSpec(
        num_scalar_prefetch=1,
        grid=(B, L // T),
        in_specs=[pl.BlockSpec((1, T, D_MODEL), lambda b, t, *_: (b, t, 0)),
                  tab_spec]
                 + [_layer_spec(w, layer) for w in weights],
        out_specs=[pl.BlockSpec((1, T, D_MODEL), lambda b, t, *_: (b, t, 0)),
                   pl.BlockSpec((1, QK, GLA_DV), lambda b, t, *_: (b, 0, 0)),
                   pl.BlockSpec((1, CONV_K - 1, CONV_WIDTH), lambda b, t, *_: (b, 0, 0)),
                   pl.BlockSpec((1, WINDOW, SWA_KV), lambda b, t, *_: (b, 0, 0)),
                   pl.BlockSpec((1, WINDOW, SWA_KV), lambda b, t, *_: (b, 0, 0))],
        scratch_shapes=[pltpu.VMEM((QK, GLA_DV), F32),
                        pltpu.VMEM((T + CONV_PAD, CONV_WIDTH), F32),
                        pltpu.VMEM((SUBLANES - 1, T + CONV_PAD, CONV_WIDTH), F32),
                        pltpu.VMEM((T + WINDOW, SWA_KV), F32),
                        pltpu.VMEM((T + WINDOW, SWA_KV), F32),
                        pltpu.VMEM((N_BRANCH, T, D_MODEL), F32)])
    return pl.pallas_call(
        functools.partial(_prompt_mixer_kernel, layer=layer),
        grid_spec=grid_spec,
        out_shape=[jax.ShapeDtypeStruct((B, L, D_MODEL), F32),
                   jax.ShapeDtypeStruct((B, QK, GLA_DV), F32),
                   jax.ShapeDtypeStruct((B, CONV_K - 1, CONV_WIDTH), F32),
                   jax.ShapeDtypeStruct((B, WINDOW, SWA_KV), F32),
                   jax.ShapeDtypeStruct((B, WINDOW, SWA_KV), F32)],
        compiler_params=_compiler_params(2),
        name="prompt_mixer",
    )(p["sinks"], x, jnp.concatenate(tables, axis=1), *weights)


def _ffn_kernel(x_ref, g_ref, wgu_ref, wd_ref, y_ref, h_ref):
    x = x_ref[...]
    xb = _rms_rows(x, g_ref[...]).astype(BF16)
    for c in range(FFN_HIDDEN // FFN_CHUNK):
        c0 = c * FFN_CHUNK
        gate = _mm(xb, wgu_ref[:, c0:c0 + FFN_CHUNK])
        up = _mm(xb, wgu_ref[:, FFN_HIDDEN + c0:FFN_HIDDEN + c0 + FFN_CHUNK])
        h_ref[:, c0:c0 + FFN_CHUNK] = (_silu(gate) * up).astype(BF16)
    y_ref[...] = x + _mm(h_ref[...], wd_ref[...])


def _ffn(x2d, p, layer):
    n = x2d.shape[0]
    tile = min(FFN_TILE, n)
    return pl.pallas_call(
        _ffn_kernel,
        grid=(n // tile,),
        in_specs=[pl.BlockSpec((tile, D_MODEL), lambda i: (i, 0)),
                  _layer_spec(p["n2g"], layer), _layer_spec(p["wgu"], layer), _layer_spec(p["wd"], layer)],
        out_specs=pl.BlockSpec((tile, D_MODEL), lambda i: (i, 0)),
        out_shape=jax.ShapeDtypeStruct((n, D_MODEL), F32),
        scratch_shapes=[pltpu.VMEM((tile, FFN_HIDDEN), BF16)],
        compiler_params=_compiler_params(1),
        name="swiglu",
    )(x2d, p["n2g"], p["wgu"], p["wd"])


def _sample_proj_kernel(x_ref, n1g_ref, win_ref, bin_ref, wal_ref, bal_ref, z_ref, ya_ref):
    xb = _rms_rows(x_ref[...], n1g_ref[...]).astype(BF16)
    z = _mm(xb, win_ref[...]) + bin_ref[...]
    z_ref[...] = z
    ya_ref[...] = _mm(z[:, C_LR:C_LR + LANES].astype(BF16), wal_ref[...]) + bal_ref[...]


def _sample_proj(x2d, p, layer):
    n = x2d.shape[0]
    weights = tuple(p[k] for k in ("n1g", "win", "bin", "wal", "bal"))
    return pl.pallas_call(
        _sample_proj_kernel,
        grid=(1,),
        in_specs=[_full_spec(x2d.shape)] + [_layer_spec(w, layer) for w in weights],
        out_specs=[_full_spec((n, IN_COLS_PACKED)), _full_spec((n, QK))],
        out_shape=[jax.ShapeDtypeStruct((n, IN_COLS_PACKED), F32),
                   jax.ShapeDtypeStruct((n, QK), F32)],
        compiler_params=_compiler_params(1),
        name="sample_proj",
    )(x2d, *weights)


def _sample_sequence(j, layer, sinks_ref, z_ref, ya_ref, s0_ref, hist_ref, kc_ref, vc_ref,
                     rc, ra, rb, glag, lng, lnb, wsc_ref, bsc, wsh_ref, cb, clg, clb, qg, kg,
                     obr_ref, s1_ref, convo_ref, ko_ref, vo_ref, gvo_ref, hext_ref):
    R = SAMPLE_ROWS
    nt = SAMPLE_TOKENS
    row = _iota((R, 1), 0)

    def seq_rows(ref):
        x = ref[(j // 2) * R:(j // 2 + 1) * R, :]
        if j % 2:
            x = pltpu.roll(x, R - nt, 0)
        return jnp.where(row < nt, x, 0.0)

    z = seq_rows(z_ref)
    head_ones = _group_ones(SWA_Q, SWA_HEAD_DIM)

    q = z[:, C_AQ:C_AQ + QK] * (GLA_DK ** -0.5)
    k = z[:, C_AK:C_AK + QK]
    v = z[:, C_AV:C_AV + QV]
    r = z[:, C_AR:C_AR + QV]
    zq = z[:, C_DQ:C_DQ + SWA_Q]
    zk = z[:, C_DK:C_DK + SWA_KV]
    zv = z[:, C_DV:C_DV + SWA_KV]
    ms_q = _mm_split_lhs(zq * zq, head_ones) * (1.0 / SWA_HEAD_DIM)
    ms_k = _mm_split_lhs(zk * zk, head_ones[:SWA_KV, :SWA_KV]) * (1.0 / SWA_HEAD_DIM)
    la = jnp.where(row < nt, _log_sigmoid(seq_rows(ya_ref)) * (1.0 / GLA_TAU), 0.0)
    bcum = jnp.zeros_like(la)
    for s in range(nt):
        bcum = bcum + jnp.where(row >= s, jnp.broadcast_to(la[s:s + 1, :], la.shape), 0.0)
    btot_row = bcum[nt - 1:nt, :]
    q_in = q * jnp.exp(bcum)
    k_out = k * jnp.exp(-bcum)
    k_dec = k * jnp.exp(jnp.broadcast_to(btot_row, la.shape) - bcum)
    s0 = s0_ref[j]
    s0_b = s0.astype(BF16)
    head_sel = (_iota((QK, LANES), 0) // GLA_DK == _iota((QK, LANES), 1)).astype(BF16)
    head_exp = (_iota((LANES, QV), 0) == _iota((LANES, QV), 1) // GLA_DV).astype(BF16)
    lane_head = _iota((R, QK), 1) // GLA_DK
    att_s = [_mm_split_lhs(q_in * jnp.broadcast_to(k_out[s:s + 1, :], q_in.shape), head_sel)
             for s in range(nt)]
    o_state = [_mm(jnp.where(lane_head == hh, q_in, 0.0).astype(BF16), s0_b) for hh in range(GLA_HEADS)]
    upd = _mm_tn(k_dec.astype(BF16).astype(F32), v.astype(BF16).astype(F32))
    dcol = jnp.exp(jnp.broadcast_to(btot_row, (GLA_DV, QK)).T)

    u = _gelu(z[:, C_BU:C_BU + GMLP_WIDTH])
    gv = _ln_rows(_gelu(z[:, C_BV:C_BV + GMLP_WIDTH]), lng, lnb)
    gvo_ref[j * nt:(j + 1) * nt, :] = gv[0:nt, :]
    sp = bsc
    for s in range(nt):
        sp = sp + jnp.where(row >= s, wsc_ref[s], 0.0) * jnp.broadcast_to(gv[s:s + 1, :], gv.shape)
    o_b = u * sp

    h = z[:, C_CA:C_CA + CONV_WIDTH] * jax.nn.sigmoid(z[:, C_CG:C_CG + CONV_WIDTH])
    hext_ref[j, 0:SUBLANES, :] = jnp.zeros((SUBLANES, CONV_WIDTH), F32)
    hext_ref[j, CONV_LEAD:CONV_PAD, :] = hist_ref[j]
    hext_ref[j, CONV_PAD:CONV_PAD + R, :] = h
    yield

    hext = hext_ref[j]
    conv = jnp.zeros((R, CONV_WIDTH), F32)
    for tt in range(nt):
        c_t = jnp.sum(hext * wsh_ref[tt], axis=0, keepdims=True)
        conv = jnp.where(row == tt, jnp.broadcast_to(c_t, conv.shape), conv)
    o_c = _silu(_ln_rows(conv + cb, clg, clb))
    convo_ref[j] = hext_ref[j, CONV_PAD + nt - (CONV_K - 1):CONV_PAD + nt, :]

    att_e = [_mm(a.astype(BF16), head_exp) for a in att_s]
    s1_ref[j] = dcol * s0 + jnp.concatenate(
        [upd[hh * GLA_DK:(hh + 1) * GLA_DK, hh * GLA_DV:(hh + 1) * GLA_DV] for hh in range(GLA_HEADS)],
        axis=0)

    qd = _rope(zq * lax.rsqrt(ms_q + NORM_EPS) * qg, rc, ra, rb)
    k_new = _rope(zk * lax.rsqrt(ms_k + NORM_EPS) * kg, rc, ra, rb)
    k_hist_b = kc_ref[j].astype(BF16)
    v_hist_b = vc_ref[j].astype(BF16)
    rows = SWA_GROUP * R
    qt = _iota((rows, WINDOW), 0) % R
    hist_ok = _iota((rows, WINDOW), 1) > qt
    qt1 = _iota((rows, 1), 0) % R
    scale = SWA_HEAD_DIM ** -0.5
    q_stacks, s_hists, s_news = [], [], []
    for g in range(SWA_KV_HEADS):
        q_stack = jnp.concatenate([_swa_q_for_group(qd, g, i) for i in range(SWA_GROUP)], axis=0)
        q_stacks.append(q_stack)
        s_hists.append(_mm_nt(q_stack.astype(BF16), k_hist_b))
        s_news.append([jnp.sum(q_stack * jnp.broadcast_to(k_new[s:s + 1, :], q_stack.shape),
                               axis=-1, keepdims=True) for s in range(nt)])
    yield

    o_intra = jnp.zeros((R, QV), F32)
    for s in range(nt):
        o_intra = o_intra + jnp.where(row >= s, att_e[s], 0.0) * jnp.broadcast_to(v[s:s + 1, :], (R, QV))
    o_a = _head_rms_gla(o_intra + jnp.concatenate(o_state, axis=1), glag) * _silu(r)

    sinks, ms, s_masked, n_masked = [], [], [], []
    for g in range(SWA_KV_HEADS):
        s_hist = jnp.where(hist_ok, s_hists[g] * scale, -jnp.inf)
        s_new = [jnp.where(qt1 >= s, s_news[g][s] * scale, -jnp.inf) for s in range(nt)]
        sink = _sink_column(sinks_ref, layer * SWA_HEADS + g * SWA_GROUP, rows, R)
        m = jnp.maximum(jnp.max(s_hist, axis=-1, keepdims=True), sink)
        for s in range(nt):
            m = jnp.maximum(m, s_new[s])
        sinks.append(sink)
        ms.append(m)
        s_masked.append(s_hist)
        n_masked.append(s_new)
    yield

    dens, pvs, p_news = [], [], []
    for g in range(SWA_KV_HEADS):
        p_hist = jnp.exp(s_masked[g] - ms[g])
        dens.append(jnp.sum(p_hist, axis=-1, keepdims=True) + jnp.exp(sinks[g] - ms[g]))
        pvs.append(_mm(p_hist.astype(BF16), v_hist_b))
        p_news.append([jnp.exp(n_masked[g][s] - ms[g]) for s in range(nt)])
    yield

    outs = []
    for g in range(SWA_KV_HEADS):
        o, den = pvs[g], dens[g]
        for s in range(nt):
            den = den + p_news[g][s]
            o = o + p_news[g][s] * jnp.broadcast_to(zv[s:s + 1, :], o.shape)
        o = o / den
        outs.append([o[i * R:(i + 1) * R] for i in range(SWA_GROUP)])
    o_d = _swa_assemble(outs, R)

    ko_ref[j, 0:WINDOW - nt, :] = kc_ref[j, nt:WINDOW, :]
    ko_ref[j, WINDOW - nt:WINDOW, :] = k_new[0:nt, :]
    vo_ref[j, 0:WINDOW - nt, :] = vc_ref[j, nt:WINDOW, :]
    vo_ref[j, WINDOW - nt:WINDOW, :] = zv[0:nt, :]

    obr_ref[j * nt:(j + 1) * nt, :] = jnp.concatenate([o_a, o_b, o_c, o_d], axis=1)[0:nt, :]


def _sample_mixer_kernel(
        sinks_ref,
        z_ref, ya_ref, s0_ref, hist_ref, kc_ref, vc_ref,
        rc_ref, ra_ref, rb_ref,
        glag_ref, lng_ref, lnb_ref, wsc_ref, bsc_ref,
        wsh_ref, cb_ref, clg_ref, clb_ref, qg_ref, kg_ref,
        s1_all_ref, convo_all_ref, ko_all_ref, vo_all_ref,
        obr_ref, s1_ref, convo_ref, ko_ref, vo_ref, gvo_ref,
        hext_ref, *, layer):
    del s1_all_ref, convo_all_ref, ko_all_ref, vo_all_ref
    consts = (rc_ref[...], ra_ref[...], rb_ref[...], glag_ref[...], lng_ref[...], lnb_ref[...])
    sequences = [
        _sample_sequence(
            j, layer, sinks_ref, z_ref, ya_ref, s0_ref, hist_ref, kc_ref, vc_ref,
            *consts, wsc_ref, bsc_ref[...], wsh_ref, cb_ref[...], clg_ref[...], clb_ref[...],
            qg_ref[...], kg_ref[...],
            obr_ref, s1_ref, convo_ref, ko_ref, vo_ref, gvo_ref, hext_ref)
        for j in range(SAMPLE_GROUP)]
    while sequences:
        sequences = [seq for seq in sequences if next(seq, StopIteration) is not StopIteration]


def _sample_mixer(z, ya, s0, hist, kc, vc, p, tables, stacked_states, layer):
    nt = SAMPLE_TOKENS
    nb = z.shape[0] // nt
    R = SAMPLE_ROWS
    G = SAMPLE_GROUP
    weights = tuple(p[n] for n in ("glag", "lng", "lnb", "wsc", "bsc", "wsh", "cb", "clg", "clb", "qg", "kg"))

    def per_group(shape):
        return pl.BlockSpec((G * nt, shape[1]), lambda b, *_: (b, 0))

    def per_group_of_layer(arr):
        nd = arr.ndim
        return pl.BlockSpec((None, G) + tuple(arr.shape[2:]), lambda b, *_: (layer, b) + (0,) * (nd - 2))

    def layer_block(arr):
        nd = arr.ndim
        return pl.BlockSpec((None,) + tuple(arr.shape[1:]), lambda b, *_: (layer,) + (0,) * (nd - 1))

    out_shapes = [jax.ShapeDtypeStruct((nb * nt, N_BRANCH * BRANCH_WIDTH), F32)] \
        + [jax.ShapeDtypeStruct(a.shape, F32) for a in stacked_states] \
        + [jax.ShapeDtypeStruct((nb * nt, GMLP_WIDTH), F32)]
    operands = (z, ya, s0, hist, kc, vc, *tables, *weights)
    n_in = 1 + len(operands)
    grid_spec = pltpu.PrefetchScalarGridSpec(
        num_scalar_prefetch=1,
        grid=(nb // G,),
        in_specs=[per_group(a.shape) for a in (z, ya)]
                 + [per_group_of_layer(a) for a in (s0, hist, kc, vc)]
                 + [_full_spec(tb.shape) for tb in tables]
                 + [layer_block(w) for w in weights]
                 + [pl.BlockSpec(memory_space=pl.ANY) for _ in stacked_states],
        out_specs=[per_group(out_shapes[0].shape)]
                  + [per_group_of_layer(a) for a in stacked_states]
                  + [per_group(out_shapes[-1].shape)],
        scratch_shapes=[pltpu.VMEM((G, CONV_PAD + R, CONV_WIDTH), F32)])
    return pl.pallas_call(
        functools.partial(_sample_mixer_kernel, layer=layer),
        grid_spec=grid_spec,
        out_shape=out_shapes,
        input_output_aliases={n_in + i: 1 + i for i in range(len(stacked_states))},
        compiler_params=_compiler_params(1),
        name="sample_mixer",
    )(p["sinks"], *operands, *stacked_states)


def _sample_merge_kernel(x_ref, obr_ref, n1g_ref, wg_ref, bg_ref, wbr_ref, wout_ref, y_ref):
    x = x_ref[...]
    xb = _rms_rows(x, n1g_ref[...]).astype(BF16)
    merged = jnp.zeros(x.shape, F32)
    for i in range(N_BRANCH):
        gate = jax.nn.sigmoid(
            _mm(xb, wg_ref[:, i * D_MODEL:(i + 1) * D_MODEL]) + bg_ref[:, i * D_MODEL:(i + 1) * D_MODEL])
        o_i = obr_ref[:, i * BRANCH_WIDTH:(i + 1) * BRANCH_WIDTH].astype(BF16)
        merged = merged + gate * _mm(o_i, wbr_ref[i, :, :D_MODEL])
    y_ref[...] = x + _mm(merged.astype(BF16), wout_ref[:, :D_MODEL])


def _sample_merge(x2d, obr, p, layer):
    weights = tuple(p[k] for k in ("n1g", "wg", "bg", "wbr", "wout"))
    return pl.pallas_call(
        _sample_merge_kernel,
        grid=(1,),
        in_specs=[_full_spec(x2d.shape), _full_spec(obr.shape)] + [_layer_spec(w, layer) for w in weights],
        out_specs=_full_spec(x2d.shape),
        out_shape=jax.ShapeDtypeStruct(x2d.shape, F32),
        compiler_params=_compiler_params(1),
        name="sample_merge",
    )(x2d, obr, *weights)


def _rope_tables(pos):
    half = ROPE_DIM // 2
    inv = jnp.exp(-math.log(ROPE_THETA) * jnp.arange(half, dtype=F32) * (2.0 / ROPE_DIM))
    ang = pos.astype(F32)[:, None] * inv[None, :]
    cos, sin = jnp.cos(ang), jnp.sin(ang)
    n = pos.shape[0]
    pad = jnp.zeros((n, SWA_HEAD_DIM - ROPE_DIM), F32)
    zero = jnp.zeros((n, half), F32)
    c_head = jnp.concatenate([cos, cos, pad + 1.0], axis=1)
    a_head = jnp.concatenate([-sin, zero, pad], axis=1)
    b_head = jnp.concatenate([zero, sin, pad], axis=1)
    rep = LANES // SWA_HEAD_DIM
    return tuple(jnp.tile(tb, (1, rep)) for tb in (c_head, a_head, b_head))


def _lane_pad(w):
    return jnp.pad(w, [(0, 0)] * (w.ndim - 1) + [(0, LANES)]).astype(BF16)


def _rows(v):
    return v.reshape(v.shape[0], 1, -1).astype(F32)


def _stacked_params(norm1_g, w_in, b_in, w_alpha2, b_alpha, gla_norm_g, gmlp_ln_g, gmlp_ln_b,
                    w_spatial, b_spatial, conv_w, conv_b, conv_ln_g, conv_ln_b, q_norm_g, k_norm_g,
                    sinks, w_gate, b_gate, w_branch, w_out, norm2_g, w_gate_up, w_down):
    o_lr = 2 * QK + 2 * QV
    o_rest = o_lr + GLA_RANK

    def repack(m, dtype):
        pad = jnp.zeros(m.shape[:-1] + (LANES - GLA_RANK,), m.dtype)
        return jnp.concatenate(
            [m[..., :o_lr], m[..., o_rest:], m[..., o_lr:o_rest], pad], axis=-1).astype(dtype)

    nt = SAMPLE_TOKENS
    gw = GMLP_WIDTH // GMLP_GROUPS
    pad_rows = ((0, 0), (0, SAMPLE_ROWS - nt), (0, 0))
    wsc = jnp.stack([
        jnp.pad(jnp.repeat(jnp.swapaxes(w_spatial[:, :, :nt, s], 1, 2), gw, axis=2), pad_rows)
        for s in range(nt)], axis=1)
    bsc = jnp.pad(jnp.repeat(jnp.swapaxes(b_spatial[:, :, :nt], 1, 2), gw, axis=2), pad_rows)
    wsh = jnp.stack([
        jnp.pad(conv_w, ((0, 0), (CONV_LEAD + t, CONV_PAD + SAMPLE_ROWS - CONV_LEAD - t - CONV_K), (0, 0)))
        for t in range(nt)], axis=1)
    return {
        "n1g": _rows(norm1_g),
        "win": repack(w_in, BF16),
        "bin": repack(b_in, F32).reshape(DEPTH, 1, -1),
        "wal": jnp.pad(w_alpha2, ((0, 0), (0, LANES - GLA_RANK), (0, 0))).astype(BF16),
        "bal": _rows(b_alpha),
        "glag": _rows(gla_norm_g),
        "lng": _rows(gmlp_ln_g),
        "lnb": _rows(gmlp_ln_b),
        "wsp": w_spatial.astype(F32),
        "bspt": jnp.swapaxes(b_spatial, 1, 2).astype(F32),
        "wsc": wsc.astype(F32),
        "bsc": bsc.astype(F32),
        "cw": jnp.pad(conv_w, ((0, 0), (0, CONV_PAD - CONV_K), (0, 0))).astype(F32),
        "wsh": wsh.astype(F32),
        "cb": _rows(conv_b),
        "clg": _rows(conv_ln_g),
        "clb": _rows(conv_ln_b),
        "qg": _rows(jnp.tile(q_norm_g, (1, SWA_HEADS))),
        "kg": _rows(jnp.tile(k_norm_g, (1, SWA_KV_HEADS))),
        "sinks": sinks.reshape(-1).astype(F32),
        "wg": _lane_pad(w_gate),
        "bg": _rows(b_gate),
        "wbr": _lane_pad(w_branch),
        "wout": _lane_pad(w_out),
        "n2g": _rows(norm2_g),
        "wgu": w_gate_up.astype(BF16),
        "wd": w_down.astype(BF16),
    }


def kernel(x_prompt, x_sample, state_gla, state_conv, cache_swa_k, cache_swa_v, norm1_g, w_in, b_in, w_alpha2, b_alpha, gla_norm_g, gmlp_ln_g, gmlp_ln_b, w_spatial, b_spatial, conv_w, conv_b, conv_ln_g, conv_ln_b, q_norm_g, k_norm_g, sinks, w_gate, b_gate, w_branch, w_out, norm2_g, w_gate_up, w_down):
    bp, lp, _ = x_prompt.shape
    bs, ls, _ = x_sample.shape
    R = SAMPLE_ROWS
    prompt_tables = _rope_tables(jnp.arange(lp))
    sample_tables = tuple(
        jnp.pad(tb, ((0, R - ls), (0, 0))) for tb in _rope_tables(PAST_LEN + jnp.arange(ls)))
    p = _stacked_params(norm1_g, w_in, b_in, w_alpha2, b_alpha, gla_norm_g, gmlp_ln_g, gmlp_ln_b,
                        w_spatial, b_spatial, conv_w, conv_b, conv_ln_g, conv_ln_b, q_norm_g,
                        k_norm_g, sinks, w_gate, b_gate, w_branch, w_out, norm2_g, w_gate_up, w_down)

    gla_in = state_gla.reshape(DEPTH, bs, QK, GLA_DV)
    swa_k_in = cache_swa_k.reshape(DEPTH, bs, WINDOW, SWA_KV)
    swa_v_in = cache_swa_v.reshape(DEPTH, bs, WINDOW, SWA_KV)

    sample_states = [jnp.zeros(a.shape, F32) for a in (gla_in, state_conv, swa_k_in, swa_v_in)]

    yp = x_prompt
    ys = x_sample.reshape(bs * ls, D_MODEL)
    outs = {k: [] for k in ("gla_p", "conv_p", "kp", "vp", "gm")}
    for l in range(DEPTH):
        yp, g1, c1, k1, v1 = _prompt_mixer(yp, p, prompt_tables, l)
        yp = _ffn(yp.reshape(bp * lp, D_MODEL), p, l).reshape(bp, lp, D_MODEL)
        outs["gla_p"].append(g1.reshape(bp, GLA_HEADS, GLA_DK, GLA_DV))
        outs["conv_p"].append(c1)
        outs["kp"].append(k1.reshape(bp, WINDOW, SWA_KV_HEADS, SWA_HEAD_DIM))
        outs["vp"].append(v1.reshape(bp, WINDOW, SWA_KV_HEADS, SWA_HEAD_DIM))
        z, ya = _sample_proj(ys, p, l)
        obr, *sample_states, gv2 = _sample_mixer(
            z, ya, gla_in, state_conv, swa_k_in, swa_v_in, p, sample_tables, sample_states, l)
        ys = _sample_merge(ys, obr, p, l)
        ys = _ffn(ys, p, l)
        outs["gm"].append(gv2.reshape(bs, ls, GMLP_WIDTH))
    st = lambda name: jnp.stack(outs[name], 0)
    gla_s, conv_s, ks, vs = sample_states
    kv_shape = (DEPTH, bs, WINDOW, SWA_KV_HEADS, SWA_HEAD_DIM)
    return (yp, ys.reshape(bs, ls, D_MODEL), st("gla_p"), gla_s.reshape(state_gla.shape), st("conv_p"),
            conv_s, st("kp"), st("vp"), ks.reshape(kv_shape), vs.reshape(kv_shape), st("gm"))
```

```python
import functools
import math

import jax
import jax.numpy as jnp
from jax import lax
from jax.experimental import pallas as pl
from jax.experimental.pallas import tpu as pltpu

F32 = jnp.float32
BF16 = jnp.bfloat16

D_MODEL = 1024
DEPTH = 2
PAST_LEN = 16384
GLA_HEADS = 4
GLA_DK = 64
GLA_DV = 128
GLA_RANK = 16
GLA_TAU = 16.0
GLA_CHUNK = 64
GMLP_GROUPS = 4
GMLP_WIDTH = 512
GMLP_CHUNK = 128
CONV_WIDTH = 512
CONV_K = 31
SWA_HEADS = 8
SWA_KV_HEADS = 2
SWA_GROUP = SWA_HEADS // SWA_KV_HEADS
SWA_HEAD_DIM = 64
WINDOW = 128
ROPE_DIM = SWA_HEAD_DIM // 4
ROPE_THETA = 500000.0
N_BRANCH = 4
BRANCH_WIDTH = 512
FFN_HIDDEN = 2816
NORM_EPS = 1e-6
LN_EPS = 1e-5

LANES = 128
SUBLANES = 8
VMEM_LIMIT_BYTES = 56 * 1024 * 1024

QK = GLA_HEADS * GLA_DK
QV = GLA_HEADS * GLA_DV
SWA_Q = SWA_HEADS * SWA_HEAD_DIM
SWA_KV = SWA_KV_HEADS * SWA_HEAD_DIM

C_AQ = 0
C_AK = C_AQ + QK
C_AV = C_AK + QK
C_AR = C_AV + QV
C_BU = C_AR + QV
C_BV = C_BU + GMLP_WIDTH
C_CA = C_BV + GMLP_WIDTH
C_CG = C_CA + CONV_WIDTH
C_DQ = C_CG + CONV_WIDTH
C_DK = C_DQ + SWA_Q
C_DV = C_DK + SWA_KV
C_LR = C_DV + SWA_KV
IN_COLS_PACKED = C_LR + LANES

PROMPT_TILE = 256
FFN_TILE = 1024
FFN_CHUNK = 256
GATE_PART = 256
CONV_PAD = 32
CONV_LEAD = CONV_PAD - (CONV_K - 1)
CONV_ROWS = 32
SAMPLE_ROWS = 8
SAMPLE_TOKENS = 4
SAMPLE_GROUP = 8


def _mm(a, b):
    return jnp.dot(a, b, preferred_element_type=F32)


def _mm_nt(a, b):
    return lax.dot_general(a, b, (((1,), (1,)), ((), ())), preferred_element_type=F32)


def _mm_tn(a, b):
    return lax.dot_general(a, b, (((0,), (0,)), ((), ())), preferred_element_type=F32)


def _hi_lo(a):
    hi = a.astype(BF16)
    lo = (a - hi.astype(F32)).astype(BF16)
    return hi, lo


def _mm_split_lhs(a_f32, m_bf16):
    hi, lo = _hi_lo(a_f32)
    return _mm(hi, m_bf16) + _mm(lo, m_bf16)


def _rms_rows(x, g):
    return x * lax.rsqrt(jnp.mean(x * x, axis=-1, keepdims=True) + NORM_EPS) * g


def _ln_rows(x, g, b):
    mu = jnp.mean(x, axis=-1, keepdims=True)
    xc = x - mu
    var = jnp.mean(xc * xc, axis=-1, keepdims=True)
    return xc * lax.rsqrt(var + LN_EPS) * g + b


def _gelu(x):
    return 0.5 * x * (1.0 + lax.erf(x * (1.0 / math.sqrt(2.0))))


def _silu(x):
    return x * jax.nn.sigmoid(x)


def _log_sigmoid(x):
    return jnp.minimum(x, 0.0) - jnp.log(1.0 + jnp.exp(-jnp.abs(x)))


def _iota(shape, dim):
    return lax.broadcasted_iota(jnp.int32, shape, dim)


def _group_ones(n, group, dtype=BF16):
    return (_iota((n, n), 0) // group == _iota((n, n), 1) // group).astype(dtype)


def _group_mean_sq(x, group):
    return _mm_split_lhs(x * x, _group_ones(x.shape[-1], group)) * (1.0 / group)


def _rope_slab(x, c, a, b):
    return x * c + pltpu.roll(x, LANES - ROPE_DIM // 2, 1) * a + pltpu.roll(x, ROPE_DIM // 2, 1) * b


def _rope(x, c, a, b):
    n = x.shape[-1] // LANES
    if n == 1:
        return _rope_slab(x, c, a, b)
    return jnp.concatenate(
        [_rope_slab(x[:, i * LANES:(i + 1) * LANES], c, a, b) for i in range(n)], axis=1)


def _qk_norm_rope(z, g, c, a, b, split=True):
    if split:
        ms = _group_mean_sq(z, SWA_HEAD_DIM)
    else:
        ms = _mm((z * z).astype(BF16), _group_ones(z.shape[-1], SWA_HEAD_DIM)) * (1.0 / SWA_HEAD_DIM)
    return _rope(z * lax.rsqrt(ms + NORM_EPS) * g, c, a, b)


def _head_rms_gla(o, g):
    return jnp.concatenate(
        [_rms_rows(o[:, h * GLA_DV:(h + 1) * GLA_DV], g) for h in range(GLA_HEADS)], axis=1)


def _swa_q_for_group(q, g, i):
    hq = g * SWA_GROUP + i
    slab = q[:, (hq // 2) * LANES:(hq // 2 + 1) * LANES]
    if hq % 2 != g:
        slab = pltpu.roll(slab, SWA_HEAD_DIM, 1)
    lane_head = _iota(slab.shape, 1) // SWA_HEAD_DIM
    return jnp.where(lane_head == g, slab, 0.0)


def _swa_assemble(outs, rows):
    lane_half = _iota((rows, LANES), 1) // SWA_HEAD_DIM
    slabs = []
    for s in range(SWA_HEADS // 2):
        pair = []
        for hq in (2 * s, 2 * s + 1):
            g, i = hq // SWA_GROUP, hq % SWA_GROUP
            o = outs[g][i]
            if hq % 2 != g:
                o = pltpu.roll(o, SWA_HEAD_DIM, 1)
            pair.append(o)
        slabs.append(jnp.where(lane_half == 0, pair[0], pair[1]))
    return jnp.concatenate(slabs, axis=1)


def _sink_column(sinks_ref, base, rows, rows_per_head):
    row_head = _iota((rows, 1), 0) // rows_per_head
    sink = jnp.zeros((rows, 1), F32)
    for i in range(SWA_GROUP):
        sink = jnp.where(row_head == i, sinks_ref[base + i], sink)
    return sink


def _prompt_mixer_kernel(
        sinks_ref,
        x_ref, rope_ref,
        n1g_ref, win_ref, bin_ref, wal_ref, bal_ref, glag_ref,
        lng_ref, lnb_ref, wsp_ref, bspt_ref,
        cw_ref, cb_ref, clg_ref, clb_ref,
        qg_ref, kg_ref,
        wg_ref, bg_ref, wbr_ref, wout_ref,
        y_ref, gla_ref, conv_ref, ko_ref, vo_ref,
        gstate_ref, hbuf_ref, hsh_ref, kfull_ref, vfull_ref, gates_ref,
        *, layer):
    T = PROMPT_TILE
    t = pl.program_id(1)
    last = pl.num_programs(1) - 1

    @pl.when(t == 0)
    def _():
        gstate_ref[...] = jnp.zeros_like(gstate_ref)
        hbuf_ref[0:CONV_PAD, :] = jnp.zeros((CONV_PAD, CONV_WIDTH), F32)
        kfull_ref[0:WINDOW, :] = jnp.zeros((WINDOW, SWA_KV), F32)
        vfull_ref[0:WINDOW, :] = jnp.zeros((WINDOW, SWA_KV), F32)

    x = x_ref[0]
    xb = _rms_rows(x, n1g_ref[...]).astype(BF16)

    def proj(c0, width):
        return _mm(xb, win_ref[:, c0:c0 + width]) + bin_ref[:, c0:c0 + width]

    def gate_part(i, j):
        c0 = i * D_MODEL + j * GATE_PART
        gates_ref[i, :, j * GATE_PART:(j + 1) * GATE_PART] = jax.nn.sigmoid(
            _mm(xb, wg_ref[:, c0:c0 + GATE_PART]) + bg_ref[:, c0:c0 + GATE_PART])

    def branch(i, o_b16):
        return _mm(o_b16, wbr_ref[i, :, :D_MODEL])


    h = proj(C_CA, CONV_WIDTH) * jax.nn.sigmoid(proj(C_CG, CONV_WIDTH))
    q = proj(C_AQ, QK) * (GLA_DK ** -0.5)
    k = proj(C_AK, QK)
    v = proj(C_AV, QV)
    kvl = proj(C_DK, 2 * SWA_KV + LANES)
    zk, zvd, lr = kvl[:, :SWA_KV], kvl[:, SWA_KV:2 * SWA_KV], kvl[:, 2 * SWA_KV:]
    hbuf_ref[CONV_PAD:CONV_PAD + T, :] = h
    taps_of = {r: [kk for kk in range(CONV_K) if (CONV_LEAD + kk) % SUBLANES == r] for r in range(SUBLANES)}
    for r in range(1, SUBLANES):
        span = max(CONV_LEAD + kk for kk in taps_of[r]) - r
        hsh_ref[r - 1, 0:span + T, :] = hbuf_ref[r:r + span + T, :]

    z = {}
    conv_fill = [
        lambda: z.update(u=proj(C_BU, GMLP_WIDTH)),
        lambda: z.update(gv=proj(C_BV, GMLP_WIDTH)),
        lambda: z.update(q=proj(C_DQ, SWA_Q)),
        lambda: z.update(r=proj(C_AR, QV)),
        lambda: (gate_part(0, 0), gate_part(0, 1)),
        lambda: (gate_part(0, 2), gate_part(0, 3)),
        lambda: (gate_part(1, 0), gate_part(1, 1)),
        lambda: (gate_part(1, 2), gate_part(1, 3)),
    ]
    acc_blocks = []
    for rb in range(T // CONV_ROWS):
        acc = jnp.broadcast_to(cb_ref[...], (CONV_ROWS, CONV_WIDTH))
        for r_ in range(SUBLANES):
            for kk in taps_of[r_]:
                a = CONV_LEAD + kk - r_ + rb * CONV_ROWS
                rows_k = hbuf_ref[a:a + CONV_ROWS, :] if r_ == 0 else hsh_ref[r_ - 1, a:a + CONV_ROWS, :]
                acc = acc + cw_ref[kk:kk + 1, :] * rows_k
        acc_blocks.append(acc)
        conv_fill[rb]()
    hbuf_ref[0:CONV_PAD, :] = hbuf_ref[T:T + CONV_PAD, :]
    o_c = _silu(_ln_rows(jnp.concatenate(acc_blocks, axis=0), clg_ref[...], clb_ref[...])).astype(BF16)
    half = D_MODEL // 2
    partial = {}

    def branch_half(i, o_b16, j):
        partial[(i, j)] = _mm(o_b16, wbr_ref[i, :, j * half:(j + 1) * half])

    zu, zgv, zq, r = z["u"], z["gv"], z["q"], z["r"]

    la = _log_sigmoid(_mm(lr.astype(BF16), wal_ref[...]) + bal_ref[...]) * (1.0 / GLA_TAU)
    gate_part(2, 0)
    gate_part(2, 1)

    n_chunks = T // GLA_CHUNK
    ri = _iota((T, T), 0)
    ci = _iota((T, T), 1)
    chunk_causal = jnp.logical_and((ri // GLA_CHUNK) == (ci // GLA_CHUNK), ci <= ri)
    la_hi, la_lo = _hi_lo(la)
    m_tril = chunk_causal.astype(BF16)
    bcum = _mm(m_tril, la_hi) + _mm(m_tril, la_lo)
    tot_rows = [bcum[(c + 1) * GLA_CHUNK - 1:(c + 1) * GLA_CHUNK, :] for c in range(n_chunks)]
    btot = jnp.concatenate([jnp.broadcast_to(tr, (GLA_CHUNK, QK)) for tr in tot_rows], axis=0)
    q_in = (q * jnp.exp(bcum)).astype(BF16)
    k_out = (k * jnp.exp(-bcum)).astype(BF16)
    k_dec_t = (k * jnp.exp(btot - bcum)).T.astype(BF16)
    v_b = v.astype(BF16)
    dec_t = jnp.exp(jnp.concatenate(
        [jnp.broadcast_to(tr, (GLA_DV, QK)) for tr in tot_rows], axis=0).T)

    lane_head = _iota((T, QK), 1) // GLA_DK
    o_heads = []
    for hh in range(GLA_HEADS):
        qh = jnp.where(lane_head == hh, q_in, jnp.zeros_like(q_in))
        att = jnp.where(chunk_causal, _mm_nt(qh, k_out), 0.0).astype(BF16)
        o_heads.append(_mm(att, v_b[:, hh * GLA_DV:(hh + 1) * GLA_DV]))
    o_intra = jnp.concatenate(o_heads, axis=1)
    gate_part(2, 2)
    gate_part(2, 3)

    bd_mask = (_iota((QK, QV), 0) // GLA_DK) == (_iota((QK, QV), 1) // GLA_DV)
    tok_chunk = _iota((GLA_DK, T), 1) // GLA_CHUNK
    o_state = []
    for c in range(n_chunks):
        rs = slice(c * GLA_CHUNK, (c + 1) * GLA_CHUNK)
        s_c = gstate_ref[...]
        s_b = s_c.astype(BF16)
        s_bd = jnp.where(bd_mask, jnp.concatenate([s_b] * GLA_HEADS, axis=1), jnp.zeros((QK, QV), BF16))
        o_state.append(_mm(q_in[rs], s_bd))
        upd = []
        for hh in range(GLA_HEADS):
            kt = k_dec_t[hh * GLA_DK:(hh + 1) * GLA_DK, :]
            kt = jnp.where(tok_chunk == c, kt, jnp.zeros_like(kt))
            upd.append(_mm(kt, v_b[:, hh * GLA_DV:(hh + 1) * GLA_DV]))
        gstate_ref[...] = dec_t[:, c * GLA_DV:(c + 1) * GLA_DV] * s_c + jnp.concatenate(upd, axis=0)
    o_gla = o_intra + jnp.concatenate(o_state, axis=0)
    gate_part(3, 0)
    gate_part(3, 1)
    o_a = (_head_rms_gla(o_gla, glag_ref[...]) * _silu(r)).astype(BF16)

    u = _gelu(zu)
    gv = _ln_rows(_gelu(zgv), lng_ref[...], lnb_ref[...])
    gate_part(3, 2)
    gate_part(3, 3)
    gv_b = gv.astype(BF16)
    gw = GMLP_WIDTH // GMLP_GROUPS
    tril = _iota((GMLP_CHUNK, GMLP_CHUNK), 1) <= _iota((GMLP_CHUNK, GMLP_CHUNK), 0)
    bspt = bspt_ref[...]
    s_rows = []
    for n in range(T // GMLP_CHUNK):
        rs = slice(n * GMLP_CHUNK, (n + 1) * GMLP_CHUNK)
        cols = []
        for g in range(GMLP_GROUPS):
            w = jnp.where(tril, wsp_ref[g], 0.0).astype(BF16)
            sg = _mm(w, gv_b[rs, g * gw:(g + 1) * gw])
            cols.append(sg + jnp.broadcast_to(bspt[:, g:g + 1], (GMLP_CHUNK, gw)))
        s_rows.append(jnp.concatenate(cols, axis=1))
    o_b = (u * jnp.concatenate(s_rows, axis=0)).astype(BF16)

    rc, ra, rb = (rope_ref[:, i * LANES:(i + 1) * LANES] for i in range(3))
    qd = _qk_norm_rope(zq, qg_ref[...], rc, ra, rb, split=False) * (SWA_HEAD_DIM ** -0.5)
    kfull_ref[WINDOW:WINDOW + T, :] = _qk_norm_rope(zk, kg_ref[...], rc, ra, rb)
    vfull_ref[WINDOW:WINDOW + T, :] = zvd
    swa_fill = [functools.partial(branch_half, i, o, j) for i, o in ((2, o_c), (0, o_a)) for j in range(2)]

    rows = SWA_GROUP * WINDOW
    own_key = _iota((rows, WINDOW), 1) <= _iota((rows, WINDOW), 0) % WINDOW
    o_blocks = []
    for qb in range(T // WINDOW):
        rs = slice(qb * WINDOW, (qb + 1) * WINDOW)
        kblk = kfull_ref[qb * WINDOW:(qb + 2) * WINDOW, :].astype(BF16)
        vblk = vfull_ref[qb * WINDOW:(qb + 2) * WINDOW, :].astype(BF16)
        outs = []
        for g in range(SWA_KV_HEADS):
            q_stack = jnp.concatenate(
                [_swa_q_for_group(qd[rs], g, i) for i in range(SWA_GROUP)], axis=0).astype(BF16)
            s2 = _mm_nt(q_stack, kblk)
            s = jnp.where(own_key, s2[:, WINDOW:], s2[:, :WINDOW])
            if qb == 0:
                s = jnp.where(jnp.logical_or(own_key, t > 0), s, -jnp.inf)
            sink = _sink_column(sinks_ref, layer * SWA_HEADS + g * SWA_GROUP, rows, WINDOW)
            m = jnp.maximum(jnp.max(s, axis=-1, keepdims=True), sink)
            p = jnp.exp(s - m)
            den = jnp.sum(p, axis=-1, keepdims=True) + jnp.exp(sink - m)
            p2 = jnp.concatenate([jnp.where(own_key, 0.0, p), jnp.where(own_key, p, 0.0)], axis=1)
            o = _mm(p2.astype(BF16), vblk) / den
            outs.append([o[i * WINDOW:(i + 1) * WINDOW] for i in range(SWA_GROUP)])
            if swa_fill:
                swa_fill.pop(0)()
        o_blocks.append(_swa_assemble(outs, WINDOW))
    for fill in swa_fill:
        fill()
    o_d = jnp.concatenate(o_blocks, axis=0).astype(BF16)
    kfull_ref[0:WINDOW, :] = kfull_ref[T:T + WINDOW, :]
    vfull_ref[0:WINDOW, :] = vfull_ref[T:T + WINDOW, :]

    merged = (gates_ref[0] * jnp.concatenate([partial[(0, 0)], partial[(0, 1)]], axis=1)
              + gates_ref[2] * jnp.concatenate([partial[(2, 0)], partial[(2, 1)]], axis=1))
    merged = merged + gates_ref[1] * branch(1, o_b)
    merged = merged + gates_ref[3] * branch(3, o_d)
    y_ref[0] = x + _mm(merged.astype(BF16), wout_ref[:, :D_MODEL])

    @pl.when(t == last)
    def _():
        conv_ref[0] = hbuf_ref[CONV_LEAD:CONV_PAD, :]
        ko_ref[0] = kfull_ref[0:WINDOW, :]
        vo_ref[0] = vfull_ref[0:WINDOW, :]
        gla_ref[0] = gstate_ref[...]


def _layer_spec(arr, layer):
    nd = arr.ndim
    return pl.BlockSpec((None,) + tuple(arr.shape[1:]), lambda *_: (layer,) + (0,) * (nd - 1),
                        pipeline_mode=pl.Buffered(1))


def _full_spec(shape):
    nd = len(shape)
    return pl.BlockSpec(shape, lambda *_: (0,) * nd)


def _compiler_params(n_axes):
    return pltpu.CompilerParams(
        dimension_semantics=("arbitrary",) * n_axes, vmem_limit_bytes=VMEM_LIMIT_BYTES)


def _prompt_mixer(x, p, tables, layer):
    B, L, _ = x.shape
    T = PROMPT_TILE
    weights = tuple(p[n] for n in (
        "n1g", "win", "bin", "wal", "bal", "glag", "lng", "lnb", "wsp", "bspt",
        "cw", "cb", "clg", "clb", "qg", "kg", "wg", "bg", "wbr", "wout"))
    tab_spec = pl.BlockSpec((T, 3 * LANES), lambda b, t, *_: (t, 0))

    grid_spec = pltpu.PrefetchScalarGridSpec(
        num_scalar_prefetch=1,
        grid=(B, L // T),
        in_specs=[pl.BlockSpec((1, T, D_MODEL), lambda b, t, *_: (b, t, 0)),
                  tab_spec]
                 + [_layer_spec(w, layer) for w in weights],
        out_specs=[pl.BlockSpec((1, T, D_MODEL), lambda b, t, *_: (b, t, 0)),
                   pl.BlockSpec((1, QK, GLA_DV), lambda b, t, *_: (b, 0, 0)),
                   pl.BlockSpec((1, CONV_K - 1, CONV_WIDTH), lambda b, t, *_: (b, 0, 0)),
                   pl.BlockSpec((1, WINDOW, SWA_KV), lambda b, t, *_: (b, 0, 0)),
                   pl.BlockSpec((1, WINDOW, SWA_KV), lambda b, t, *_: (b, 0, 0))],
        scratch_shapes=[pltpu.VMEM((QK, GLA_DV), F32),
                        pltpu.VMEM((T + CONV_PAD, CONV_WIDTH), F32),
                        pltpu.VMEM((SUBLANES - 1, T + CONV_PAD, CONV_WIDTH), F32),
                        pltpu.VMEM((T + WINDOW, SWA_KV), F32),
                        pltpu.VMEM((T + WINDOW, SWA_KV), F32),
                        pltpu.VMEM((N_BRANCH, T, D_MODEL), F32)])
    return pl.pallas_call(
        functools.partial(_prompt_mixer_kernel, layer=layer),
        grid_spec=grid_spec,
        out_shape=[jax.ShapeDtypeStruct((B, L, D_MODEL), F32),
                   jax.ShapeDtypeStruct((B, QK, GLA_DV), F32),
                   jax.ShapeDtypeStruct((B, CONV_K - 1, CONV_WIDTH), F32),
                   jax.ShapeDtypeStruct((B, WINDOW, SWA_KV), F32),
                   jax.ShapeDtypeStruct((B, WINDOW, SWA_KV), F32)],
        compiler_params=_compiler_params(2),
        name="prompt_mixer",
    )(p["sinks"], x, jnp.concatenate(tables, axis=1), *weights)


def _ffn_kernel(x_ref, g_ref, wgu_ref, wd_ref, y_ref, h_ref):
    x = x_ref[...]
    xb = _rms_rows(x, g_ref[...]).astype(BF16)
    for c in range(FFN_HIDDEN // FFN_CHUNK):
        c0 = c * FFN_CHUNK
        gate = _mm(xb, wgu_ref[:, c0:c0 + FFN_CHUNK])
        up = _mm(xb, wgu_ref[:, FFN_HIDDEN + c0:FFN_HIDDEN + c0 + FFN_CHUNK])
        h_ref[:, c0:c0 + FFN_CHUNK] = (_silu(gate) * up).astype(BF16)
    y_ref[...] = x + _mm(h_ref[...], wd_ref[...])


def _ffn(x2d, p, layer):
    n = x2d.shape[0]
    tile = min(FFN_TILE, n)
    return pl.pallas_call(
        _ffn_kernel,
        grid=(n // tile,),
        in_specs=[pl.BlockSpec((tile, D_MODEL), lambda i: (i, 0)),
                  _layer_spec(p["n2g"], layer), _layer_spec(p["wgu"], layer), _layer_spec(p["wd"], layer)],
        out_specs=pl.BlockSpec((tile, D_MODEL), lambda i: (i, 0)),
        out_shape=jax.ShapeDtypeStruct((n, D_MODEL), F32),
        scratch_shapes=[pltpu.VMEM((tile, FFN_HIDDEN), BF16)],
        compiler_params=_compiler_params(1),
        name="swiglu",
    )(x2d, p["n2g"], p["wgu"], p["wd"])


def _sample_proj_kernel(x_ref, n1g_ref, win_ref, bin_ref, wal_ref, bal_ref, z_ref, ya_ref):
    xb = _rms_rows(x_ref[...], n1g_ref[...]).astype(BF16)
    z = _mm(xb, win_ref[...]) + bin_ref[...]
    z_ref[...] = z
    ya_ref[...] = _mm(z[:, C_LR:C_LR + LANES].astype(BF16), wal_ref[...]) + bal_ref[...]


def _sample_proj(x2d, p, layer):
    n = x2d.shape[0]
    weights = tuple(p[k] for k in ("n1g", "win", "bin", "wal", "bal"))
    return pl.pallas_call(
        _sample_proj_kernel,
        grid=(1,),
        in_specs=[_full_spec(x2d.shape)] + [_layer_spec(w, layer) for w in weights],
        out_specs=[_full_spec((n, IN_COLS_PACKED)), _full_spec((n, QK))],
        out_shape=[jax.ShapeDtypeStruct((n, IN_COLS_PACKED), F32),
                   jax.ShapeDtypeStruct((n, QK), F32)],
        compiler_params=_compiler_params(1),
        name="sample_proj",
    )(x2d, *weights)


def _sample_sequence(j, layer, sinks_ref, z_ref, ya_ref, s0_ref, hist_ref, kc_ref, vc_ref,
                     rc, ra, rb, glag, lng, lnb, wsc_ref, bsc, wsh_ref, cb, clg, clb, qg, kg,
                     obr_ref, s1_ref, convo_ref, ko_ref, vo_ref, gvo_ref, hext_ref):
    R = SAMPLE_ROWS
    nt = SAMPLE_TOKENS
    row = _iota((R, 1), 0)

    def seq_rows(ref):
        x = ref[(j // 2) * R:(j // 2 + 1) * R, :]
        if j % 2:
            x = pltpu.roll(x, R - nt, 0)
        return jnp.where(row < nt, x, 0.0)

    z = seq_rows(z_ref)
    head_ones = _group_ones(SWA_Q, SWA_HEAD_DIM)

    q = z[:, C_AQ:C_AQ + QK] * (GLA_DK ** -0.5)
    k = z[:, C_AK:C_AK + QK]
    v = z[:, C_AV:C_AV + QV]
    r = z[:, C_AR:C_AR + QV]
    zq = z[:, C_DQ:C_DQ + SWA_Q]
    zk = z[:, C_DK:C_DK + SWA_KV]
    zv = z[:, C_DV:C_DV + SWA_KV]
    ms_q = _mm_split_lhs(zq * zq, head_ones) * (1.0 / SWA_HEAD_DIM)
    ms_k = _mm_split_lhs(zk * zk, head_ones[:SWA_KV, :SWA_KV]) * (1.0 / SWA_HEAD_DIM)
    la = jnp.where(row < nt, _log_sigmoid(seq_rows(ya_ref)) * (1.0 / GLA_TAU), 0.0)
    bcum = jnp.zeros_like(la)
    for s in range(nt):
        bcum = bcum + jnp.where(row >= s, jnp.broadcast_to(la[s:s + 1, :], la.shape), 0.0)
    btot_row = bcum[nt - 1:nt, :]
    q_in = q * jnp.exp(bcum)
    k_out = k * jnp.exp(-bcum)
    k_dec = k * jnp.exp(jnp.broadcast_to(btot_row, la.shape) - bcum)
    s0 = s0_ref[j]
    s0_b = s0.astype(BF16)
    head_sel = (_iota((QK, LANES), 0) // GLA_DK == _iota((QK, LANES), 1)).astype(BF16)
    head_exp = (_iota((LANES, QV), 0) == _iota((LANES, QV), 1) // GLA_DV).astype(BF16)
    lane_head = _iota((R, QK), 1) // GLA_DK
    att_s = [_mm_split_lhs(q_in * jnp.broadcast_to(k_out[s:s + 1, :], q_in.shape), head_sel)
             for s in range(nt)]
    o_state = [_mm(jnp.where(lane_head == hh, q_in, 0.0).astype(BF16), s0_b) for hh in range(GLA_HEADS)]
    upd = _mm_tn(k_dec.astype(BF16).astype(F32), v.astype(BF16).astype(F32))
    dcol = jnp.exp(jnp.broadcast_to(btot_row, (GLA_DV, QK)).T)

    u = _gelu(z[:, C_BU:C_BU + GMLP_WIDTH])
    gv = _ln_rows(_gelu(z[:, C_BV:C_BV + GMLP_WIDTH]), lng, lnb)
    gvo_ref[j * nt:(j + 1) * nt, :] = gv[0:nt, :]
    sp = bsc
    for s in range(nt):
        sp = sp + jnp.where(row >= s, wsc_ref[s], 0.0) * jnp.broadcast_to(gv[s:s + 1, :], gv.shape)
    o_b = u * sp

    h = z[:, C_CA:C_CA + CONV_WIDTH] * jax.nn.sigmoid(z[:, C_CG:C_CG + CONV_WIDTH])
    hext_ref[j, 0:SUBLANES, :] = jnp.zeros((SUBLANES, CONV_WIDTH), F32)
    hext_ref[j, CONV_LEAD:CONV_PAD, :] = hist_ref[j]
    hext_ref[j, CONV_PAD:CONV_PAD + R, :] = h
    yield

    hext = hext_ref[j]
    conv = jnp.zeros((R, CONV_WIDTH), F32)
    for tt in range(nt):
        c_t = jnp.sum(hext * wsh_ref[tt], axis=0, keepdims=True)
        conv = jnp.where(row == tt, jnp.broadcast_to(c_t, conv.shape), conv)
    o_c = _silu(_ln_rows(conv + cb, clg, clb))
    convo_ref[j] = hext_ref[j, CONV_PAD + nt - (CONV_K - 1):CONV_PAD + nt, :]

    att_e = [_mm(a.astype(BF16), head_exp) for a in att_s]
    s1_ref[j] = dcol * s0 + jnp.concatenate(
        [upd[hh * GLA_DK:(hh + 1) * GLA_DK, hh * GLA_DV:(hh + 1) * GLA_DV] for hh in range(GLA_HEADS)],
        axis=0)

    qd = _rope(zq * lax.rsqrt(ms_q + NORM_EPS) * qg, rc, ra, rb)
    k_new = _rope(zk * lax.rsqrt(ms_k + NORM_EPS) * kg, rc, ra, rb)
    k_hist_b = kc_ref[j].astype(BF16)
    v_hist_b = vc_ref[j].astype(BF16)
    rows = SWA_GROUP * R
    qt = _iota((rows, WINDOW), 0) % R
    hist_ok = _iota((rows, WINDOW), 1) > qt
    qt1 = _iota((rows, 1), 0) % R
    scale = SWA_HEAD_DIM ** -0.5
    q_stacks, s_hists, s_news = [], [], []
    for g in range(SWA_KV_HEADS):
        q_stack = jnp.concatenate([_swa_q_for_group(qd, g, i) for i in range(SWA_GROUP)], axis=0)
        q_stacks.append(q_stack)
        s_hists.append(_mm_nt(q_stack.astype(BF16), k_hist_b))
        s_news.append([jnp.sum(q_stack * jnp.broadcast_to(k_new[s:s + 1, :], q_stack.shape),
                               axis=-1, keepdims=True) for s in range(nt)])
    yield

    o_intra = jnp.zeros((R, QV), F32)
    for s in range(nt):
        o_intra = o_intra + jnp.where(row >= s, att_e[s], 0.0) * jnp.broadcast_to(v[s:s + 1, :], (R, QV))
    o_a = _head_rms_gla(o_intra + jnp.concatenate(o_state, axis=1), glag) * _silu(r)

    sinks, ms, s_masked, n_masked = [], [], [], []
    for g in range(SWA_KV_HEADS):
        s_hist = jnp.where(hist_ok, s_hists[g] * scale, -jnp.inf)
        s_new = [jnp.where(qt1 >= s, s_news[g][s] * scale, -jnp.inf) for s in range(nt)]
        sink = _sink_column(sinks_ref, layer * SWA_HEADS + g * SWA_GROUP, rows, R)
        m = jnp.maximum(jnp.max(s_hist, axis=-1, keepdims=True), sink)
        for s in range(nt):
            m = jnp.maximum(m, s_new[s])
        sinks.append(sink)
        ms.append(m)
        s_masked.append(s_hist)
        n_masked.append(s_new)
    yield

    dens, pvs, p_news = [], [], []
    for g in range(SWA_KV_HEADS):
        p_hist = jnp.exp(s_masked[g] - ms[g])
        dens.append(jnp.sum(p_hist, axis=-1, keepdims=True) + jnp.exp(sinks[g] - ms[g]))
        pvs.append(_mm(p_hist.astype(BF16), v_hist_b))
        p_news.append([jnp.exp(n_masked[g][s] - ms[g]) for s in range(nt)])
    yield

    outs = []
    for g in range(SWA_KV_HEADS):
        o, den = pvs[g], dens[g]
        for s in range(nt):
            den = den + p_news[g][s]
            o = o + p_news[g][s] * jnp.broadcast_to(zv[s:s + 1, :], o.shape)
        o = o / den
        outs.append([o[i * R:(i + 1) * R] for i in range(SWA_GROUP)])
    o_d = _swa_assemble(outs, R)

    ko_ref[j, 0:WINDOW - nt, :] = kc_ref[j, nt:WINDOW, :]
    ko_ref[j, WINDOW - nt:WINDOW, :] = k_new[0:nt, :]
    vo_ref[j, 0:WINDOW - nt, :] = vc_ref[j, nt:WINDOW, :]
    vo_ref[j, WINDOW - nt:WINDOW, :] = zv[0:nt, :]

    obr_ref[j * nt:(j + 1) * nt, :] = jnp.concatenate([o_a, o_b, o_c, o_d], axis=1)[0:nt, :]


def _sample_mixer_kernel(
        sinks_ref,
        z_ref, ya_ref, s0_ref, hist_ref, kc_ref, vc_ref,
        rc_ref, ra_ref, rb_ref,
        glag_ref, lng_ref, lnb_ref, wsc_ref, bsc_ref,
        wsh_ref, cb_ref, clg_ref, clb_ref, qg_ref, kg_ref,
        s1_all_ref, convo_all_ref, ko_all_ref, vo_all_ref,
        obr_ref, s1_ref, convo_ref, ko_ref, vo_ref, gvo_ref,
        hext_ref, *, layer):
    del s1_all_ref, convo_all_ref, ko_all_ref, vo_all_ref
    consts = (rc_ref[...], ra_ref[...], rb_ref[...], glag_ref[...], lng_ref[...], lnb_ref[...])
    sequences = [
        _sample_sequence(
            j, layer, sinks_ref, z_ref, ya_ref, s0_ref, hist_ref, kc_ref, vc_ref,
            *consts, wsc_ref, bsc_ref[...], wsh_ref, cb_ref[...], clg_ref[...], clb_ref[...],
            qg_ref[...], kg_ref[...],
            obr_ref, s1_ref, convo_ref, ko_ref, vo_ref, gvo_ref, hext_ref)
        for j in range(SAMPLE_GROUP)]
    while sequences:
        sequences = [seq for seq in sequences if next(seq, StopIteration) is not StopIteration]


def _sample_mixer(z, ya, s0, hist, kc, vc, p, tables, stacked_states, layer):
    nt = SAMPLE_TOKENS
    nb = z.shape[0] // nt
    R = SAMPLE_ROWS
    G = SAMPLE_GROUP
    weights = tuple(p[n] for n in ("glag", "lng", "lnb", "wsc", "bsc", "wsh", "cb", "clg", "clb", "qg", "kg"))

    def per_group(shape):
        return pl.BlockSpec((G * nt, shape[1]), lambda b, *_: (b, 0))

    def per_group_of_layer(arr):
        nd = arr.ndim
        return pl.BlockSpec((None, G) + tuple(arr.shape[2:]), lambda b, *_: (layer, b) + (0,) * (nd - 2))

    def layer_block(arr):
        nd = arr.ndim
        return pl.BlockSpec((None,) + tuple(arr.shape[1:]), lambda b, *_: (layer,) + (0,) * (nd - 1))

    out_shapes = [jax.ShapeDtypeStruct((nb * nt, N_BRANCH * BRANCH_WIDTH), F32)] \
        + [jax.ShapeDtypeStruct(a.shape, F32) for a in stacked_states] \
        + [jax.ShapeDtypeStruct((nb * nt, GMLP_WIDTH), F32)]
    operands = (z, ya, s0, hist, kc, vc, *tables, *weights)
    n_in = 1 + len(operands)
    grid_spec = pltpu.PrefetchScalarGridSpec(
        num_scalar_prefetch=1,
        grid=(nb // G,),
        in_specs=[per_group(a.shape) for a in (z, ya)]
                 + [per_group_of_layer(a) for a in (s0, hist, kc, vc)]
                 + [_full_spec(tb.shape) for tb in tables]
                 + [layer_block(w) for w in weights]
                 + [pl.BlockSpec(memory_space=pl.ANY) for _ in stacked_states],
        out_specs=[per_group(out_shapes[0].shape)]
                  + [per_group_of_layer(a) for a in stacked_states]
                  + [per_group(out_shapes[-1].shape)],
        scratch_shapes=[pltpu.VMEM((G, CONV_PAD + R, CONV_WIDTH), F32)])
    return pl.pallas_call(
        functools.partial(_sample_mixer_kernel, layer=layer),
        grid_spec=grid_spec,
        out_shape=out_shapes,
        input_output_aliases={n_in + i: 1 + i for i in range(len(stacked_states))},
        compiler_params=_compiler_params(1),
        name="sample_mixer",
    )(p["sinks"], *operands, *stacked_states)


def _sample_merge_kernel(x_ref, obr_ref, n1g_ref, wg_ref, bg_ref, wbr_ref, wout_ref, y_ref):
    x = x_ref[...]
    xb = _rms_rows(x, n1g_ref[...]).astype(BF16)
    merged = jnp.zeros(x.shape, F32)
    for i in range(N_BRANCH):
        gate = jax.nn.sigmoid(
            _mm(xb, wg_ref[:, i * D_MODEL:(i + 1) * D_MODEL]) + bg_ref[:, i * D_MODEL:(i + 1) * D_MODEL])
        o_i = obr_ref[:, i * BRANCH_WIDTH:(i + 1) * BRANCH_WIDTH].astype(BF16)
        merged = merged + gate * _mm(o_i, wbr_ref[i, :, :D_MODEL])
    y_ref[...] = x + _mm(merged.astype(BF16), wout_ref[:, :D_MODEL])


def _sample_merge(x2d, obr, p, layer):
    weights = tuple(p[k] for k in ("n1g", "wg", "bg", "wbr", "wout"))
    return pl.pallas_call(
        _sample_merge_kernel,
        grid=(1,),
        in_specs=[_full_spec(x2d.shape), _full_spec(obr.shape)] + [_layer_spec(w, layer) for w in weights],
        out_specs=_full_spec(x2d.shape),
        out_shape=jax.ShapeDtypeStruct(x2d.shape, F32),
        compiler_params=_compiler_params(1),
        name="sample_merge",
    )(x2d, obr, *weights)


def _rope_tables(pos):
    half = ROPE_DIM // 2
    inv = jnp.exp(-math.log(ROPE_THETA) * jnp.arange(half, dtype=F32) * (2.0 / ROPE_DIM))
    ang = pos.astype(F32)[:, None] * inv[None, :]
    cos, sin = jnp.cos(ang), jnp.sin(ang)
    n = pos.shape[0]
    pad = jnp.zeros((n, SWA_HEAD_DIM - ROPE_DIM), F32)
    zero = jnp.zeros((n, half), F32)
    c_head = jnp.concatenate([cos, cos, pad + 1.0], axis=1)
    a_head = jnp.concatenate([-sin, zero, pad], axis=1)
    b_head = jnp.concatenate([zero, sin, pad], axis=1)
    rep = LANES // SWA_HEAD_DIM
    return tuple(jnp.tile(tb, (1, rep)) for tb in (c_head, a_head, b_head))


def _lane_pad(w):
    return jnp.pad(w, [(0, 0)] * (w.ndim - 1) + [(0, LANES)]).astype(BF16)


def _rows(v):
    return v.reshape(v.shape[0], 1, -1).astype(F32)


def _stacked_params(norm1_g, w_in, b_in, w_alpha2, b_alpha, gla_norm_g, gmlp_ln_g, gmlp_ln_b,
                    w_spatial, b_spatial, conv_w, conv_b, conv_ln_g, conv_ln_b, q_norm_g, k_norm_g,
                    sinks, w_gate, b_gate, w_branch, w_out, norm2_g, w_gate_up, w_down):
    o_lr = 2 * QK + 2 * QV
    o_rest = o_lr + GLA_RANK

    def repack(m, dtype):
        pad = jnp.zeros(m.shape[:-1] + (LANES - GLA_RANK,), m.dtype)
        return jnp.concatenate(
            [m[..., :o_lr], m[..., o_rest:], m[..., o_lr:o_rest], pad], axis=-1).astype(dtype)

    nt = SAMPLE_TOKENS
    gw = GMLP_WIDTH // GMLP_GROUPS
    pad_rows = ((0, 0), (0, SAMPLE_ROWS - nt), (0, 0))
    wsc = jnp.stack([
        jnp.pad(jnp.repeat(jnp.swapaxes(w_spatial[:, :, :nt, s], 1, 2), gw, axis=2), pad_rows)
        for s in range(nt)], axis=1)
    bsc = jnp.pad(jnp.repeat(jnp.swapaxes(b_spatial[:, :, :nt], 1, 2), gw, axis=2), pad_rows)
    wsh = jnp.stack([
        jnp.pad(conv_w, ((0, 0), (CONV_LEAD + t, CONV_PAD + SAMPLE_ROWS - CONV_LEAD - t - CONV_K), (0, 0)))
        for t in range(nt)], axis=1)
    return {
        "n1g": _rows(norm1_g),
        "win": repack(w_in, BF16),
        "bin": repack(b_in, F32).reshape(DEPTH, 1, -1),
        "wal": jnp.pad(w_alpha2, ((0, 0), (0, LANES - GLA_RANK), (0, 0))).astype(BF16),
        "bal": _rows(b_alpha),
        "glag": _rows(gla_norm_g),
        "lng": _rows(gmlp_ln_g),
        "lnb": _rows(gmlp_ln_b),
        "wsp": w_spatial.astype(F32),
        "bspt": jnp.swapaxes(b_spatial, 1, 2).astype(F32),
        "wsc": wsc.astype(F32),
        "bsc": bsc.astype(F32),
        "cw": jnp.pad(conv_w, ((0, 0), (0, CONV_PAD - CONV_K), (0, 0))).astype(F32),
        "wsh": wsh.astype(F32),
        "cb": _rows(conv_b),
        "clg": _rows(conv_ln_g),
        "clb": _rows(conv_ln_b),
        "qg": _rows(jnp.tile(q_norm_g, (1, SWA_HEADS))),
        "kg": _rows(jnp.tile(k_norm_g, (1, SWA_KV_HEADS))),
        "sinks": sinks.reshape(-1).astype(F32),
        "wg": _lane_pad(w_gate),
        "bg": _rows(b_gate),
        "wbr": _lane_pad(w_branch),
        "wout": _lane_pad(w_out),
        "n2g": _rows(norm2_g),
        "wgu": w_gate_up.astype(BF16),
        "wd": w_down.astype(BF16),
    }


def kernel(x_prompt, x_sample, state_gla, state_conv, cache_swa_k, cache_swa_v, norm1_g, w_in, b_in, w_alpha2, b_alpha, gla_norm_g, gmlp_ln_g, gmlp_ln_b, w_spatial, b_spatial, conv_w, conv_b, conv_ln_g, conv_ln_b, q_norm_g, k_norm_g, sinks, w_gate, b_gate, w_branch, w_out, norm2_g, w_gate_up, w_down):
    bp, lp, _ = x_prompt.shape
    bs, ls, _ = x_sample.shape
    R = SAMPLE_ROWS
    prompt_tables = _rope_tables(jnp.arange(lp))
    sample_tables = tuple(
        jnp.pad(tb, ((0, R - ls), (0, 0))) for tb in _rope_tables(PAST_LEN + jnp.arange(ls)))
    p = _stacked_params(norm1_g, w_in, b_in, w_alpha2, b_alpha, gla_norm_g, gmlp_ln_g, gmlp_ln_b,
                        w_spatial, b_spatial, conv_w, conv_b, conv_ln_g, conv_ln_b, q_norm_g,
                        k_norm_g, sinks, w_gate, b_gate, w_branch, w_out, norm2_g, w_gate_up, w_down)

    gla_in = state_gla.reshape(DEPTH, bs, QK, GLA_DV)
    swa_k_in = cache_swa_k.reshape(DEPTH, bs, WINDOW, SWA_KV)
    swa_v_in = cache_swa_v.reshape(DEPTH, bs, WINDOW, SWA_KV)

    sample_states = [jnp.zeros(a.shape, F32) for a in (gla_in, state_conv, swa_k_in, swa_v_in)]

    yp = x_prompt
    ys = x_sample.reshape(bs * ls, D_MODEL)
    outs = {k: [] for k in ("gla_p", "conv_p", "kp", "vp", "gm")}
    for l in range(DEPTH):
        yp, g1, c1, k1, v1 = _prompt_mixer(yp, p, prompt_tables, l)
        yp = _ffn(yp.reshape(bp * lp, D_MODEL), p, l).reshape(bp, lp, D_MODEL)
        outs["gla_p"].append(g1.reshape(bp, GLA_HEADS, GLA_DK, GLA_DV))
        outs["conv_p"].append(c1)
        outs["kp"].append(k1.reshape(bp, WINDOW, SWA_KV_HEADS, SWA_HEAD_DIM))
        outs["vp"].append(v1.reshape(bp, WINDOW, SWA_KV_HEADS, SWA_HEAD_DIM))
        z, ya = _sample_proj(ys, p, l)
        obr, *sample_states, gv2 = _sample_mixer(
            z, ya, gla_in, state_conv, swa_k_in, swa_v_in, p, sample_tables, sample_states, l)
        ys = _sample_merge(ys, obr, p, l)
        ys = _ffn(ys, p, l)
        outs["gm"].append(gv2.reshape(bs, ls, GMLP_WIDTH))
    st = lambda name: jnp.stack(outs[name], 0)
    gla_s, conv_s, ks, vs = sample_states
    kv_shape = (DEPTH, bs, WINDOW, SWA_KV_HEADS, SWA_HEAD_DIM)
    return (yp, ys.reshape(bs, ls, D_MODEL), st("gla_p"), gla_s.reshape(state_gla.shape), st("conv_p"),
            conv_s, st("kp"), st("vp"), ks.reshape(kv_shape), vs.reshape(kv_shape), st("gm"))
```

```python
import functools
import math

import jax
import jax.numpy as jnp
from jax import lax
from jax.experimental import pallas as pl
from jax.experimental.pallas import tpu as pltpu

F32 = jnp.float32
BF16 = jnp.bfloat16

D_MODEL = 1024
DEPTH = 2
PAST_LEN = 16384
GLA_HEADS = 4
GLA_DK = 64
GLA_DV = 128
GLA_RANK = 16
GLA_TAU = 16.0
GLA_CHUNK = 64
GMLP_GROUPS = 4
GMLP_WIDTH = 512
GMLP_CHUNK = 128
CONV_WIDTH = 512
CONV_K = 31
SWA_HEADS = 8
SWA_KV_HEADS = 2
SWA_GROUP = SWA_HEADS // SWA_KV_HEADS
SWA_HEAD_DIM = 64
WINDOW = 128
ROPE_DIM = SWA_HEAD_DIM // 4
ROPE_THETA = 500000.0
N_BRANCH = 4
BRANCH_WIDTH = 512
FFN_HIDDEN = 2816
NORM_EPS = 1e-6
LN_EPS = 1e-5

LANES = 128
SUBLANES = 8
VMEM_LIMIT_BYTES = 56 * 1024 * 1024

QK = GLA_HEADS * GLA_DK
QV = GLA_HEADS * GLA_DV
SWA_Q = SWA_HEADS * SWA_HEAD_DIM
SWA_KV = SWA_KV_HEADS * SWA_HEAD_DIM

C_AQ = 0
C_AK = C_AQ + QK
C_AV = C_AK + QK
C_AR = C_AV + QV
C_BU = C_AR + QV
C_BV = C_BU + GMLP_WIDTH
C_CA = C_BV + GMLP_WIDTH
C_CG = C_CA + CONV_WIDTH
C_DQ = C_CG + CONV_WIDTH
C_DK = C_DQ + SWA_Q
C_DV = C_DK + SWA_KV
C_LR = C_DV + SWA_KV
IN_COLS_PACKED = C_LR + LANES

PROMPT_TILE = 256
FFN_TILE = 1024
FFN_CHUNK = 256
GATE_PART = 256
CONV_PAD = 32
CONV_LEAD = CONV_PAD - (CONV_K - 1)
CONV_ROWS = 32
SAMPLE_ROWS = 8
SAMPLE_TOKENS = 4
SAMPLE_GROUP = 8


def _mm(a, b):
    return jnp.dot(a, b, preferred_element_type=F32)


def _mm_nt(a, b):
    return lax.dot_general(a, b, (((1,), (1,)), ((), ())), preferred_element_type=F32)


def _mm_tn(a, b):
    return lax.dot_general(a, b, (((0,), (0,)), ((), ())), preferred_element_type=F32)


def _hi_lo(a):
    hi = a.astype(BF16)
    lo = (a - hi.astype(F32)).astype(BF16)
    return hi, lo


def _mm_split_lhs(a_f32, m_bf16):
    hi, lo = _hi_lo(a_f32)
    return _mm(hi, m_bf16) + _mm(lo, m_bf16)


def _rms_rows(x, g):
    return x * lax.rsqrt(jnp.mean(x * x, axis=-1, keepdims=True) + NORM_EPS) * g


def _ln_rows(x, g, b):
    mu = jnp.mean(x, axis=-1, keepdims=True)
    xc = x - mu
    var = jnp.mean(xc * xc, axis=-1, keepdims=True)
    return xc * lax.rsqrt(var + LN_EPS) * g + b


def _gelu(x):
    return 0.5 * x * (1.0 + lax.erf(x * (1.0 / math.sqrt(2.0))))


def _silu(x):
    return x * jax.nn.sigmoid(x)


def _log_sigmoid(x):
    return jnp.minimum(x, 0.0) - jnp.log(1.0 + jnp.exp(-jnp.abs(x)))


def _iota(shape, dim):
    return lax.broadcasted_iota(jnp.int32, shape, dim)


def _group_ones(n, group, dtype=BF16):
    return (_iota((n, n), 0) // group == _iota((n, n), 1) // group).astype(dtype)


def _group_mean_sq(x, group):
    return _mm_split_lhs(x * x, _group_ones(x.shape[-1], group)) * (1.0 / group)


def _rope_slab(x, c, a, b):
    return x * c + pltpu.roll(x, LANES - ROPE_DIM // 2, 1) * a + pltpu.roll(x, ROPE_DIM // 2, 1) * b


def _rope(x, c, a, b):
    n = x.shape[-1] // LANES
    if n == 1:
        return _rope_slab(x, c, a, b)
    return jnp.concatenate(
        [_rope_slab(x[:, i * LANES:(i + 1) * LANES], c, a, b) for i in range(n)], axis=1)


def _qk_norm_rope(z, g, c, a, b, split=True):
    if split:
        ms = _group_mean_sq(z, SWA_HEAD_DIM)
    else:
        ms = _mm((z * z).astype(BF16), _group_ones(z.shape[-1], SWA_HEAD_DIM)) * (1.0 / SWA_HEAD_DIM)
    return _rope(z * lax.rsqrt(ms + NORM_EPS) * g, c, a, b)


def _head_rms_gla(o, g):
    return jnp.concatenate(
        [_rms_rows(o[:, h * GLA_DV:(h + 1) * GLA_DV], g) for h in range(GLA_HEADS)], axis=1)


def _swa_q_for_group(q, g, i):
    hq = g * SWA_GROUP + i
    slab = q[:, (hq // 2) * LANES:(hq // 2 + 1) * LANES]
    if hq % 2 != g:
        slab = pltpu.roll(slab, SWA_HEAD_DIM, 1)
    lane_head = _iota(slab.shape, 1) // SWA_HEAD_DIM
    return jnp.where(lane_head == g, slab, 0.0)


def _swa_assemble(outs, rows):
    lane_half = _iota((rows, LANES), 1) // SWA_HEAD_DIM
    slabs = []
    for s in range(SWA_HEADS // 2):
        pair = []
        for hq in (2 * s, 2 * s + 1):
            g, i = hq // SWA_GROUP, hq % SWA_GROUP
            o = outs[g][i]
            if hq % 2 != g:
                o = pltpu.roll(o, SWA_HEAD_DIM, 1)
            pair.append(o)
        slabs.append(jnp.where(lane_half == 0, pair[0], pair[1]))
    return jnp.concatenate(slabs, axis=1)


def _sink_column(sinks_ref, base, rows, rows_per_head):
    row_head = _iota((rows, 1), 0) // rows_per_head
    sink = jnp.zeros((rows, 1), F32)
    for i in range(SWA_GROUP):
        sink = jnp.where(row_head == i, sinks_ref[base + i], sink)
    return sink


def _prompt_mixer_kernel(
        sinks_ref,
        x_ref, rope_ref,
        n1g_ref, win_ref, bin_ref, wal_ref, bal_ref, glag_ref,
        lng_ref, lnb_ref, wsp_ref, bspt_ref,
        cw_ref, cb_ref, clg_ref, clb_ref,
        qg_ref, kg_ref,
        wg_ref, bg_ref, wbr_ref, wout_ref,
        y_ref, gla_ref, conv_ref, ko_ref, vo_ref,
        gstate_ref, hbuf_ref, hsh_ref, kfull_ref, vfull_ref, gates_ref,
        *, layer):
    T = PROMPT_TILE
    t = pl.program_id(1)
    last = pl.num_programs(1) - 1

    @pl.when(t == 0)
    def _():
        gstate_ref[...] = jnp.zeros_like(gstate_ref)
        hbuf_ref[0:CONV_PAD, :] = jnp.zeros((CONV_PAD, CONV_WIDTH), F32)
        kfull_ref[0:WINDOW, :] = jnp.zeros((WINDOW, SWA_KV), F32)
        vfull_ref[0:WINDOW, :] = jnp.zeros((WINDOW, SWA_KV), F32)

    x = x_ref[0]
    xb = _rms_rows(x, n1g_ref[...]).astype(BF16)

    def proj(c0, width):
        return _mm(xb, win_ref[:, c0:c0 + width]) + bin_ref[:, c0:c0 + width]

    def gate_part(i, j):
        c0 = i * D_MODEL + j * GATE_PART
        gates_ref[i, :, j * GATE_PART:(j + 1) * GATE_PART] = jax.nn.sigmoid(
            _mm(xb, wg_ref[:, c0:c0 + GATE_PART]) + bg_ref[:, c0:c0 + GATE_PART])

    def branch(i, o_b16):
        return _mm(o_b16, wbr_ref[i, :, :D_MODEL])

    half = D_MODEL // 2
    partial = {}
    shared = {}

    def branch_half(i, o_b16, j):
        partial[(i, j)] = _mm(o_b16, wbr_ref[i, :, j * half:(j + 1) * half])

    def branch_c():
        h = proj(C_CA, CONV_WIDTH) * jax.nn.sigmoid(proj(C_CG, CONV_WIDTH))
        hbuf_ref[CONV_PAD:CONV_PAD + T, :] = h
        taps_of = {r: [kk for kk in range(CONV_K) if (CONV_LEAD + kk) % SUBLANES == r] for r in range(SUBLANES)}
        for r in range(1, SUBLANES):
            span = max(CONV_LEAD + kk for kk in taps_of[r]) - r
            hsh_ref[r - 1, 0:span + T, :] = hbuf_ref[r:r + span + T, :]
        yield
        acc_blocks = []
        for rb in range(T // CONV_ROWS):
            acc = jnp.broadcast_to(cb_ref[...], (CONV_ROWS, CONV_WIDTH))
            for r_ in range(SUBLANES):
                for kk in taps_of[r_]:
                    a = CONV_LEAD + kk - r_ + rb * CONV_ROWS
                    rows_k = hbuf_ref[a:a + CONV_ROWS, :] if r_ == 0 else hsh_ref[r_ - 1, a:a + CONV_ROWS, :]
                    acc = acc + cw_ref[kk:kk + 1, :] * rows_k
            acc_blocks.append(acc)
            yield
        hbuf_ref[0:CONV_PAD, :] = hbuf_ref[T:T + CONV_PAD, :]
        o_c = _silu(_ln_rows(jnp.concatenate(acc_blocks, axis=0), clg_ref[...], clb_ref[...])).astype(BF16)
        yield
        branch_half(2, o_c, 0)
        yield
        branch_half(2, o_c, 1)

    def branch_a():
        q = proj(C_AQ, QK) * (GLA_DK ** -0.5)
        k = proj(C_AK, QK)
        v = proj(C_AV, QV)
        kvl = proj(C_DK, 2 * SWA_KV + LANES)
        shared["zk"], shared["zvd"], lr = kvl[:, :SWA_KV], kvl[:, SWA_KV:2 * SWA_KV], kvl[:, 2 * SWA_KV:]
        yield
        la = _log_sigmoid(_mm(lr.astype(BF16), wal_ref[...]) + bal_ref[...]) * (1.0 / GLA_TAU)
        r = proj(C_AR, QV)
        yield
        yield from branch_a_core(q, k, v, la, r)

    def branch_a_core(q, k, v, la, r):
        n_chunks = T // GLA_CHUNK
        ri = _iota((T, T), 0)
        ci = _iota((T, T), 1)
        chunk_causal = jnp.logical_and((ri // GLA_CHUNK) == (ci // GLA_CHUNK), ci <= ri)
        la_hi, la_lo = _hi_lo(la)
        m_tril = chunk_causal.astype(BF16)
        bcum = _mm(m_tril, la_hi) + _mm(m_tril, la_lo)
        yield
        tot_rows = [bcum[(c + 1) * GLA_CHUNK - 1:(c + 1) * GLA_CHUNK, :] for c in range(n_chunks)]
        btot = jnp.concatenate([jnp.broadcast_to(tr, (GLA_CHUNK, QK)) for tr in tot_rows], axis=0)
        q_in = (q * jnp.exp(bcum)).astype(BF16)
        k_out = (k * jnp.exp(-bcum)).astype(BF16)
        k_dec_t = (k * jnp.exp(btot - bcum)).T.astype(BF16)
        v_b = v.astype(BF16)
        dec_t = jnp.exp(jnp.concatenate(
            [jnp.broadcast_to(tr, (GLA_DV, QK)) for tr in tot_rows], axis=0).T)

        lane_head = _iota((T, QK), 1) // GLA_DK
        o_heads = []
        for hh in range(GLA_HEADS):
            qh = jnp.where(lane_head == hh, q_in, jnp.zeros_like(q_in))
            att = jnp.where(chunk_causal, _mm_nt(qh, k_out), 0.0).astype(BF16)
            o_heads.append(_mm(att, v_b[:, hh * GLA_DV:(hh + 1) * GLA_DV]))
            yield
        o_intra = jnp.concatenate(o_heads, axis=1)

        bd_mask = (_iota((QK, QV), 0) // GLA_DK) == (_iota((QK, QV), 1) // GLA_DV)
        tok_chunk = _iota((GLA_DK, T), 1) // GLA_CHUNK
        o_state = []
        for c in range(n_chunks):
            rs = slice(c * GLA_CHUNK, (c + 1) * GLA_CHUNK)
            s_c = gstate_ref[...]
            s_b = s_c.astype(BF16)
            s_bd = jnp.where(bd_mask, jnp.concatenate([s_b] * GLA_HEADS, axis=1), jnp.zeros((QK, QV), BF16))
            o_state.append(_mm(q_in[rs], s_bd))
            upd = []
            for hh in range(GLA_HEADS):
                kt = k_dec_t[hh * GLA_DK:(hh + 1) * GLA_DK, :]
                kt = jnp.where(tok_chunk == c, kt, jnp.zeros_like(kt))
                upd.append(_mm(kt, v_b[:, hh * GLA_DV:(hh + 1) * GLA_DV]))
            gstate_ref[...] = dec_t[:, c * GLA_DV:(c + 1) * GLA_DV] * s_c + jnp.concatenate(upd, axis=0)
            yield
        o_gla = o_intra + jnp.concatenate(o_state, axis=0)
        o_a = (_head_rms_gla(o_gla, glag_ref[...]) * _silu(r)).astype(BF16)
        yield
        branch_half(0, o_a, 0)
        yield
        branch_half(0, o_a, 1)

    def branch_b():
        zu = proj(C_BU, GMLP_WIDTH)
        yield
        zgv = proj(C_BV, GMLP_WIDTH)
        yield
        u = _gelu(zu)
        gv = _ln_rows(_gelu(zgv), lng_ref[...], lnb_ref[...])
        gv_b = gv.astype(BF16)
        gw = GMLP_WIDTH // GMLP_GROUPS
        tril = _iota((GMLP_CHUNK, GMLP_CHUNK), 1) <= _iota((GMLP_CHUNK, GMLP_CHUNK), 0)
        bspt = bspt_ref[...]
        s_rows = []
        for n in range(T // GMLP_CHUNK):
            rs = slice(n * GMLP_CHUNK, (n + 1) * GMLP_CHUNK)
            cols = []
            for g in range(GMLP_GROUPS):
                w = jnp.where(tril, wsp_ref[g], 0.0).astype(BF16)
                sg = _mm(w, gv_b[rs, g * gw:(g + 1) * gw])
                cols.append(sg + jnp.broadcast_to(bspt[:, g:g + 1], (GMLP_CHUNK, gw)))
            s_rows.append(jnp.concatenate(cols, axis=1))
            yield
        o_b = (u * jnp.concatenate(s_rows, axis=0)).astype(BF16)
        yield
        branch_half(1, o_b, 0)
        yield
        branch_half(1, o_b, 1)

    def branch_d():
        zq = proj(C_DQ, SWA_Q)
        yield
        rc, ra, rb = (rope_ref[:, i * LANES:(i + 1) * LANES] for i in range(3))
        qd = _qk_norm_rope(zq, qg_ref[...], rc, ra, rb, split=False) * (SWA_HEAD_DIM ** -0.5)
        kfull_ref[WINDOW:WINDOW + T, :] = _qk_norm_rope(shared["zk"], kg_ref[...], rc, ra, rb)
        vfull_ref[WINDOW:WINDOW + T, :] = shared["zvd"]
        yield
        rows = SWA_GROUP * WINDOW
        own_key = _iota((rows, WINDOW), 1) <= _iota((rows, WINDOW), 0) % WINDOW
        o_blocks = []
        for qb in range(T // WINDOW):
            rs = slice(qb * WINDOW, (qb + 1) * WINDOW)
            kblk = kfull_ref[qb * WINDOW:(qb + 2) * WINDOW, :].astype(BF16)
            vblk = vfull_ref[qb * WINDOW:(qb + 2) * WINDOW, :].astype(BF16)
            outs = []
            for g in range(SWA_KV_HEADS):
                q_stack = jnp.concatenate(
                    [_swa_q_for_group(qd[rs], g, i) for i in range(SWA_GROUP)], axis=0).astype(BF16)
                s2 = _mm_nt(q_stack, kblk)
                yield
                s = jnp.where(own_key, s2[:, WINDOW:], s2[:, :WINDOW])
                if qb == 0:
                    s = jnp.where(jnp.logical_or(own_key, t > 0), s, -jnp.inf)
                sink = _sink_column(sinks_ref, layer * SWA_HEADS + g * SWA_GROUP, rows, WINDOW)
                m = jnp.maximum(jnp.max(s, axis=-1, keepdims=True), sink)
                p = jnp.exp(s - m)
                den = jnp.sum(p, axis=-1, keepdims=True) + jnp.exp(sink - m)
                p2 = jnp.concatenate([jnp.where(own_key, 0.0, p), jnp.where(own_key, p, 0.0)], axis=1)
                o = _mm(p2.astype(BF16), vblk) / den
                outs.append([o[i * WINDOW:(i + 1) * WINDOW] for i in range(SWA_GROUP)])
                yield
            o_blocks.append(_swa_assemble(outs, WINDOW))
        o_d = jnp.concatenate(o_blocks, axis=0).astype(BF16)
        kfull_ref[0:WINDOW, :] = kfull_ref[T:T + WINDOW, :]
        vfull_ref[0:WINDOW, :] = vfull_ref[T:T + WINDOW, :]
        yield
        branch_half(3, o_d, 0)
        yield
        branch_half(3, o_d, 1)

    def gate_parts():
        for i in range(N_BRANCH):
            for j in range(D_MODEL // GATE_PART):
                gate_part(i, j)
                yield

    stages = [branch_a(), branch_c(), branch_b(), branch_d(), gate_parts()]
    while stages:
        stages = [stage for stage in stages if next(stage, StopIteration) is not StopIteration]

    merged = jnp.zeros((T, D_MODEL), F32)
    for i in range(N_BRANCH):
        merged = merged + gates_ref[i] * jnp.concatenate([partial[(i, 0)], partial[(i, 1)]], axis=1)
    y_ref[0] = x + _mm(merged.astype(BF16), wout_ref[:, :D_MODEL])

    @pl.when(t == last)
    def _():
        conv_ref[0] = hbuf_ref[CONV_LEAD:CONV_PAD, :]
        ko_ref[0] = kfull_ref[0:WINDOW, :]
        vo_ref[0] = vfull_ref[0:WINDOW, :]
        gla_ref[0] = gstate_ref[...]


def _layer_spec(arr, layer):
    nd = arr.ndim
    return pl.BlockSpec((None,) + tuple(arr.shape[1:]), lambda *_: (layer,) + (0,) * (nd - 1),
                        pipeline_mode=pl.Buffered(1))


def _full_spec(shape):
    nd = len(shape)
    return pl.BlockSpec(shape, lambda *_: (0,) * nd)


def _compiler_params(n_axes):
    return pltpu.CompilerParams(
        dimension_semantics=("arbitrary",) * n_axes, vmem_limit_bytes=VMEM_LIMIT_BYTES)


def _prompt_mixer(x, p, tables, layer):
    B, L, _ = x.shape
    T = PROMPT_TILE
    weights = tuple(p[n] for n in (
        "n1g", "win", "bin", "wal", "bal", "glag", "lng", "lnb", "wsp", "bspt",
        "cw", "cb", "clg", "clb", "qg", "kg", "wg", "bg", "wbr", "wout"))
    tab_spec = pl.BlockSpec((T, 3 * LANES), lambda b, t, *_: (t, 0))

    grid_spec = pltpu.PrefetchScalarGridSpec(
        num_scalar_prefetch=1,
        grid=(B, L // T),
        in_specs=[pl.BlockSpec((1, T, D_MODEL), lambda b, t, *_: (b, t, 0)),
                  tab_spec]
                 + [_layer_spec(w, layer) for w in weights],
        out_specs=[pl.BlockSpec((1, T, D_MODEL), lambda b, t, *_: (b, t, 0)),
                   pl.BlockSpec((1, QK, GLA_DV), lambda b, t, *_: (b, 0, 0)),
                   pl.BlockSpec((1, CONV_K - 1, CONV_WIDTH), lambda b, t, *_: (b, 0, 0)),
                   pl.BlockSpec((1, WINDOW, SWA_KV), lambda b, t, *_: (b, 0, 0)),
                   pl.BlockSpec((1, WINDOW, SWA_KV), lambda b, t, *_: (b, 0, 0))],
        scratch_shapes=[pltpu.VMEM((QK, GLA_DV), F32),
                        pltpu.VMEM((T + CONV_PAD, CONV_WIDTH), F32),
                        pltpu.VMEM((SUBLANES - 1, T + CONV_PAD, CONV_WIDTH), F32),
                        pltpu.VMEM((T + WINDOW, SWA_KV), F32),
                        pltpu.VMEM((T + WINDOW, SWA_KV), F32),
                        pltpu.VMEM((N_BRANCH, T, D_MODEL), F32)])
    return pl.pallas_call(
        functools.partial(_prompt_mixer_kernel, layer=layer),
        grid_spec=grid_spec,
        out_shape=[jax.ShapeDtypeStruct((B, L, D_MODEL), F32),
                   jax.ShapeDtypeStruct((B, QK, GLA_DV), F32),
                   jax.ShapeDtypeStruct((B, CONV_K - 1, CONV_WIDTH), F32),
                   jax.ShapeDtypeStruct((B, WINDOW, SWA_KV), F32),
                   jax.ShapeDtypeStruct((B, WINDOW, SWA_KV), F32)],
        compiler_params=_compiler_params(2),
        name="prompt_mixer",
    )(p["sinks"], x, jnp.concatenate(tables, axis=1), *weights)


def _ffn_kernel(x_ref, g_ref, wgu_ref, wd_ref, y_ref, h_ref):
    x = x_ref[...]
    xb = _rms_rows(x, g_ref[...]).astype(BF16)
    for c in range(FFN_HIDDEN // FFN_CHUNK):
        c0 = c * FFN_CHUNK
        gate = _mm(xb, wgu_ref[:, c0:c0 + FFN_CHUNK])
        up = _mm(xb, wgu_ref[:, FFN_HIDDEN + c0:FFN_HIDDEN + c0 + FFN_CHUNK])
        h_ref[:, c0:c0 + FFN_CHUNK] = (_silu(gate) * up).astype(BF16)
    y_ref[...] = x + _mm(h_ref[...], wd_ref[...])


def _ffn(x2d, p, layer):
    n = x2d.shape[0]
    tile = min(FFN_TILE, n)
    return pl.pallas_call(
        _ffn_kernel,
        grid=(n // tile,),
        in_specs=[pl.BlockSpec((tile, D_MODEL), lambda i: (i, 0)),
                  _layer_spec(p["n2g"], layer), _layer_spec(p["wgu"], layer), _layer_spec(p["wd"], layer)],
        out_specs=pl.BlockSpec((tile, D_MODEL), lambda i: (i, 0)),
        out_shape=jax.ShapeDtypeStruct((n, D_MODEL), F32),
        scratch_shapes=[pltpu.VMEM((tile, FFN_HIDDEN), BF16)],
        compiler_params=_compiler_params(1),
        name="swiglu",
    )(x2d, p["n2g"], p["wgu"], p["wd"])


def _sample_proj_kernel(x_ref, n1g_ref, win_ref, bin_ref, wal_ref, bal_ref, z_ref, ya_ref):
    xb = _rms_rows(x_ref[...], n1g_ref[...]).astype(BF16)
    z = _mm(xb, win_ref[...]) + bin_ref[...]
    z_ref[...] = z
    ya_ref[...] = _mm(z[:, C_LR:C_LR + LANES].astype(BF16), wal_ref[...]) + bal_ref[...]


def _sample_proj(x2d, p, layer):
    n = x2d.shape[0]
    weights = tuple(p[k] for k in ("n1g", "win", "bin", "wal", "bal"))
    return pl.pallas_call(
        _sample_proj_kernel,
        grid=(1,),
        in_specs=[_full_spec(x2d.shape)] + [_layer_spec(w, layer) for w in weights],
        out_specs=[_full_spec((n, IN_COLS_PACKED)), _full_spec((n, QK))],
        out_shape=[jax.ShapeDtypeStruct((n, IN_COLS_PACKED), F32),
                   jax.ShapeDtypeStruct((n, QK), F32)],
        compiler_params=_compiler_params(1),
        name="sample_proj",
    )(x2d, *weights)


def _sample_sequence(j, layer, sinks_ref, z_ref, ya_ref, s0_ref, hist_ref, kc_ref, vc_ref,
                     rc, ra, rb, glag, lng, lnb, wsc_ref, bsc, wsh_ref, cb, clg, clb, qg, kg,
                     obr_ref, s1_ref, convo_ref, ko_ref, vo_ref, gvo_ref, hext_ref):
    R = SAMPLE_ROWS
    nt = SAMPLE_TOKENS
    row = _iota((R, 1), 0)

    def seq_rows(ref):
        x = ref[(j // 2) * R:(j // 2 + 1) * R, :]
        if j % 2:
            x = pltpu.roll(x, R - nt, 0)
        return jnp.where(row < nt, x, 0.0)

    z = seq_rows(z_ref)
    head_ones = _group_ones(SWA_Q, SWA_HEAD_DIM)

    q = z[:, C_AQ:C_AQ + QK] * (GLA_DK ** -0.5)
    k = z[:, C_AK:C_AK + QK]
    v = z[:, C_AV:C_AV + QV]
    r = z[:, C_AR:C_AR + QV]
    zq = z[:, C_DQ:C_DQ + SWA_Q]
    zk = z[:, C_DK:C_DK + SWA_KV]
    zv = z[:, C_DV:C_DV + SWA_KV]
    ms_q = _mm_split_lhs(zq * zq, head_ones) * (1.0 / SWA_HEAD_DIM)
    ms_k = _mm_split_lhs(zk * zk, head_ones[:SWA_KV, :SWA_KV]) * (1.0 / SWA_HEAD_DIM)
    la = jnp.where(row < nt, _log_sigmoid(seq_rows(ya_ref)) * (1.0 / GLA_TAU), 0.0)
    bcum = jnp.zeros_like(la)
    for s in range(nt):
        bcum = bcum + jnp.where(row >= s, jnp.broadcast_to(la[s:s + 1, :], la.shape), 0.0)
    btot_row = bcum[nt - 1:nt, :]
    q_in = q * jnp.exp(bcum)
    k_out = k * jnp.exp(-bcum)
    k_dec = k * jnp.exp(jnp.broadcast_to(btot_row, la.shape) - bcum)
    s0 = s0_ref[j]
    s0_b = s0.astype(BF16)
    head_sel = (_iota((QK, LANES), 0) // GLA_DK == _iota((QK, LANES), 1)).astype(BF16)
    head_exp = (_iota((LANES, QV), 0) == _iota((LANES, QV), 1) // GLA_DV).astype(BF16)
    lane_head = _iota((R, QK), 1) // GLA_DK
    att_s = [_mm_split_lhs(q_in * jnp.broadcast_to(k_out[s:s + 1, :], q_in.shape), head_sel)
             for s in range(nt)]
    o_state = [_mm(jnp.where(lane_head == hh, q_in, 0.0).astype(BF16), s0_b) for hh in range(GLA_HEADS)]
    upd = _mm_tn(k_dec.astype(BF16).astype(F32), v.astype(BF16).astype(F32))
    dcol = jnp.exp(jnp.broadcast_to(btot_row, (GLA_DV, QK)).T)

    u = _gelu(z[:, C_BU:C_BU + GMLP_WIDTH])
    gv = _ln_rows(_gelu(z[:, C_BV:C_BV + GMLP_WIDTH]), lng, lnb)
    gvo_ref[j * nt:(j + 1) * nt, :] = gv[0:nt, :]
    sp = bsc
    for s in range(nt):
        sp = sp + jnp.where(row >= s, wsc_ref[s], 0.0) * jnp.broadcast_to(gv[s:s + 1, :], gv.shape)
    o_b = u * sp

    h = z[:, C_CA:C_CA + CONV_WIDTH] * jax.nn.sigmoid(z[:, C_CG:C_CG + CONV_WIDTH])
    hext_ref[j, 0:SUBLANES, :] = jnp.zeros((SUBLANES, CONV_WIDTH), F32)
    hext_ref[j, CONV_LEAD:CONV_PAD, :] = hist_ref[j]
    hext_ref[j, CONV_PAD:CONV_PAD + R, :] = h
    yield

    hext = hext_ref[j]
    conv = jnp.zeros((R, CONV_WIDTH), F32)
    for tt in range(nt):
        c_t = jnp.sum(hext * wsh_ref[tt], axis=0, keepdims=True)
        conv = jnp.where(row == tt, jnp.broadcast_to(c_t, conv.shape), conv)
    o_c = _silu(_ln_rows(conv + cb, clg, clb))
    convo_ref[j] = hext_ref[j, CONV_PAD + nt - (CONV_K - 1):CONV_PAD + nt, :]

    att_e = [_mm(a.astype(BF16), head_exp) for a in att_s]
    s1_ref[j] = dcol * s0 + jnp.concatenate(
        [upd[hh * GLA_DK:(hh + 1) * GLA_DK, hh * GLA_DV:(hh + 1) * GLA_DV] for hh in range(GLA_HEADS)],
        axis=0)

    qd = _rope(zq * lax.rsqrt(ms_q + NORM_EPS) * qg, rc, ra, rb)
    k_new = _rope(zk * lax.rsqrt(ms_k + NORM_EPS) * kg, rc, ra, rb)
    k_hist_b = kc_ref[j].astype(BF16)
    v_hist_b = vc_ref[j].astype(BF16)
    rows = SWA_GROUP * R
    qt = _iota((rows, WINDOW), 0) % R
    hist_ok = _iota((rows, WINDOW), 1) > qt
    qt1 = _iota((rows, 1), 0) % R
    scale = SWA_HEAD_DIM ** -0.5
    q_stacks, s_hists, s_news = [], [], []
    for g in range(SWA_KV_HEADS):
        q_stack = jnp.concatenate([_swa_q_for_group(qd, g, i) for i in range(SWA_GROUP)], axis=0)
        q_stacks.append(q_stack)
        s_hists.append(_mm_nt(q_stack.astype(BF16), k_hist_b))
        s_news.append([jnp.sum(q_stack * jnp.broadcast_to(k_new[s:s + 1, :], q_stack.shape),
                               axis=-1, keepdims=True) for s in range(nt)])
    yield

    o_intra = jnp.zeros((R, QV), F32)
    for s in range(nt):
        o_intra = o_intra + jnp.where(row >= s, att_e[s], 0.0) * jnp.broadcast_to(v[s:s + 1, :], (R, QV))
    o_a = _head_rms_gla(o_intra + jnp.concatenate(o_state, axis=1), glag) * _silu(r)

    sinks, ms, s_masked, n_masked = [], [], [], []
    for g in range(SWA_KV_HEADS):
        s_hist = jnp.where(hist_ok, s_hists[g] * scale, -jnp.inf)
        s_new = [jnp.where(qt1 >= s, s_news[g][s] * scale, -jnp.inf) for s in range(nt)]
        sink = _sink_column(sinks_ref, layer * SWA_HEADS + g * SWA_GROUP, rows, R)
        m = jnp.maximum(jnp.max(s_hist, axis=-1, keepdims=True), sink)
        for s in range(nt):
            m = jnp.maximum(m, s_new[s])
        sinks.append(sink)
        ms.append(m)
        s_masked.append(s_hist)
        n_masked.append(s_new)
    yield

    dens, pvs, p_news = [], [], []
    for g in range(SWA_KV_HEADS):
        p_hist = jnp.exp(s_masked[g] - ms[g])
        dens.append(jnp.sum(p_hist, axis=-1, keepdims=True) + jnp.exp(sinks[g] - ms[g]))
        pvs.append(_mm(p_hist.astype(BF16), v_hist_b))
        p_news.append([jnp.exp(n_masked[g][s] - ms[g]) for s in range(nt)])
    yield

    outs = []
    for g in range(SWA_KV_HEADS):
        o, den = pvs[g], dens[g]
        for s in range(nt):
            den = den + p_news[g][s]
            o = o + p_news[g][s] * jnp.broadcast_to(zv[s:s + 1, :], o.shape)
        o = o / den
        outs.append([o[i * R:(i + 1) * R] for i in range(SWA_GROUP)])
    o_d = _swa_assemble(outs, R)

    ko_ref[j, 0:WINDOW - nt, :] = kc_ref[j, nt:WINDOW, :]
    ko_ref[j, WINDOW - nt:WINDOW, :] = k_new[0:nt, :]
    vo_ref[j, 0:WINDOW - nt, :] = vc_ref[j, nt:WINDOW, :]
    vo_ref[j, WINDOW - nt:WINDOW, :] = zv[0:nt, :]

    obr_ref[j * nt:(j + 1) * nt, :] = jnp.concatenate([o_a, o_b, o_c, o_d], axis=1)[0:nt, :]


def _sample_mixer_kernel(
        sinks_ref,
        z_ref, ya_ref, s0_ref, hist_ref, kc_ref, vc_ref,
        rc_ref, ra_ref, rb_ref,
        glag_ref, lng_ref, lnb_ref, wsc_ref, bsc_ref,
        wsh_ref, cb_ref, clg_ref, clb_ref, qg_ref, kg_ref,
        s1_all_ref, convo_all_ref, ko_all_ref, vo_all_ref,
        obr_ref, s1_ref, convo_ref, ko_ref, vo_ref, gvo_ref,
        hext_ref, *, layer):
    del s1_all_ref, convo_all_ref, ko_all_ref, vo_all_ref
    consts = (rc_ref[...], ra_ref[...], rb_ref[...], glag_ref[...], lng_ref[...], lnb_ref[...])
    sequences = [
        _sample_sequence(
            j, layer, sinks_ref, z_ref, ya_ref, s0_ref, hist_ref, kc_ref, vc_ref,
            *consts, wsc_ref, bsc_ref[...], wsh_ref, cb_ref[...], clg_ref[...], clb_ref[...],
            qg_ref[...], kg_ref[...],
            obr_ref, s1_ref, convo_ref, ko_ref, vo_ref, gvo_ref, hext_ref)
        for j in range(SAMPLE_GROUP)]
    while sequences:
        sequences = [seq for seq in sequences if next(seq, StopIteration) is not StopIteration]


def _sample_mixer(z, ya, s0, hist, kc, vc, p, tables, stacked_states, layer):
    nt = SAMPLE_TOKENS
    nb = z.shape[0] // nt
    R = SAMPLE_ROWS
    G = SAMPLE_GROUP
    weights = tuple(p[n] for n in ("glag", "lng", "lnb", "wsc", "bsc", "wsh", "cb", "clg", "clb", "qg", "kg"))

    def per_group(shape):
        return pl.BlockSpec((G * nt, shape[1]), lambda b, *_: (b, 0))

    def per_group_of_layer(arr):
        nd = arr.ndim
        return pl.BlockSpec((None, G) + tuple(arr.shape[2:]), lambda b, *_: (layer, b) + (0,) * (nd - 2))

    def layer_block(arr):
        nd = arr.ndim
        return pl.BlockSpec((None,) + tuple(arr.shape[1:]), lambda b, *_: (layer,) + (0,) * (nd - 1))

    out_shapes = [jax.ShapeDtypeStruct((nb * nt, N_BRANCH * BRANCH_WIDTH), F32)] \
        + [jax.ShapeDtypeStruct(a.shape, F32) for a in stacked_states] \
        + [jax.ShapeDtypeStruct((nb * nt, GMLP_WIDTH), F32)]
    operands = (z, ya, s0, hist, kc, vc, *tables, *weights)
    n_in = 1 + len(operands)
    grid_spec = pltpu.PrefetchScalarGridSpec(
        num_scalar_prefetch=1,
        grid=(nb // G,),
        in_specs=[per_group(a.shape) for a in (z, ya)]
                 + [per_group_of_layer(a) for a in (s0, hist, kc, vc)]
                 + [_full_spec(tb.shape) for tb in tables]
                 + [layer_block(w) for w in weights]
                 + [pl.BlockSpec(memory_space=pl.ANY) for _ in stacked_states],
        out_specs=[per_group(out_shapes[0].shape)]
                  + [per_group_of_layer(a) for a in stacked_states]
                  + [per_group(out_shapes[-1].shape)],
        scratch_shapes=[pltpu.VMEM((G, CONV_PAD + R, CONV_WIDTH), F32)])
    return pl.pallas_call(
        functools.partial(_sample_mixer_kernel, layer=layer),
        grid_spec=grid_spec,
        out_shape=out_shapes,
        input_output_aliases={n_in + i: 1 + i for i in range(len(stacked_states))},
        compiler_params=_compiler_params(1),
        name="sample_mixer",
    )(p["sinks"], *operands, *stacked_states)


def _sample_merge_kernel(x_ref, obr_ref, n1g_ref, wg_ref, bg_ref, wbr_ref, wout_ref, y_ref):
    x = x_ref[...]
    xb = _rms_rows(x, n1g_ref[...]).astype(BF16)
    merged = jnp.zeros(x.shape, F32)
    for i in range(N_BRANCH):
        gate = jax.nn.sigmoid(
            _mm(xb, wg_ref[:, i * D_MODEL:(i + 1) * D_MODEL]) + bg_ref[:, i * D_MODEL:(i + 1) * D_MODEL])
        o_i = obr_ref[:, i * BRANCH_WIDTH:(i + 1) * BRANCH_WIDTH].astype(BF16)
        merged = merged + gate * _mm(o_i, wbr_ref[i, :, :D_MODEL])
    y_ref[...] = x + _mm(merged.astype(BF16), wout_ref[:, :D_MODEL])


def _sample_merge(x2d, obr, p, layer):
    weights = tuple(p[k] for k in ("n1g", "wg", "bg", "wbr", "wout"))
    return pl.pallas_call(
        _sample_merge_kernel,
        grid=(1,),
        in_specs=[_full_spec(x2d.shape), _full_spec(obr.shape)] + [_layer_spec(w, layer) for w in weights],
        out_specs=_full_spec(x2d.shape),
        out_shape=jax.ShapeDtypeStruct(x2d.shape, F32),
        compiler_params=_compiler_params(1),
        name="sample_merge",
    )(x2d, obr, *weights)


def _rope_tables(pos):
    half = ROPE_DIM // 2
    inv = jnp.exp(-math.log(ROPE_THETA) * jnp.arange(half, dtype=F32) * (2.0 / ROPE_DIM))
    ang = pos.astype(F32)[:, None] * inv[None, :]
    cos, sin = jnp.cos(ang), jnp.sin(ang)
    n = pos.shape[0]
    pad = jnp.zeros((n, SWA_HEAD_DIM - ROPE_DIM), F32)
    zero = jnp.zeros((n, half), F32)
    c_head = jnp.concatenate([cos, cos, pad + 1.0], axis=1)
    a_head = jnp.concatenate([-sin, zero, pad], axis=1)
    b_head = jnp.concatenate([zero, sin, pad], axis=1)
    rep = LANES // SWA_HEAD_DIM
    return tuple(jnp.tile(tb, (1, rep)) for tb in (c_head, a_head, b_head))


def _lane_pad(w):
    return jnp.pad(w, [(0, 0)] * (w.ndim - 1) + [(0, LANES)]).astype(BF16)


def _rows(v):
    return v.reshape(v.shape[0], 1, -1).astype(F32)


def _stacked_params(norm1_g, w_in, b_in, w_alpha2, b_alpha, gla_norm_g, gmlp_ln_g, gmlp_ln_b,
                    w_spatial, b_spatial, conv_w, conv_b, conv_ln_g, conv_ln_b, q_norm_g, k_norm_g,
                    sinks, w_gate, b_gate, w_branch, w_out, norm2_g, w_gate_up, w_down):
    o_lr = 2 * QK + 2 * QV
    o_rest = o_lr + GLA_RANK

    def repack(m, dtype):
        pad = jnp.zeros(m.shape[:-1] + (LANES - GLA_RANK,), m.dtype)
        return jnp.concatenate(
            [m[..., :o_lr], m[..., o_rest:], m[..., o_lr:o_rest], pad], axis=-1).astype(dtype)

    nt = SAMPLE_TOKENS
    gw = GMLP_WIDTH // GMLP_GROUPS
    pad_rows = ((0, 0), (0, SAMPLE_ROWS - nt), (0, 0))
    wsc = jnp.stack([
        jnp.pad(jnp.repeat(jnp.swapaxes(w_spatial[:, :, :nt, s], 1, 2), gw, axis=2), pad_rows)
        for s in range(nt)], axis=1)
    bsc = jnp.pad(jnp.repeat(jnp.swapaxes(b_spatial[:, :, :nt], 1, 2), gw, axis=2), pad_rows)
    wsh = jnp.stack([
        jnp.pad(conv_w, ((0, 0), (CONV_LEAD + t, CONV_PAD + SAMPLE_ROWS - CONV_LEAD - t - CONV_K), (0, 0)))
        for t in range(nt)], axis=1)
    return {
        "n1g": _rows(norm1_g),
        "win": repack(w_in, BF16),
        "bin": repack(b_in, F32).reshape(DEPTH, 1, -1),
        "wal": jnp.pad(w_alpha2, ((0, 0), (0, LANES - GLA_RANK), (0, 0))).astype(BF16),
        "bal": _rows(b_alpha),
        "glag": _rows(gla_norm_g),
        "lng": _rows(gmlp_ln_g),
        "lnb": _rows(gmlp_ln_b),
        "wsp": w_spatial.astype(F32),
        "bspt": jnp.swapaxes(b_spatial, 1, 2).astype(F32),
        "wsc": wsc.astype(F32),
        "bsc": bsc.astype(F32),
        "cw": jnp.pad(conv_w, ((0, 0), (0, CONV_PAD - CONV_K), (0, 0))).astype(F32),
        "wsh": wsh.astype(F32),
        "cb": _rows(conv_b),
        "clg": _rows(conv_ln_g),
        "clb": _rows(conv_ln_b),
        "qg": _rows(jnp.tile(q_norm_g, (1, SWA_HEADS))),
        "kg": _rows(jnp.tile(k_norm_g, (1, SWA_KV_HEADS))),
        "sinks": sinks.reshape(-1).astype(F32),
        "wg": _lane_pad(w_gate),
        "bg": _rows(b_gate),
        "wbr": _lane_pad(w_branch),
        "wout": _lane_pad(w_out),
        "n2g": _rows(norm2_g),
        "wgu": w_gate_up.astype(BF16),
        "wd": w_down.astype(BF16),
    }


def kernel(x_prompt, x_sample, state_gla, state_conv, cache_swa_k, cache_swa_v, norm1_g, w_in, b_in, w_alpha2, b_alpha, gla_norm_g, gmlp_ln_g, gmlp_ln_b, w_spatial, b_spatial, conv_w, conv_b, conv_ln_g, conv_ln_b, q_norm_g, k_norm_g, sinks, w_gate, b_gate, w_branch, w_out, norm2_g, w_gate_up, w_down):
    bp, lp, _ = x_prompt.shape
    bs, ls, _ = x_sample.shape
    R = SAMPLE_ROWS
    prompt_tables = _rope_tables(jnp.arange(lp))
    sample_tables = tuple(
        jnp.pad(tb, ((0, R - ls), (0, 0))) for tb in _rope_tables(PAST_LEN + jnp.arange(ls)))
    p = _stacked_params(norm1_g, w_in, b_in, w_alpha2, b_alpha, gla_norm_g, gmlp_ln_g, gmlp_ln_b,
                        w_spatial, b_spatial, conv_w, conv_b, conv_ln_g, conv_ln_b, q_norm_g,
                        k_norm_g, sinks, w_gate, b_gate, w_branch, w_out, norm2_g, w_gate_up, w_down)

    gla_in = state_gla.reshape(DEPTH, bs, QK, GLA_DV)
    swa_k_in = cache_swa_k.reshape(DEPTH, bs, WINDOW, SWA_KV)
    swa_v_in = cache_swa_v.reshape(DEPTH, bs, WINDOW, SWA_KV)

    sample_states = [jnp.zeros(a.shape, F32) for a in (gla_in, state_conv, swa_k_in, swa_v_in)]

    yp = x_prompt
    ys = x_sample.reshape(bs * ls, D_MODEL)
    outs = {k: [] for k in ("gla_p", "conv_p", "kp", "vp", "gm")}
    for l in range(DEPTH):
        yp, g1, c1, k1, v1 = _prompt_mixer(yp, p, prompt_tables, l)
        yp = _ffn(yp.reshape(bp * lp, D_MODEL), p, l).reshape(bp, lp, D_MODEL)
        outs["gla_p"].append(g1.reshape(bp, GLA_HEADS, GLA_DK, GLA_DV))
        outs["conv_p"].append(c1)
        outs["kp"].append(k1.reshape(bp, WINDOW, SWA_KV_HEADS, SWA_HEAD_DIM))
        outs["vp"].append(v1.reshape(bp, WINDOW, SWA_KV_HEADS, SWA_HEAD_DIM))
        z, ya = _sample_proj(ys, p, l)
        obr, *sample_states, gv2 = _sample_mixer(
            z, ya, gla_in, state_conv, swa_k_in, swa_v_in, p, sample_tables, sample_states, l)
        ys = _sample_merge(ys, obr, p, l)
        ys = _ffn(ys, p, l)
        outs["gm"].append(gv2.reshape(bs, ls, GMLP_WIDTH))
    st = lambda name: jnp.stack(outs[name], 0)
    gla_s, conv_s, ks, vs = sample_states
    kv_shape = (DEPTH, bs, WINDOW, SWA_KV_HEADS, SWA_HEAD_DIM)
    return (yp, ys.reshape(bs, ls, D_MODEL), st("gla_p"), gla_s.reshape(state_gla.shape), st("conv_p"),
            conv_s, st("kp"), st("vp"), ks.reshape(kv_shape), vs.reshape(kv_shape), st("gm"))
```

```python
import functools
import math

import jax
import jax.numpy as jnp
from jax import lax
from jax.experimental import pallas as pl
from jax.experimental.pallas import tpu as pltpu

F32 = jnp.float32
BF16 = jnp.bfloat16

D_MODEL = 1024
DEPTH = 2
PAST_LEN = 16384
GLA_HEADS = 4
GLA_DK = 64
GLA_DV = 128
GLA_RANK = 16
GLA_TAU = 16.0
GLA_CHUNK = 64
GMLP_GROUPS = 4
GMLP_WIDTH = 512
GMLP_CHUNK = 128
CONV_WIDTH = 512
CONV_K = 31
SWA_HEADS = 8
SWA_KV_HEADS = 2
SWA_GROUP = SWA_HEADS // SWA_KV_HEADS
SWA_HEAD_DIM = 64
WINDOW = 128
ROPE_DIM = SWA_HEAD_DIM // 4
ROPE_THETA = 500000.0
N_BRANCH = 4
BRANCH_WIDTH = 512
FFN_HIDDEN = 2816
NORM_EPS = 1e-6
LN_EPS = 1e-5

LANES = 128
SUBLANES = 8
VMEM_LIMIT_BYTES = 56 * 1024 * 1024

QK = GLA_HEADS * GLA_DK
QV = GLA_HEADS * GLA_DV
SWA_Q = SWA_HEADS * SWA_HEAD_DIM
SWA_KV = SWA_KV_HEADS * SWA_HEAD_DIM

C_AQ = 0
C_AK = C_AQ + QK
C_AV = C_AK + QK
C_AR = C_AV + QV
C_BU = C_AR + QV
C_BV = C_BU + GMLP_WIDTH
C_CA = C_BV + GMLP_WIDTH
C_CG = C_CA + CONV_WIDTH
C_DQ = C_CG + CONV_WIDTH
C_DK = C_DQ + SWA_Q
C_DV = C_DK + SWA_KV
C_LR = C_DV + SWA_KV
IN_COLS_PACKED = C_LR + LANES

PROMPT_TILE = 256
FFN_TILE = 1024
FFN_CHUNK = 256
GATE_PART = 256
CONV_PAD = 32
CONV_LEAD = CONV_PAD - (CONV_K - 1)
CONV_ROWS = 32
SAMPLE_ROWS = 8
SAMPLE_TOKENS = 4
SAMPLE_GROUP = 8


def _mm(a, b):
    return jnp.dot(a, b, preferred_element_type=F32)


def _mm_nt(a, b):
    return lax.dot_general(a, b, (((1,), (1,)), ((), ())), preferred_element_type=F32)


def _mm_tn(a, b):
    return lax.dot_general(a, b, (((0,), (0,)), ((), ())), preferred_element_type=F32)


def _hi_lo(a):
    hi = a.astype(BF16)
    lo = (a - hi.astype(F32)).astype(BF16)
    return hi, lo


def _mm_split_lhs(a_f32, m_bf16):
    hi, lo = _hi_lo(a_f32)
    return _mm(hi, m_bf16) + _mm(lo, m_bf16)


def _rms_rows(x, g):
    return x * lax.rsqrt(jnp.mean(x * x, axis=-1, keepdims=True) + NORM_EPS) * g


def _ln_rows(x, g, b):
    mu = jnp.mean(x, axis=-1, keepdims=True)
    xc = x - mu
    var = jnp.mean(xc * xc, axis=-1, keepdims=True)
    return xc * lax.rsqrt(var + LN_EPS) * g + b


def _gelu(x):
    return 0.5 * x * (1.0 + lax.erf(x * (1.0 / math.sqrt(2.0))))


def _silu(x):
    return x * jax.nn.sigmoid(x)


def _log_sigmoid(x):
    return jnp.minimum(x, 0.0) - jnp.log(1.0 + jnp.exp(-jnp.abs(x)))


def _iota(shape, dim):
    return lax.broadcasted_iota(jnp.int32, shape, dim)


def _group_ones(n, group, dtype=BF16):
    return (_iota((n, n), 0) // group == _iota((n, n), 1) // group).astype(dtype)


def _group_mean_sq(x, group):
    return _mm_split_lhs(x * x, _group_ones(x.shape[-1], group)) * (1.0 / group)


def _rope_slab(x, c, a, b):
    return x * c + pltpu.roll(x, LANES - ROPE_DIM // 2, 1) * a + pltpu.roll(x, ROPE_DIM // 2, 1) * b


def _rope(x, c, a, b):
    n = x.shape[-1] // LANES
    if n == 1:
        return _rope_slab(x, c, a, b)
    return jnp.concatenate(
        [_rope_slab(x[:, i * LANES:(i + 1) * LANES], c, a, b) for i in range(n)], axis=1)


def _qk_norm_rope(z, g, c, a, b, split=True):
    if split:
        ms = _group_mean_sq(z, SWA_HEAD_DIM)
    else:
        ms = _mm((z * z).astype(BF16), _group_ones(z.shape[-1], SWA_HEAD_DIM)) * (1.0 / SWA_HEAD_DIM)
    return _rope(z * lax.rsqrt(ms + NORM_EPS) * g, c, a, b)


def _head_rms_gla(o, g):
    return jnp.concatenate(
        [_rms_rows(o[:, h * GLA_DV:(h + 1) * GLA_DV], g) for h in range(GLA_HEADS)], axis=1)


def _swa_q_for_group(q, g, i):
    hq = g * SWA_GROUP + i
    slab = q[:, (hq // 2) * LANES:(hq // 2 + 1) * LANES]
    if hq % 2 != g:
        slab = pltpu.roll(slab, SWA_HEAD_DIM, 1)
    lane_head = _iota(slab.shape, 1) // SWA_HEAD_DIM
    return jnp.where(lane_head == g, slab, 0.0)


def _swa_assemble(outs, rows):
    lane_half = _iota((rows, LANES), 1) // SWA_HEAD_DIM
    slabs = []
    for s in range(SWA_HEADS // 2):
        pair = []
        for hq in (2 * s, 2 * s + 1):
            g, i = hq // SWA_GROUP, hq % SWA_GROUP
            o = outs[g][i]
            if hq % 2 != g:
                o = pltpu.roll(o, SWA_HEAD_DIM, 1)
            pair.append(o)
        slabs.append(jnp.where(lane_half == 0, pair[0], pair[1]))
    return jnp.concatenate(slabs, axis=1)


def _sink_column(sinks_ref, base, rows, rows_per_head):
    row_head = _iota((rows, 1), 0) // rows_per_head
    sink = jnp.zeros((rows, 1), F32)
    for i in range(SWA_GROUP):
        sink = jnp.where(row_head == i, sinks_ref[base + i], sink)
    return sink


def _prompt_mixer_kernel(
        sinks_ref,
        x_ref, rope_ref,
        n1g_ref, win_ref, bin_ref, wal_ref, bal_ref, glag_ref,
        lng_ref, lnb_ref, wsp_ref, bspt_ref,
        cw_ref, cb_ref, clg_ref, clb_ref,
        qg_ref, kg_ref,
        wg_ref, bg_ref, wbr_ref, wout_ref,
        y_ref, gla_ref, conv_ref, ko_ref, vo_ref,
        gstate_ref, hbuf_ref, hsh_ref, kfull_ref, vfull_ref, gates_ref,
        *, layer):
    T = PROMPT_TILE
    t = pl.program_id(1)
    last = pl.num_programs(1) - 1

    @pl.when(t == 0)
    def _():
        gstate_ref[...] = jnp.zeros_like(gstate_ref)
        hbuf_ref[0:CONV_PAD, :] = jnp.zeros((CONV_PAD, CONV_WIDTH), F32)
        kfull_ref[0:WINDOW, :] = jnp.zeros((WINDOW, SWA_KV), F32)
        vfull_ref[0:WINDOW, :] = jnp.zeros((WINDOW, SWA_KV), F32)

    x = x_ref[0]
    xb = _rms_rows(x, n1g_ref[...]).astype(BF16)

    def proj(c0, width):
        return _mm(xb, win_ref[:, c0:c0 + width]) + bin_ref[:, c0:c0 + width]

    def gate_part(i, j):
        c0 = i * D_MODEL + j * GATE_PART
        gates_ref[i, :, j * GATE_PART:(j + 1) * GATE_PART] = jax.nn.sigmoid(
            _mm(xb, wg_ref[:, c0:c0 + GATE_PART]) + bg_ref[:, c0:c0 + GATE_PART])

    half = D_MODEL // 2
    partial = {}
    shared = {}

    def branch_half(i, o_b16, j):
        partial[(i, j)] = _mm(o_b16, wbr_ref[i, :, j * half:(j + 1) * half])

    def branch_c():
        h = proj(C_CA, CONV_WIDTH) * jax.nn.sigmoid(proj(C_CG, CONV_WIDTH))
        hbuf_ref[CONV_PAD:CONV_PAD + T, :] = h
        taps_of = {r: [kk for kk in range(CONV_K) if (CONV_LEAD + kk) % SUBLANES == r] for r in range(SUBLANES)}
        for r in range(1, SUBLANES):
            span = max(CONV_LEAD + kk for kk in taps_of[r]) - r
            hsh_ref[r - 1, 0:span + T, :] = hbuf_ref[r:r + span + T, :]
        yield
        acc_blocks = []
        for rb in range(T // CONV_ROWS):
            acc = jnp.broadcast_to(cb_ref[...], (CONV_ROWS, CONV_WIDTH))
            for r_ in range(SUBLANES):
                for kk in taps_of[r_]:
                    a = CONV_LEAD + kk - r_ + rb * CONV_ROWS
                    rows_k = hbuf_ref[a:a + CONV_ROWS, :] if r_ == 0 else hsh_ref[r_ - 1, a:a + CONV_ROWS, :]
                    acc = acc + cw_ref[kk:kk + 1, :] * rows_k
            acc_blocks.append(acc)
            yield
        hbuf_ref[0:CONV_PAD, :] = hbuf_ref[T:T + CONV_PAD, :]
        o_c = _silu(_ln_rows(jnp.concatenate(acc_blocks, axis=0), clg_ref[...], clb_ref[...])).astype(BF16)
        yield
        branch_half(2, o_c, 0)
        yield
        branch_half(2, o_c, 1)

    def branch_a():
        q = proj(C_AQ, QK) * (GLA_DK ** -0.5)
        k = proj(C_AK, QK)
        v = proj(C_AV, QV)
        kvl = proj(C_DK, 2 * SWA_KV + LANES)
        shared["zk"], shared["zvd"], lr = kvl[:, :SWA_KV], kvl[:, SWA_KV:2 * SWA_KV], kvl[:, 2 * SWA_KV:]
        yield
        la = _log_sigmoid(_mm(lr.astype(BF16), wal_ref[...]) + bal_ref[...]) * (1.0 / GLA_TAU)
        r = proj(C_AR, QV)
        yield
        yield from branch_a_core(q, k, v, la, r)

    def branch_a_core(q, k, v, la, r):
        n_chunks = T // GLA_CHUNK
        ri = _iota((T, T), 0)
        ci = _iota((T, T), 1)
        chunk_causal = jnp.logical_and((ri // GLA_CHUNK) == (ci // GLA_CHUNK), ci <= ri)
        la_hi, la_lo = _hi_lo(la)
        m_tril = chunk_causal.astype(BF16)
        bcum = _mm(m_tril, la_hi) + _mm(m_tril, la_lo)
        yield
        tot_rows = [bcum[(c + 1) * GLA_CHUNK - 1:(c + 1) * GLA_CHUNK, :] for c in range(n_chunks)]
        btot = jnp.concatenate([jnp.broadcast_to(tr, (GLA_CHUNK, QK)) for tr in tot_rows], axis=0)
        q_in = (q * jnp.exp(bcum)).astype(BF16)
        k_out = (k * jnp.exp(-bcum)).astype(BF16)
        k_dec_t = (k * jnp.exp(btot - bcum)).T.astype(BF16)
        v_b = v.astype(BF16)
        dec_t = jnp.exp(jnp.concatenate(
            [jnp.broadcast_to(tr, (GLA_DV, QK)) for tr in tot_rows], axis=0).T)

        lane_head = _iota((T, QK), 1) // GLA_DK
        o_heads = []
        for hh in range(GLA_HEADS):
            qh = jnp.where(lane_head == hh, q_in, jnp.zeros_like(q_in))
            att = jnp.where(chunk_causal, _mm_nt(qh, k_out), 0.0).astype(BF16)
            o_heads.append(_mm(att, v_b[:, hh * GLA_DV:(hh + 1) * GLA_DV]))
            yield
        o_intra = jnp.concatenate(o_heads, axis=1)

        bd_mask = (_iota((QK, QV), 0) // GLA_DK) == (_iota((QK, QV), 1) // GLA_DV)
        tok_chunk = _iota((GLA_DK, T), 1) // GLA_CHUNK
        o_state = []
        for c in range(n_chunks):
            rs = slice(c * GLA_CHUNK, (c + 1) * GLA_CHUNK)
            s_c = gstate_ref[...]
            s_b = s_c.astype(BF16)
            s_bd = jnp.where(bd_mask, jnp.concatenate([s_b] * GLA_HEADS, axis=1), jnp.zeros((QK, QV), BF16))
            o_state.append(_mm(q_in[rs], s_bd))
            upd = []
            for hh in range(GLA_HEADS):
                kt = k_dec_t[hh * GLA_DK:(hh + 1) * GLA_DK, :]
                kt = jnp.where(tok_chunk == c, kt, jnp.zeros_like(kt))
                upd.append(_mm(kt, v_b[:, hh * GLA_DV:(hh + 1) * GLA_DV]))
            gstate_ref[...] = dec_t[:, c * GLA_DV:(c + 1) * GLA_DV] * s_c + jnp.concatenate(upd, axis=0)
            yield
        o_gla = o_intra + jnp.concatenate(o_state, axis=0)
        o_a = (_head_rms_gla(o_gla, glag_ref[...]) * _silu(r)).astype(BF16)
        yield
        branch_half(0, o_a, 0)
        yield
        branch_half(0, o_a, 1)

    def branch_b():
        zu = proj(C_BU, GMLP_WIDTH)
        yield
        zgv = proj(C_BV, GMLP_WIDTH)
        yield
        u = _gelu(zu)
        gv = _ln_rows(_gelu(zgv), lng_ref[...], lnb_ref[...])
        gv_b = gv.astype(BF16)
        gw = GMLP_WIDTH // GMLP_GROUPS
        tril = _iota((GMLP_CHUNK, GMLP_CHUNK), 1) <= _iota((GMLP_CHUNK, GMLP_CHUNK), 0)
        bspt = bspt_ref[...]
        s_rows = []
        for n in range(T // GMLP_CHUNK):
            rs = slice(n * GMLP_CHUNK, (n + 1) * GMLP_CHUNK)
            cols = []
            for g in range(GMLP_GROUPS):
                w = jnp.where(tril, wsp_ref[g], 0.0).astype(BF16)
                sg = _mm(w, gv_b[rs, g * gw:(g + 1) * gw])
                cols.append(sg + jnp.broadcast_to(bspt[:, g:g + 1], (GMLP_CHUNK, gw)))
            s_rows.append(jnp.concatenate(cols, axis=1))
            yield
        o_b = (u * jnp.concatenate(s_rows, axis=0)).astype(BF16)
        yield
        branch_half(1, o_b, 0)
        yield
        branch_half(1, o_b, 1)

    def branch_d():
        zq = proj(C_DQ, SWA_Q)
        yield
        rc, ra, rb = (rope_ref[:, i * LANES:(i + 1) * LANES] for i in range(3))
        qd = _qk_norm_rope(zq, qg_ref[...], rc, ra, rb, split=False) * (SWA_HEAD_DIM ** -0.5)
        kfull_ref[WINDOW:WINDOW + T, :] = _qk_norm_rope(shared["zk"], kg_ref[...], rc, ra, rb)
        vfull_ref[WINDOW:WINDOW + T, :] = shared["zvd"]
        yield
        rows = SWA_GROUP * WINDOW
        own_key = _iota((rows, WINDOW), 1) <= _iota((rows, WINDOW), 0) % WINDOW
        o_blocks = []
        for qb in range(T // WINDOW):
            rs = slice(qb * WINDOW, (qb + 1) * WINDOW)
            kblk = kfull_ref[qb * WINDOW:(qb + 2) * WINDOW, :].astype(BF16)
            vblk = vfull_ref[qb * WINDOW:(qb + 2) * WINDOW, :].astype(BF16)
            outs = []
            for g in range(SWA_KV_HEADS):
                q_stack = jnp.concatenate(
                    [_swa_q_for_group(qd[rs], g, i) for i in range(SWA_GROUP)], axis=0).astype(BF16)
                s2 = _mm_nt(q_stack, kblk)
                yield
                s = jnp.where(own_key, s2[:, WINDOW:], s2[:, :WINDOW])
                if qb == 0:
                    s = jnp.where(jnp.logical_or(own_key, t > 0), s, -jnp.inf)
                sink = _sink_column(sinks_ref, layer * SWA_HEADS + g * SWA_GROUP, rows, WINDOW)
                m = jnp.maximum(jnp.max(s, axis=-1, keepdims=True), sink)
                p = jnp.exp(s - m)
                den = jnp.sum(p, axis=-1, keepdims=True) + jnp.exp(sink - m)
                p2 = jnp.concatenate([jnp.where(own_key, 0.0, p), jnp.where(own_key, p, 0.0)], axis=1)
                o = _mm(p2.astype(BF16), vblk) / den
                outs.append([o[i * WINDOW:(i + 1) * WINDOW] for i in range(SWA_GROUP)])
                yield
            o_blocks.append(_swa_assemble(outs, WINDOW))
        o_d = jnp.concatenate(o_blocks, axis=0).astype(BF16)
        kfull_ref[0:WINDOW, :] = kfull_ref[T:T + WINDOW, :]
        vfull_ref[0:WINDOW, :] = vfull_ref[T:T + WINDOW, :]
        yield
        branch_half(3, o_d, 0)
        yield
        branch_half(3, o_d, 1)

    def gate_parts():
        for i in range(N_BRANCH):
            for j in range(0, D_MODEL // GATE_PART, 2):
                gate_part(i, j)
                gate_part(i, j + 1)
                yield

    stages = [branch_a(), branch_c(), branch_b(), branch_d(), gate_parts()]
    while stages:
        stages = [stage for stage in stages if next(stage, StopIteration) is not StopIteration]

    merged = jnp.zeros((T, D_MODEL), F32)
    for i in range(N_BRANCH):
        merged = merged + gates_ref[i] * jnp.concatenate([partial[(i, 0)], partial[(i, 1)]], axis=1)
    y_ref[0] = x + _mm(merged.astype(BF16), wout_ref[:, :D_MODEL])

    @pl.when(t == last)
    def _():
        conv_ref[0] = hbuf_ref[CONV_LEAD:CONV_PAD, :]
        ko_ref[0] = kfull_ref[0:WINDOW, :]
        vo_ref[0] = vfull_ref[0:WINDOW, :]
        gla_ref[0] = gstate_ref[...]


def _layer_spec(arr, layer):
    nd = arr.ndim
    return pl.BlockSpec((None,) + tuple(arr.shape[1:]), lambda *_: (layer,) + (0,) * (nd - 1),
                        pipeline_mode=pl.Buffered(1))


def _full_spec(shape):
    nd = len(shape)
    return pl.BlockSpec(shape, lambda *_: (0,) * nd)


def _compiler_params(n_axes):
    return pltpu.CompilerParams(
        dimension_semantics=("arbitrary",) * n_axes, vmem_limit_bytes=VMEM_LIMIT_BYTES)


def _prompt_mixer(x, p, tables, layer):
    B, L, _ = x.shape
    T = PROMPT_TILE
    weights = tuple(p[n] for n in (
        "n1g", "win", "bin", "wal", "bal", "glag", "lng", "lnb", "wsp", "bspt",
        "cw", "cb", "clg", "clb", "qg", "kg", "wg", "bg", "wbr", "wout"))
    tab_spec = pl.BlockSpec((T, 3 * LANES), lambda b, t, *_: (t, 0))

    grid_spec = pltpu.PrefetchScalarGridSpec(
        num_scalar_prefetch=1,
        grid=(B, L // T),
        in_specs=[pl.BlockSpec((1, T, D_MODEL), lambda b, t, *_: (b, t, 0)),
                  tab_spec]
                 + [_layer_spec(w, layer) for w in weights],
        out_specs=[pl.BlockSpec((1, T, D_MODEL), lambda b, t, *_: (b, t, 0)),
                   pl.BlockSpec((1, QK, GLA_DV), lambda b, t, *_: (b, 0, 0)),
                   pl.BlockSpec((1, CONV_K - 1, CONV_WIDTH), lambda b, t, *_: (b, 0, 0)),
                   pl.BlockSpec((1, WINDOW, SWA_KV), lambda b, t, *_: (b, 0, 0)),
                   pl.BlockSpec((1, WINDOW, SWA_KV), lambda b, t, *_: (b, 0, 0))],
        scratch_shapes=[pltpu.VMEM((QK, GLA_DV), F32),
                        pltpu.VMEM((T + CONV_PAD, CONV_WIDTH), F32),
                        pltpu.VMEM((SUBLANES - 1, T + CONV_PAD, CONV_WIDTH), F32),
                        pltpu.VMEM((T + WINDOW, SWA_KV), F32),
                        pltpu.VMEM((T + WINDOW, SWA_KV), F32),
                        pltpu.VMEM((N_BRANCH, T, D_MODEL), F32)])
    return pl.pallas_call(
        functools.partial(_prompt_mixer_kernel, layer=layer),
        grid_spec=grid_spec,
        out_shape=[jax.ShapeDtypeStruct((B, L, D_MODEL), F32),
                   jax.ShapeDtypeStruct((B, QK, GLA_DV), F32),
                   jax.ShapeDtypeStruct((B, CONV_K - 1, CONV_WIDTH), F32),
                   jax.ShapeDtypeStruct((B, WINDOW, SWA_KV), F32),
                   jax.ShapeDtypeStruct((B, WINDOW, SWA_KV), F32)],
        compiler_params=_compiler_params(2),
        name="prompt_mixer",
    )(p["sinks"], x, jnp.concatenate(tables, axis=1), *weights)


def _ffn_kernel(x_ref, g_ref, wgu_ref, wd_ref, y_ref, h_ref):
    x = x_ref[...]
    xb = _rms_rows(x, g_ref[...]).astype(BF16)
    for c in range(FFN_HIDDEN // FFN_CHUNK):
        c0 = c * FFN_CHUNK
        gate = _mm(xb, wgu_ref[:, c0:c0 + FFN_CHUNK])
        up = _mm(xb, wgu_ref[:, FFN_HIDDEN + c0:FFN_HIDDEN + c0 + FFN_CHUNK])
        h_ref[:, c0:c0 + FFN_CHUNK] = (_silu(gate) * up).astype(BF16)
    y_ref[...] = x + _mm(h_ref[...], wd_ref[...])


def _ffn(x2d, p, layer):
    n = x2d.shape[0]
    tile = min(FFN_TILE, n)
    return pl.pallas_call(
        _ffn_kernel,
        grid=(n // tile,),
        in_specs=[pl.BlockSpec((tile, D_MODEL), lambda i: (i, 0)),
                  _layer_spec(p["n2g"], layer), _layer_spec(p["wgu"], layer), _layer_spec(p["wd"], layer)],
        out_specs=pl.BlockSpec((tile, D_MODEL), lambda i: (i, 0)),
        out_shape=jax.ShapeDtypeStruct((n, D_MODEL), F32),
        scratch_shapes=[pltpu.VMEM((tile, FFN_HIDDEN), BF16)],
        compiler_params=_compiler_params(1),
        name="swiglu",
    )(x2d, p["n2g"], p["wgu"], p["wd"])


def _sample_proj_kernel(x_ref, n1g_ref, win_ref, bin_ref, wal_ref, bal_ref, z_ref, ya_ref):
    xb = _rms_rows(x_ref[...], n1g_ref[...]).astype(BF16)
    z = _mm(xb, win_ref[...]) + bin_ref[...]
    z_ref[...] = z
    ya_ref[...] = _mm(z[:, C_LR:C_LR + LANES].astype(BF16), wal_ref[...]) + bal_ref[...]


def _sample_proj(x2d, p, layer):
    n = x2d.shape[0]
    weights = tuple(p[k] for k in ("n1g", "win", "bin", "wal", "bal"))
    return pl.pallas_call(
        _sample_proj_kernel,
        grid=(1,),
        in_specs=[_full_spec(x2d.shape)] + [_layer_spec(w, layer) for w in weights],
        out_specs=[_full_spec((n, IN_COLS_PACKED)), _full_spec((n, QK))],
        out_shape=[jax.ShapeDtypeStruct((n, IN_COLS_PACKED), F32),
                   jax.ShapeDtypeStruct((n, QK), F32)],
        compiler_params=_compiler_params(1),
        name="sample_proj",
    )(x2d, *weights)


def _sample_sequence(j, layer, sinks_ref, z_ref, ya_ref, s0_ref, hist_ref, kc_ref, vc_ref,
                     rc, ra, rb, glag, lng, lnb, wsc_ref, bsc, wsh_ref, cb, clg, clb, qg, kg,
                     obr_ref, s1_ref, convo_ref, ko_ref, vo_ref, gvo_ref, hext_ref):
    R = SAMPLE_ROWS
    nt = SAMPLE_TOKENS
    row = _iota((R, 1), 0)

    def seq_rows(ref):
        x = ref[(j // 2) * R:(j // 2 + 1) * R, :]
        if j % 2:
            x = pltpu.roll(x, R - nt, 0)
        return jnp.where(row < nt, x, 0.0)

    z = seq_rows(z_ref)
    head_ones = _group_ones(SWA_Q, SWA_HEAD_DIM)

    q = z[:, C_AQ:C_AQ + QK] * (GLA_DK ** -0.5)
    k = z[:, C_AK:C_AK + QK]
    v = z[:, C_AV:C_AV + QV]
    r = z[:, C_AR:C_AR + QV]
    zq = z[:, C_DQ:C_DQ + SWA_Q]
    zk = z[:, C_DK:C_DK + SWA_KV]
    zv = z[:, C_DV:C_DV + SWA_KV]
    ms_q = _mm_split_lhs(zq * zq, head_ones) * (1.0 / SWA_HEAD_DIM)
    ms_k = _mm_split_lhs(zk * zk, head_ones[:SWA_KV, :SWA_KV]) * (1.0 / SWA_HEAD_DIM)
    la = jnp.where(row < nt, _log_sigmoid(seq_rows(ya_ref)) * (1.0 / GLA_TAU), 0.0)
    bcum = jnp.zeros_like(la)
    for s in range(nt):
        bcum = bcum + jnp.where(row >= s, jnp.broadcast_to(la[s:s + 1, :], la.shape), 0.0)
    btot_row = bcum[nt - 1:nt, :]
    q_in = q * jnp.exp(bcum)
    k_out = k * jnp.exp(-bcum)
    k_dec = k * jnp.exp(jnp.broadcast_to(btot_row, la.shape) - bcum)
    s0 = s0_ref[j]
    s0_b = s0.astype(BF16)
    head_sel = (_iota((QK, LANES), 0) // GLA_DK == _iota((QK, LANES), 1)).astype(BF16)
    head_exp = (_iota((LANES, QV), 0) == _iota((LANES, QV), 1) // GLA_DV).astype(BF16)
    lane_head = _iota((R, QK), 1) // GLA_DK
    att_s = [_mm_split_lhs(q_in * jnp.broadcast_to(k_out[s:s + 1, :], q_in.shape), head_sel)
             for s in range(nt)]
    o_state = [_mm(jnp.where(lane_head == hh, q_in, 0.0).astype(BF16), s0_b) for hh in range(GLA_HEADS)]
    upd = _mm_tn(k_dec.astype(BF16).astype(F32), v.astype(BF16).astype(F32))
    dcol = jnp.exp(jnp.broadcast_to(btot_row, (GLA_DV, QK)).T)

    u = _gelu(z[:, C_BU:C_BU + GMLP_WIDTH])
    gv = _ln_rows(_gelu(z[:, C_BV:C_BV + GMLP_WIDTH]), lng, lnb)
    gvo_ref[j * nt:(j + 1) * nt, :] = gv[0:nt, :]
    sp = bsc
    for s in range(nt):
        sp = sp + jnp.where(row >= s, wsc_ref[s], 0.0) * jnp.broadcast_to(gv[s:s + 1, :], gv.shape)
    o_b = u * sp

    h = z[:, C_CA:C_CA + CONV_WIDTH] * jax.nn.sigmoid(z[:, C_CG:C_CG + CONV_WIDTH])
    hext_ref[j, 0:SUBLANES, :] = jnp.zeros((SUBLANES, CONV_WIDTH), F32)
    hext_ref[j, CONV_LEAD:CONV_PAD, :] = hist_ref[j]
    hext_ref[j, CONV_PAD:CONV_PAD + R, :] = h
    yield

    hext = hext_ref[j]
    conv = jnp.zeros((R, CONV_WIDTH), F32)
    for tt in range(nt):
        c_t = jnp.sum(hext * wsh_ref[tt], axis=0, keepdims=True)
        conv = jnp.where(row == tt, jnp.broadcast_to(c_t, conv.shape), conv)
    o_c = _silu(_ln_rows(conv + cb, clg, clb))
    convo_ref[j] = hext_ref[j, CONV_PAD + nt - (CONV_K - 1):CONV_PAD + nt, :]

    att_e = [_mm(a.astype(BF16), head_exp) for a in att_s]
    s1_ref[j] = dcol * s0 + jnp.concatenate(
        [upd[hh * GLA_DK:(hh + 1) * GLA_DK, hh * GLA_DV:(hh + 1) * GLA_DV] for hh in range(GLA_HEADS)],
        axis=0)

    qd = _rope(zq * lax.rsqrt(ms_q + NORM_EPS) * qg, rc, ra, rb)
    k_new = _rope(zk * lax.rsqrt(ms_k + NORM_EPS) * kg, rc, ra, rb)
    k_hist_b = kc_ref[j].astype(BF16)
    v_hist_b = vc_ref[j].astype(BF16)
    rows = SWA_GROUP * R
    qt = _iota((rows, WINDOW), 0) % R
    hist_ok = _iota((rows, WINDOW), 1) > qt
    qt1 = _iota((rows, 1), 0) % R
    scale = SWA_HEAD_DIM ** -0.5
    q_stacks, s_hists, s_news = [], [], []
    for g in range(SWA_KV_HEADS):
        q_stack = jnp.concatenate([_swa_q_for_group(qd, g, i) for i in range(SWA_GROUP)], axis=0)
        q_stacks.append(q_stack)
        s_hists.append(_mm_nt(q_stack.astype(BF16), k_hist_b))
        s_news.append([jnp.sum(q_stack * jnp.broadcast_to(k_new[s:s + 1, :], q_stack.shape),
                               axis=-1, keepdims=True) for s in range(nt)])
    yield

    o_intra = jnp.zeros((R, QV), F32)
    for s in range(nt):
        o_intra = o_intra + jnp.where(row >= s, att_e[s], 0.0) * jnp.broadcast_to(v[s:s + 1, :], (R, QV))
    o_a = _head_rms_gla(o_intra + jnp.concatenate(o_state, axis=1), glag) * _silu(r)

    sinks, ms, s_masked, n_masked = [], [], [], []
    for g in range(SWA_KV_HEADS):
        s_hist = jnp.where(hist_ok, s_hists[g] * scale, -jnp.inf)
        s_new = [jnp.where(qt1 >= s, s_news[g][s] * scale, -jnp.inf) for s in range(nt)]
        sink = _sink_column(sinks_ref, layer * SWA_HEADS + g * SWA_GROUP, rows, R)
        m = jnp.maximum(jnp.max(s_hist, axis=-1, keepdims=True), sink)
        for s in range(nt):
            m = jnp.maximum(m, s_new[s])
        sinks.append(sink)
        ms.append(m)
        s_masked.append(s_hist)
        n_masked.append(s_new)
    yield

    dens, pvs, p_news = [], [], []
    for g in range(SWA_KV_HEADS):
        p_hist = jnp.exp(s_masked[g] - ms[g])
        dens.append(jnp.sum(p_hist, axis=-1, keepdims=True) + jnp.exp(sinks[g] - ms[g]))
        pvs.append(_mm(p_hist.astype(BF16), v_hist_b))
        p_news.append([jnp.exp(n_masked[g][s] - ms[g]) for s in range(nt)])
    yield

    outs = []
    for g in range(SWA_KV_HEADS):
        o, den = pvs[g], dens[g]
        for s in range(nt):
            den = den + p_news[g][s]
            o = o + p_news[g][s] * jnp.broadcast_to(zv[s:s + 1, :], o.shape)
        o = o / den
        outs.append([o[i * R:(i + 1) * R] for i in range(SWA_GROUP)])
    o_d = _swa_assemble(outs, R)

    ko_ref[j, 0:WINDOW - nt, :] = kc_ref[j, nt:WINDOW, :]
    ko_ref[j, WINDOW - nt:WINDOW, :] = k_new[0:nt, :]
    vo_ref[j, 0:WINDOW - nt, :] = vc_ref[j, nt:WINDOW, :]
    vo_ref[j, WINDOW - nt:WINDOW, :] = zv[0:nt, :]

    obr_ref[j * nt:(j + 1) * nt, :] = jnp.concatenate([o_a, o_b, o_c, o_d], axis=1)[0:nt, :]


def _sample_mixer_kernel(
        sinks_ref,
        z_ref, ya_ref, s0_ref, hist_ref, kc_ref, vc_ref,
        rc_ref, ra_ref, rb_ref,
        glag_ref, lng_ref, lnb_ref, wsc_ref, bsc_ref,
        wsh_ref, cb_ref, clg_ref, clb_ref, qg_ref, kg_ref,
        s1_all_ref, convo_all_ref, ko_all_ref, vo_all_ref,
        obr_ref, s1_ref, convo_ref, ko_ref, vo_ref, gvo_ref,
        hext_ref, *, layer):
    del s1_all_ref, convo_all_ref, ko_all_ref, vo_all_ref
    consts = (rc_ref[...], ra_ref[...], rb_ref[...], glag_ref[...], lng_ref[...], lnb_ref[...])
    sequences = [
        _sample_sequence(
            j, layer, sinks_ref, z_ref, ya_ref, s0_ref, hist_ref, kc_ref, vc_ref,
            *consts, wsc_ref, bsc_ref[...], wsh_ref, cb_ref[...], clg_ref[...], clb_ref[...],
            qg_ref[...], kg_ref[...],
            obr_ref, s1_ref, convo_ref, ko_ref, vo_ref, gvo_ref, hext_ref)
        for j in range(SAMPLE_GROUP)]
    while sequences:
        sequences = [seq for seq in sequences if next(seq, StopIteration) is not StopIteration]


def _sample_mixer(z, ya, s0, hist, kc, vc, p, tables, stacked_states, layer):
    nt = SAMPLE_TOKENS
    nb = z.shape[0] // nt
    R = SAMPLE_ROWS
    G = SAMPLE_GROUP
    weights = tuple(p[n] for n in ("glag", "lng", "lnb", "wsc", "bsc", "wsh", "cb", "clg", "clb", "qg", "kg"))

    def per_group(shape):
        return pl.BlockSpec((G * nt, shape[1]), lambda b, *_: (b, 0))

    def per_group_of_layer(arr):
        nd = arr.ndim
        return pl.BlockSpec((None, G) + tuple(arr.shape[2:]), lambda b, *_: (layer, b) + (0,) * (nd - 2))

    def layer_block(arr):
        nd = arr.ndim
        return pl.BlockSpec((None,) + tuple(arr.shape[1:]), lambda b, *_: (layer,) + (0,) * (nd - 1))

    out_shapes = [jax.ShapeDtypeStruct((nb * nt, N_BRANCH * BRANCH_WIDTH), F32)] \
        + [jax.ShapeDtypeStruct(a.shape, F32) for a in stacked_states] \
        + [jax.ShapeDtypeStruct((nb * nt, GMLP_WIDTH), F32)]
    operands = (z, ya, s0, hist, kc, vc, *tables, *weights)
    n_in = 1 + len(operands)
    grid_spec = pltpu.PrefetchScalarGridSpec(
        num_scalar_prefetch=1,
        grid=(nb // G,),
        in_specs=[per_group(a.shape) for a in (z, ya)]
                 + [per_group_of_layer(a) for a in (s0, hist, kc, vc)]
                 + [_full_spec(tb.shape) for tb in tables]
                 + [layer_block(w) for w in weights]
                 + [pl.BlockSpec(memory_space=pl.ANY) for _ in stacked_states],
        out_specs=[per_group(out_shapes[0].shape)]
                  + [per_group_of_layer(a) for a in stacked_states]
                  + [per_group(out_shapes[-1].shape)],
        scratch_shapes=[pltpu.VMEM((G, CONV_PAD + R, CONV_WIDTH), F32)])
    return pl.pallas_call(
        functools.partial(_sample_mixer_kernel, layer=layer),
        grid_spec=grid_spec,
        out_shape=out_shapes,
        input_output_aliases={n_in + i: 1 + i for i in range(len(stacked_states))},
        compiler_params=_compiler_params(1),
        name="sample_mixer",
    )(p["sinks"], *operands, *stacked_states)


def _sample_merge_kernel(x_ref, obr_ref, n1g_ref, wg_ref, bg_ref, wbr_ref, wout_ref, y_ref):
    x = x_ref[...]
    xb = _rms_rows(x, n1g_ref[...]).astype(BF16)
    merged = jnp.zeros(x.shape, F32)
    for i in range(N_BRANCH):
        gate = jax.nn.sigmoid(
            _mm(xb, wg_ref[:, i * D_MODEL:(i + 1) * D_MODEL]) + bg_ref[:, i * D_MODEL:(i + 1) * D_MODEL])
        o_i = obr_ref[:, i * BRANCH_WIDTH:(i + 1) * BRANCH_WIDTH].astype(BF16)
        merged = merged + gate * _mm(o_i, wbr_ref[i, :, :D_MODEL])
    y_ref[...] = x + _mm(merged.astype(BF16), wout_ref[:, :D_MODEL])


def _sample_merge(x2d, obr, p, layer):
    weights = tuple(p[k] for k in ("n1g", "wg", "bg", "wbr", "wout"))
    return pl.pallas_call(
        _sample_merge_kernel,
        grid=(1,),
        in_specs=[_full_spec(x2d.shape), _full_spec(obr.shape)] + [_layer_spec(w, layer) for w in weights],
        out_specs=_full_spec(x2d.shape),
        out_shape=jax.ShapeDtypeStruct(x2d.shape, F32),
        compiler_params=_compiler_params(1),
        name="sample_merge",
    )(x2d, obr, *weights)


def _rope_tables(pos):
    half = ROPE_DIM // 2
    inv = jnp.exp(-math.log(ROPE_THETA) * jnp.arange(half, dtype=F32) * (2.0 / ROPE_DIM))
    ang = pos.astype(F32)[:, None] * inv[None, :]
    cos, sin = jnp.cos(ang), jnp.sin(ang)
    n = pos.shape[0]
    pad = jnp.zeros((n, SWA_HEAD_DIM - ROPE_DIM), F32)
    zero = jnp.zeros((n, half), F32)
    c_head = jnp.concatenate([cos, cos, pad + 1.0], axis=1)
    a_head = jnp.concatenate([-sin, zero, pad], axis=1)
    b_head = jnp.concatenate([zero, sin, pad], axis=1)
    rep = LANES // SWA_HEAD_DIM
    return tuple(jnp.tile(tb, (1, rep)) for tb in (c_head, a_head, b_head))


def _lane_pad(w):
    return jnp.pad(w, [(0, 0)] * (w.ndim - 1) + [(0, LANES)]).astype(BF16)


def _rows(v):
    return v.reshape(v.shape[0], 1, -1).astype(F32)


def _stacked_params(norm1_g, w_in, b_in, w_alpha2, b_alpha, gla_norm_g, gmlp_ln_g, gmlp_ln_b,
                    w_spatial, b_spatial, conv_w, conv_b, conv_ln_g, conv_ln_b, q_norm_g, k_norm_g,
                    sinks, w_gate, b_gate, w_branch, w_out, norm2_g, w_gate_up, w_down):
    o_lr = 2 * QK + 2 * QV
    o_rest = o_lr + GLA_RANK

    def repack(m, dtype):
        pad = jnp.zeros(m.shape[:-1] + (LANES - GLA_RANK,), m.dtype)
        return jnp.concatenate(
            [m[..., :o_lr], m[..., o_rest:], m[..., o_lr:o_rest], pad], axis=-1).astype(dtype)

    nt = SAMPLE_TOKENS
    gw = GMLP_WIDTH // GMLP_GROUPS
    pad_rows = ((0, 0), (0, SAMPLE_ROWS - nt), (0, 0))
    wsc = jnp.stack([
        jnp.pad(jnp.repeat(jnp.swapaxes(w_spatial[:, :, :nt, s], 1, 2), gw, axis=2), pad_rows)
        for s in range(nt)], axis=1)
    bsc = jnp.pad(jnp.repeat(jnp.swapaxes(b_spatial[:, :, :nt], 1, 2), gw, axis=2), pad_rows)
    wsh = jnp.stack([
        jnp.pad(conv_w, ((0, 0), (CONV_LEAD + t, CONV_PAD + SAMPLE_ROWS - CONV_LEAD - t - CONV_K), (0, 0)))
        for t in range(nt)], axis=1)
    return {
        "n1g": _rows(norm1_g),
        "win": repack(w_in, BF16),
        "bin": repack(b_in, F32).reshape(DEPTH, 1, -1),
        "wal": jnp.pad(w_alpha2, ((0, 0), (0, LANES - GLA_RANK), (0, 0))).astype(BF16),
        "bal": _rows(b_alpha),
        "glag": _rows(gla_norm_g),
        "lng": _rows(gmlp_ln_g),
        "lnb": _rows(gmlp_ln_b),
        "wsp": w_spatial.astype(F32),
        "bspt": jnp.swapaxes(b_spatial, 1, 2).astype(F32),
        "wsc": wsc.astype(F32),
        "bsc": bsc.astype(F32),
        "cw": jnp.pad(conv_w, ((0, 0), (0, CONV_PAD - CONV_K), (0, 0))).astype(F32),
        "wsh": wsh.astype(F32),
        "cb": _rows(conv_b),
        "clg": _rows(conv_ln_g),
        "clb": _rows(conv_ln_b),
        "qg": _rows(jnp.tile(q_norm_g, (1, SWA_HEADS))),
        "kg": _rows(jnp.tile(k_norm_g, (1, SWA_KV_HEADS))),
        "sinks": sinks.reshape(-1).astype(F32),
        "wg": _lane_pad(w_gate),
        "bg": _rows(b_gate),
        "wbr": _lane_pad(w_branch),
        "wout": _lane_pad(w_out),
        "n2g": _rows(norm2_g),
        "wgu": w_gate_up.astype(BF16),
        "wd": w_down.astype(BF16),
    }


def kernel(x_prompt, x_sample, state_gla, state_conv, cache_swa_k, cache_swa_v, norm1_g, w_in, b_in, w_alpha2, b_alpha, gla_norm_g, gmlp_ln_g, gmlp_ln_b, w_spatial, b_spatial, conv_w, conv_b, conv_ln_g, conv_ln_b, q_norm_g, k_norm_g, sinks, w_gate, b_gate, w_branch, w_out, norm2_g, w_gate_up, w_down):
    bp, lp, _ = x_prompt.shape
    bs, ls, _ = x_sample.shape
    R = SAMPLE_ROWS
    prompt_tables = _rope_tables(jnp.arange(lp))
    sample_tables = tuple(
        jnp.pad(tb, ((0, R - ls), (0, 0))) for tb in _rope_tables(PAST_LEN + jnp.arange(ls)))
    p = _stacked_params(norm1_g, w_in, b_in, w_alpha2, b_alpha, gla_norm_g, gmlp_ln_g, gmlp_ln_b,
                        w_spatial, b_spatial, conv_w, conv_b, conv_ln_g, conv_ln_b, q_norm_g,
                        k_norm_g, sinks, w_gate, b_gate, w_branch, w_out, norm2_g, w_gate_up, w_down)

    gla_in = state_gla.reshape(DEPTH, bs, QK, GLA_DV)
    swa_k_in = cache_swa_k.reshape(DEPTH, bs, WINDOW, SWA_KV)
    swa_v_in = cache_swa_v.reshape(DEPTH, bs, WINDOW, SWA_KV)

    sample_states = [jnp.zeros(a.shape, F32) for a in (gla_in, state_conv, swa_k_in, swa_v_in)]

    yp = x_prompt
    ys = x_sample.reshape(bs * ls, D_MODEL)
    outs = {k: [] for k in ("gla_p", "conv_p", "kp", "vp", "gm")}
    for l in range(DEPTH):
        yp, g1, c1, k1, v1 = _prompt_mixer(yp, p, prompt_tables, l)
        yp = _ffn(yp.reshape(bp * lp, D_MODEL), p, l).reshape(bp, lp, D_MODEL)
        outs["gla_p"].append(g1.reshape(bp, GLA_HEADS, GLA_DK, GLA_DV))
        outs["conv_p"].append(c1)
        outs["kp"].append(k1.reshape(bp, WINDOW, SWA_KV_HEADS, SWA_HEAD_DIM))
        outs["vp"].append(v1.reshape(bp, WINDOW, SWA_KV_HEADS, SWA_HEAD_DIM))
        z, ya = _sample_proj(ys, p, l)
        obr, *sample_states, gv2 = _sample_mixer(
            z, ya, gla_in, state_conv, swa_k_in, swa_v_in, p, sample_tables, sample_states, l)
        ys = _sample_merge(ys, obr, p, l)
        ys = _ffn(ys, p, l)
        outs["gm"].append(gv2.reshape(bs, ls, GMLP_WIDTH))
    st = lambda name: jnp.stack(outs[name], 0)
    gla_s, conv_s, ks, vs = sample_states
    kv_shape = (DEPTH, bs, WINDOW, SWA_KV_HEADS, SWA_HEAD_DIM)
    return (yp, ys.reshape(bs, ls, D_MODEL), st("gla_p"), gla_s.reshape(state_gla.shape), st("conv_p"),
            conv_s, st("kp"), st("vp"), ks.reshape(kv_shape), vs.reshape(kv_shape), st("gm"))
```
